```python
import math
import jax, jax.numpy as jnp
from jax import lax
import numpy as np

D_MODEL = 1024
BATCH = 1
SEQ = 16384
DEPTH = 2
DEC_BATCH = 8
DEC_SEQ = 16
PAST_LEN = 1024

CHUNK = 64
Q_BLOCK = 128
N_AB = (DEPTH + 1) // 2
N_C = DEPTH // 2
A_HEADS = 4
A_HEAD_DIM = 64
A_WIDTH = A_HEADS * 2 * A_HEAD_DIM
B_WIDTH = 512
B_BLOCKS = 8
B_BLOCK_DIM = B_WIDTH // B_BLOCKS
B_CONV = 4
RG_C = 8.0
C_HEADS = 16
C_HEAD_DIM = 64
C_WIDTH = C_HEADS * C_HEAD_DIM
D_FF = 2816
FFN_CONV = 3

AB_IN = 3 * A_WIDTH + 2 * B_WIDTH
C_IN = 3 * C_WIDTH + C_HEADS
ROPE_THETA = 10000.0
ALPHA = (2 * DEPTH) ** 0.25
BETA = (8 * DEPTH) ** -0.25
LN_EPS = 1e-5
NEG = -1e30

kernel_name = "hybrid_streaming_diffattn_rglru_fox_step"


def layer_norm(x, g, b):
    xf = x.astype(jnp.float32)
    mu = jnp.mean(xf, -1, keepdims=True)
    var = jnp.mean(jnp.square(xf - mu), -1, keepdims=True)
    return ((xf - mu) * lax.rsqrt(var + LN_EPS) * g + b).astype(x.dtype)


def rope(x, pos):
    d = x.shape[-1]
    inv = ROPE_THETA ** (-jnp.arange(0, d, 2, dtype=jnp.float32) / d)
    ang = pos.astype(jnp.float32)[:, None] * inv[None, :]
    cos = jnp.cos(ang)[None, :, None, :]
    sin = jnp.sin(ang)[None, :, None, :]
    xf = x.astype(jnp.float32)
    x1, x2 = xf[..., : d // 2], xf[..., d // 2:]
    return jnp.concatenate([x1 * cos - x2 * sin, x2 * cos + x1 * sin], -1).astype(x.dtype)


def causal_conv(x, buf, w, b):
    width = w.shape[0]
    t = x.shape[1]
    xp = jnp.concatenate([buf.astype(x.dtype), x], axis=1)
    y = xp[:, 0:t] * w[0]
    for j in range(1, width):
        y = y + xp[:, j:j + t] * w[j]
    return y + b, xp[:, xp.shape[1] - (width - 1):]


def sweep_query_blocks(fn, qpos, *qs):
    t = qpos.shape[0]
    if t <= Q_BLOCK:
        return fn(qpos, *qs)
    nb = t // Q_BLOCK
    pos_b = qpos.reshape(nb, Q_BLOCK)
    q_b = tuple(jnp.moveaxis(q.reshape(q.shape[0], nb, Q_BLOCK, *q.shape[2:]), 1, 0) for q in qs)
    out = lax.map(lambda a: fn(a[0], *a[1]), (pos_b, q_b))
    out = jnp.moveaxis(out, 0, 1)
    return out.reshape(out.shape[0], t, *out.shape[3:])


def linear_scan(a, b, h0):
    def comb(l, r):
        return (l[0] * r[0], r[0] * l[1] + r[1])
    acum, bcum = lax.associative_scan(comb, (a, b), axis=1)
    return acum * h0.astype(jnp.float32)[:, None, :] + bcum


def ab_mixer(u, cache_k, cache_v, h0, conv_buf, p, j, lam_init):
    bsz, t, _ = u.shape
    past = cache_k.shape[1]
    pos_q = past + jnp.arange(t, dtype=jnp.int32)
    pos_k = jnp.arange(past + t, dtype=jnp.int32)
    proj = u @ p['w_in_ab'][j]
    q, k, v, xb, gb = jnp.split(proj, [A_WIDTH, 2 * A_WIDTH, 3 * A_WIDTH, 3 * A_WIDTH + B_WIDTH], axis=-1)
    q = rope(q.reshape(bsz, t, A_HEADS * 2, A_HEAD_DIM), pos_q).reshape(bsz, t, A_HEADS, 2, A_HEAD_DIM)
    k = rope(k.reshape(bsz, t, A_HEADS * 2, A_HEAD_DIM), pos_q).reshape(bsz, t, A_HEADS, 2 * A_HEAD_DIM)
    v = v.reshape(bsz, t, A_HEADS, 2 * A_HEAD_DIM)
    k_all = jnp.concatenate([cache_k.astype(k.dtype), k], 1).astype(jnp.float32)
    v_all = jnp.concatenate([cache_v.astype(v.dtype), v], 1).astype(jnp.float32)
    k1, k2 = k_all[..., :A_HEAD_DIM], k_all[..., A_HEAD_DIM:]
    lam = (jnp.exp(jnp.sum(p['lam_q1'][j].astype(jnp.float32) * p['lam_k1'][j]))
           - jnp.exp(jnp.sum(p['lam_q2'][j].astype(jnp.float32) * p['lam_k2'][j])) + lam_init)
    scale = A_HEAD_DIM ** -0.5

    def block(pq, q1b, q2b):
        mask = (pos_k[None, :] // CHUNK) <= (pq[:, None] // CHUNK)
        def probs(qb, kb):
            s = jnp.einsum('bqhd,bkhd->bhqk', qb.astype(jnp.float32), kb) * scale
            return jax.nn.softmax(jnp.where(mask, s, NEG), axis=-1)
        pm = probs(q1b, k1) - lam * probs(q2b, k2)
        return jnp.einsum('bhqk,bkhe->bqhe', pm, v_all)

    o = sweep_query_blocks(block, pos_q, q[..., 0, :], q[..., 1, :])
    o = o * lax.rsqrt(jnp.mean(jnp.square(o), -1, keepdims=True) + LN_EPS) * p['attn_gain'][j] * (1.0 - lam_init)
    o = o.reshape(bsz, t, A_WIDTH).astype(u.dtype)
    xc, new_buf = causal_conv(xb, conv_buf, p['b_conv_w'][j], p['b_conv_b'][j])
    xr = xc.reshape(bsz, t, B_BLOCKS, B_BLOCK_DIM)
    r = jax.nn.sigmoid(jnp.einsum('btni,nio->btno', xr, p['w_rg_a'][j]).reshape(bsz, t, B_WIDTH) + p['b_rg_a'][j])
    ig = jax.nn.sigmoid(jnp.einsum('btni,nio->btno', xr, p['w_rg_x'][j]).reshape(bsz, t, B_WIDTH) + p['b_rg_x'][j])
    log_a = -RG_C * r.astype(jnp.float32) * jax.nn.softplus(-p['rg_L'][j].astype(jnp.float32))
    a = jnp.exp(log_a)
    bx = jnp.sqrt(-jnp.expm1(2.0 * log_a)) * (ig * xc).astype(jnp.float32)
    h = linear_scan(a, bx, h0)
    yb = (h * jax.nn.gelu(gb.astype(jnp.float32))).astype(u.dtype)
    mix = jnp.concatenate([o, yb], -1) @ p['w_out_ab'][j]
    return mix, (k, v, h[:, -1], new_buf)


def c_mixer(u, cache_k, cache_v, cache_logf, p, j):
    bsz, t, _ = u.shape
    past = cache_k.shape[1]
    pos_q = past + jnp.arange(t, dtype=jnp.int32)
    pos_k = jnp.arange(past + t, dtype=jnp.int32)
    proj = u @ p['w_in_c'][j]
    q, k, v, fl = jnp.split(proj, [C_WIDTH, 2 * C_WIDTH, 3 * C_WIDTH], axis=-1)
    q = q.reshape(bsz, t, C_HEADS, C_HEAD_DIM)
    k = k.reshape(bsz, t, C_HEADS, C_HEAD_DIM)
    v = v.reshape(bsz, t, C_HEADS, C_HEAD_DIM)
    logf = jax.nn.log_sigmoid((fl + p['b_f'][j]).astype(jnp.float32))
    k_all = jnp.concatenate([cache_k.astype(k.dtype), k], 1).astype(jnp.float32)
    v_all = jnp.concatenate([cache_v.astype(v.dtype), v], 1).astype(jnp.float32)
    cum = jnp.cumsum(jnp.concatenate([cache_logf.astype(jnp.float32), logf], 1), axis=1)
    cum_q = cum[:, past:]
    cum_k = jnp.transpose(cum, (0, 2, 1))[:, :, None, :]
    scale = C_HEAD_DIM ** -0.5

    def block(pq, qb, cqb):
        s = jnp.einsum('bqhd,bkhd->bhqk', qb.astype(jnp.float32), k_all) * scale
        s = s + jnp.transpose(cqb, (0, 2, 1))[..., None] - cum_k
        mask = pos_k[None, :] <= pq[:, None]
        pm = jax.nn.softmax(jnp.where(mask, s, NEG), axis=-1)
        return jnp.einsum('bhqk,bkhd->bqhd', pm, v_all)

    o = sweep_query_blocks(block, pos_q, q, cum_q)
    mix = o.reshape(bsz, t, C_WIDTH).astype(u.dtype) @ p['w_out_c'][j]
    return mix, (k, v, logf)


def conv_ffn(u, buf, p, i):
    up = u @ p['w_up'][i]
    a, g = jnp.split(up, 2, axis=-1)
    gc, new_buf = causal_conv(g, buf, p['ffn_conv_w'][i], p['ffn_conv_b'][i])
    return (a * jax.nn.silu(gc)) @ p['w_down'][i], new_buf


def trunk(x, c, cache_a_k, cache_a_v, state_b_h, state_b_conv, cache_c_k, cache_c_v, cache_c_logf, state_ffn_conv, p):
    mods = jnp.einsum('bd,lde->lbe', jax.nn.silu(c), p['w_ada']) + p['b_ada'][:, None, :]
    ak, av, bh, bc, ck, cv, cl, fc = [], [], [], [], [], [], [], []
    for i in range(DEPTH):
        j = i // 2
        sh1, sc1, g1, sh2, sc2, g2 = [m[:, None, :] for m in jnp.split(mods[i], 6, axis=-1)]
        u = x * (1 + sc1) + sh1
        if i % 2 == 0:
            lam_init = 0.8 - 0.6 * math.exp(-0.3 * i)
            mix, (k, v, h, cb) = ab_mixer(u, cache_a_k[j], cache_a_v[j], state_b_h[j], state_b_conv[j], p, j, lam_init)
            ak.append(k); av.append(v); bh.append(h); bc.append(cb)
        else:
            mix, (k, v, lf) = c_mixer(u, cache_c_k[j], cache_c_v[j], cache_c_logf[j], p, j)
            ck.append(k); cv.append(v); cl.append(lf)
        x = layer_norm(ALPHA * x + g1 * mix, p['ln1_g'][i], p['ln1_b'][i])
        u = x * (1 + sc2) + sh2
        f, fb = conv_ffn(u, state_ffn_conv[i], p, i)
        fc.append(fb)
        x = layer_norm(ALPHA * x + g2 * f, p['ln2_g'][i], p['ln2_b'][i])
    return (x, jnp.stack(ak), jnp.stack(av), jnp.stack(bh), jnp.stack(bc),
            jnp.stack(ck), jnp.stack(cv), jnp.stack(cl), jnp.stack(fc))


def setup_inputs(seed: int = 0) -> dict:
    key = jax.random.key(seed)
    ks = iter(jax.random.split(key, 48))
    f32 = jnp.float32

    def nrm(shape, scale):
        return scale * jax.random.normal(next(ks), shape, f32)

    D = D_MODEL
    gate_off = jnp.asarray(np.concatenate([np.zeros(2 * D), np.ones(D), np.zeros(2 * D), np.ones(D)]), f32)
    u_a = jax.random.uniform(next(ks), (N_AB, B_WIDTH), f32, minval=0.9, maxval=0.999)
    s_a = u_a ** (1.0 / RG_C)
    return {
        'x_prompt': nrm((BATCH, SEQ, D), 1.0),
        'x_sample': nrm((DEC_BATCH, DEC_SEQ, D), 1.0),
        'c_prompt': nrm((BATCH, D), 1.0),
        'c_sample': nrm((DEC_BATCH, D), 1.0),
        'cache_a_k': nrm((N_AB, DEC_BATCH, PAST_LEN, A_HEADS, 2 * A_HEAD_DIM), 1.0),
        'cache_a_v': nrm((N_AB, DEC_BATCH, PAST_LEN, A_HEADS, 2 * A_HEAD_DIM), 1.0),
        'state_b_h': nrm((N_AB, DEC_BATCH, B_WIDTH), 0.5),
        'state_b_conv': nrm((N_AB, DEC_BATCH, B_CONV - 1, B_WIDTH), 1.0),
        'cache_c_k': nrm((N_C, DEC_BATCH, PAST_LEN, C_HEADS, C_HEAD_DIM), 1.0),
        'cache_c_v': nrm((N_C, DEC_BATCH, PAST_LEN, C_HEADS, C_HEAD_DIM), 1.0),
        'cache_c_logf': jax.nn.log_sigmoid(3.0 + nrm((N_C, DEC_BATCH, PAST_LEN, C_HEADS), 1.0)),
        'state_ffn_conv': nrm((DEPTH, DEC_BATCH, FFN_CONV - 1, D_FF), 1.0),
        'w_ada': nrm((DEPTH, D, 6 * D), 0.1 * D ** -0.5),
        'b_ada': nrm((DEPTH, 6 * D), 0.02) + gate_off,
        'ln1_g': 1.0 + nrm((DEPTH, D), 0.02),
        'ln1_b': nrm((DEPTH, D), 0.02),
        'ln2_g': 1.0 + nrm((DEPTH, D), 0.02),
        'ln2_b': nrm((DEPTH, D), 0.02),
        'w_in_ab': nrm((N_AB, D, AB_IN), D ** -0.5),
        'lam_q1': nrm((N_AB, A_HEAD_DIM), 0.1),
        'lam_k1': nrm((N_AB, A_HEAD_DIM), 0.1),
        'lam_q2': nrm((N_AB, A_HEAD_DIM), 0.1),
        'lam_k2': nrm((N_AB, A_HEAD_DIM), 0.1),
        'attn_gain': 1.0 + nrm((N_AB, A_HEADS, 2 * A_HEAD_DIM), 0.02),
        'b_conv_w': nrm((N_AB, B_CONV, B_WIDTH), B_CONV ** -0.5),
        'b_conv_b': nrm((N_AB, B_WIDTH), 0.02),
        'w_rg_a': nrm((N_AB, B_BLOCKS, B_BLOCK_DIM, B_BLOCK_DIM), B_BLOCK_DIM ** -0.5),
        'b_rg_a': nrm((N_AB, B_WIDTH), 0.02),
        'w_rg_x': nrm((N_AB, B_BLOCKS, B_BLOCK_DIM, B_BLOCK_DIM), B_BLOCK_DIM ** -0.5),
        'b_rg_x': nrm((N_AB, B_WIDTH), 0.02),
        'rg_L': jnp.log(s_a) - jnp.log1p(-s_a),
        'w_out_ab': nrm((N_AB, A_WIDTH + B_WIDTH, D), BETA * (A_WIDTH + B_WIDTH) ** -0.5),
        'w_in_c': nrm((N_C, D, C_IN), D ** -0.5),
        'b_f': 3.0 + nrm((N_C, C_HEADS), 0.1),
        'w_out_c': nrm((N_C, C_WIDTH, D), BETA * C_WIDTH ** -0.5),
        'w_up': nrm((DEPTH, D, 2 * D_FF), D ** -0.5),
        'ffn_conv_w': nrm((DEPTH, FFN_CONV, D_FF), FFN_CONV ** -0.5),
        'ffn_conv_b': nrm((DEPTH, D_FF), 0.02),
        'w_down': nrm((DEPTH, D_FF, D), BETA * D_FF ** -0.5),
    }


def reference(x_prompt, x_sample, c_prompt, c_sample, cache_a_k, cache_a_v, state_b_h, state_b_conv,
              cache_c_k, cache_c_v, cache_c_logf, state_ffn_conv, w_ada, b_ada, ln1_g, ln1_b, ln2_g, ln2_b,
              w_in_ab, lam_q1, lam_k1, lam_q2, lam_k2, attn_gain, b_conv_w, b_conv_b, w_rg_a, b_rg_a,
              w_rg_x, b_rg_x, rg_L, w_out_ab, w_in_c, b_f, w_out_c, w_up, ffn_conv_w, ffn_conv_b, w_down):
    p = dict(w_ada=w_ada, b_ada=b_ada, ln1_g=ln1_g, ln1_b=ln1_b, ln2_g=ln2_g, ln2_b=ln2_b,
             w_in_ab=w_in_ab, lam_q1=lam_q1, lam_k1=lam_k1, lam_q2=lam_q2, lam_k2=lam_k2, attn_gain=attn_gain,
             b_conv_w=b_conv_w, b_conv_b=b_conv_b, w_rg_a=w_rg_a, b_rg_a=b_rg_a, w_rg_x=w_rg_x, b_rg_x=b_rg_x,
             rg_L=rg_L, w_out_ab=w_out_ab, w_in_c=w_in_c, b_f=b_f, w_out_c=w_out_c, w_up=w_up,
             ffn_conv_w=ffn_conv_w, ffn_conv_b=ffn_conv_b, w_down=w_down)
    bp = x_prompt.shape[0]
    dt = x_prompt.dtype
    y_p, p_a_k, p_a_v, p_b_h, p_b_conv, p_c_k, p_c_v, p_c_logf, p_ffn_conv = trunk(
        x_prompt, c_prompt,
        jnp.zeros((N_AB, bp, 0, A_HEADS, 2 * A_HEAD_DIM), dt), jnp.zeros((N_AB, bp, 0, A_HEADS, 2 * A_HEAD_DIM), dt),
        jnp.zeros((N_AB, bp, B_WIDTH), dt), jnp.zeros((N_AB, bp, B_CONV - 1, B_WIDTH), dt),
        jnp.zeros((N_C, bp, 0, C_HEADS, C_HEAD_DIM), dt), jnp.zeros((N_C, bp, 0, C_HEADS, C_HEAD_DIM), dt),
        jnp.zeros((N_C, bp, 0, C_HEADS), dt), jnp.zeros((DEPTH, bp, FFN_CONV - 1, D_FF), dt), p)
    y_s, s_a_k, s_a_v, s_b_h, s_b_conv, s_c_k, s_c_v, s_c_logf, s_ffn_conv = trunk(
        x_sample, c_sample, cache_a_k, cache_a_v, state_b_h, state_b_conv,
        cache_c_k, cache_c_v, cache_c_logf, state_ffn_conv, p)
    return (y_p, y_s, p_a_k, p_a_v, p_b_h, p_b_conv, p_c_k, p_c_v, p_c_logf, p_ffn_conv,
            s_a_k, s_a_v, s_b_h, s_b_conv, s_c_k, s_c_v, s_c_logf, s_ffn_conv)
```

```python
import functools
import math

import jax
import jax.numpy as jnp
from jax import lax
from jax.experimental import pallas as pl
from jax.experimental.pallas import tpu as pltpu

F32 = jnp.float32
BF16 = jnp.bfloat16

D_MODEL = 1024
DEPTH = 2
CHUNK = 64
CHUNK_SHIFT = 6
A_HEADS = 4
A_HEAD_DIM = 64
A_WIDTH = A_HEADS * 2 * A_HEAD_DIM
B_WIDTH = 512
B_BLOCKS = 8
B_CONV = 4
RG_C = 8.0
C_HEADS = 16
C_HEAD_DIM = 64
C_WIDTH = C_HEADS * C_HEAD_DIM
D_FF = 2816
FFN_CONV = 3
ROPE_THETA = 10000.0
ALPHA = (2 * DEPTH) ** 0.25
LN_EPS = 1e-5
NEG = -1e30
LOG2E = 1.4426950408889634

LANES = 128
SUBLANES = 8
VMEM_LIMIT = 56 * 1024 * 1024


def _params(sem):
    return pltpu.CompilerParams(dimension_semantics=sem, vmem_limit_bytes=VMEM_LIMIT)


def _row_tile(t, pref):
    if t <= pref:
        return t
    tm = pref
    while t % tm:
        tm //= 2
    return tm


def _modulate(x, sc, sh):
    return (x * (1.0 + sc) + sh).astype(BF16)


def _sigmoid(x):
    return 1.0 / (1.0 + jnp.exp(-x))


def _softplus(x):
    return jnp.maximum(x, 0.0) + jnp.log1p(jnp.exp(-jnp.abs(x)))


def _split3(x):
    hi = x.astype(BF16)
    r1 = x - hi.astype(F32)
    mid = r1.astype(BF16)
    lo = (r1 - mid.astype(F32)).astype(BF16)
    return hi, mid, lo


def _dot(a, b):
    return jnp.dot(a, b, preferred_element_type=F32)


def _dot_nt(a, b):
    return lax.dot_general(a, b, (((1,), (1,)), ((), ())), preferred_element_type=F32)


def _mods_kernel(c_ref, w_ref, b_ref, o_ref):
    c = c_ref[...]
    s = (c * _sigmoid(c)).astype(BF16)
    o_ref[0] = _dot(s, w_ref[0].astype(BF16)) + b_ref[0]


def _mods(c_all, w_ada, b_ada):
    rows, d = c_all.shape
    n = w_ada.shape[-1]
    tn = 1536
    return pl.pallas_call(
        _mods_kernel,
        grid=(DEPTH, n // tn),
        in_specs=[pl.BlockSpec((rows, d), lambda l, j: (0, 0)),
                  pl.BlockSpec((1, d, tn), lambda l, j: (l, 0, j)),
                  pl.BlockSpec((1, 1, tn), lambda l, j: (l, 0, j))],
        out_specs=pl.BlockSpec((1, rows, tn), lambda l, j: (l, 0, j)),
        out_shape=jax.ShapeDtypeStruct((DEPTH, rows, n), F32),
        compiler_params=_params(("arbitrary", "arbitrary")),
        name="mods",
    )(c_all, w_ada, b_ada.reshape(DEPTH, 1, n))


def _rope_slab(x, cos, sin_signed, first_half):
    fwd = pltpu.roll(x, LANES - A_HEAD_DIM // 2, 1)
    bwd = pltpu.roll(x, A_HEAD_DIM // 2, 1)
    partner = jnp.where(first_half, fwd, bwd)
    return x * cos + partner * sin_signed


def _proj_ab_kernel(x_ref, sc_ref, sh_ref, w_ref, cos_ref, sin_ref, qkv_ref, k_ref, v_ref, xg_ref, *, q_scale):
    u = _modulate(x_ref[0], sc_ref[0], sh_ref[0])
    pr = _dot(u, w_ref[...])
    cos = cos_ref[...]
    sin = sin_ref[...]
    lane = lax.broadcasted_iota(jnp.int32, (1, LANES), 1)
    first_half = (lane & (A_HEAD_DIM - 1)) < (A_HEAD_DIM // 2)
    for h in range(A_HEADS):
        sl = slice(h * LANES, (h + 1) * LANES)
        q = _rope_slab(pr[:, sl], cos, sin, first_half)
        qkv_ref[0, :, sl] = (q * q_scale).astype(BF16)
        ksl = slice(A_WIDTH + h * LANES, A_WIDTH + (h + 1) * LANES)
        k = _rope_slab(pr[:, ksl], cos, sin, first_half)
        k_ref[0, :, sl] = k
        qkv_ref[0, :, ksl] = k.astype(BF16)
    v = pr[:, 2 * A_WIDTH:3 * A_WIDTH]
    v_ref[0] = v
    qkv_ref[0, :, 2 * A_WIDTH:3 * A_WIDTH] = v.astype(BF16)
    xg_ref[0] = pr[:, 3 * A_WIDTH:]


def _proj_ab(x, sc, sh, w, cos, sin, tm_pref=512):
    b, t, d = x.shape
    tm = _row_tile(t, tm_pref)
    n = w.shape[1]
    q_scale = A_HEAD_DIM ** -0.5 * LOG2E
    return pl.pallas_call(
        functools.partial(_proj_ab_kernel, q_scale=q_scale),
        grid=(b, t // tm),
        in_specs=[pl.BlockSpec((1, tm, d), lambda bi, i: (bi, i, 0)),
                  pl.BlockSpec((1, 1, d), lambda bi, i: (bi, 0, 0)),
                  pl.BlockSpec((1, 1, d), lambda bi, i: (bi, 0, 0)),
                  pl.BlockSpec((d, n), lambda bi, i: (0, 0)),
                  pl.BlockSpec((tm, LANES), lambda bi, i: (i, 0)),
                  pl.BlockSpec((tm, LANES), lambda bi, i: (i, 0))],
        out_specs=[pl.BlockSpec((1, tm, 3 * A_WIDTH), lambda bi, i: (bi, i, 0)),
                   pl.BlockSpec((1, tm, A_WIDTH), lambda bi, i: (bi, i, 0)),
                   pl.BlockSpec((1, tm, A_WIDTH), lambda bi, i: (bi, i, 0)),
                   pl.BlockSpec((1, tm, 2 * B_WIDTH), lambda bi, i: (bi, i, 0))],
        out_shape=[jax.ShapeDtypeStruct((b, t, 3 * A_WIDTH), BF16),
                   jax.ShapeDtypeStruct((b, t, A_WIDTH), F32),
                   jax.ShapeDtypeStruct((b, t, A_WIDTH), F32),
                   jax.ShapeDtypeStruct((b, t, 2 * B_WIDTH), F32)],
        compiler_params=_params(("arbitrary", "arbitrary")),
        name="proj_ab",
    )(x, sc, sh, w, cos, sin)


def _log_sigmoid(x):
    return jnp.minimum(x, 0.0) - jnp.log1p(jnp.exp(-jnp.abs(x)))


def _proj_c_kernel(x_ref, sc_ref, sh_ref, w_ref, wft_ref, bfr_ref, bfc_ref,
                   qkv_ref, k_ref, v_ref, lf_ref, lft_ref, bias_ref, run_ref, *, q_scale):
    i = pl.program_id(1)
    tm = x_ref.shape[1]

    @pl.when(i == 0)
    def _():
        run_ref[...] = jnp.zeros_like(run_ref)

    u = _modulate(x_ref[0], sc_ref[0], sh_ref[0])
    pr = _dot(u, w_ref[...])
    qkv_ref[0, :, :C_WIDTH] = (pr[:, :C_WIDTH] * q_scale).astype(BF16)
    k = pr[:, C_WIDTH:2 * C_WIDTH]
    v = pr[:, 2 * C_WIDTH:3 * C_WIDTH]
    k_ref[0] = k
    v_ref[0] = v
    qkv_ref[0, :, C_WIDTH:2 * C_WIDTH] = k.astype(BF16)
    qkv_ref[0, :, 2 * C_WIDTH:3 * C_WIDTH] = v.astype(BF16)
    lf = _log_sigmoid(pr[:, 3 * C_WIDTH:] + bfr_ref[...])
    lf_ref[0] = lf[:, :C_HEADS]
    lft = _log_sigmoid(_dot_nt(wft_ref[...], u) + bfc_ref[...])
    lft_ref[0] = lft
    r = lax.broadcasted_iota(jnp.int32, (tm, tm), 0)
    c = lax.broadcasted_iota(jnp.int32, (tm, tm), 1)
    upper = jnp.where(r <= c, 1.0, 0.0).astype(BF16)
    hi, mid, lo = _split3(lft)
    cum = (_dot(hi, upper) + _dot(mid, upper)) + _dot(lo, upper) + run_ref[...]
    bias_ref[0] = cum * (-LOG2E)
    run_ref[...] = run_ref[...] + jnp.sum(lft, axis=1, keepdims=True)


def _proj_c(x, sc, sh, w, wft, bf_row, bf_col, tm_pref=512):
    b, t, d = x.shape
    tm = _row_tile(t, tm_pref)
    n = w.shape[1]
    q_scale = C_HEAD_DIM ** -0.5 * LOG2E
    return pl.pallas_call(
        functools.partial(_proj_c_kernel, q_scale=q_scale),
        grid=(b, t // tm),
        in_specs=[pl.BlockSpec((1, tm, d), lambda bi, i: (bi, i, 0)),
                  pl.BlockSpec((1, 1, d), lambda bi, i: (bi, 0, 0)),
                  pl.BlockSpec((1, 1, d), lambda bi, i: (bi, 0, 0)),
                  pl.BlockSpec((d, n), lambda bi, i: (0, 0)),
                  pl.BlockSpec((C_HEADS, d), lambda bi, i: (0, 0)),
                  pl.BlockSpec((1, LANES), lambda bi, i: (0, 0)),
                  pl.BlockSpec((C_HEADS, 1), lambda bi, i: (0, 0))],
        out_specs=[pl.BlockSpec((1, tm, 3 * C_WIDTH), lambda bi, i: (bi, i, 0)),
                   pl.BlockSpec((1, tm, C_WIDTH), lambda bi, i: (bi, i, 0)),
                   pl.BlockSpec((1, tm, C_WIDTH), lambda bi, i: (bi, i, 0)),
                   pl.BlockSpec((1, tm, C_HEADS), lambda bi, i: (bi, i, 0)),
                   pl.BlockSpec((1, C_HEADS, tm), lambda bi, i: (bi, 0, i)),
                   pl.BlockSpec((1, C_HEADS, tm), lambda bi, i: (bi, 0, i))],
        out_shape=[jax.ShapeDtypeStruct((b, t, 3 * C_WIDTH), BF16),
                   jax.ShapeDtypeStruct((b, t, C_WIDTH), F32),
                   jax.ShapeDtypeStruct((b, t, C_WIDTH), F32),
                   jax.ShapeDtypeStruct((b, t, C_HEADS), F32),
                   jax.ShapeDtypeStruct((b, C_HEADS, t), F32),
                   jax.ShapeDtypeStruct((b, C_HEADS, t), F32)],
        scratch_shapes=[pltpu.VMEM((C_HEADS, 1), F32)],
        compiler_params=_params(("arbitrary", "arbitrary")),
        name="proj_c",
    )(x, sc, sh, w, wft, bf_row, bf_col)


def _lambda(lam_ref, lam_init):
    lq1, lk1, lq2, lk2 = (lam_ref[r:r + 1, :] for r in range(4))
    return (jnp.exp(jnp.sum(lq1 * lk1, axis=1, keepdims=True))
            - jnp.exp(jnp.sum(lq2 * lk2, axis=1, keepdims=True)) + lam_init)


def _diff_finish(o0, o1, lam, gain, lam_init):
    o = o0 - lam * o1
    o = o * lax.rsqrt(jnp.mean(o * o, axis=-1, keepdims=True) + LN_EPS)
    return o * gain * (1.0 - lam_init)


def _attn_kernel(*refs, mode, tq, lam_init):
    if mode == "A":
        q_ref, k_ref, v_ref, lam_ref, gain_ref, o_ref, qm_ref, m_ref, l_ref, acc_ref = refs
        bias_ref = None
    else:
        q_ref, k_ref, v_ref, bias_ref, o_ref, qm_ref, m_ref, l_ref, acc_ref = refs
    qi = pl.program_id(2)
    lane = lax.broadcasted_iota(jnp.int32, (1, LANES), 1)
    low = lane < (LANES // 2)
    q = q_ref[0]
    qm_ref[0] = jnp.where(low, q, jnp.zeros_like(q))
    qm_ref[1] = jnp.where(low, jnp.zeros_like(q), q)
    m_ref[...] = jnp.full(m_ref.shape, NEG, F32)
    l_ref[...] = jnp.zeros(l_ref.shape, F32)
    acc_ref[...] = jnp.zeros(acc_ref.shape, F32)

    def step(j, masked):
        off = pl.multiple_of(j * tq, tq)
        k = k_ref[0, pl.ds(off, tq), :]
        v = v_ref[0, pl.ds(off, tq), :]
        if masked:
            row = lax.broadcasted_iota(jnp.int32, (tq, tq), 0)
            col = lax.broadcasted_iota(jnp.int32, (tq, tq), 1)
            keep = ((col >> CHUNK_SHIFT) <= (row >> CHUNK_SHIFT)) if mode == "A" else (col <= row)
        for a in range(2):
            s = _dot_nt(qm_ref[a], k)
            if bias_ref is not None:
                s = s + bias_ref[0, a, pl.ds(j, 1), :]
            if masked:
                s = jnp.where(keep, s, NEG)
            m_prev = m_ref[a]
            m_new = jnp.maximum(m_prev, jnp.max(s, axis=1, keepdims=True))
            alpha = jnp.exp2(m_prev - m_new)
            p = jnp.exp2(s - m_new)
            l_ref[a] = alpha * l_ref[a] + jnp.sum(p, axis=1, keepdims=True)
            acc_ref[a] = alpha * acc_ref[a] + _dot(p.astype(BF16), v)
            m_ref[a] = m_new

    def body(j, carry):
        step(j, False)
        return carry

    lax.fori_loop(0, qi, body, 0)
    step(qi, True)

    o0 = acc_ref[0] * (1.0 / l_ref[0])
    o1 = acc_ref[1] * (1.0 / l_ref[1])
    if mode == "A":
        o = _diff_finish(o0, o1, _lambda(lam_ref, lam_init), gain_ref[0], lam_init)
    else:
        o = jnp.where(low, o0, o1)
    o_ref[0] = o.astype(o_ref.dtype)


def _attention(qkv, mode, tq, lam=None, gain=None, bias=None, lam_init=0.0):
    b, t, w3 = qkv.shape
    width = w3 // 3
    groups = width // LANES
    nq = t // tq
    in_specs = [pl.BlockSpec((1, tq, LANES), lambda bi, g, i: (bi, i, g)),
                pl.BlockSpec((1, t, LANES), lambda bi, g, i: (bi, 0, groups + g)),
                pl.BlockSpec((1, t, LANES), lambda bi, g, i: (bi, 0, 2 * groups + g))]
    args = [qkv, qkv, qkv]
    if mode == "A":
        in_specs += [pl.BlockSpec((4, A_HEAD_DIM), lambda bi, g, i: (0, 0)),
                     pl.BlockSpec((1, 1, LANES), lambda bi, g, i: (g, 0, 0))]
        args += [lam, gain]
    else:
        in_specs += [pl.BlockSpec((1, 2, nq, tq), lambda bi, g, i: (bi, g, 0, 0))]
        args += [bias.reshape(b, 2 * groups, nq, tq)]
    return pl.pallas_call(
        functools.partial(_attn_kernel, mode=mode, tq=tq, lam_init=lam_init),
        grid=(b, groups, nq),
        in_specs=in_specs,
        out_specs=pl.BlockSpec((1, tq, LANES), lambda bi, g, i: (bi, i, g)),
        out_shape=jax.ShapeDtypeStruct((b, t, width), BF16),
        scratch_shapes=[pltpu.VMEM((2, tq, LANES), BF16),
                        pltpu.VMEM((2, tq, 1), F32),
                        pltpu.VMEM((2, tq, 1), F32),
                        pltpu.VMEM((2, tq, LANES), F32)],
        compiler_params=_params(("arbitrary", "arbitrary", "arbitrary")),
        name="attn_" + mode,
    )(*args)


def _cached_attn_kernel(*refs, mode, past, lam_init):
    if mode == "A":
        qkv_ref, ck_ref, cv_ref, lam_ref, gain_ref, o_ref = refs
    else:
        qkv_ref, ck_ref, cv_ref, clf_ref, nlf_ref, o_ref = refs
    t = qkv_ref.shape[1]
    width = o_ref.shape[2]
    groups = width // LANES
    lane = lax.broadcasted_iota(jnp.int32, (1, LANES), 1)
    low = lane < (LANES // 2)
    row = lax.broadcasted_iota(jnp.int32, (t, t), 0)
    col = lax.broadcasted_iota(jnp.int32, (t, t), 1)
    if mode == "A":
        keep = ((past + col) >> CHUNK_SHIFT) <= ((past + row) >> CHUNK_SHIFT)
        lam = _lambda(lam_ref, lam_init)
    else:
        keep = col <= row
        r = lax.broadcasted_iota(jnp.int32, (past, past), 0)
        c = lax.broadcasted_iota(jnp.int32, (past, past), 1)
        upper = jnp.where(r <= c, 1.0, 0.0).astype(BF16)
        clf = clf_ref[0]
        hi, mid, lo = _split3(clf)
        cum_c = (_dot(hi, upper) + _dot(mid, upper)) + _dot(lo, upper)
        upper_n = jnp.where(row <= col, 1.0, 0.0).astype(BF16)
        hi, mid, lo = _split3(nlf_ref[0])
        cum_n = ((_dot(hi, upper_n) + _dot(mid, upper_n)) + _dot(lo, upper_n)
                 + jnp.sum(clf, axis=1, keepdims=True))
        bias_c = cum_c * (-LOG2E)
        bias_n = cum_n * (-LOG2E)
    for g in range(groups):
        sl = slice(g * LANES, (g + 1) * LANES)
        q = qkv_ref[0, :, sl]
        kn = qkv_ref[0, :, width + g * LANES:width + (g + 1) * LANES]
        vn = qkv_ref[0, :, 2 * width + g * LANES:2 * width + (g + 1) * LANES]
        kc = ck_ref[0, :, sl].astype(BF16)
        vc = cv_ref[0, :, sl].astype(BF16)
        outs = []
        for a in range(2):
            qa = jnp.where(low, q, jnp.zeros_like(q)) if a == 0 else jnp.where(low, jnp.zeros_like(q), q)
            s_c = _dot_nt(qa, kc)
            s_n = _dot_nt(qa, kn)
            if mode == "C":
                h = 2 * g + a
                s_c = s_c + bias_c[h:h + 1, :]
                s_n = s_n + bias_n[h:h + 1, :]
            s_n = jnp.where(keep, s_n, NEG)
            m = jnp.maximum(jnp.max(s_c, axis=1, keepdims=True), jnp.max(s_n, axis=1, keepdims=True))
            p_c = jnp.exp2(s_c - m)
            p_n = jnp.exp2(s_n - m)
            l = jnp.sum(p_c, axis=1, keepdims=True) + jnp.sum(p_n, axis=1, keepdims=True)
            acc = _dot(p_c.astype(BF16), vc) + _dot(p_n.astype(BF16), vn)
            outs.append(acc * (1.0 / l))
        if mode == "A":
            o = _diff_finish(outs[0], outs[1], lam, gain_ref[g], lam_init)
        else:
            o = jnp.where(low, outs[0], outs[1])
        o_ref[0, :, sl] = o.astype(o_ref.dtype)


def _cached_attention(qkv, cache_k, cache_v, mode, lam=None, gain=None, cache_lft=None, new_lft=None, lam_init=0.0):
    b, t, w3 = qkv.shape
    width = w3 // 3
    past = cache_k.shape[1]
    in_specs = [pl.BlockSpec((1, t, w3), lambda bi: (bi, 0, 0)),
                pl.BlockSpec((1, past, width), lambda bi: (bi, 0, 0)),
                pl.BlockSpec((1, past, width), lambda bi: (bi, 0, 0))]
    args = [qkv, cache_k, cache_v]
    if mode == "A":
        in_specs += [pl.BlockSpec((4, A_HEAD_DIM), lambda bi: (0, 0)),
                     pl.BlockSpec((A_HEADS, 1, LANES), lambda bi: (0, 0, 0))]
        args += [lam, gain]
    else:
        in_specs += [pl.BlockSpec((1, C_HEADS, past), lambda bi: (bi, 0, 0)),
                     pl.BlockSpec((1, C_HEADS, t), lambda bi: (bi, 0, 0))]
        args += [cache_lft, new_lft]
    return pl.pallas_call(
        functools.partial(_cached_attn_kernel, mode=mode, past=past, lam_init=lam_init),
        grid=(b,),
        in_specs=in_specs,
        out_specs=pl.BlockSpec((1, t, width), lambda bi: (bi, 0, 0)),
        out_shape=jax.ShapeDtypeStruct((b, t, width), BF16),
        compiler_params=_params(("arbitrary",)),
        name="cached_attn_" + mode,
    )(*args)


def _gelu_tanh(x):
    return 0.5 * x * (1.0 + jnp.tanh(math.sqrt(2.0 / math.pi) * (x + 0.044715 * (x * x * x))))


def _causal_conv(x, tail, w_ref, b_ref):
    width = w_ref.shape[0]
    tm = x.shape[0]
    cat = jnp.concatenate([tail, x], axis=0)
    y = None
    for j in range(width):
        back = width - 1 - j
        src = cat if back == 0 else pltpu.roll(cat, back, 0)
        term = src[SUBLANES:SUBLANES + tm] * w_ref[j:j + 1, :]
        y = term if y is None else y + term
    return y + b_ref[...]


def _rglru_kernel(xg_ref, h0_ref, cb_ref, cw_ref, cbias_ref, wa_ref, ba_ref, wx_ref, bx_ref, sp_ref,
                  y_ref, hl_ref, ct_ref, hc_ref, tail_ref):
    i = pl.program_id(1)
    tm = xg_ref.shape[1]

    @pl.when(i == 0)
    def _():
        hc_ref[...] = h0_ref[0]
        tail_ref[...] = cb_ref[0]

    x = xg_ref[0, :, :B_WIDTH]
    gate_in = xg_ref[0, :, B_WIDTH:]
    xc = _causal_conv(x, tail_ref[...], cw_ref, cbias_ref)
    xcb = xc.astype(BF16)
    r = _sigmoid(_dot(xcb, wa_ref[...]) + ba_ref[...])
    ig = _sigmoid(_dot(xcb, wx_ref[...]) + bx_ref[...])
    log_a = (-RG_C) * r * sp_ref[...]
    a = jnp.exp(log_a)
    th = jnp.tanh(log_a)
    bx = jnp.sqrt((-2.0 * th) / (1.0 - th)) * (ig * xc)
    row = lax.broadcasted_iota(jnp.int32, (tm, 1), 0)
    s = 1
    while s < tm:
        valid = row >= s
        a_sh = pltpu.roll(a, s, 0)
        b_sh = pltpu.roll(bx, s, 0)
        bx = jnp.where(valid, a * b_sh + bx, bx)
        a = jnp.where(valid, a * a_sh, a)
        s *= 2
    h = a * hc_ref[...] + bx
    y_ref[0] = (h * _gelu_tanh(gate_in)).astype(y_ref.dtype)
    h_tail = h[tm - SUBLANES:, :]
    x_tail = x[tm - SUBLANES:, :]
    hc_ref[...] = h_tail[SUBLANES - 1:, :]
    tail_ref[...] = x_tail
    hl_ref[0] = h_tail
    ct_ref[0] = x_tail


def _rglru(xg, h0, conv_tail, cw, cbias, wa, ba, wx, bx, sp, tm_pref=256):
    b, t, _ = xg.shape
    tm = _row_tile(t, tm_pref)
    wspec = lambda shape: pl.BlockSpec(shape, lambda bi, i: tuple(0 for _ in shape))
    return pl.pallas_call(
        _rglru_kernel,
        grid=(b, t // tm),
        in_specs=[pl.BlockSpec((1, tm, 2 * B_WIDTH), lambda bi, i: (bi, i, 0)),
                  pl.BlockSpec((1, 1, B_WIDTH), lambda bi, i: (bi, 0, 0)),
                  pl.BlockSpec((1, SUBLANES, B_WIDTH), lambda bi, i: (bi, 0, 0)),
                  wspec((B_CONV, B_WIDTH)), wspec((1, B_WIDTH)),
                  wspec((B_WIDTH, B_WIDTH)), wspec((1, B_WIDTH)),
                  wspec((B_WIDTH, B_WIDTH)), wspec((1, B_WIDTH)), wspec((1, B_WIDTH))],
        out_specs=[pl.BlockSpec((1, tm, B_WIDTH), lambda bi, i: (bi, i, 0)),
                   pl.BlockSpec((1, SUBLANES, B_WIDTH), lambda bi, i: (bi, 0, 0)),
                   pl.BlockSpec((1, SUBLANES, B_WIDTH), lambda bi, i: (bi, 0, 0))],
        out_shape=[jax.ShapeDtypeStruct((b, t, B_WIDTH), BF16),
                   jax.ShapeDtypeStruct((b, SUBLANES, B_WIDTH), F32),
                   jax.ShapeDtypeStruct((b, SUBLANES, B_WIDTH), F32)],
        scratch_shapes=[pltpu.VMEM((1, B_WIDTH), F32), pltpu.VMEM((SUBLANES, B_WIDTH), F32)],
        compiler_params=_params(("arbitrary", "arbitrary")),
        name="rglru",
    )(xg, h0, conv_tail, cw, cbias, wa, ba, wx, bx, sp)


def _proj_ln_kernel(*refs, n_in):
    h_refs = refs[:n_in]
    w_ref, x_ref, gate_ref, g_ref, b_ref, o_ref = refs[n_in:]
    hs = [r[0] for r in h_refs]
    h = hs[0] if n_in == 1 else jnp.concatenate(hs, axis=-1)
    y = ALPHA * x_ref[0] + gate_ref[0] * _dot(h, w_ref[...])
    mu = jnp.mean(y, axis=-1, keepdims=True)
    yc = y - mu
    var = jnp.mean(yc * yc, axis=-1, keepdims=True)
    o_ref[0] = yc * lax.rsqrt(var + LN_EPS) * g_ref[...] + b_ref[...]


def _proj_ln(hs, w, x, gate, ln_g, ln_b, tm_pref=512):
    b, t, d = x.shape
    tm = _row_tile(t, tm_pref)
    k = w.shape[0]
    in_specs = [pl.BlockSpec((1, tm, h.shape[2]), lambda bi, i: (bi, i, 0)) for h in hs]
    in_specs += [pl.BlockSpec((k, d), lambda bi, i: (0, 0)),
                 pl.BlockSpec((1, tm, d), lambda bi, i: (bi, i, 0)),
                 pl.BlockSpec((1, 1, d), lambda bi, i: (bi, 0, 0)),
                 pl.BlockSpec((1, d), lambda bi, i: (0, 0)),
                 pl.BlockSpec((1, d), lambda bi, i: (0, 0))]
    return pl.pallas_call(
        functools.partial(_proj_ln_kernel, n_in=len(hs)),
        grid=(b, t // tm),
        in_specs=in_specs,
        out_specs=pl.BlockSpec((1, tm, d), lambda bi, i: (bi, i, 0)),
        out_shape=jax.ShapeDtypeStruct((b, t, d), F32),
        compiler_params=_params(("arbitrary", "arbitrary")),
        name="proj_ln",
    )(*hs, w, x, gate, ln_g, ln_b)


def _ffn_up_kernel(x_ref, sc_ref, sh_ref, w_ref, cb_ref, cw_ref, cbias_ref, h_ref, ct_ref, tail_ref):
    i = pl.program_id(1)
    tm = x_ref.shape[1]

    @pl.when(i == 0)
    def _():
        tail_ref[...] = cb_ref[0]

    u = _modulate(x_ref[0], sc_ref[0], sh_ref[0])
    up = _dot(u, w_ref[...])
    a = up[:, :D_FF]
    g = up[:, D_FF:]
    gc = _causal_conv(g, tail_ref[...], cw_ref, cbias_ref)
    h_ref[0] = (a * (gc * _sigmoid(gc))).astype(h_ref.dtype)
    g_tail = g[tm - SUBLANES:, :]
    tail_ref[...] = g_tail
    ct_ref[0] = g_tail


def _ffn_up(x, sc, sh, w, conv_tail, cw, cbias, tm_pref=256):
    b, t, d = x.shape
    tm = _row_tile(t, tm_pref)
    return pl.pallas_call(
        _ffn_up_kernel,
        grid=(b, t // tm),
        in_specs=[pl.BlockSpec((1, tm, d), lambda bi, i: (bi, i, 0)),
                  pl.BlockSpec((1, 1, d), lambda bi, i: (bi, 0, 0)),
                  pl.BlockSpec((1, 1, d), lambda bi, i: (bi, 0, 0)),
                  pl.BlockSpec((d, 2 * D_FF), lambda bi, i: (0, 0)),
                  pl.BlockSpec((1, SUBLANES, D_FF), lambda bi, i: (bi, 0, 0)),
                  pl.BlockSpec((FFN_CONV, D_FF), lambda bi, i: (0, 0)),
                  pl.BlockSpec((1, D_FF), lambda bi, i: (0, 0))],
        out_specs=[pl.BlockSpec((1, tm, D_FF), lambda bi, i: (bi, i, 0)),
                   pl.BlockSpec((1, SUBLANES, D_FF), lambda bi, i: (bi, 0, 0))],
        out_shape=[jax.ShapeDtypeStruct((b, t, D_FF), BF16),
                   jax.ShapeDtypeStruct((b, SUBLANES, D_FF), F32)],
        scratch_shapes=[pltpu.VMEM((SUBLANES, D_FF), F32)],
        compiler_params=_params(("arbitrary", "arbitrary")),
        name="ffn_up",
    )(x, sc, sh, w, conv_tail, cw, cbias)


def _pad_tail(buf):
    return jnp.pad(buf, ((0, 0), (SUBLANES - buf.shape[1], 0), (0, 0)))


def _rope_tables(past, t):
    half = A_HEAD_DIM // 2
    inv = ROPE_THETA ** (-jnp.arange(0, A_HEAD_DIM, 2, dtype=F32) / A_HEAD_DIM)
    pos = (past + jnp.arange(t, dtype=jnp.int32)).astype(F32)
    ang = pos[:, None] * inv[None, :]
    cos = jnp.tile(jnp.cos(ang), (1, LANES // half))
    sin = jnp.sin(ang)
    sin_signed = jnp.tile(jnp.concatenate([-sin, sin], axis=1), (1, LANES // A_HEAD_DIM))
    return cos, sin_signed


def _block_diag(w):
    n, i, o = w.shape
    return jnp.einsum("nio,nm->nimo", w, jnp.eye(n, dtype=w.dtype)).reshape(n * i, n * o)


def _prepare(p):
    w = {}
    w["in_ab"] = p["w_in_ab"][0].astype(BF16)
    w["out_ab"] = p["w_out_ab"][0].astype(BF16)
    w["lam"] = jnp.stack([p["lam_q1"][0], p["lam_k1"][0], p["lam_q2"][0], p["lam_k2"][0]])
    w["gain"] = p["attn_gain"][0].reshape(A_HEADS, 1, LANES)
    w["b_conv_w"] = p["b_conv_w"][0]
    w["b_conv_b"] = p["b_conv_b"][0].reshape(1, B_WIDTH)
    w["rg_a"] = _block_diag(p["w_rg_a"][0]).astype(BF16)
    w["rg_x"] = _block_diag(p["w_rg_x"][0]).astype(BF16)
    w["b_rg_a"] = p["b_rg_a"][0].reshape(1, B_WIDTH)
    w["b_rg_x"] = p["b_rg_x"][0].reshape(1, B_WIDTH)
    w["rg_L"] = p["rg_L"][0].reshape(1, B_WIDTH)
    w_in_c = p["w_in_c"][0]
    w["in_c"] = jnp.pad(w_in_c, ((0, 0), (0, LANES - C_HEADS))).astype(BF16)
    w["in_c_ft"] = w_in_c[:, 3 * C_WIDTH:].T.astype(BF16)
    w["bf_row"] = jnp.pad(p["b_f"][0], (0, LANES - C_HEADS)).reshape(1, LANES)
    w["bf_col"] = p["b_f"][0].reshape(C_HEADS, 1)
    w["out_c"] = p["w_out_c"][0].astype(BF16)
    w["up"] = [p["w_up"][i].astype(BF16) for i in range(DEPTH)]
    w["down"] = [p["w_down"][i].astype(BF16) for i in range(DEPTH)]
    w["ffn_conv_w"] = [p["ffn_conv_w"][i] for i in range(DEPTH)]
    w["ffn_conv_b"] = [p["ffn_conv_b"][i].reshape(1, D_FF) for i in range(DEPTH)]
    w["ln1_g"] = [p["ln1_g"][i].reshape(1, D_MODEL) for i in range(DEPTH)]
    w["ln1_b"] = [p["ln1_b"][i].reshape(1, D_MODEL) for i in range(DEPTH)]
    w["ln2_g"] = [p["ln2_g"][i].reshape(1, D_MODEL) for i in range(DEPTH)]
    w["ln2_b"] = [p["ln2_b"][i].reshape(1, D_MODEL) for i in range(DEPTH)]
    return w


def _softplus_kernel(x_ref, o_ref):
    o_ref[...] = _softplus(-x_ref[...])


def _trunk(x, mods, w, sp, cache_a_k=None, cache_a_v=None, state_b_h=None, state_b_conv=None,
           cache_c_k=None, cache_c_v=None, cache_c_logf=None, state_ffn_conv=None):
    b, t, d = x.shape
    cached = cache_a_k is not None
    past = cache_a_k.shape[2] if cached else 0
    tq = _row_tile(t, 512)
    outs = {}
    for i in range(DEPTH):
        sh1, sc1, g1, sh2, sc2, g2 = [m[:, None, :] for m in jnp.split(mods[i], 6, axis=-1)]
        if i % 2 == 0:
            lam_init = 0.8 - 0.6 * math.exp(-0.3 * i)
            cos, sin = _rope_tables(past, t)
            qkv, k32, v32, xg = _proj_ab(x, sc1, sh1, w["in_ab"], cos, sin)
            if cached:
                o = _cached_attention(qkv, cache_a_k[0].reshape(b, past, A_WIDTH), cache_a_v[0].reshape(b, past, A_WIDTH),
                                      "A", lam=w["lam"], gain=w["gain"], lam_init=lam_init)
                h0 = state_b_h[0][:, None, :]
                ctail = _pad_tail(state_b_conv[0])
            else:
                o = _attention(qkv, "A", tq, lam=w["lam"], gain=w["gain"], lam_init=lam_init)
                h0 = jnp.zeros((b, 1, B_WIDTH), F32)
                ctail = jnp.zeros((b, SUBLANES, B_WIDTH), F32)
            yb, h_tail, x_tail = _rglru(xg, h0, ctail, w["b_conv_w"], w["b_conv_b"], w["rg_a"], w["b_rg_a"],
                                        w["rg_x"], w["b_rg_x"], sp)
            outs["a_k"] = k32.reshape(1, b, t, A_HEADS, 2 * A_HEAD_DIM)
            outs["a_v"] = v32.reshape(1, b, t, A_HEADS, 2 * A_HEAD_DIM)
            outs["b_h"] = h_tail[:, SUBLANES - 1, :][None]
            outs["b_conv"] = x_tail[:, SUBLANES - (B_CONV - 1):, :][None]
            x = _proj_ln([o, yb], w["out_ab"], x, g1, w["ln1_g"][i], w["ln1_b"][i])
        else:
            qkv, k32, v32, lf, lft, bias = _proj_c(x, sc1, sh1, w["in_c"], w["in_c_ft"], w["bf_row"], w["bf_col"])
            if cached:
                o = _cached_attention(qkv, cache_c_k[0].reshape(b, past, C_WIDTH), cache_c_v[0].reshape(b, past, C_WIDTH),
                                      "C", cache_lft=jnp.swapaxes(cache_c_logf[0], 1, 2), new_lft=lft)
            else:
                o = _attention(qkv, "C", tq, bias=bias)
            outs["c_k"] = k32.reshape(1, b, t, C_HEADS, C_HEAD_DIM)
            outs["c_v"] = v32.reshape(1, b, t, C_HEADS, C_HEAD_DIM)
            outs["c_logf"] = lf[None]
            x = _proj_ln([o], w["out_c"], x, g1, w["ln1_g"][i], w["ln1_b"][i])
        ftail = _pad_tail(state_ffn_conv[i]) if cached else jnp.zeros((b, SUBLANES, D_FF), F32)
        hmid, g_tail = _ffn_up(x, sc2, sh2, w["up"][i], ftail, w["ffn_conv_w"][i], w["ffn_conv_b"][i])
        outs.setdefault("ffn", []).append(g_tail[:, SUBLANES - (FFN_CONV - 1):, :])
        x = _proj_ln([hmid], w["down"][i], x, g2, w["ln2_g"][i], w["ln2_b"][i])
    return (x, outs["a_k"], outs["a_v"], outs["b_h"], outs["b_conv"],
            outs["c_k"], outs["c_v"], outs["c_logf"], jnp.stack(outs["ffn"]))


def kernel(x_prompt, x_sample, c_prompt, c_sample, cache_a_k, cache_a_v, state_b_h, state_b_conv, cache_c_k, cache_c_v, cache_c_logf, state_ffn_conv, w_ada, b_ada, ln1_g, ln1_b, ln2_g, ln2_b, w_in_ab, lam_q1, lam_k1, lam_q2, lam_k2, attn_gain, b_conv_w, b_conv_b, w_rg_a, b_rg_a, w_rg_x, b_rg_x, rg_L, w_out_ab, w_in_c, b_f, w_out_c, w_up, ffn_conv_w, ffn_conv_b, w_down):
    p = dict(w_in_ab=w_in_ab, lam_q1=lam_q1, lam_k1=lam_k1, lam_q2=lam_q2, lam_k2=lam_k2, attn_gain=attn_gain,
             b_conv_w=b_conv_w, b_conv_b=b_conv_b, w_rg_a=w_rg_a, b_rg_a=b_rg_a, w_rg_x=w_rg_x, b_rg_x=b_rg_x,
             rg_L=rg_L, w_out_ab=w_out_ab, w_in_c=w_in_c, b_f=b_f, w_out_c=w_out_c, w_up=w_up,
             ffn_conv_w=ffn_conv_w, ffn_conv_b=ffn_conv_b, w_down=w_down,
             ln1_g=ln1_g, ln1_b=ln1_b, ln2_g=ln2_g, ln2_b=ln2_b)
    w = _prepare(p)
    bp = c_prompt.shape[0]
    bs = c_sample.shape[0]
    rows = -(-(bp + bs) // 16) * 16
    c_all = jnp.pad(jnp.concatenate([c_prompt, c_sample], axis=0), ((0, rows - bp - bs), (0, 0)))
    mods = _mods(c_all, w_ada, b_ada)
    sp = pl.pallas_call(_softplus_kernel, out_shape=jax.ShapeDtypeStruct((1, B_WIDTH), F32),
                        name="softplus")(w["rg_L"])
    res_p = _trunk(x_prompt, mods[:, :bp], w, sp)
    res_s = _trunk(x_sample, mods[:, bp:bp + bs], w, sp, cache_a_k, cache_a_v, state_b_h, state_b_conv,
                   cache_c_k, cache_c_v, cache_c_logf, state_ffn_conv)
    return (res_p[0], res_s[0]) + res_p[1:] + res_s[1:]
```

```python
import functools
import math

import numpy as np
import jax
import jax.numpy as jnp
from jax import lax
from jax.experimental import pallas as pl
from jax.experimental.pallas import tpu as pltpu

F32 = jnp.float32
BF16 = jnp.bfloat16

D_MODEL = 1024
DEPTH = 2
CHUNK = 64
CHUNK_SHIFT = 6
A_HEADS = 4
A_HEAD_DIM = 64
A_WIDTH = A_HEADS * 2 * A_HEAD_DIM
B_WIDTH = 512
B_BLOCKS = 8
B_CONV = 4
RG_C = 8.0
C_HEADS = 16
C_HEAD_DIM = 64
C_WIDTH = C_HEADS * C_HEAD_DIM
D_FF = 2816
FFN_CONV = 3
ROPE_THETA = 10000.0
ALPHA = (2 * DEPTH) ** 0.25
LN_EPS = 1e-5
NEG = -1e30
LOG2E = 1.4426950408889634

LANES = 128
SUBLANES = 8
BF16_ROWS = 16
MXU_DIM = 256
VMEM_LIMIT = 56 * 1024 * 1024
BIAS_WIDTH = (C_HEADS // 2) * LANES
SCORE_SKEW = 3
ATTN_BLOCK = 512


def _params(sem, flags=None):
    return pltpu.CompilerParams(dimension_semantics=sem, vmem_limit_bytes=VMEM_LIMIT, flags=flags)


def _row_tile(t, pref):
    if t <= pref:
        return t
    tm = pref
    while t % tm:
        tm //= 2
    return tm


def _modulate(x, sc, sh):
    return (x * (1.0 + sc) + sh).astype(BF16)


def _sigmoid(x):
    return 1.0 / (1.0 + jnp.exp(-x))


def _softplus(x):
    return jnp.maximum(x, 0.0) + jnp.log1p(jnp.exp(-jnp.abs(x)))


def _log_sigmoid(x):
    return jnp.minimum(x, 0.0) - jnp.log1p(jnp.exp(-jnp.abs(x)))


def _split3(x):
    hi = x.astype(BF16)
    r1 = x - hi.astype(F32)
    mid = r1.astype(BF16)
    lo = (r1 - mid.astype(F32)).astype(BF16)
    return hi, mid, lo


def _dot(a, b):
    return jnp.dot(a, b, preferred_element_type=F32)


def _dot_nt(a, b):
    return lax.dot_general(a, b, (((1,), (1,)), ((), ())), preferred_element_type=F32)


def _mods_kernel(c_ref, w_ref, b_ref, o_ref):
    c = c_ref[...]
    s = (c * _sigmoid(c)).astype(BF16)
    o_ref[0] = _dot(s, w_ref[0].astype(BF16)) + b_ref[0]


def _mods(c_all, w_ada, b_ada):
    rows, d = c_all.shape
    n = w_ada.shape[-1]
    tn = 1536
    return pl.pallas_call(
        _mods_kernel,
        grid=(DEPTH, n // tn),
        in_specs=[pl.BlockSpec((rows, d), lambda l, j: (0, 0)),
                  pl.BlockSpec((1, d, tn), lambda l, j: (l, 0, j)),
                  pl.BlockSpec((1, 1, tn), lambda l, j: (l, 0, j))],
        out_specs=pl.BlockSpec((1, rows, tn), lambda l, j: (l, 0, j)),
        out_shape=jax.ShapeDtypeStruct((DEPTH, rows, n), F32),
        compiler_params=_params(("arbitrary", "arbitrary")),
        name="mods",
    )(c_all, w_ada, b_ada.reshape(DEPTH, 1, n))


def _rope_slab(x, cos, sin_signed, first_half):
    fwd = pltpu.roll(x, LANES - A_HEAD_DIM // 2, 1)
    bwd = pltpu.roll(x, A_HEAD_DIM // 2, 1)
    partner = jnp.where(first_half, fwd, bwd)
    return x * cos + partner * sin_signed


def _proj_ab_kernel(x_ref, sc_ref, sh_ref, w_ref, wvt_ref, cos_ref, sin_ref,
                    q_ref, kb_ref, vt_ref, k_ref, v_ref, xg_ref, *, q_scale):
    u = _modulate(x_ref[0], sc_ref[0], sh_ref[0])
    pr = _dot(u, w_ref[...])
    cos = cos_ref[...]
    sin = sin_ref[...]
    lane = lax.broadcasted_iota(jnp.int32, (1, LANES), 1)
    first_half = (lane & (A_HEAD_DIM - 1)) < (A_HEAD_DIM // 2)
    for h in range(A_HEADS):
        sl = slice(h * LANES, (h + 1) * LANES)
        q = _rope_slab(pr[:, sl], cos, sin, first_half)
        q_ref[0, :, sl] = (q * q_scale).astype(BF16)
        k = _rope_slab(pr[:, A_WIDTH + h * LANES:A_WIDTH + (h + 1) * LANES], cos, sin, first_half)
        k_ref[0, :, sl] = k
        kb_ref[0, :, sl] = k.astype(BF16)
    v_ref[0] = pr[:, 2 * A_WIDTH:3 * A_WIDTH]
    vt_ref[0, 0] = _dot_nt(wvt_ref[...], u).astype(BF16)
    xg_ref[0] = pr[:, 3 * A_WIDTH:]


def _proj_ab(x, sc, sh, w, wvt, cos, sin):
    b, t, d = x.shape
    tm = _row_tile(t, ATTN_BLOCK)
    nt = t // tm
    n = w.shape[1]
    q_scale = A_HEAD_DIM ** -0.5 * LOG2E
    row = lambda width: pl.BlockSpec((1, tm, width), lambda bi, i: (bi, i, 0))
    const = lambda shape: pl.BlockSpec(shape, lambda bi, i: tuple(0 for _ in shape))
    return pl.pallas_call(
        functools.partial(_proj_ab_kernel, q_scale=q_scale),
        grid=(b, nt),
        in_specs=[row(d),
                  pl.BlockSpec((1, 1, d), lambda bi, i: (bi, 0, 0)),
                  pl.BlockSpec((1, 1, d), lambda bi, i: (bi, 0, 0)),
                  const((d, n)), const((A_WIDTH, d)),
                  pl.BlockSpec((tm, LANES), lambda bi, i: (i, 0)),
                  pl.BlockSpec((tm, LANES), lambda bi, i: (i, 0))],
        out_specs=[row(A_WIDTH), row(A_WIDTH),
                   pl.BlockSpec((1, 1, A_WIDTH, tm), lambda bi, i: (bi, i, 0, 0)),
                   row(A_WIDTH), row(A_WIDTH), row(2 * B_WIDTH)],
        out_shape=[jax.ShapeDtypeStruct((b, t, A_WIDTH), BF16),
                   jax.ShapeDtypeStruct((b, t, A_WIDTH), BF16),
                   jax.ShapeDtypeStruct((b, nt, A_WIDTH, tm), BF16),
                   jax.ShapeDtypeStruct((b, t, A_WIDTH), F32),
                   jax.ShapeDtypeStruct((b, t, A_WIDTH), F32),
                   jax.ShapeDtypeStruct((b, t, 2 * B_WIDTH), F32)],
        compiler_params=_params(("arbitrary", "arbitrary")),
        name="proj_ab",
    )(x, sc, sh, w, wvt, cos, sin)


def _bias_placement():
    e = np.zeros((3 * LANES, BIAS_WIDTH), np.float32)
    for piece in range(3):
        for h in range(C_HEADS):
            e[piece * LANES + h, (h // 2) * LANES + 3 * (h % 2) + piece] = 1.0
    return jnp.asarray(e, BF16)


def _proj_c_kernel(x_ref, sc_ref, sh_ref, w_ref, wvt_ref, wft_ref, bfr_ref, bfc_ref, place_ref,
                   q_ref, kb_ref, bias_ref, vt_ref, k_ref, v_ref, lf_ref, lft_ref, run_ref, *, q_scale):
    i = pl.program_id(1)
    tm = x_ref.shape[1]

    @pl.when(i == 0)
    def _():
        run_ref[...] = jnp.zeros_like(run_ref)

    u = _modulate(x_ref[0], sc_ref[0], sh_ref[0])
    pr = _dot(u, w_ref[...])
    q_ref[0] = (pr[:, :C_WIDTH] * q_scale).astype(BF16)
    k = pr[:, C_WIDTH:2 * C_WIDTH]
    k_ref[0] = k
    kb_ref[0] = k.astype(BF16)
    v_ref[0] = pr[:, 2 * C_WIDTH:3 * C_WIDTH]
    vt_ref[0, 0] = _dot_nt(wvt_ref[...], u).astype(BF16)
    lf = _log_sigmoid(pr[:, 3 * C_WIDTH:] + bfr_ref[...])
    lf_ref[0] = lf[:, :C_HEADS]
    lft_ref[0] = _log_sigmoid(_dot_nt(wft_ref[...], u) + bfc_ref[...])
    r = lax.broadcasted_iota(jnp.int32, (tm, tm), 0)
    c = lax.broadcasted_iota(jnp.int32, (tm, tm), 1)
    lower = jnp.where(c <= r, 1.0, 0.0).astype(BF16)
    hi, mid, lo = _split3(lf)
    cum = (_dot(lower, hi) + _dot(lower, mid)) + _dot(lower, lo) + run_ref[...]
    pieces = jnp.concatenate(_split3(cum * (-LOG2E)), axis=1)
    bias_ref[0] = _dot(pieces, place_ref[...]).astype(BF16)
    run_ref[...] = run_ref[...] + jnp.sum(lf, axis=0, keepdims=True)


def _proj_c(x, sc, sh, w, wvt, wft, bf_row, bf_col, place):
    b, t, d = x.shape
    tm = _row_tile(t, ATTN_BLOCK)
    nt = t // tm
    n = w.shape[1]
    q_scale = C_HEAD_DIM ** -0.5 * LOG2E
    row = lambda width: pl.BlockSpec((1, tm, width), lambda bi, i: (bi, i, 0))
    const = lambda shape: pl.BlockSpec(shape, lambda bi, i: tuple(0 for _ in shape))
    return pl.pallas_call(
        functools.partial(_proj_c_kernel, q_scale=q_scale),
        grid=(b, nt),
        in_specs=[row(d),
                  pl.BlockSpec((1, 1, d), lambda bi, i: (bi, 0, 0)),
                  pl.BlockSpec((1, 1, d), lambda bi, i: (bi, 0, 0)),
                  const((d, n)), const((C_WIDTH, d)), const((C_HEADS, d)),
                  const((1, LANES)), const((C_HEADS, 1)), const((3 * LANES, BIAS_WIDTH))],
        out_specs=[row(C_WIDTH), row(C_WIDTH), row(BIAS_WIDTH),
                   pl.BlockSpec((1, 1, C_WIDTH, tm), lambda bi, i: (bi, i, 0, 0)),
                   row(C_WIDTH), row(C_WIDTH), row(C_HEADS),
                   pl.BlockSpec((1, C_HEADS, tm), lambda bi, i: (bi, 0, i))],
        out_shape=[jax.ShapeDtypeStruct((b, t, C_WIDTH), BF16),
                   jax.ShapeDtypeStruct((b, t, C_WIDTH), BF16),
                   jax.ShapeDtypeStruct((b, t, BIAS_WIDTH), BF16),
                   jax.ShapeDtypeStruct((b, nt, C_WIDTH, tm), BF16),
                   jax.ShapeDtypeStruct((b, t, C_WIDTH), F32),
                   jax.ShapeDtypeStruct((b, t, C_WIDTH), F32),
                   jax.ShapeDtypeStruct((b, t, C_HEADS), F32),
                   jax.ShapeDtypeStruct((b, C_HEADS, t), F32)],
        scratch_shapes=[pltpu.VMEM((1, LANES), F32)],
        compiler_params=_params(("arbitrary", "arbitrary")),
        name="proj_c",
    )(x, sc, sh, w, wvt, wft, bf_row, bf_col, place)


def _lambda(lam_ref, lam_init):
    lq1, lk1, lq2, lk2 = (lam_ref[r:r + 1, :] for r in range(4))
    return (jnp.exp(jnp.sum(lq1 * lk1, axis=1, keepdims=True))
            - jnp.exp(jnp.sum(lq2 * lk2, axis=1, keepdims=True)) + lam_init)


def _attn_kernel(*refs, mode, lam_init):
    if mode == "A":
        q_ref, k_ref, vt_ref, lam_ref, gain_ref, o_ref, qc_ref, m_ref, acc_ref = refs
        bias_ref = None
        d_val = 2 * A_HEAD_DIM
    else:
        q_ref, k_ref, bias_ref, vt_ref, o_ref, qc_ref, m_ref, acc_ref = refs
        d_val = C_HEAD_DIM
    tq = q_ref.shape[1]
    tk = k_ref.shape[2]
    qi = pl.program_id(2)
    lane = lax.broadcasted_iota(jnp.int32, (1, LANES), 1)
    low = lane < (LANES // 2)
    q = q_ref[0]
    zero = jnp.zeros_like(q)
    for a in range(2):
        qa = jnp.where(low, q, zero) if a == 0 else jnp.where(low, zero, q)
        if bias_ref is not None:
            pick = jnp.where((lane >= 3 * a) & (lane < 3 * a + 3), 1.0, 0.0).astype(BF16)
            qa = jnp.concatenate([qa, jnp.broadcast_to(pick, (tq, LANES))], axis=1)
        qc_ref[a] = qa
    m_ref[...] = jnp.full(m_ref.shape, NEG, F32)
    acc_ref[...] = jnp.zeros(acc_ref.shape, F32)
    ones = jnp.ones((BF16_ROWS, tk), BF16)

    def step(j, masked):
        kc = k_ref[0, j]
        if bias_ref is not None:
            kc = jnp.concatenate([kc, bias_ref[0, j]], axis=1)
        vt = vt_ref[0, j]
        vas = []
        for a in range(2):
            va = vt if mode == "A" else vt[a * d_val:(a + 1) * d_val]
            vas.append(jnp.concatenate([va, ones], axis=0))
        units = [(a, slice(n * MXU_DIM, (n + 1) * MXU_DIM)) for a in range(2) for n in range(tq // MXU_DIM)]

        def scores(a, cs):
            return _dot_nt(kc, qc_ref[a, cs, :])

        def update(a, cs, st):
            if masked:
                key = lax.broadcasted_iota(jnp.int32, (tk, MXU_DIM), 0)
                qry = lax.broadcasted_iota(jnp.int32, (tk, MXU_DIM), 1) + cs.start
                keep = ((key >> CHUNK_SHIFT) <= (qry >> CHUNK_SHIFT)) if mode == "A" else (key <= qry)
                st = jnp.where(keep, st, NEG)
            m_prev = m_ref[a, :, cs]
            m_new = jnp.maximum(m_prev, jnp.max(st, axis=0, keepdims=True))
            alpha = jnp.exp2(m_prev - m_new)
            pt = jnp.exp2(st - m_new).astype(BF16)
            return alpha * acc_ref[a, :, cs] + _dot(vas[a], pt), m_new

        pending, updates = {}, []
        for t in range(len(units) + SCORE_SKEW):
            if t < len(units):
                pending[t] = scores(*units[t])
            u = t - SCORE_SKEW
            if u >= 0:
                updates.append(units[u] + update(*units[u], pending.pop(u)))
        for a, cs, acc_new, m_new in updates:
            acc_ref[a, :, cs] = acc_new
            m_ref[a, :, cs] = m_new

    def body(j, carry):
        step(j, False)
        return carry

    lax.fori_loop(0, qi, body, 0)
    step(qi, True)

    outs = []
    for a in range(2):
        acc = acc_ref[a]
        outs.append(acc[:d_val] * (1.0 / acc[d_val:d_val + 1]))
    if mode == "A":
        ot = outs[0] - _lambda(lam_ref, lam_init) * outs[1]
        ot = ot * lax.rsqrt(jnp.mean(ot * ot, axis=0, keepdims=True) + LN_EPS)
        ot = ot * (gain_ref[0] * (1.0 - lam_init))
    else:
        ot = jnp.concatenate(outs, axis=0)
    o_ref[0] = ot.T.astype(o_ref.dtype)


def _attention(q, k, vt, mode, lam=None, gain=None, bias=None, lam_init=0.0):
    b, t, width = q.shape
    nk, tk = vt.shape[1], vt.shape[3]
    tq = tk
    groups = width // LANES
    d_aug = (2 * A_HEAD_DIM if mode == "A" else C_HEAD_DIM) + BF16_ROWS
    kspec = pl.BlockSpec((1, nk, tk, LANES), lambda bi, g, i: (bi, 0, 0, g))
    in_specs = [pl.BlockSpec((1, tq, LANES), lambda bi, g, i: (bi, i, g)), kspec]
    args = [q, k.reshape(b, nk, tk, width)]
    if mode == "C":
        in_specs.append(kspec)
        args.append(bias.reshape(b, nk, tk, groups * LANES))
    in_specs.append(pl.BlockSpec((1, nk, LANES, tk), lambda bi, g, i: (bi, 0, g, 0)))
    args.append(vt)
    if mode == "A":
        in_specs += [pl.BlockSpec((4, A_HEAD_DIM), lambda bi, g, i: (0, 0)),
                     pl.BlockSpec((1, LANES, 1), lambda bi, g, i: (g, 0, 0))]
        args += [lam, gain]
    return pl.pallas_call(
        functools.partial(_attn_kernel, mode=mode, lam_init=lam_init),
        grid=(b, groups, t // tq),
        in_specs=in_specs,
        out_specs=pl.BlockSpec((1, tq, LANES), lambda bi, g, i: (bi, i, g)),
        out_shape=jax.ShapeDtypeStruct((b, t, width), BF16),
        scratch_shapes=[pltpu.VMEM((2, tq, LANES if mode == "A" else 2 * LANES), BF16),
                        pltpu.VMEM((2, 1, tq), F32),
                        pltpu.VMEM((2, d_aug, tq), F32)],
        compiler_params=_params(("arbitrary", "arbitrary", "arbitrary")),
        name="attn_" + mode,
    )(*args)


def _diff_finish(o0, o1, lam, gain, lam_init):
    o = o0 - lam * o1
    o = o * lax.rsqrt(jnp.mean(o * o, axis=-1, keepdims=True) + LN_EPS)
    return o * gain * (1.0 - lam_init)


def _cached_attn_kernel(*refs, mode, past, lam_init):
    if mode == "A":
        q_ref, kn_ref, vn_ref, ck_ref, cv_ref, lam_ref, gain_ref, o_ref = refs
    else:
        q_ref, kn_ref, vn_ref, ck_ref, cv_ref, clf_ref, nlf_ref, o_ref = refs
    t = q_ref.shape[1]
    width = o_ref.shape[2]
    groups = width // LANES
    lane = lax.broadcasted_iota(jnp.int32, (1, LANES), 1)
    low = lane < (LANES // 2)
    row = lax.broadcasted_iota(jnp.int32, (t, t), 0)
    col = lax.broadcasted_iota(jnp.int32, (t, t), 1)
    if mode == "A":
        keep = ((past + col) >> CHUNK_SHIFT) <= ((past + row) >> CHUNK_SHIFT)
        lam = _lambda(lam_ref, lam_init)
    else:
        keep = col <= row
        r = lax.broadcasted_iota(jnp.int32, (past, past), 0)
        c = lax.broadcasted_iota(jnp.int32, (past, past), 1)
        upper = jnp.where(r <= c, 1.0, 0.0).astype(BF16)
        clf = clf_ref[0]
        hi, mid, lo = _split3(clf)
        cum_c = (_dot(hi, upper) + _dot(mid, upper)) + _dot(lo, upper)
        upper_n = jnp.where(row <= col, 1.0, 0.0).astype(BF16)
        hi, mid, lo = _split3(nlf_ref[0])
        cum_n = ((_dot(hi, upper_n) + _dot(mid, upper_n)) + _dot(lo, upper_n)
                 + jnp.sum(clf, axis=1, keepdims=True))
        bias_c = cum_c * (-LOG2E)
        bias_n = cum_n * (-LOG2E)
    for g in range(groups):
        sl = slice(g * LANES, (g + 1) * LANES)
        q = q_ref[0, :, sl]
        kn = kn_ref[0, :, sl]
        vn = vn_ref[0, :, sl].astype(BF16)
        kc = ck_ref[0, :, sl].astype(BF16)
        vc = cv_ref[0, :, sl].astype(BF16)
        outs = []
        for a in range(2):
            qa = jnp.where(low, q, jnp.zeros_like(q)) if a == 0 else jnp.where(low, jnp.zeros_like(q), q)
            s_c = _dot_nt(qa, kc)
            s_n = _dot_nt(qa, kn)
            if mode == "C":
                h = 2 * g + a
                s_c = s_c + bias_c[h:h + 1, :]
                s_n = s_n + bias_n[h:h + 1, :]
            s_n = jnp.where(keep, s_n, NEG)
            m = jnp.maximum(jnp.max(s_c, axis=1, keepdims=True), jnp.max(s_n, axis=1, keepdims=True))
            p_c = jnp.exp2(s_c - m)
            p_n = jnp.exp2(s_n - m)
            l = jnp.sum(p_c, axis=1, keepdims=True) + jnp.sum(p_n, axis=1, keepdims=True)
            acc = _dot(p_c.astype(BF16), vc) + _dot(p_n.astype(BF16), vn)
            outs.append(acc * (1.0 / l))
        if mode == "A":
            o = _diff_finish(outs[0], outs[1], lam, gain_ref[g], lam_init)
        else:
            o = jnp.where(low, outs[0], outs[1])
        o_ref[0, :, sl] = o.astype(o_ref.dtype)


def _cached_attention(q, kn, vn, cache_k, cache_v, mode, lam=None, gain=None, cache_lft=None, new_lft=None,
                      lam_init=0.0):
    b, t, width = q.shape
    past = cache_k.shape[1]
    new = pl.BlockSpec((1, t, width), lambda bi: (bi, 0, 0))
    old = pl.BlockSpec((1, past, width), lambda bi: (bi, 0, 0))
    in_specs = [new, new, new, old, old]
    args = [q, kn, vn, cache_k, cache_v]
    if mode == "A":
        in_specs += [pl.BlockSpec((4, A_HEAD_DIM), lambda bi: (0, 0)),
                     pl.BlockSpec((A_HEADS, 1, LANES), lambda bi: (0, 0, 0))]
        args += [lam, gain]
    else:
        in_specs += [pl.BlockSpec((1, C_HEADS, past), lambda bi: (bi, 0, 0)),
                     pl.BlockSpec((1, C_HEADS, t), lambda bi: (bi, 0, 0))]
        args += [cache_lft, new_lft]
    return pl.pallas_call(
        functools.partial(_cached_attn_kernel, mode=mode, past=past, lam_init=lam_init),
        grid=(b,),
        in_specs=in_specs,
        out_specs=new,
        out_shape=jax.ShapeDtypeStruct((b, t, width), BF16),
        compiler_params=_params(("arbitrary",)),
        name="cached_attn_" + mode,
    )(*args)


def _gelu_tanh(x):
    return 0.5 * x * (1.0 + jnp.tanh(math.sqrt(2.0 / math.pi) * (x + 0.044715 * (x * x * x))))


def _causal_conv(x, tail, w_ref, b_ref):
    width = w_ref.shape[0]
    tm = x.shape[0]
    cat = jnp.concatenate([tail, x], axis=0)
    y = None
    for j in range(width):
        back = width - 1 - j
        src = cat if back == 0 else pltpu.roll(cat, back, 0)
        term = src[SUBLANES:SUBLANES + tm] * w_ref[j:j + 1, :]
        y = term if y is None else y + term
    return y + b_ref[...]


def _rglru_kernel(xg_ref, h0_ref, cb_ref, cw_ref, cbias_ref, wa_ref, ba_ref, wx_ref, bx_ref, sp_ref,
                  y_ref, hl_ref, ct_ref, hc_ref, tail_ref):
    i = pl.program_id(1)
    tm = xg_ref.shape[1]

    @pl.when(i == 0)
    def _():
        hc_ref[...] = h0_ref[0]
        tail_ref[...] = cb_ref[0]

    x = xg_ref[0, :, :B_WIDTH]
    gate_in = xg_ref[0, :, B_WIDTH:]
    xc = _causal_conv(x, tail_ref[...], cw_ref, cbias_ref)
    xcb = xc.astype(BF16)
    r = _sigmoid(_dot(xcb, wa_ref[...]) + ba_ref[...])
    ig = _sigmoid(_dot(xcb, wx_ref[...]) + bx_ref[...])
    log_a = (-RG_C) * r * sp_ref[...]
    a = jnp.exp(log_a)
    th = jnp.tanh(log_a)
    bx = jnp.sqrt((-2.0 * th) / (1.0 - th)) * (ig * xc)
    row = lax.broadcasted_iota(jnp.int32, (tm, 1), 0)
    s = 1
    while s < tm:
        valid = row >= s
        a_sh = pltpu.roll(a, s, 0)
        b_sh = pltpu.roll(bx, s, 0)
        bx = jnp.where(valid, a * b_sh + bx, bx)
        a = jnp.where(valid, a * a_sh, a)
        s *= 2
    h = a * hc_ref[...] + bx
    y_ref[0] = (h * _gelu_tanh(gate_in)).astype(y_ref.dtype)
    h_tail = h[tm - SUBLANES:, :]
    x_tail = x[tm - SUBLANES:, :]
    hc_ref[...] = h_tail[SUBLANES - 1:, :]
    tail_ref[...] = x_tail
    hl_ref[0] = h_tail
    ct_ref[0] = x_tail


def _rglru(xg, h0, conv_tail, cw, cbias, wa, ba, wx, bx, sp, tm_pref=256):
    b, t, _ = xg.shape
    tm = _row_tile(t, tm_pref)
    wspec = lambda shape: pl.BlockSpec(shape, lambda bi, i: tuple(0 for _ in shape))
    return pl.pallas_call(
        _rglru_kernel,
        grid=(b, t // tm),
        in_specs=[pl.BlockSpec((1, tm, 2 * B_WIDTH), lambda bi, i: (bi, i, 0)),
                  pl.BlockSpec((1, 1, B_WIDTH), lambda bi, i: (bi, 0, 0)),
                  pl.BlockSpec((1, SUBLANES, B_WIDTH), lambda bi, i: (bi, 0, 0)),
                  wspec((B_CONV, B_WIDTH)), wspec((1, B_WIDTH)),
                  wspec((B_WIDTH, B_WIDTH)), wspec((1, B_WIDTH)),
                  wspec((B_WIDTH, B_WIDTH)), wspec((1, B_WIDTH)), wspec((1, B_WIDTH))],
        out_specs=[pl.BlockSpec((1, tm, B_WIDTH), lambda bi, i: (bi, i, 0)),
                   pl.BlockSpec((1, SUBLANES, B_WIDTH), lambda bi, i: (bi, 0, 0)),
                   pl.BlockSpec((1, SUBLANES, B_WIDTH), lambda bi, i: (bi, 0, 0))],
        out_shape=[jax.ShapeDtypeStruct((b, t, B_WIDTH), BF16),
                   jax.ShapeDtypeStruct((b, SUBLANES, B_WIDTH), F32),
                   jax.ShapeDtypeStruct((b, SUBLANES, B_WIDTH), F32)],
        scratch_shapes=[pltpu.VMEM((1, B_WIDTH), F32), pltpu.VMEM((SUBLANES, B_WIDTH), F32)],
        compiler_params=_params(("arbitrary", "arbitrary")),
        name="rglru",
    )(xg, h0, conv_tail, cw, cbias, wa, ba, wx, bx, sp)


def _proj_ln_kernel(*refs, n_in):
    h_refs = refs[:n_in]
    w_ref, x_ref, gate_ref, g_ref, b_ref, o_ref = refs[n_in:]
    hs = [r[0] for r in h_refs]
    h = hs[0] if n_in == 1 else jnp.concatenate(hs, axis=-1)
    y = ALPHA * x_ref[0] + gate_ref[0] * _dot(h, w_ref[...])
    mu = jnp.mean(y, axis=-1, keepdims=True)
    yc = y - mu
    var = jnp.mean(yc * yc, axis=-1, keepdims=True)
    o_ref[0] = yc * lax.rsqrt(var + LN_EPS) * g_ref[...] + b_ref[...]


def _proj_ln(hs, w, x, gate, ln_g, ln_b, tm_pref=512):
    b, t, d = x.shape
    tm = _row_tile(t, tm_pref)
    k = w.shape[0]
    in_specs = [pl.BlockSpec((1, tm, h.shape[2]), lambda bi, i: (bi, i, 0)) for h in hs]
    in_specs += [pl.BlockSpec((k, d), lambda bi, i: (0, 0)),
                 pl.BlockSpec((1, tm, d), lambda bi, i: (bi, i, 0)),
                 pl.BlockSpec((1, 1, d), lambda bi, i: (bi, 0, 0)),
                 pl.BlockSpec((1, d), lambda bi, i: (0, 0)),
                 pl.BlockSpec((1, d), lambda bi, i: (0, 0))]
    return pl.pallas_call(
        functools.partial(_proj_ln_kernel, n_in=len(hs)),
        grid=(b, t // tm),
        in_specs=in_specs,
        out_specs=pl.BlockSpec((1, tm, d), lambda bi, i: (bi, i, 0)),
        out_shape=jax.ShapeDtypeStruct((b, t, d), F32),
        compiler_params=_params(("arbitrary", "arbitrary")),
        name="proj_ln",
    )(*hs, w, x, gate, ln_g, ln_b)


def _ffn_up_kernel(x_ref, sc_ref, sh_ref, w_ref, cb_ref, cw_ref, cbias_ref, h_ref, ct_ref, tail_ref):
    i = pl.program_id(1)
    tm = x_ref.shape[1]

    @pl.when(i == 0)
    def _():
        tail_ref[...] = cb_ref[0]

    u = _modulate(x_ref[0], sc_ref[0], sh_ref[0])
    up = _dot(u, w_ref[...])
    a = up[:, :D_FF]
    g = up[:, D_FF:]
    gc = _causal_conv(g, tail_ref[...], cw_ref, cbias_ref)
    h_ref[0] = (a * (gc * _sigmoid(gc))).astype(h_ref.dtype)
    g_tail = g[tm - SUBLANES:, :]
    tail_ref[...] = g_tail
    ct_ref[0] = g_tail


def _ffn_up(x, sc, sh, w, conv_tail, cw, cbias, tm_pref=256):
    b, t, d = x.shape
    tm = _row_tile(t, tm_pref)
    return pl.pallas_call(
        _ffn_up_kernel,
        grid=(b, t // tm),
        in_specs=[pl.BlockSpec((1, tm, d), lambda bi, i: (bi, i, 0)),
                  pl.BlockSpec((1, 1, d), lambda bi, i: (bi, 0, 0)),
                  pl.BlockSpec((1, 1, d), lambda bi, i: (bi, 0, 0)),
                  pl.BlockSpec((d, 2 * D_FF), lambda bi, i: (0, 0)),
                  pl.BlockSpec((1, SUBLANES, D_FF), lambda bi, i: (bi, 0, 0)),
                  pl.BlockSpec((FFN_CONV, D_FF), lambda bi, i: (0, 0)),
                  pl.BlockSpec((1, D_FF), lambda bi, i: (0, 0))],
        out_specs=[pl.BlockSpec((1, tm, D_FF), lambda bi, i: (bi, i, 0)),
                   pl.BlockSpec((1, SUBLANES, D_FF), lambda bi, i: (bi, 0, 0))],
        out_shape=[jax.ShapeDtypeStruct((b, t, D_FF), BF16),
                   jax.ShapeDtypeStruct((b, SUBLANES, D_FF), F32)],
        scratch_shapes=[pltpu.VMEM((SUBLANES, D_FF), F32)],
        compiler_params=_params(("arbitrary", "arbitrary")),
        name="ffn_up",
    )(x, sc, sh, w, conv_tail, cw, cbias)


def _pad_tail(buf):
    return jnp.pad(buf, ((0, 0), (SUBLANES - buf.shape[1], 0), (0, 0)))


def _rope_tables(past, t):
    half = A_HEAD_DIM // 2
    inv = ROPE_THETA ** (-jnp.arange(0, A_HEAD_DIM, 2, dtype=F32) / A_HEAD_DIM)
    pos = (past + jnp.arange(t, dtype=jnp.int32)).astype(F32)
    ang = pos[:, None] * inv[None, :]
    cos = jnp.tile(jnp.cos(ang), (1, LANES // half))
    sin = jnp.sin(ang)
    sin_signed = jnp.tile(jnp.concatenate([-sin, sin], axis=1), (1, LANES // A_HEAD_DIM))
    return cos, sin_signed


def _block_diag(w):
    n, i, o = w.shape
    return jnp.einsum("nio,nm->nimo", w, jnp.eye(n, dtype=w.dtype)).reshape(n * i, n * o)


def _prepare(p):
    w = {}
    w_in_ab = p["w_in_ab"][0]
    w["in_ab"] = w_in_ab.astype(BF16)
    w["in_ab_vt"] = w_in_ab[:, 2 * A_WIDTH:3 * A_WIDTH].T.astype(BF16)
    w["out_ab"] = p["w_out_ab"][0].astype(BF16)
    w["lam"] = jnp.stack([p["lam_q1"][0], p["lam_k1"][0], p["lam_q2"][0], p["lam_k2"][0]])
    w["gain_row"] = p["attn_gain"][0].reshape(A_HEADS, 1, LANES)
    w["gain_col"] = p["attn_gain"][0].reshape(A_HEADS, LANES, 1)
    w["b_conv_w"] = p["b_conv_w"][0]
    w["b_conv_b"] = p["b_conv_b"][0].reshape(1, B_WIDTH)
    w["rg_a"] = _block_diag(p["w_rg_a"][0]).astype(BF16)
    w["rg_x"] = _block_diag(p["w_rg_x"][0]).astype(BF16)
    w["b_rg_a"] = p["b_rg_a"][0].reshape(1, B_WIDTH)
    w["b_rg_x"] = p["b_rg_x"][0].reshape(1, B_WIDTH)
    w["rg_L"] = p["rg_L"][0].reshape(1, B_WIDTH)
    w_in_c = p["w_in_c"][0]
    w["in_c"] = jnp.pad(w_in_c, ((0, 0), (0, LANES - C_HEADS))).astype(BF16)
    w["in_c_vt"] = w_in_c[:, 2 * C_WIDTH:3 * C_WIDTH].T.astype(BF16)
    w["in_c_ft"] = w_in_c[:, 3 * C_WIDTH:].T.astype(BF16)
    w["bf_row"] = jnp.pad(p["b_f"][0], (0, LANES - C_HEADS)).reshape(1, LANES)
    w["bf_col"] = p["b_f"][0].reshape(C_HEADS, 1)
    w["place"] = _bias_placement()
    w["out_c"] = p["w_out_c"][0].astype(BF16)
    w["up"] = [p["w_up"][i].astype(BF16) for i in range(DEPTH)]
    w["down"] = [p["w_down"][i].astype(BF16) for i in range(DEPTH)]
    w["ffn_conv_w"] = [p["ffn_conv_w"][i] for i in range(DEPTH)]
    w["ffn_conv_b"] = [p["ffn_conv_b"][i].reshape(1, D_FF) for i in range(DEPTH)]
    w["ln1_g"] = [p["ln1_g"][i].reshape(1, D_MODEL) for i in range(DEPTH)]
    w["ln1_b"] = [p["ln1_b"][i].reshape(1, D_MODEL) for i in range(DEPTH)]
    w["ln2_g"] = [p["ln2_g"][i].reshape(1, D_MODEL) for i in range(DEPTH)]
    w["ln2_b"] = [p["ln2_b"][i].reshape(1, D_MODEL) for i in range(DEPTH)]
    return w


def _softplus_kernel(x_ref, o_ref):
    o_ref[...] = _softplus(-x_ref[...])


def _trunk(x, mods, w, sp, cache_a_k=None, cache_a_v=None, state_b_h=None, state_b_conv=None,
           cache_c_k=None, cache_c_v=None, cache_c_logf=None, state_ffn_conv=None):
    b, t, d = x.shape
    cached = cache_a_k is not None
    past = cache_a_k.shape[2] if cached else 0
    outs = {}
    for i in range(DEPTH):
        sh1, sc1, g1, sh2, sc2, g2 = [m[:, None, :] for m in jnp.split(mods[i], 6, axis=-1)]
        if i % 2 == 0:
            lam_init = 0.8 - 0.6 * math.exp(-0.3 * i)
            cos, sin = _rope_tables(past, t)
            q, kb, vt, k32, v32, xg = _proj_ab(x, sc1, sh1, w["in_ab"], w["in_ab_vt"], cos, sin)
            if cached:
                o = _cached_attention(q, kb, v32, cache_a_k[0].reshape(b, past, A_WIDTH),
                                      cache_a_v[0].reshape(b, past, A_WIDTH),
                                      "A", lam=w["lam"], gain=w["gain_row"], lam_init=lam_init)
                h0 = state_b_h[0][:, None, :]
                ctail = _pad_tail(state_b_conv[0])
            else:
                o = _attention(q, kb, vt, "A", lam=w["lam"], gain=w["gain_col"], lam_init=lam_init)
                h0 = jnp.zeros((b, 1, B_WIDTH), F32)
                ctail = jnp.zeros((b, SUBLANES, B_WIDTH), F32)
            yb, h_tail, x_tail = _rglru(xg, h0, ctail, w["b_conv_w"], w["b_conv_b"], w["rg_a"], w["b_rg_a"],
                                        w["rg_x"], w["b_rg_x"], sp)
            outs["a_k"] = k32.reshape(1, b, t, A_HEADS, 2 * A_HEAD_DIM)
            outs["a_v"] = v32.reshape(1, b, t, A_HEADS, 2 * A_HEAD_DIM)
            outs["b_h"] = h_tail[:, SUBLANES - 1, :][None]
            outs["b_conv"] = x_tail[:, SUBLANES - (B_CONV - 1):, :][None]
            x = _proj_ln([o, yb], w["out_ab"], x, g1, w["ln1_g"][i], w["ln1_b"][i])
        else:
            q, kb, bias, vt, k32, v32, lf, lft = _proj_c(x, sc1, sh1, w["in_c"], w["in_c_vt"], w["in_c_ft"],
                                                         w["bf_row"], w["bf_col"], w["place"])
            if cached:
                o = _cached_attention(q, kb, v32, cache_c_k[0].reshape(b, past, C_WIDTH),
                                      cache_c_v[0].reshape(b, past, C_WIDTH),
                                      "C", cache_lft=jnp.swapaxes(cache_c_logf[0], 1, 2), new_lft=lft)
            else:
                o = _attention(q, kb, vt, "C", bias=bias)
            outs["c_k"] = k32.reshape(1, b, t, C_HEADS, C_HEAD_DIM)
            outs["c_v"] = v32.reshape(1, b, t, C_HEADS, C_HEAD_DIM)
            outs["c_logf"] = lf[None]
            x = _proj_ln([o], w["out_c"], x, g1, w["ln1_g"][i], w["ln1_b"][i])
        ftail = _pad_tail(state_ffn_conv[i]) if cached else jnp.zeros((b, SUBLANES, D_FF), F32)
        hmid, g_tail = _ffn_up(x, sc2, sh2, w["up"][i], ftail, w["ffn_conv_w"][i], w["ffn_conv_b"][i])
        outs.setdefault("ffn", []).append(g_tail[:, SUBLANES - (FFN_CONV - 1):, :])
        x = _proj_ln([hmid], w["down"][i], x, g2, w["ln2_g"][i], w["ln2_b"][i])
    return (x, outs["a_k"], outs["a_v"], outs["b_h"], outs["b_conv"],
            outs["c_k"], outs["c_v"], outs["c_logf"], jnp.stack(outs["ffn"]))


def kernel(x_prompt, x_sample, c_prompt, c_sample, cache_a_k, cache_a_v, state_b_h, state_b_conv, cache_c_k, cache_c_v, cache_c_logf, state_ffn_conv, w_ada, b_ada, ln1_g, ln1_b, ln2_g, ln2_b, w_in_ab, lam_q1, lam_k1, lam_q2, lam_k2, attn_gain, b_conv_w, b_conv_b, w_rg_a, b_rg_a, w_rg_x, b_rg_x, rg_L, w_out_ab, w_in_c, b_f, w_out_c, w_up, ffn_conv_w, ffn_conv_b, w_down):
    p = dict(w_in_ab=w_in_ab, lam_q1=lam_q1, lam_k1=lam_k1, lam_q2=lam_q2, lam_k2=lam_k2, attn_gain=attn_gain,
             b_conv_w=b_conv_w, b_conv_b=b_conv_b, w_rg_a=w_rg_a, b_rg_a=b_rg_a, w_rg_x=w_rg_x, b_rg_x=b_rg_x,
             rg_L=rg_L, w_out_ab=w_out_ab, w_in_c=w_in_c, b_f=b_f, w_out_c=w_out_c, w_up=w_up,
             ffn_conv_w=ffn_conv_w, ffn_conv_b=ffn_conv_b, w_down=w_down,
             ln1_g=ln1_g, ln1_b=ln1_b, ln2_g=ln2_g, ln2_b=ln2_b)
    w = _prepare(p)
    bp = c_prompt.shape[0]
    bs = c_sample.shape[0]
    rows = -(-(bp + bs) // 16) * 16
    c_all = jnp.pad(jnp.concatenate([c_prompt, c_sample], axis=0), ((0, rows - bp - bs), (0, 0)))
    mods = _mods(c_all, w_ada, b_ada)
    sp = pl.pallas_call(_softplus_kernel, out_shape=jax.ShapeDtypeStruct((1, B_WIDTH), F32),
                        name="softplus")(w["rg_L"])
    res_p = _trunk(x_prompt, mods[:, :bp], w, sp)
    res_s = _trunk(x_sample, mods[:, bp:bp + bs], w, sp, cache_a_k, cache_a_v, state_b_h, state_b_conv,
                   cache_c_k, cache_c_v, cache_c_logf, state_ffn_conv)
    return (res_p[0], res_s[0]) + res_p[1:] + res_s[1:]
```

```python
import functools
import math

import numpy as np
import jax
import jax.numpy as jnp
from jax import lax
from jax.experimental import pallas as pl
from jax.experimental.pallas import tpu as pltpu

F32 = jnp.float32
BF16 = jnp.bfloat16

D_MODEL = 1024
DEPTH = 2
CHUNK = 64
CHUNK_SHIFT = 6
A_HEADS = 4
A_HEAD_DIM = 64
A_WIDTH = A_HEADS * 2 * A_HEAD_DIM
B_WIDTH = 512
B_BLOCKS = 8
B_CONV = 4
RG_C = 8.0
C_HEADS = 16
C_HEAD_DIM = 64
C_WIDTH = C_HEADS * C_HEAD_DIM
D_FF = 2816
FFN_CONV = 3
ROPE_THETA = 10000.0
ALPHA = (2 * DEPTH) ** 0.25
LN_EPS = 1e-5
NEG = -1e30
LOG2E = 1.4426950408889634

LANES = 128
SUBLANES = 8
BF16_ROWS = 16
MXU_DIM = 256
VMEM_LIMIT = 56 * 1024 * 1024
BIAS_WIDTH = (C_HEADS // 2) * LANES
ATTN_BLOCK = 512


def _params(sem, flags=None):
    return pltpu.CompilerParams(dimension_semantics=sem, vmem_limit_bytes=VMEM_LIMIT, flags=flags)


def _row_tile(t, pref):
    if t <= pref:
        return t
    tm = pref
    while t % tm:
        tm //= 2
    return tm


def _modulate(x, sc, sh):
    return (x * (1.0 + sc) + sh).astype(BF16)


def _sigmoid(x):
    return 1.0 / (1.0 + jnp.exp(-x))


def _softplus(x):
    return jnp.maximum(x, 0.0) + jnp.log1p(jnp.exp(-jnp.abs(x)))


def _log_sigmoid(x):
    return jnp.minimum(x, 0.0) - jnp.log1p(jnp.exp(-jnp.abs(x)))


def _split3(x):
    hi = x.astype(BF16)
    r1 = x - hi.astype(F32)
    mid = r1.astype(BF16)
    lo = (r1 - mid.astype(F32)).astype(BF16)
    return hi, mid, lo


def _dot(a, b):
    return jnp.dot(a, b, preferred_element_type=F32)


def _dot_nt(a, b):
    return lax.dot_general(a, b, (((1,), (1,)), ((), ())), preferred_element_type=F32)


def _mods_kernel(c_ref, w_ref, b_ref, o_ref):
    c = c_ref[...]
    s = (c * _sigmoid(c)).astype(BF16)
    o_ref[0] = _dot(s, w_ref[0].astype(BF16)) + b_ref[0]


def _mods(c_all, w_ada, b_ada):
    rows, d = c_all.shape
    n = w_ada.shape[-1]
    tn = 1536
    return pl.pallas_call(
        _mods_kernel,
        grid=(DEPTH, n // tn),
        in_specs=[pl.BlockSpec((rows, d), lambda l, j: (0, 0)),
                  pl.BlockSpec((1, d, tn), lambda l, j: (l, 0, j)),
                  pl.BlockSpec((1, 1, tn), lambda l, j: (l, 0, j))],
        out_specs=pl.BlockSpec((1, rows, tn), lambda l, j: (l, 0, j)),
        out_shape=jax.ShapeDtypeStruct((DEPTH, rows, n), F32),
        compiler_params=_params(("arbitrary", "arbitrary")),
        name="mods",
    )(c_all, w_ada, b_ada.reshape(DEPTH, 1, n))


def _rope_slab(x, cos, sin_signed, first_half):
    fwd = pltpu.roll(x, LANES - A_HEAD_DIM // 2, 1)
    bwd = pltpu.roll(x, A_HEAD_DIM // 2, 1)
    partner = jnp.where(first_half, fwd, bwd)
    return x * cos + partner * sin_signed


def _proj_ab_kernel(x_ref, sc_ref, sh_ref, w_ref, wvt_ref, cos_ref, sin_ref,
                    q_ref, kb_ref, vt_ref, k_ref, v_ref, xg_ref, *, q_scale):
    u = _modulate(x_ref[0], sc_ref[0], sh_ref[0])
    pr = _dot(u, w_ref[...])
    cos = cos_ref[...]
    sin = sin_ref[...]
    lane = lax.broadcasted_iota(jnp.int32, (1, LANES), 1)
    first_half = (lane & (A_HEAD_DIM - 1)) < (A_HEAD_DIM // 2)
    for h in range(A_HEADS):
        sl = slice(h * LANES, (h + 1) * LANES)
        q = _rope_slab(pr[:, sl], cos, sin, first_half)
        q_ref[0, :, sl] = (q * q_scale).astype(BF16)
        k = _rope_slab(pr[:, A_WIDTH + h * LANES:A_WIDTH + (h + 1) * LANES], cos, sin, first_half)
        k_ref[0, :, sl] = k
        kb_ref[0, :, sl] = k.astype(BF16)
    v_ref[0] = pr[:, 2 * A_WIDTH:3 * A_WIDTH]
    vt_ref[0, 0] = _dot_nt(wvt_ref[...], u).astype(BF16)
    xg_ref[0] = pr[:, 3 * A_WIDTH:]


def _proj_ab(x, sc, sh, w, wvt, cos, sin):
    b, t, d = x.shape
    tm = _row_tile(t, ATTN_BLOCK)
    nt = t // tm
    n = w.shape[1]
    q_scale = A_HEAD_DIM ** -0.5 * LOG2E
    row = lambda width: pl.BlockSpec((1, tm, width), lambda bi, i: (bi, i, 0))
    const = lambda shape: pl.BlockSpec(shape, lambda bi, i: tuple(0 for _ in shape))
    return pl.pallas_call(
        functools.partial(_proj_ab_kernel, q_scale=q_scale),
        grid=(b, nt),
        in_specs=[row(d),
                  pl.BlockSpec((1, 1, d), lambda bi, i: (bi, 0, 0)),
                  pl.BlockSpec((1, 1, d), lambda bi, i: (bi, 0, 0)),
                  const((d, n)), const((A_WIDTH, d)),
                  pl.BlockSpec((tm, LANES), lambda bi, i: (i, 0)),
                  pl.BlockSpec((tm, LANES), lambda bi, i: (i, 0))],
        out_specs=[row(A_WIDTH), row(A_WIDTH),
                   pl.BlockSpec((1, 1, A_WIDTH, tm), lambda bi, i: (bi, i, 0, 0)),
                   row(A_WIDTH), row(A_WIDTH), row(2 * B_WIDTH)],
        out_shape=[jax.ShapeDtypeStruct((b, t, A_WIDTH), BF16),
                   jax.ShapeDtypeStruct((b, t, A_WIDTH), BF16),
                   jax.ShapeDtypeStruct((b, nt, A_WIDTH, tm), BF16),
                   jax.ShapeDtypeStruct((b, t, A_WIDTH), F32),
                   jax.ShapeDtypeStruct((b, t, A_WIDTH), F32),
                   jax.ShapeDtypeStruct((b, t, 2 * B_WIDTH), F32)],
        compiler_params=_params(("arbitrary", "arbitrary")),
        name="proj_ab",
    )(x, sc, sh, w, wvt, cos, sin)


def _bias_placement():
    e = np.zeros((3 * LANES, BIAS_WIDTH), np.float32)
    for piece in range(3):
        for h in range(C_HEADS):
            e[piece * LANES + h, (h // 2) * LANES + 3 * (h % 2) + piece] = 1.0
    return jnp.asarray(e, BF16)


def _proj_c_kernel(x_ref, sc_ref, sh_ref, w_ref, wvt_ref, wft_ref, bfr_ref, bfc_ref, place_ref,
                   q_ref, kb_ref, bias_ref, vt_ref, k_ref, v_ref, lf_ref, lft_ref, run_ref, *, q_scale):
    i = pl.program_id(1)
    tm = x_ref.shape[1]

    @pl.when(i == 0)
    def _():
        run_ref[...] = jnp.zeros_like(run_ref)

    u = _modulate(x_ref[0], sc_ref[0], sh_ref[0])
    pr = _dot(u, w_ref[...])
    q_ref[0] = (pr[:, :C_WIDTH] * q_scale).astype(BF16)
    k = pr[:, C_WIDTH:2 * C_WIDTH]
    k_ref[0] = k
    kb_ref[0] = k.astype(BF16)
    v_ref[0] = pr[:, 2 * C_WIDTH:3 * C_WIDTH]
    vt_ref[0, 0] = _dot_nt(wvt_ref[...], u).astype(BF16)
    lf = _log_sigmoid(pr[:, 3 * C_WIDTH:] + bfr_ref[...])
    lf_ref[0] = lf[:, :C_HEADS]
    lft_ref[0] = _log_sigmoid(_dot_nt(wft_ref[...], u) + bfc_ref[...])
    r = lax.broadcasted_iota(jnp.int32, (tm, tm), 0)
    c = lax.broadcasted_iota(jnp.int32, (tm, tm), 1)
    lower = jnp.where(c <= r, 1.0, 0.0).astype(BF16)
    hi, mid, lo = _split3(lf)
    cum = (_dot(lower, hi) + _dot(lower, mid)) + _dot(lower, lo) + run_ref[...]
    pieces = jnp.concatenate(_split3(cum * (-LOG2E)), axis=1)
    bias_ref[0] = _dot(pieces, place_ref[...]).astype(BF16)
    run_ref[...] = run_ref[...] + jnp.sum(lf, axis=0, keepdims=True)


def _proj_c(x, sc, sh, w, wvt, wft, bf_row, bf_col, place):
    b, t, d = x.shape
    tm = _row_tile(t, ATTN_BLOCK)
    nt = t // tm
    n = w.shape[1]
    q_scale = C_HEAD_DIM ** -0.5 * LOG2E
    row = lambda width: pl.BlockSpec((1, tm, width), lambda bi, i: (bi, i, 0))
    const = lambda shape: pl.BlockSpec(shape, lambda bi, i: tuple(0 for _ in shape))
    return pl.pallas_call(
        functools.partial(_proj_c_kernel, q_scale=q_scale),
        grid=(b, nt),
        in_specs=[row(d),
                  pl.BlockSpec((1, 1, d), lambda bi, i: (bi, 0, 0)),
                  pl.BlockSpec((1, 1, d), lambda bi, i: (bi, 0, 0)),
                  const((d, n)), const((C_WIDTH, d)), const((C_HEADS, d)),
                  const((1, LANES)), const((C_HEADS, 1)), const((3 * LANES, BIAS_WIDTH))],
        out_specs=[row(C_WIDTH), row(C_WIDTH), row(BIAS_WIDTH),
                   pl.BlockSpec((1, 1, C_WIDTH, tm), lambda bi, i: (bi, i, 0, 0)),
                   row(C_WIDTH), row(C_WIDTH), row(C_HEADS),
                   pl.BlockSpec((1, C_HEADS, tm), lambda bi, i: (bi, 0, i))],
        out_shape=[jax.ShapeDtypeStruct((b, t, C_WIDTH), BF16),
                   jax.ShapeDtypeStruct((b, t, C_WIDTH), BF16),
                   jax.ShapeDtypeStruct((b, t, BIAS_WIDTH), BF16),
                   jax.ShapeDtypeStruct((b, nt, C_WIDTH, tm), BF16),
                   jax.ShapeDtypeStruct((b, t, C_WIDTH), F32),
                   jax.ShapeDtypeStruct((b, t, C_WIDTH), F32),
                   jax.ShapeDtypeStruct((b, t, C_HEADS), F32),
                   jax.ShapeDtypeStruct((b, C_HEADS, t), F32)],
        scratch_shapes=[pltpu.VMEM((1, LANES), F32)],
        compiler_params=_params(("arbitrary", "arbitrary")),
        name="proj_c",
    )(x, sc, sh, w, wvt, wft, bf_row, bf_col, place)


def _lambda(lam_ref, lam_init):
    lq1, lk1, lq2, lk2 = (lam_ref[r:r + 1, :] for r in range(4))
    return (jnp.exp(jnp.sum(lq1 * lk1, axis=1, keepdims=True))
            - jnp.exp(jnp.sum(lq2 * lk2, axis=1, keepdims=True)) + lam_init)


def _attn_kernel(*refs, mode, lam_init):
    if mode == "A":
        q_ref, k_ref, vt_ref, lam_ref, gain_ref, o_ref, qc_ref, m_ref, acc_ref, s0_ref, s1_ref = refs
        bias_ref = None
        d_val = 2 * A_HEAD_DIM
    else:
        q_ref, k_ref, bias_ref, vt_ref, o_ref, qc_ref, m_ref, acc_ref, s0_ref, s1_ref = refs
        d_val = C_HEAD_DIM
    tq = q_ref.shape[1]
    tk = k_ref.shape[2]
    qi = pl.program_id(2)
    lane = lax.broadcasted_iota(jnp.int32, (1, LANES), 1)
    low = lane < (LANES // 2)
    q = q_ref[0]
    zero = jnp.zeros_like(q)
    for a in range(2):
        qa = jnp.where(low, q, zero) if a == 0 else jnp.where(low, zero, q)
        if bias_ref is not None:
            pick = jnp.where((lane >= 3 * a) & (lane < 3 * a + 3), 1.0, 0.0).astype(BF16)
            qa = jnp.concatenate([qa, jnp.broadcast_to(pick, (tq, LANES))], axis=1)
        qc_ref[a] = qa
    m_ref[...] = jnp.full(m_ref.shape, NEG, F32)
    acc_ref[...] = jnp.zeros(acc_ref.shape, F32)
    ones = jnp.ones((BF16_ROWS, tk), BF16)

    units = [(a, slice(n * MXU_DIM, (n + 1) * MXU_DIM)) for a in range(2) for n in range(tq // MXU_DIM)]

    def score_chain(j, u):
        a, cs = units[u]
        kc = k_ref[0, j]
        if bias_ref is not None:
            kc = jnp.concatenate([kc, bias_ref[0, j]], axis=1)
        return _dot_nt(kc, qc_ref[a, cs, :])

    def value_chain(j, u, st, masked):
        a, cs = units[u]
        if masked:
            key = lax.broadcasted_iota(jnp.int32, (tk, MXU_DIM), 0)
            qry = lax.broadcasted_iota(jnp.int32, (tk, MXU_DIM), 1) + cs.start
            keep = ((key >> CHUNK_SHIFT) <= (qry >> CHUNK_SHIFT)) if mode == "A" else (key <= qry)
            st = jnp.where(keep, st, NEG)
        m_prev = m_ref[a, :, cs]
        m_new = jnp.maximum(m_prev, jnp.max(st, axis=0, keepdims=True))
        alpha = jnp.exp2(m_prev - m_new)
        pt = jnp.exp2(st - m_new).astype(BF16)
        vt = vt_ref[0, j]
        va = vt if mode == "A" else vt[a * d_val:(a + 1) * d_val]
        va = jnp.concatenate([va, ones], axis=0)
        acc_ref[a, :, cs] = alpha * acc_ref[a, :, cs] + _dot(va, pt)
        m_ref[a, :, cs] = m_new

    def stage(j_scores, s_dst, j_values, s_src, masked=False):
        for t in range(len(units) + 1):
            if j_scores is not None and t < len(units):
                s_dst[t] = score_chain(j_scores, t)
            if j_values is not None and t >= 1:
                value_chain(j_values, t - 1, s_src[t - 1], masked)

    stage(0, s0_ref, None, None)

    def pair(p, carry):
        j = 2 * p
        stage(j + 1, s1_ref, j, s0_ref)
        stage(j + 2, s0_ref, j + 1, s1_ref)
        return carry

    lax.fori_loop(0, qi >> 1, pair, 0)

    @pl.when((qi & 1) == 1)
    def _():
        stage(qi, s1_ref, qi - 1, s0_ref)
        stage(None, None, qi, s1_ref, masked=True)

    @pl.when((qi & 1) == 0)
    def _():
        stage(None, None, qi, s0_ref, masked=True)

    outs = []
    for a in range(2):
        acc = acc_ref[a]
        outs.append(acc[:d_val] * (1.0 / acc[d_val:d_val + 1]))
    if mode == "A":
        ot = outs[0] - _lambda(lam_ref, lam_init) * outs[1]
        ot = ot * lax.rsqrt(jnp.mean(ot * ot, axis=0, keepdims=True) + LN_EPS)
        ot = ot * (gain_ref[0] * (1.0 - lam_init))
    else:
        ot = jnp.concatenate(outs, axis=0)
    o_ref[0] = ot.T.astype(o_ref.dtype)


def _attention(q, k, vt, mode, lam=None, gain=None, bias=None, lam_init=0.0):
    b, t, width = q.shape
    nk, tk = vt.shape[1], vt.shape[3]
    tq = tk
    groups = width // LANES
    d_aug = (2 * A_HEAD_DIM if mode == "A" else C_HEAD_DIM) + BF16_ROWS
    kspec = pl.BlockSpec((1, nk, tk, LANES), lambda bi, g, i: (bi, 0, 0, g))
    in_specs = [pl.BlockSpec((1, tq, LANES), lambda bi, g, i: (bi, i, g)), kspec]
    args = [q, k.reshape(b, nk, tk, width)]
    if mode == "C":
        in_specs.append(kspec)
        args.append(bias.reshape(b, nk, tk, groups * LANES))
    in_specs.append(pl.BlockSpec((1, nk, LANES, tk), lambda bi, g, i: (bi, 0, g, 0)))
    args.append(vt)
    if mode == "A":
        in_specs += [pl.BlockSpec((4, A_HEAD_DIM), lambda bi, g, i: (0, 0)),
                     pl.BlockSpec((1, LANES, 1), lambda bi, g, i: (g, 0, 0))]
        args += [lam, gain]
    return pl.pallas_call(
        functools.partial(_attn_kernel, mode=mode, lam_init=lam_init),
        grid=(b, groups, t // tq),
        in_specs=in_specs,
        out_specs=pl.BlockSpec((1, tq, LANES), lambda bi, g, i: (bi, i, g)),
        out_shape=jax.ShapeDtypeStruct((b, t, width), BF16),
        scratch_shapes=[pltpu.VMEM((2, tq, LANES if mode == "A" else 2 * LANES), BF16),
                        pltpu.VMEM((2, 1, tq), F32),
                        pltpu.VMEM((2, d_aug, tq), F32),
                        pltpu.VMEM((2 * tq // MXU_DIM, tk, MXU_DIM), F32),
                        pltpu.VMEM((2 * tq // MXU_DIM, tk, MXU_DIM), F32)],
        compiler_params=_params(("arbitrary", "arbitrary", "arbitrary")),
        name="attn_" + mode,
    )(*args)


def _diff_finish(o0, o1, lam, gain, lam_init):
    o = o0 - lam * o1
    o = o * lax.rsqrt(jnp.mean(o * o, axis=-1, keepdims=True) + LN_EPS)
    return o * gain * (1.0 - lam_init)


def _cached_attn_kernel(*refs, mode, past, lam_init):
    if mode == "A":
        q_ref, kn_ref, vn_ref, ck_ref, cv_ref, lam_ref, gain_ref, o_ref = refs
    else:
        q_ref, kn_ref, vn_ref, ck_ref, cv_ref, clf_ref, nlf_ref, o_ref = refs
    t = q_ref.shape[1]
    width = o_ref.shape[2]
    groups = width // LANES
    lane = lax.broadcasted_iota(jnp.int32, (1, LANES), 1)
    low = lane < (LANES // 2)
    row = lax.broadcasted_iota(jnp.int32, (t, t), 0)
    col = lax.broadcasted_iota(jnp.int32, (t, t), 1)
    if mode == "A":
        keep = ((past + col) >> CHUNK_SHIFT) <= ((past + row) >> CHUNK_SHIFT)
        lam = _lambda(lam_ref, lam_init)
    else:
        keep = col <= row
        r = lax.broadcasted_iota(jnp.int32, (past, past), 0)
        c = lax.broadcasted_iota(jnp.int32, (past, past), 1)
        upper = jnp.where(r <= c, 1.0, 0.0).astype(BF16)
        clf = clf_ref[0]
        hi, mid, lo = _split3(clf)
        cum_c = (_dot(hi, upper) + _dot(mid, upper)) + _dot(lo, upper)
        upper_n = jnp.where(row <= col, 1.0, 0.0).astype(BF16)
        hi, mid, lo = _split3(nlf_ref[0])
        cum_n = ((_dot(hi, upper_n) + _dot(mid, upper_n)) + _dot(lo, upper_n)
                 + jnp.sum(clf, axis=1, keepdims=True))
        bias_c = cum_c * (-LOG2E)
        bias_n = cum_n * (-LOG2E)
    for g in range(groups):
        sl = slice(g * LANES, (g + 1) * LANES)
        q = q_ref[0, :, sl]
        kn = kn_ref[0, :, sl]
        vn = vn_ref[0, :, sl].astype(BF16)
        kc = ck_ref[0, :, sl].astype(BF16)
        vc = cv_ref[0, :, sl].astype(BF16)
        outs = []
        for a in range(2):
            qa = jnp.where(low, q, jnp.zeros_like(q)) if a == 0 else jnp.where(low, jnp.zeros_like(q), q)
            s_c = _dot_nt(qa, kc)
            s_n = _dot_nt(qa, kn)
            if mode == "C":
                h = 2 * g + a
                s_c = s_c + bias_c[h:h + 1, :]
                s_n = s_n + bias_n[h:h + 1, :]
            s_n = jnp.where(keep, s_n, NEG)
            m = jnp.maximum(jnp.max(s_c, axis=1, keepdims=True), jnp.max(s_n, axis=1, keepdims=True))
            p_c = jnp.exp2(s_c - m)
            p_n = jnp.exp2(s_n - m)
            l = jnp.sum(p_c, axis=1, keepdims=True) + jnp.sum(p_n, axis=1, keepdims=True)
            acc = _dot(p_c.astype(BF16), vc) + _dot(p_n.astype(BF16), vn)
            outs.append(acc * (1.0 / l))
        if mode == "A":
            o = _diff_finish(outs[0], outs[1], lam, gain_ref[g], lam_init)
        else:
            o = jnp.where(low, outs[0], outs[1])
        o_ref[0, :, sl] = o.astype(o_ref.dtype)


def _cached_attention(q, kn, vn, cache_k, cache_v, mode, lam=None, gain=None, cache_lft=None, new_lft=None,
                      lam_init=0.0):
    b, t, width = q.shape
    past = cache_k.shape[1]
    new = pl.BlockSpec((1, t, width), lambda bi: (bi, 0, 0))
    old = pl.BlockSpec((1, past, width), lambda bi: (bi, 0, 0))
    in_specs = [new, new, new, old, old]
    args = [q, kn, vn, cache_k, cache_v]
    if mode == "A":
        in_specs += [pl.BlockSpec((4, A_HEAD_DIM), lambda bi: (0, 0)),
                     pl.BlockSpec((A_HEADS, 1, LANES), lambda bi: (0, 0, 0))]
        args += [lam, gain]
    else:
        in_specs += [pl.BlockSpec((1, C_HEADS, past), lambda bi: (bi, 0, 0)),
                     pl.BlockSpec((1, C_HEADS, t), lambda bi: (bi, 0, 0))]
        args += [cache_lft, new_lft]
    return pl.pallas_call(
        functools.partial(_cached_attn_kernel, mode=mode, past=past, lam_init=lam_init),
        grid=(b,),
        in_specs=in_specs,
        out_specs=new,
        out_shape=jax.ShapeDtypeStruct((b, t, width), BF16),
        compiler_params=_params(("arbitrary",)),
        name="cached_attn_" + mode,
    )(*args)


def _gelu_tanh(x):
    return 0.5 * x * (1.0 + jnp.tanh(math.sqrt(2.0 / math.pi) * (x + 0.044715 * (x * x * x))))


def _causal_conv(x, tail, w_ref, b_ref):
    width = w_ref.shape[0]
    tm = x.shape[0]
    cat = jnp.concatenate([tail, x], axis=0)
    y = None
    for j in range(width):
        back = width - 1 - j
        src = cat if back == 0 else pltpu.roll(cat, back, 0)
        term = src[SUBLANES:SUBLANES + tm] * w_ref[j:j + 1, :]
        y = term if y is None else y + term
    return y + b_ref[...]


def _rglru_kernel(xg_ref, h0_ref, cb_ref, cw_ref, cbias_ref, wa_ref, ba_ref, wx_ref, bx_ref, sp_ref,
                  y_ref, hl_ref, ct_ref, hc_ref, tail_ref):
    i = pl.program_id(1)
    tm = xg_ref.shape[1]

    @pl.when(i == 0)
    def _():
        hc_ref[...] = h0_ref[0]
        tail_ref[...] = cb_ref[0]

    x = xg_ref[0, :, :B_WIDTH]
    gate_in = xg_ref[0, :, B_WIDTH:]
    xc = _causal_conv(x, tail_ref[...], cw_ref, cbias_ref)
    xcb = xc.astype(BF16)
    r = _sigmoid(_dot(xcb, wa_ref[...]) + ba_ref[...])
    ig = _sigmoid(_dot(xcb, wx_ref[...]) + bx_ref[...])
    log_a = (-RG_C) * r * sp_ref[...]
    a = jnp.exp(log_a)
    th = jnp.tanh(log_a)
    bx = jnp.sqrt((-2.0 * th) / (1.0 - th)) * (ig * xc)
    row = lax.broadcasted_iota(jnp.int32, (tm, 1), 0)
    s = 1
    while s < tm:
        valid = row >= s
        a_sh = pltpu.roll(a, s, 0)
        b_sh = pltpu.roll(bx, s, 0)
        bx = jnp.where(valid, a * b_sh + bx, bx)
        a = jnp.where(valid, a * a_sh, a)
        s *= 2
    h = a * hc_ref[...] + bx
    y_ref[0] = (h * _gelu_tanh(gate_in)).astype(y_ref.dtype)
    h_tail = h[tm - SUBLANES:, :]
    x_tail = x[tm - SUBLANES:, :]
    hc_ref[...] = h_tail[SUBLANES - 1:, :]
    tail_ref[...] = x_tail
    hl_ref[0] = h_tail
    ct_ref[0] = x_tail


def _rglru(xg, h0, conv_tail, cw, cbias, wa, ba, wx, bx, sp, tm_pref=256):
    b, t, _ = xg.shape
    tm = _row_tile(t, tm_pref)
    wspec = lambda shape: pl.BlockSpec(shape, lambda bi, i: tuple(0 for _ in shape))
    return pl.pallas_call(
        _rglru_kernel,
        grid=(b, t // tm),
        in_specs=[pl.BlockSpec((1, tm, 2 * B_WIDTH), lambda bi, i: (bi, i, 0)),
                  pl.BlockSpec((1, 1, B_WIDTH), lambda bi, i: (bi, 0, 0)),
                  pl.BlockSpec((1, SUBLANES, B_WIDTH), lambda bi, i: (bi, 0, 0)),
                  wspec((B_CONV, B_WIDTH)), wspec((1, B_WIDTH)),
                  wspec((B_WIDTH, B_WIDTH)), wspec((1, B_WIDTH)),
                  wspec((B_WIDTH, B_WIDTH)), wspec((1, B_WIDTH)), wspec((1, B_WIDTH))],
        out_specs=[pl.BlockSpec((1, tm, B_WIDTH), lambda bi, i: (bi, i, 0)),
                   pl.BlockSpec((1, SUBLANES, B_WIDTH), lambda bi, i: (bi, 0, 0)),
                   pl.BlockSpec((1, SUBLANES, B_WIDTH), lambda bi, i: (bi, 0, 0))],
        out_shape=[jax.ShapeDtypeStruct((b, t, B_WIDTH), BF16),
                   jax.ShapeDtypeStruct((b, SUBLANES, B_WIDTH), F32),
                   jax.ShapeDtypeStruct((b, SUBLANES, B_WIDTH), F32)],
        scratch_shapes=[pltpu.VMEM((1, B_WIDTH), F32), pltpu.VMEM((SUBLANES, B_WIDTH), F32)],
        compiler_params=_params(("arbitrary", "arbitrary")),
        name="rglru",
    )(xg, h0, conv_tail, cw, cbias, wa, ba, wx, bx, sp)


def _proj_ln_kernel(*refs, n_in):
    h_refs = refs[:n_in]
    w_ref, x_ref, gate_ref, g_ref, b_ref, o_ref = refs[n_in:]
    hs = [r[0] for r in h_refs]
    h = hs[0] if n_in == 1 else jnp.concatenate(hs, axis=-1)
    y = ALPHA * x_ref[0] + gate_ref[0] * _dot(h, w_ref[...])
    mu = jnp.mean(y, axis=-1, keepdims=True)
    yc = y - mu
    var = jnp.mean(yc * yc, axis=-1, keepdims=True)
    o_ref[0] = yc * lax.rsqrt(var + LN_EPS) * g_ref[...] + b_ref[...]


def _proj_ln(hs, w, x, gate, ln_g, ln_b, tm_pref=512):
    b, t, d = x.shape
    tm = _row_tile(t, tm_pref)
    k = w.shape[0]
    in_specs = [pl.BlockSpec((1, tm, h.shape[2]), lambda bi, i: (bi, i, 0)) for h in hs]
    in_specs += [pl.BlockSpec((k, d), lambda bi, i: (0, 0)),
                 pl.BlockSpec((1, tm, d), lambda bi, i: (bi, i, 0)),
                 pl.BlockSpec((1, 1, d), lambda bi, i: (bi, 0, 0)),
                 pl.BlockSpec((1, d), lambda bi, i: (0, 0)),
                 pl.BlockSpec((1, d), lambda bi, i: (0, 0))]
    return pl.pallas_call(
        functools.partial(_proj_ln_kernel, n_in=len(hs)),
        grid=(b, t // tm),
        in_specs=in_specs,
        out_specs=pl.BlockSpec((1, tm, d), lambda bi, i: (bi, i, 0)),
        out_shape=jax.ShapeDtypeStruct((b, t, d), F32),
        compiler_params=_params(("arbitrary", "arbitrary")),
        name="proj_ln",
    )(*hs, w, x, gate, ln_g, ln_b)


def _ffn_up_kernel(x_ref, sc_ref, sh_ref, w_ref, cb_ref, cw_ref, cbias_ref, h_ref, ct_ref, tail_ref):
    i = pl.program_id(1)
    tm = x_ref.shape[1]

    @pl.when(i == 0)
    def _():
        tail_ref[...] = cb_ref[0]

    u = _modulate(x_ref[0], sc_ref[0], sh_ref[0])
    up = _dot(u, w_ref[...])
    a = up[:, :D_FF]
    g = up[:, D_FF:]
    gc = _causal_conv(g, tail_ref[...], cw_ref, cbias_ref)
    h_ref[0] = (a * (gc * _sigmoid(gc))).astype(h_ref.dtype)
    g_tail = g[tm - SUBLANES:, :]
    tail_ref[...] = g_tail
    ct_ref[0] = g_tail


def _ffn_up(x, sc, sh, w, conv_tail, cw, cbias, tm_pref=256):
    b, t, d = x.shape
    tm = _row_tile(t, tm_pref)
    return pl.pallas_call(
        _ffn_up_kernel,
        grid=(b, t // tm),
        in_specs=[pl.BlockSpec((1, tm, d), lambda bi, i: (bi, i, 0)),
                  pl.BlockSpec((1, 1, d), lambda bi, i: (bi, 0, 0)),
                  pl.BlockSpec((1, 1, d), lambda bi, i: (bi, 0, 0)),
                  pl.BlockSpec((d, 2 * D_FF), lambda bi, i: (0, 0)),
                  pl.BlockSpec((1, SUBLANES, D_FF), lambda bi, i: (bi, 0, 0)),
                  pl.BlockSpec((FFN_CONV, D_FF), lambda bi, i: (0, 0)),
                  pl.BlockSpec((1, D_FF), lambda bi, i: (0, 0))],
        out_specs=[pl.BlockSpec((1, tm, D_FF), lambda bi, i: (bi, i, 0)),
                   pl.BlockSpec((1, SUBLANES, D_FF), lambda bi, i: (bi, 0, 0))],
        out_shape=[jax.ShapeDtypeStruct((b, t, D_FF), BF16),
                   jax.ShapeDtypeStruct((b, SUBLANES, D_FF), F32)],
        scratch_shapes=[pltpu.VMEM((SUBLANES, D_FF), F32)],
        compiler_params=_params(("arbitrary", "arbitrary")),
        name="ffn_up",
    )(x, sc, sh, w, conv_tail, cw, cbias)


def _pad_tail(buf):
    return jnp.pad(buf, ((0, 0), (SUBLANES - buf.shape[1], 0), (0, 0)))


def _rope_tables(past, t):
    half = A_HEAD_DIM // 2
    inv = ROPE_THETA ** (-jnp.arange(0, A_HEAD_DIM, 2, dtype=F32) / A_HEAD_DIM)
    pos = (past + jnp.arange(t, dtype=jnp.int32)).astype(F32)
    ang = pos[:, None] * inv[None, :]
    cos = jnp.tile(jnp.cos(ang), (1, LANES // half))
    sin = jnp.sin(ang)
    sin_signed = jnp.tile(jnp.concatenate([-sin, sin], axis=1), (1, LANES // A_HEAD_DIM))
    return cos, sin_signed


def _block_diag(w):
    n, i, o = w.shape
    return jnp.einsum("nio,nm->nimo", w, jnp.eye(n, dtype=w.dtype)).reshape(n * i, n * o)


def _prepare(p):
    w = {}
    w_in_ab = p["w_in_ab"][0]
    w["in_ab"] = w_in_ab.astype(BF16)
    w["in_ab_vt"] = w_in_ab[:, 2 * A_WIDTH:3 * A_WIDTH].T.astype(BF16)
    w["out_ab"] = p["w_out_ab"][0].astype(BF16)
    w["lam"] = jnp.stack([p["lam_q1"][0], p["lam_k1"][0], p["lam_q2"][0], p["lam_k2"][0]])
    w["gain_row"] = p["attn_gain"][0].reshape(A_HEADS, 1, LANES)
    w["gain_col"] = p["attn_gain"][0].reshape(A_HEADS, LANES, 1)
    w["b_conv_w"] = p["b_conv_w"][0]
    w["b_conv_b"] = p["b_conv_b"][0].reshape(1, B_WIDTH)
    w["rg_a"] = _block_diag(p["w_rg_a"][0]).astype(BF16)
    w["rg_x"] = _block_diag(p["w_rg_x"][0]).astype(BF16)
    w["b_rg_a"] = p["b_rg_a"][0].reshape(1, B_WIDTH)
    w["b_rg_x"] = p["b_rg_x"][0].reshape(1, B_WIDTH)
    w["rg_L"] = p["rg_L"][0].reshape(1, B_WIDTH)
    w_in_c = p["w_in_c"][0]
    w["in_c"] = jnp.pad(w_in_c, ((0, 0), (0, LANES - C_HEADS))).astype(BF16)
    w["in_c_vt"] = w_in_c[:, 2 * C_WIDTH:3 * C_WIDTH].T.astype(BF16)
    w["in_c_ft"] = w_in_c[:, 3 * C_WIDTH:].T.astype(BF16)
    w["bf_row"] = jnp.pad(p["b_f"][0], (0, LANES - C_HEADS)).reshape(1, LANES)
    w["bf_col"] = p["b_f"][0].reshape(C_HEADS, 1)
    w["place"] = _bias_placement()
    w["out_c"] = p["w_out_c"][0].astype(BF16)
    w["up"] = [p["w_up"][i].astype(BF16) for i in range(DEPTH)]
    w["down"] = [p["w_down"][i].astype(BF16) for i in range(DEPTH)]
    w["ffn_conv_w"] = [p["ffn_conv_w"][i] for i in range(DEPTH)]
    w["ffn_conv_b"] = [p["ffn_conv_b"][i].reshape(1, D_FF) for i in range(DEPTH)]
    w["ln1_g"] = [p["ln1_g"][i].reshape(1, D_MODEL) for i in range(DEPTH)]
    w["ln1_b"] = [p["ln1_b"][i].reshape(1, D_MODEL) for i in range(DEPTH)]
    w["ln2_g"] = [p["ln2_g"][i].reshape(1, D_MODEL) for i in range(DEPTH)]
    w["ln2_b"] = [p["ln2_b"][i].reshape(1, D_MODEL) for i in range(DEPTH)]
    return w


def _softplus_kernel(x_ref, o_ref):
    o_ref[...] = _softplus(-x_ref[...])


def _trunk(x, mods, w, sp, cache_a_k=None, cache_a_v=None, state_b_h=None, state_b_conv=None,
           cache_c_k=None, cache_c_v=None, cache_c_logf=None, state_ffn_conv=None):
    b, t, d = x.shape
    cached = cache_a_k is not None
    past = cache_a_k.shape[2] if cached else 0
    outs = {}
    for i in range(DEPTH):
        sh1, sc1, g1, sh2, sc2, g2 = [m[:, None, :] for m in jnp.split(mods[i], 6, axis=-1)]
        if i % 2 == 0:
            lam_init = 0.8 - 0.6 * math.exp(-0.3 * i)
            cos, sin = _rope_tables(past, t)
            q, kb, vt, k32, v32, xg = _proj_ab(x, sc1, sh1, w["in_ab"], w["in_ab_vt"], cos, sin)
            if cached:
                o = _cached_attention(q, kb, v32, cache_a_k[0].reshape(b, past, A_WIDTH),
                                      cache_a_v[0].reshape(b, past, A_WIDTH),
                                      "A", lam=w["lam"], gain=w["gain_row"], lam_init=lam_init)
                h0 = state_b_h[0][:, None, :]
                ctail = _pad_tail(state_b_conv[0])
            else:
                o = _attention(q, kb, vt, "A", lam=w["lam"], gain=w["gain_col"], lam_init=lam_init)
                h0 = jnp.zeros((b, 1, B_WIDTH), F32)
                ctail = jnp.zeros((b, SUBLANES, B_WIDTH), F32)
            yb, h_tail, x_tail = _rglru(xg, h0, ctail, w["b_conv_w"], w["b_conv_b"], w["rg_a"], w["b_rg_a"],
                                        w["rg_x"], w["b_rg_x"], sp)
            outs["a_k"] = k32.reshape(1, b, t, A_HEADS, 2 * A_HEAD_DIM)
            outs["a_v"] = v32.reshape(1, b, t, A_HEADS, 2 * A_HEAD_DIM)
            outs["b_h"] = h_tail[:, SUBLANES - 1, :][None]
            outs["b_conv"] = x_tail[:, SUBLANES - (B_CONV - 1):, :][None]
            x = _proj_ln([o, yb], w["out_ab"], x, g1, w["ln1_g"][i], w["ln1_b"][i])
        else:
            q, kb, bias, vt, k32, v32, lf, lft = _proj_c(x, sc1, sh1, w["in_c"], w["in_c_vt"], w["in_c_ft"],
                                                         w["bf_row"], w["bf_col"], w["place"])
            if cached:
                o = _cached_attention(q, kb, v32, cache_c_k[0].reshape(b, past, C_WIDTH),
                                      cache_c_v[0].reshape(b, past, C_WIDTH),
                                      "C", cache_lft=jnp.swapaxes(cache_c_logf[0], 1, 2), new_lft=lft)
            else:
                o = _attention(q, kb, vt, "C", bias=bias)
            outs["c_k"] = k32.reshape(1, b, t, C_HEADS, C_HEAD_DIM)
            outs["c_v"] = v32.reshape(1, b, t, C_HEADS, C_HEAD_DIM)
            outs["c_logf"] = lf[None]
            x = _proj_ln([o], w["out_c"], x, g1, w["ln1_g"][i], w["ln1_b"][i])
        ftail = _pad_tail(state_ffn_conv[i]) if cached else jnp.zeros((b, SUBLANES, D_FF), F32)
        hmid, g_tail = _ffn_up(x, sc2, sh2, w["up"][i], ftail, w["ffn_conv_w"][i], w["ffn_conv_b"][i])
        outs.setdefault("ffn", []).append(g_tail[:, SUBLANES - (FFN_CONV - 1):, :])
        x = _proj_ln([hmid], w["down"][i], x, g2, w["ln2_g"][i], w["ln2_b"][i])
    return (x, outs["a_k"], outs["a_v"], outs["b_h"], outs["b_conv"],
            outs["c_k"], outs["c_v"], outs["c_logf"], jnp.stack(outs["ffn"]))


def kernel(x_prompt, x_sample, c_prompt, c_sample, cache_a_k, cache_a_v, state_b_h, state_b_conv, cache_c_k, cache_c_v, cache_c_logf, state_ffn_conv, w_ada, b_ada, ln1_g, ln1_b, ln2_g, ln2_b, w_in_ab, lam_q1, lam_k1, lam_q2, lam_k2, attn_gain, b_conv_w, b_conv_b, w_rg_a, b_rg_a, w_rg_x, b_rg_x, rg_L, w_out_ab, w_in_c, b_f, w_out_c, w_up, ffn_conv_w, ffn_conv_b, w_down):
    p = dict(w_in_ab=w_in_ab, lam_q1=lam_q1, lam_k1=lam_k1, lam_q2=lam_q2, lam_k2=lam_k2, attn_gain=attn_gain,
             b_conv_w=b_conv_w, b_conv_b=b_conv_b, w_rg_a=w_rg_a, b_rg_a=b_rg_a, w_rg_x=w_rg_x, b_rg_x=b_rg_x,
             rg_L=rg_L, w_out_ab=w_out_ab, w_in_c=w_in_c, b_f=b_f, w_out_c=w_out_c, w_up=w_up,
             ffn_conv_w=ffn_conv_w, ffn_conv_b=ffn_conv_b, w_down=w_down,
             ln1_g=ln1_g, ln1_b=ln1_b, ln2_g=ln2_g, ln2_b=ln2_b)
    w = _prepare(p)
    bp = c_prompt.shape[0]
    bs = c_sample.shape[0]
    rows = -(-(bp + bs) // 16) * 16
    c_all = jnp.pad(jnp.concatenate([c_prompt, c_sample], axis=0), ((0, rows - bp - bs), (0, 0)))
    mods = _mods(c_all, w_ada, b_ada)
    sp = pl.pallas_call(_softplus_kernel, out_shape=jax.ShapeDtypeStruct((1, B_WIDTH), F32),
                        name="softplus")(w["rg_L"])
    res_p = _trunk(x_prompt, mods[:, :bp], w, sp)
    res_s = _trunk(x_sample, mods[:, bp:bp + bs], w, sp, cache_a_k, cache_a_v, state_b_h, state_b_conv,
                   cache_c_k, cache_c_v, cache_c_logf, state_ffn_conv)
    return (res_p[0], res_s[0]) + res_p[1:] + res_s[1:]
```

```python
import functools
import math

import numpy as np
import jax
import jax.numpy as jnp
from jax import lax
from jax.experimental import pallas as pl
from jax.experimental.pallas import tpu as pltpu

F32 = jnp.float32
BF16 = jnp.bfloat16

D_MODEL = 1024
DEPTH = 2
CHUNK = 64
CHUNK_SHIFT = 6
A_HEADS = 4
A_HEAD_DIM = 64
A_WIDTH = A_HEADS * 2 * A_HEAD_DIM
B_WIDTH = 512
B_BLOCKS = 8
B_CONV = 4
RG_C = 8.0
C_HEADS = 16
C_HEAD_DIM = 64
C_WIDTH = C_HEADS * C_HEAD_DIM
D_FF = 2816
FFN_CONV = 3
ROPE_THETA = 10000.0
ALPHA = (2 * DEPTH) ** 0.25
LN_EPS = 1e-5
NEG = -1e30
LOG2E = 1.4426950408889634

LANES = 128
SUBLANES = 8
BF16_ROWS = 16
MXU_DIM = 256
VMEM_LIMIT = 56 * 1024 * 1024
BIAS_WIDTH = (C_HEADS // 2) * LANES
ATTN_BLOCK = 512


def _params(sem, flags=None):
    return pltpu.CompilerParams(dimension_semantics=sem, vmem_limit_bytes=VMEM_LIMIT, flags=flags)


def _row_tile(t, pref):
    if t <= pref:
        return t
    tm = pref
    while t % tm:
        tm //= 2
    return tm


def _modulate(x, sc, sh):
    return (x * (1.0 + sc) + sh).astype(BF16)


def _sigmoid(x):
    return 1.0 / (1.0 + jnp.exp(-x))


def _softplus(x):
    return jnp.maximum(x, 0.0) + jnp.log1p(jnp.exp(-jnp.abs(x)))


def _log_sigmoid(x):
    return jnp.minimum(x, 0.0) - jnp.log1p(jnp.exp(-jnp.abs(x)))


def _split3(x):
    hi = x.astype(BF16)
    r1 = x - hi.astype(F32)
    mid = r1.astype(BF16)
    lo = (r1 - mid.astype(F32)).astype(BF16)
    return hi, mid, lo


def _dot(a, b):
    return jnp.dot(a, b, preferred_element_type=F32)


def _dot_nt(a, b):
    return lax.dot_general(a, b, (((1,), (1,)), ((), ())), preferred_element_type=F32)


def _mods_kernel(c_ref, w_ref, b_ref, o_ref):
    c = c_ref[...]
    s = (c * _sigmoid(c)).astype(BF16)
    o_ref[0] = _dot(s, w_ref[0].astype(BF16)) + b_ref[0]


def _mods(c_all, w_ada, b_ada):
    rows, d = c_all.shape
    n = w_ada.shape[-1]
    tn = 1536
    return pl.pallas_call(
        _mods_kernel,
        grid=(DEPTH, n // tn),
        in_specs=[pl.BlockSpec((rows, d), lambda l, j: (0, 0)),
                  pl.BlockSpec((1, d, tn), lambda l, j: (l, 0, j)),
                  pl.BlockSpec((1, 1, tn), lambda l, j: (l, 0, j))],
        out_specs=pl.BlockSpec((1, rows, tn), lambda l, j: (l, 0, j)),
        out_shape=jax.ShapeDtypeStruct((DEPTH, rows, n), F32),
        compiler_params=_params(("arbitrary", "arbitrary")),
        name="mods",
    )(c_all, w_ada, b_ada.reshape(DEPTH, 1, n))


def _rope_slab(x, cos, sin_signed, first_half):
    fwd = pltpu.roll(x, LANES - A_HEAD_DIM // 2, 1)
    bwd = pltpu.roll(x, A_HEAD_DIM // 2, 1)
    partner = jnp.where(first_half, fwd, bwd)
    return x * cos + partner * sin_signed


def _proj_ab_kernel(x_ref, sc_ref, sh_ref, w_ref, wvt_ref, cos_ref, sin_ref,
                    q_ref, kb_ref, vt_ref, k_ref, v_ref, xg_ref, *, q_scale):
    u = _modulate(x_ref[0], sc_ref[0], sh_ref[0])
    pr = _dot(u, w_ref[...])
    cos = cos_ref[...]
    sin = sin_ref[...]
    lane = lax.broadcasted_iota(jnp.int32, (1, LANES), 1)
    first_half = (lane & (A_HEAD_DIM - 1)) < (A_HEAD_DIM // 2)
    for h in range(A_HEADS):
        sl = slice(h * LANES, (h + 1) * LANES)
        q = _rope_slab(pr[:, sl], cos, sin, first_half)
        q_ref[0, :, sl] = (q * q_scale).astype(BF16)
        k = _rope_slab(pr[:, A_WIDTH + h * LANES:A_WIDTH + (h + 1) * LANES], cos, sin, first_half)
        k_ref[0, :, sl] = k
        kb_ref[0, :, sl] = k.astype(BF16)
    v_ref[0] = pr[:, 2 * A_WIDTH:3 * A_WIDTH]
    vt_ref[0, 0] = _dot_nt(wvt_ref[...], u).astype(BF16)
    xg_ref[0] = pr[:, 3 * A_WIDTH:]


def _proj_ab(x, sc, sh, w, wvt, cos, sin):
    b, t, d = x.shape
    tm = _row_tile(t, ATTN_BLOCK)
    nt = t // tm
    n = w.shape[1]
    q_scale = A_HEAD_DIM ** -0.5 * LOG2E
    row = lambda width: pl.BlockSpec((1, tm, width), lambda bi, i: (bi, i, 0))
    const = lambda shape: pl.BlockSpec(shape, lambda bi, i: tuple(0 for _ in shape))
    return pl.pallas_call(
        functools.partial(_proj_ab_kernel, q_scale=q_scale),
        grid=(b, nt),
        in_specs=[row(d),
                  pl.BlockSpec((1, 1, d), lambda bi, i: (bi, 0, 0)),
                  pl.BlockSpec((1, 1, d), lambda bi, i: (bi, 0, 0)),
                  const((d, n)), const((A_WIDTH, d)),
                  pl.BlockSpec((tm, LANES), lambda bi, i: (i, 0)),
                  pl.BlockSpec((tm, LANES), lambda bi, i: (i, 0))],
        out_specs=[row(A_WIDTH), row(A_WIDTH),
                   pl.BlockSpec((1, 1, A_WIDTH, tm), lambda bi, i: (bi, i, 0, 0)),
                   row(A_WIDTH), row(A_WIDTH), row(2 * B_WIDTH)],
        out_shape=[jax.ShapeDtypeStruct((b, t, A_WIDTH), BF16),
                   jax.ShapeDtypeStruct((b, t, A_WIDTH), BF16),
                   jax.ShapeDtypeStruct((b, nt, A_WIDTH, tm), BF16),
                   jax.ShapeDtypeStruct((b, t, A_WIDTH), F32),
                   jax.ShapeDtypeStruct((b, t, A_WIDTH), F32),
                   jax.ShapeDtypeStruct((b, t, 2 * B_WIDTH), F32)],
        compiler_params=_params(("arbitrary", "arbitrary")),
        name="proj_ab",
    )(x, sc, sh, w, wvt, cos, sin)


def _bias_placement():
    e = np.zeros((3 * LANES, BIAS_WIDTH), np.float32)
    for piece in range(3):
        for h in range(C_HEADS):
            e[piece * LANES + h, (h // 2) * LANES + 3 * (h % 2) + piece] = 1.0
    return jnp.asarray(e, BF16)


def _proj_c_kernel(x_ref, sc_ref, sh_ref, w_ref, wvt_ref, wft_ref, bfr_ref, bfc_ref, place_ref,
                   q_ref, kb_ref, bias_ref, vt_ref, k_ref, v_ref, lf_ref, lft_ref, run_ref, *, q_scale):
    i = pl.program_id(1)
    tm = x_ref.shape[1]

    @pl.when(i == 0)
    def _():
        run_ref[...] = jnp.zeros_like(run_ref)

    u = _modulate(x_ref[0], sc_ref[0], sh_ref[0])
    pr = _dot(u, w_ref[...])
    q_ref[0] = (pr[:, :C_WIDTH] * q_scale).astype(BF16)
    k = pr[:, C_WIDTH:2 * C_WIDTH]
    k_ref[0] = k
    kb_ref[0] = k.astype(BF16)
    v_ref[0] = pr[:, 2 * C_WIDTH:3 * C_WIDTH]
    vt_ref[0, 0] = _dot_nt(wvt_ref[...], u).astype(BF16)
    lf = _log_sigmoid(pr[:, 3 * C_WIDTH:] + bfr_ref[...])
    lf_ref[0] = lf[:, :C_HEADS]
    lft_ref[0] = _log_sigmoid(_dot_nt(wft_ref[...], u) + bfc_ref[...])
    r = lax.broadcasted_iota(jnp.int32, (tm, tm), 0)
    c = lax.broadcasted_iota(jnp.int32, (tm, tm), 1)
    lower = jnp.where(c <= r, 1.0, 0.0).astype(BF16)
    hi, mid, lo = _split3(lf)
    cum = (_dot(lower, hi) + _dot(lower, mid)) + _dot(lower, lo) + run_ref[...]
    pieces = jnp.concatenate(_split3(cum * (-LOG2E)), axis=1)
    bias_ref[0] = _dot(pieces, place_ref[...]).astype(BF16)
    run_ref[...] = run_ref[...] + jnp.sum(lf, axis=0, keepdims=True)


def _proj_c(x, sc, sh, w, wvt, wft, bf_row, bf_col, place):
    b, t, d = x.shape
    tm = _row_tile(t, ATTN_BLOCK)
    nt = t // tm
    n = w.shape[1]
    q_scale = C_HEAD_DIM ** -0.5 * LOG2E
    row = lambda width: pl.BlockSpec((1, tm, width), lambda bi, i: (bi, i, 0))
    const = lambda shape: pl.BlockSpec(shape, lambda bi, i: tuple(0 for _ in shape))
    return pl.pallas_call(
        functools.partial(_proj_c_kernel, q_scale=q_scale),
        grid=(b, nt),
        in_specs=[row(d),
                  pl.BlockSpec((1, 1, d), lambda bi, i: (bi, 0, 0)),
                  pl.BlockSpec((1, 1, d), lambda bi, i: (bi, 0, 0)),
                  const((d, n)), const((C_WIDTH, d)), const((C_HEADS, d)),
                  const((1, LANES)), const((C_HEADS, 1)), const((3 * LANES, BIAS_WIDTH))],
        out_specs=[row(C_WIDTH), row(C_WIDTH), row(BIAS_WIDTH),
                   pl.BlockSpec((1, 1, C_WIDTH, tm), lambda bi, i: (bi, i, 0, 0)),
                   row(C_WIDTH), row(C_WIDTH), row(C_HEADS),
                   pl.BlockSpec((1, C_HEADS, tm), lambda bi, i: (bi, 0, i))],
        out_shape=[jax.ShapeDtypeStruct((b, t, C_WIDTH), BF16),
                   jax.ShapeDtypeStruct((b, t, C_WIDTH), BF16),
                   jax.ShapeDtypeStruct((b, t, BIAS_WIDTH), BF16),
                   jax.ShapeDtypeStruct((b, nt, C_WIDTH, tm), BF16),
                   jax.ShapeDtypeStruct((b, t, C_WIDTH), F32),
                   jax.ShapeDtypeStruct((b, t, C_WIDTH), F32),
                   jax.ShapeDtypeStruct((b, t, C_HEADS), F32),
                   jax.ShapeDtypeStruct((b, C_HEADS, t), F32)],
        scratch_shapes=[pltpu.VMEM((1, LANES), F32)],
        compiler_params=_params(("arbitrary", "arbitrary")),
        name="proj_c",
    )(x, sc, sh, w, wvt, wft, bf_row, bf_col, place)


def _lambda(lam_ref, lam_init):
    lq1, lk1, lq2, lk2 = (lam_ref[r:r + 1, :] for r in range(4))
    return (jnp.exp(jnp.sum(lq1 * lk1, axis=1, keepdims=True))
            - jnp.exp(jnp.sum(lq2 * lk2, axis=1, keepdims=True)) + lam_init)


def _attn_kernel(*refs, mode, lam_init):
    if mode == "A":
        q_ref, k_ref, vt_ref, lam_ref, gain_ref, o_ref, qc_ref, m_ref, acc_ref, s0_ref, s1_ref, c0_ref, c1_ref = refs
        bias_ref = None
        d_val = 2 * A_HEAD_DIM
    else:
        q_ref, k_ref, bias_ref, vt_ref, o_ref, qc_ref, m_ref, acc_ref, s0_ref, s1_ref, c0_ref, c1_ref = refs
        d_val = C_HEAD_DIM
    tq = q_ref.shape[1]
    tk = k_ref.shape[2]
    qi = pl.program_id(2)
    lane = lax.broadcasted_iota(jnp.int32, (1, LANES), 1)
    low = lane < (LANES // 2)
    q = q_ref[0]
    zero = jnp.zeros_like(q)
    for a in range(2):
        qa = jnp.where(low, q, zero) if a == 0 else jnp.where(low, zero, q)
        if bias_ref is not None:
            pick = jnp.where((lane >= 3 * a) & (lane < 3 * a + 3), 1.0, 0.0).astype(BF16)
            qa = jnp.concatenate([qa, jnp.broadcast_to(pick, (tq, LANES))], axis=1)
        qc_ref[a] = qa
    m_ref[...] = jnp.full(m_ref.shape, NEG, F32)
    acc_ref[...] = jnp.zeros(acc_ref.shape, F32)
    ones = jnp.ones((BF16_ROWS, tk), BF16)

    units = [(a, slice(n * MXU_DIM, (n + 1) * MXU_DIM)) for a in range(2) for n in range(tq // MXU_DIM)]

    def score_chain(j, u):
        a, cs = units[u]
        kc = k_ref[0, j]
        if bias_ref is not None:
            kc = jnp.concatenate([kc, bias_ref[0, j]], axis=1)
        return _dot_nt(kc, qc_ref[a, cs, :])

    def value_chain(j, u, st, cmax, masked):
        a, cs = units[u]
        if masked:
            key = lax.broadcasted_iota(jnp.int32, (tk, MXU_DIM), 0)
            qry = lax.broadcasted_iota(jnp.int32, (tk, MXU_DIM), 1) + cs.start
            keep = ((key >> CHUNK_SHIFT) <= (qry >> CHUNK_SHIFT)) if mode == "A" else (key <= qry)
            st = jnp.where(keep, st, NEG)
            cmax = jnp.max(st, axis=0, keepdims=True)
        m_prev = m_ref[a, :, cs]
        m_new = jnp.maximum(m_prev, cmax)
        alpha = jnp.exp2(m_prev - m_new)
        pt = jnp.exp2(st - m_new).astype(BF16)
        vt = vt_ref[0, j]
        va = vt if mode == "A" else vt[a * d_val:(a + 1) * d_val]
        va = jnp.concatenate([va, ones], axis=0)
        acc_ref[a, :, cs] = alpha * acc_ref[a, :, cs] + _dot(va, pt)
        m_ref[a, :, cs] = m_new

    def stage(j_scores, dst, j_values, src, masked=False):
        for t in range(len(units) + 1):
            if j_scores is not None and t < len(units):
                st = score_chain(j_scores, t)
                dst[0][t] = st
                dst[1][t] = jnp.max(st, axis=0, keepdims=True)
            if j_values is not None and t >= 1:
                value_chain(j_values, t - 1, src[0][t - 1], src[1][t - 1], masked)

    buf0 = (s0_ref, c0_ref)
    buf1 = (s1_ref, c1_ref)
    stage(0, buf0, None, None)

    def pair(j):
        stage(j + 1, buf1, j, buf0)
        stage(j + 2, buf0, j + 1, buf1)

    def quad(p, carry):
        pair(4 * p)
        pair(4 * p + 2)
        return carry

    lax.fori_loop(0, qi >> 2, quad, 0)

    @pl.when((qi & 2) == 2)
    def _():
        pair(qi & ~3)

    @pl.when((qi & 1) == 1)
    def _():
        stage(qi, buf1, qi - 1, buf0)
        stage(None, None, qi, buf1, masked=True)

    @pl.when((qi & 1) == 0)
    def _():
        stage(None, None, qi, buf0, masked=True)

    outs = []
    for a in range(2):
        acc = acc_ref[a]
        outs.append(acc[:d_val] * (1.0 / acc[d_val:d_val + 1]))
    if mode == "A":
        ot = outs[0] - _lambda(lam_ref, lam_init) * outs[1]
        ot = ot * lax.rsqrt(jnp.mean(ot * ot, axis=0, keepdims=True) + LN_EPS)
        ot = ot * (gain_ref[0] * (1.0 - lam_init))
    else:
        ot = jnp.concatenate(outs, axis=0)
    o_ref[0] = ot.T.astype(o_ref.dtype)


def _attention(q, k, vt, mode, lam=None, gain=None, bias=None, lam_init=0.0):
    b, t, width = q.shape
    nk, tk = vt.shape[1], vt.shape[3]
    tq = tk
    groups = width // LANES
    d_aug = (2 * A_HEAD_DIM if mode == "A" else C_HEAD_DIM) + BF16_ROWS
    kspec = pl.BlockSpec((1, nk, tk, LANES), lambda bi, g, i: (bi, 0, 0, g))
    in_specs = [pl.BlockSpec((1, tq, LANES), lambda bi, g, i: (bi, i, g)), kspec]
    args = [q, k.reshape(b, nk, tk, width)]
    if mode == "C":
        in_specs.append(kspec)
        args.append(bias.reshape(b, nk, tk, groups * LANES))
    in_specs.append(pl.BlockSpec((1, nk, LANES, tk), lambda bi, g, i: (bi, 0, g, 0)))
    args.append(vt)
    if mode == "A":
        in_specs += [pl.BlockSpec((4, A_HEAD_DIM), lambda bi, g, i: (0, 0)),
                     pl.BlockSpec((1, LANES, 1), lambda bi, g, i: (g, 0, 0))]
        args += [lam, gain]
    return pl.pallas_call(
        functools.partial(_attn_kernel, mode=mode, lam_init=lam_init),
        grid=(b, groups, t // tq),
        in_specs=in_specs,
        out_specs=pl.BlockSpec((1, tq, LANES), lambda bi, g, i: (bi, i, g)),
        out_shape=jax.ShapeDtypeStruct((b, t, width), BF16),
        scratch_shapes=[pltpu.VMEM((2, tq, LANES if mode == "A" else 2 * LANES), BF16),
                        pltpu.VMEM((2, 1, tq), F32),
                        pltpu.VMEM((2, d_aug, tq), F32),
                        pltpu.VMEM((2 * tq // MXU_DIM, tk, MXU_DIM), F32),
                        pltpu.VMEM((2 * tq // MXU_DIM, tk, MXU_DIM), F32),
                        pltpu.VMEM((2 * tq // MXU_DIM, 1, MXU_DIM), F32),
                        pltpu.VMEM((2 * tq // MXU_DIM, 1, MXU_DIM), F32)],
        compiler_params=_params(("arbitrary", "arbitrary", "arbitrary")),
        name="attn_" + mode,
    )(*args)


def _diff_finish(o0, o1, lam, gain, lam_init):
    o = o0 - lam * o1
    o = o * lax.rsqrt(jnp.mean(o * o, axis=-1, keepdims=True) + LN_EPS)
    return o * gain * (1.0 - lam_init)


def _cached_attn_kernel(*refs, mode, past, lam_init):
    if mode == "A":
        q_ref, kn_ref, vn_ref, ck_ref, cv_ref, lam_ref, gain_ref, o_ref = refs
    else:
        q_ref, kn_ref, vn_ref, ck_ref, cv_ref, clf_ref, nlf_ref, o_ref = refs
    t = q_ref.shape[1]
    width = o_ref.shape[2]
    groups = width // LANES
    lane = lax.broadcasted_iota(jnp.int32, (1, LANES), 1)
    low = lane < (LANES // 2)
    row = lax.broadcasted_iota(jnp.int32, (t, t), 0)
    col = lax.broadcasted_iota(jnp.int32, (t, t), 1)
    if mode == "A":
        keep = ((past + col) >> CHUNK_SHIFT) <= ((past + row) >> CHUNK_SHIFT)
        lam = _lambda(lam_ref, lam_init)
    else:
        keep = col <= row
        r = lax.broadcasted_iota(jnp.int32, (past, past), 0)
        c = lax.broadcasted_iota(jnp.int32, (past, past), 1)
        upper = jnp.where(r <= c, 1.0, 0.0).astype(BF16)
        clf = clf_ref[0]
        hi, mid, lo = _split3(clf)
        cum_c = (_dot(hi, upper) + _dot(mid, upper)) + _dot(lo, upper)
        upper_n = jnp.where(row <= col, 1.0, 0.0).astype(BF16)
        hi, mid, lo = _split3(nlf_ref[0])
        cum_n = ((_dot(hi, upper_n) + _dot(mid, upper_n)) + _dot(lo, upper_n)
                 + jnp.sum(clf, axis=1, keepdims=True))
        bias_c = cum_c * (-LOG2E)
        bias_n = cum_n * (-LOG2E)
    for g in range(groups):
        sl = slice(g * LANES, (g + 1) * LANES)
        q = q_ref[0, :, sl]
        kn = kn_ref[0, :, sl]
        vn = vn_ref[0, :, sl].astype(BF16)
        kc = ck_ref[0, :, sl].astype(BF16)
        vc = cv_ref[0, :, sl].astype(BF16)
        outs = []
        for a in range(2):
            qa = jnp.where(low, q, jnp.zeros_like(q)) if a == 0 else jnp.where(low, jnp.zeros_like(q), q)
            s_c = _dot_nt(qa, kc)
            s_n = _dot_nt(qa, kn)
            if mode == "C":
                h = 2 * g + a
                s_c = s_c + bias_c[h:h + 1, :]
                s_n = s_n + bias_n[h:h + 1, :]
            s_n = jnp.where(keep, s_n, NEG)
            m = jnp.maximum(jnp.max(s_c, axis=1, keepdims=True), jnp.max(s_n, axis=1, keepdims=True))
            p_c = jnp.exp2(s_c - m)
            p_n = jnp.exp2(s_n - m)
            l = jnp.sum(p_c, axis=1, keepdims=True) + jnp.sum(p_n, axis=1, keepdims=True)
            acc = _dot(p_c.astype(BF16), vc) + _dot(p_n.astype(BF16), vn)
            outs.append(acc * (1.0 / l))
        if mode == "A":
            o = _diff_finish(outs[0], outs[1], lam, gain_ref[g], lam_init)
        else:
            o = jnp.where(low, outs[0], outs[1])
        o_ref[0, :, sl] = o.astype(o_ref.dtype)


def _cached_attention(q, kn, vn, cache_k, cache_v, mode, lam=None, gain=None, cache_lft=None, new_lft=None,
                      lam_init=0.0):
    b, t, width = q.shape
    past = cache_k.shape[1]
    new = pl.BlockSpec((1, t, width), lambda bi: (bi, 0, 0))
    old = pl.BlockSpec((1, past, width), lambda bi: (bi, 0, 0))
    in_specs = [new, new, new, old, old]
    args = [q, kn, vn, cache_k, cache_v]
    if mode == "A":
        in_specs += [pl.BlockSpec((4, A_HEAD_DIM), lambda bi: (0, 0)),
                     pl.BlockSpec((A_HEADS, 1, LANES), lambda bi: (0, 0, 0))]
        args += [lam, gain]
    else:
        in_specs += [pl.BlockSpec((1, C_HEADS, past), lambda bi: (bi, 0, 0)),
                     pl.BlockSpec((1, C_HEADS, t), lambda bi: (bi, 0, 0))]
        args += [cache_lft, new_lft]
    return pl.pallas_call(
        functools.partial(_cached_attn_kernel, mode=mode, past=past, lam_init=lam_init),
        grid=(b,),
        in_specs=in_specs,
        out_specs=new,
        out_shape=jax.ShapeDtypeStruct((b, t, width), BF16),
        compiler_params=_params(("arbitrary",)),
        name="cached_attn_" + mode,
    )(*args)


def _gelu_tanh(x):
    return 0.5 * x * (1.0 + jnp.tanh(math.sqrt(2.0 / math.pi) * (x + 0.044715 * (x * x * x))))


def _causal_conv(x, tail, w_ref, b_ref):
    width = w_ref.shape[0]
    tm = x.shape[0]
    cat = jnp.concatenate([tail, x], axis=0)
    y = None
    for j in range(width):
        back = width - 1 - j
        src = cat if back == 0 else pltpu.roll(cat, back, 0)
        term = src[SUBLANES:SUBLANES + tm] * w_ref[j:j + 1, :]
        y = term if y is None else y + term
    return y + b_ref[...]


def _rglru_kernel(xg_ref, h0_ref, cb_ref, cw_ref, cbias_ref, wa_ref, ba_ref, wx_ref, bx_ref, sp_ref,
                  y_ref, hl_ref, ct_ref, hc_ref, tail_ref):
    i = pl.program_id(1)
    tm = xg_ref.shape[1]

    @pl.when(i == 0)
    def _():
        hc_ref[...] = h0_ref[0]
        tail_ref[...] = cb_ref[0]

    x = xg_ref[0, :, :B_WIDTH]
    gate_in = xg_ref[0, :, B_WIDTH:]
    xc = _causal_conv(x, tail_ref[...], cw_ref, cbias_ref)
    xcb = xc.astype(BF16)
    r = _sigmoid(_dot(xcb, wa_ref[...]) + ba_ref[...])
    ig = _sigmoid(_dot(xcb, wx_ref[...]) + bx_ref[...])
    log_a = (-RG_C) * r * sp_ref[...]
    a = jnp.exp(log_a)
    th = jnp.tanh(log_a)
    bx = jnp.sqrt((-2.0 * th) / (1.0 - th)) * (ig * xc)
    row = lax.broadcasted_iota(jnp.int32, (tm, 1), 0)
    s = 1
    while s < tm:
        valid = row >= s
        a_sh = pltpu.roll(a, s, 0)
        b_sh = pltpu.roll(bx, s, 0)
        bx = jnp.where(valid, a * b_sh + bx, bx)
        a = jnp.where(valid, a * a_sh, a)
        s *= 2
    h = a * hc_ref[...] + bx
    y_ref[0] = (h * _gelu_tanh(gate_in)).astype(y_ref.dtype)
    h_tail = h[tm - SUBLANES:, :]
    x_tail = x[tm - SUBLANES:, :]
    hc_ref[...] = h_tail[SUBLANES - 1:, :]
    tail_ref[...] = x_tail
    hl_ref[0] = h_tail
    ct_ref[0] = x_tail


def _rglru(xg, h0, conv_tail, cw, cbias, wa, ba, wx, bx, sp, tm_pref=256):
    b, t, _ = xg.shape
    tm = _row_tile(t, tm_pref)
    wspec = lambda shape: pl.BlockSpec(shape, lambda bi, i: tuple(0 for _ in shape))
    return pl.pallas_call(
        _rglru_kernel,
        grid=(b, t // tm),
        in_specs=[pl.BlockSpec((1, tm, 2 * B_WIDTH), lambda bi, i: (bi, i, 0)),
                  pl.BlockSpec((1, 1, B_WIDTH), lambda bi, i: (bi, 0, 0)),
                  pl.BlockSpec((1, SUBLANES, B_WIDTH), lambda bi, i: (bi, 0, 0)),
                  wspec((B_CONV, B_WIDTH)), wspec((1, B_WIDTH)),
                  wspec((B_WIDTH, B_WIDTH)), wspec((1, B_WIDTH)),
                  wspec((B_WIDTH, B_WIDTH)), wspec((1, B_WIDTH)), wspec((1, B_WIDTH))],
        out_specs=[pl.BlockSpec((1, tm, B_WIDTH), lambda bi, i: (bi, i, 0)),
                   pl.BlockSpec((1, SUBLANES, B_WIDTH), lambda bi, i: (bi, 0, 0)),
                   pl.BlockSpec((1, SUBLANES, B_WIDTH), lambda bi, i: (bi, 0, 0))],
        out_shape=[jax.ShapeDtypeStruct((b, t, B_WIDTH), BF16),
                   jax.ShapeDtypeStruct((b, SUBLANES, B_WIDTH), F32),
                   jax.ShapeDtypeStruct((b, SUBLANES, B_WIDTH), F32)],
        scratch_shapes=[pltpu.VMEM((1, B_WIDTH), F32), pltpu.VMEM((SUBLANES, B_WIDTH), F32)],
        compiler_params=_params(("arbitrary", "arbitrary")),
        name="rglru",
    )(xg, h0, conv_tail, cw, cbias, wa, ba, wx, bx, sp)


def _proj_ln_kernel(*refs, n_in):
    h_refs = refs[:n_in]
    w_ref, x_ref, gate_ref, g_ref, b_ref, o_ref = refs[n_in:]
    hs = [r[0] for r in h_refs]
    h = hs[0] if n_in == 1 else jnp.concatenate(hs, axis=-1)
    y = ALPHA * x_ref[0] + gate_ref[0] * _dot(h, w_ref[...])
    mu = jnp.mean(y, axis=-1, keepdims=True)
    yc = y - mu
    var = jnp.mean(yc * yc, axis=-1, keepdims=True)
    o_ref[0] = yc * lax.rsqrt(var + LN_EPS) * g_ref[...] + b_ref[...]


def _proj_ln(hs, w, x, gate, ln_g, ln_b, tm_pref=512):
    b, t, d = x.shape
    tm = _row_tile(t, tm_pref)
    k = w.shape[0]
    in_specs = [pl.BlockSpec((1, tm, h.shape[2]), lambda bi, i: (bi, i, 0)) for h in hs]
    in_specs += [pl.BlockSpec((k, d), lambda bi, i: (0, 0)),
                 pl.BlockSpec((1, tm, d), lambda bi, i: (bi, i, 0)),
                 pl.BlockSpec((1, 1, d), lambda bi, i: (bi, 0, 0)),
                 pl.BlockSpec((1, d), lambda bi, i: (0, 0)),
                 pl.BlockSpec((1, d), lambda bi, i: (0, 0))]
    return pl.pallas_call(
        functools.partial(_proj_ln_kernel, n_in=len(hs)),
        grid=(b, t // tm),
        in_specs=in_specs,
        out_specs=pl.BlockSpec((1, tm, d), lambda bi, i: (bi, i, 0)),
        out_shape=jax.ShapeDtypeStruct((b, t, d), F32),
        compiler_params=_params(("arbitrary", "arbitrary")),
        name="proj_ln",
    )(*hs, w, x, gate, ln_g, ln_b)


def _ffn_up_kernel(x_ref, sc_ref, sh_ref, w_ref, cb_ref, cw_ref, cbias_ref, h_ref, ct_ref, tail_ref):
    i = pl.program_id(1)
    tm = x_ref.shape[1]

    @pl.when(i == 0)
    def _():
        tail_ref[...] = cb_ref[0]

    u = _modulate(x_ref[0], sc_ref[0], sh_ref[0])
    up = _dot(u, w_ref[...])
    a = up[:, :D_FF]
    g = up[:, D_FF:]
    gc = _causal_conv(g, tail_ref[...], cw_ref, cbias_ref)
    h_ref[0] = (a * (gc * _sigmoid(gc))).astype(h_ref.dtype)
    g_tail = g[tm - SUBLANES:, :]
    tail_ref[...] = g_tail
    ct_ref[0] = g_tail


def _ffn_up(x, sc, sh, w, conv_tail, cw, cbias, tm_pref=256):
    b, t, d = x.shape
    tm = _row_tile(t, tm_pref)
    return pl.pallas_call(
        _ffn_up_kernel,
        grid=(b, t // tm),
        in_specs=[pl.BlockSpec((1, tm, d), lambda bi, i: (bi, i, 0)),
                  pl.BlockSpec((1, 1, d), lambda bi, i: (bi, 0, 0)),
                  pl.BlockSpec((1, 1, d), lambda bi, i: (bi, 0, 0)),
                  pl.BlockSpec((d, 2 * D_FF), lambda bi, i: (0, 0)),
                  pl.BlockSpec((1, SUBLANES, D_FF), lambda bi, i: (bi, 0, 0)),
                  pl.BlockSpec((FFN_CONV, D_FF), lambda bi, i: (0, 0)),
                  pl.BlockSpec((1, D_FF), lambda bi, i: (0, 0))],
        out_specs=[pl.BlockSpec((1, tm, D_FF), lambda bi, i: (bi, i, 0)),
                   pl.BlockSpec((1, SUBLANES, D_FF), lambda bi, i: (bi, 0, 0))],
        out_shape=[jax.ShapeDtypeStruct((b, t, D_FF), BF16),
                   jax.ShapeDtypeStruct((b, SUBLANES, D_FF), F32)],
        scratch_shapes=[pltpu.VMEM((SUBLANES, D_FF), F32)],
        compiler_params=_params(("arbitrary", "arbitrary")),
        name="ffn_up",
    )(x, sc, sh, w, conv_tail, cw, cbias)


def _pad_tail(buf):
    return jnp.pad(buf, ((0, 0), (SUBLANES - buf.shape[1], 0), (0, 0)))


def _rope_tables(past, t):
    half = A_HEAD_DIM // 2
    inv = ROPE_THETA ** (-jnp.arange(0, A_HEAD_DIM, 2, dtype=F32) / A_HEAD_DIM)
    pos = (past + jnp.arange(t, dtype=jnp.int32)).astype(F32)
    ang = pos[:, None] * inv[None, :]
    cos = jnp.tile(jnp.cos(ang), (1, LANES // half))
    sin = jnp.sin(ang)
    sin_signed = jnp.tile(jnp.concatenate([-sin, sin], axis=1), (1, LANES // A_HEAD_DIM))
    return cos, sin_signed


def _block_diag(w):
    n, i, o = w.shape
    return jnp.einsum("nio,nm->nimo", w, jnp.eye(n, dtype=w.dtype)).reshape(n * i, n * o)


def _prepare(p):
    w = {}
    w_in_ab = p["w_in_ab"][0]
    w["in_ab"] = w_in_ab.astype(BF16)
    w["in_ab_vt"] = w_in_ab[:, 2 * A_WIDTH:3 * A_WIDTH].T.astype(BF16)
    w["out_ab"] = p["w_out_ab"][0].astype(BF16)
    w["lam"] = jnp.stack([p["lam_q1"][0], p["lam_k1"][0], p["lam_q2"][0], p["lam_k2"][0]])
    w["gain_row"] = p["attn_gain"][0].reshape(A_HEADS, 1, LANES)
    w["gain_col"] = p["attn_gain"][0].reshape(A_HEADS, LANES, 1)
    w["b_conv_w"] = p["b_conv_w"][0]
    w["b_conv_b"] = p["b_conv_b"][0].reshape(1, B_WIDTH)
    w["rg_a"] = _block_diag(p["w_rg_a"][0]).astype(BF16)
    w["rg_x"] = _block_diag(p["w_rg_x"][0]).astype(BF16)
    w["b_rg_a"] = p["b_rg_a"][0].reshape(1, B_WIDTH)
    w["b_rg_x"] = p["b_rg_x"][0].reshape(1, B_WIDTH)
    w["rg_L"] = p["rg_L"][0].reshape(1, B_WIDTH)
    w_in_c = p["w_in_c"][0]
    w["in_c"] = jnp.pad(w_in_c, ((0, 0), (0, LANES - C_HEADS))).astype(BF16)
    w["in_c_vt"] = w_in_c[:, 2 * C_WIDTH:3 * C_WIDTH].T.astype(BF16)
    w["in_c_ft"] = w_in_c[:, 3 * C_WIDTH:].T.astype(BF16)
    w["bf_row"] = jnp.pad(p["b_f"][0], (0, LANES - C_HEADS)).reshape(1, LANES)
    w["bf_col"] = p["b_f"][0].reshape(C_HEADS, 1)
    w["place"] = _bias_placement()
    w["out_c"] = p["w_out_c"][0].astype(BF16)
    w["up"] = [p["w_up"][i].astype(BF16) for i in range(DEPTH)]
    w["down"] = [p["w_down"][i].astype(BF16) for i in range(DEPTH)]
    w["ffn_conv_w"] = [p["ffn_conv_w"][i] for i in range(DEPTH)]
    w["ffn_conv_b"] = [p["ffn_conv_b"][i].reshape(1, D_FF) for i in range(DEPTH)]
    w["ln1_g"] = [p["ln1_g"][i].reshape(1, D_MODEL) for i in range(DEPTH)]
    w["ln1_b"] = [p["ln1_b"][i].reshape(1, D_MODEL) for i in range(DEPTH)]
    w["ln2_g"] = [p["ln2_g"][i].reshape(1, D_MODEL) for i in range(DEPTH)]
    w["ln2_b"] = [p["ln2_b"][i].reshape(1, D_MODEL) for i in range(DEPTH)]
    return w


def _softplus_kernel(x_ref, o_ref):
    o_ref[...] = _softplus(-x_ref[...])


def _trunk(x, mods, w, sp, cache_a_k=None, cache_a_v=None, state_b_h=None, state_b_conv=None,
           cache_c_k=None, cache_c_v=None, cache_c_logf=None, state_ffn_conv=None):
    b, t, d = x.shape
    cached = cache_a_k is not None
    past = cache_a_k.shape[2] if cached else 0
    outs = {}
    for i in range(DEPTH):
        sh1, sc1, g1, sh2, sc2, g2 = [m[:, None, :] for m in jnp.split(mods[i], 6, axis=-1)]
        if i % 2 == 0:
            lam_init = 0.8 - 0.6 * math.exp(-0.3 * i)
            cos, sin = _rope_tables(past, t)
            q, kb, vt, k32, v32, xg = _proj_ab(x, sc1, sh1, w["in_ab"], w["in_ab_vt"], cos, sin)
            if cached:
                o = _cached_attention(q, kb, v32, cache_a_k[0].reshape(b, past, A_WIDTH),
                                      cache_a_v[0].reshape(b, past, A_WIDTH),
                                      "A", lam=w["lam"], gain=w["gain_row"], lam_init=lam_init)
                h0 = state_b_h[0][:, None, :]
                ctail = _pad_tail(state_b_conv[0])
            else:
                o = _attention(q, kb, vt, "A", lam=w["lam"], gain=w["gain_col"], lam_init=lam_init)
                h0 = jnp.zeros((b, 1, B_WIDTH), F32)
                ctail = jnp.zeros((b, SUBLANES, B_WIDTH), F32)
            yb, h_tail, x_tail = _rglru(xg, h0, ctail, w["b_conv_w"], w["b_conv_b"], w["rg_a"], w["b_rg_a"],
                                        w["rg_x"], w["b_rg_x"], sp)
            outs["a_k"] = k32.reshape(1, b, t, A_HEADS, 2 * A_HEAD_DIM)
            outs["a_v"] = v32.reshape(1, b, t, A_HEADS, 2 * A_HEAD_DIM)
            outs["b_h"] = h_tail[:, SUBLANES - 1, :][None]
            outs["b_conv"] = x_tail[:, SUBLANES - (B_CONV - 1):, :][None]
            x = _proj_ln([o, yb], w["out_ab"], x, g1, w["ln1_g"][i], w["ln1_b"][i])
        else:
            q, kb, bias, vt, k32, v32, lf, lft = _proj_c(x, sc1, sh1, w["in_c"], w["in_c_vt"], w["in_c_ft"],
                                                         w["bf_row"], w["bf_col"], w["place"])
            if cached:
                o = _cached_attention(q, kb, v32, cache_c_k[0].reshape(b, past, C_WIDTH),
                                      cache_c_v[0].reshape(b, past, C_WIDTH),
                                      "C", cache_lft=jnp.swapaxes(cache_c_logf[0], 1, 2), new_lft=lft)
            else:
                o = _attention(q, kb, vt, "C", bias=bias)
            outs["c_k"] = k32.reshape(1, b, t, C_HEADS, C_HEAD_DIM)
            outs["c_v"] = v32.reshape(1, b, t, C_HEADS, C_HEAD_DIM)
            outs["c_logf"] = lf[None]
            x = _proj_ln([o], w["out_c"], x, g1, w["ln1_g"][i], w["ln1_b"][i])
        ftail = _pad_tail(state_ffn_conv[i]) if cached else jnp.zeros((b, SUBLANES, D_FF), F32)
        hmid, g_tail = _ffn_up(x, sc2, sh2, w["up"][i], ftail, w["ffn_conv_w"][i], w["ffn_conv_b"][i])
        outs.setdefault("ffn", []).append(g_tail[:, SUBLANES - (FFN_CONV - 1):, :])
        x = _proj_ln([hmid], w["down"][i], x, g2, w["ln2_g"][i], w["ln2_b"][i])
    return (x, outs["a_k"], outs["a_v"], outs["b_h"], outs["b_conv"],
            outs["c_k"], outs["c_v"], outs["c_logf"], jnp.stack(outs["ffn"]))


def kernel(x_prompt, x_sample, c_prompt, c_sample, cache_a_k, cache_a_v, state_b_h, state_b_conv, cache_c_k, cache_c_v, cache_c_logf, state_ffn_conv, w_ada, b_ada, ln1_g, ln1_b, ln2_g, ln2_b, w_in_ab, lam_q1, lam_k1, lam_q2, lam_k2, attn_gain, b_conv_w, b_conv_b, w_rg_a, b_rg_a, w_rg_x, b_rg_x, rg_L, w_out_ab, w_in_c, b_f, w_out_c, w_up, ffn_conv_w, ffn_conv_b, w_down):
    p = dict(w_in_ab=w_in_ab, lam_q1=lam_q1, lam_k1=lam_k1, lam_q2=lam_q2, lam_k2=lam_k2, attn_gain=attn_gain,
             b_conv_w=b_conv_w, b_conv_b=b_conv_b, w_rg_a=w_rg_a, b_rg_a=b_rg_a, w_rg_x=w_rg_x, b_rg_x=b_rg_x,
             rg_L=rg_L, w_out_ab=w_out_ab, w_in_c=w_in_c, b_f=b_f, w_out_c=w_out_c, w_up=w_up,
             ffn_conv_w=ffn_conv_w, ffn_conv_b=ffn_conv_b, w_down=w_down,
             ln1_g=ln1_g, ln1_b=ln1_b, ln2_g=ln2_g, ln2_b=ln2_b)
    w = _prepare(p)
    bp = c_prompt.shape[0]
    bs = c_sample.shape[0]
    rows = -(-(bp + bs) // 16) * 16
    c_all = jnp.pad(jnp.concatenate([c_prompt, c_sample], axis=0), ((0, rows - bp - bs), (0, 0)))
    mods = _mods(c_all, w_ada, b_ada)
    sp = pl.pallas_call(_softplus_kernel, out_shape=jax.ShapeDtypeStruct((1, B_WIDTH), F32),
                        name="softplus")(w["rg_L"])
    res_p = _trunk(x_prompt, mods[:, :bp], w, sp)
    res_s = _trunk(x_sample, mods[:, bp:bp + bs], w, sp, cache_a_k, cache_a_v, state_b_h, state_b_conv,
                   cache_c_k, cache_c_v, cache_c_logf, state_ffn_conv)
    return (res_p[0], res_s[0]) + res_p[1:] + res_s[1:]
```

```python
import functools
import math

import numpy as np
import jax
import jax.numpy as jnp
from jax import lax
from jax.experimental import pallas as pl
from jax.experimental.pallas import tpu as pltpu

F32 = jnp.float32
BF16 = jnp.bfloat16

D_MODEL = 1024
DEPTH = 2
CHUNK = 64
CHUNK_SHIFT = 6
A_HEADS = 4
A_HEAD_DIM = 64
A_WIDTH = A_HEADS * 2 * A_HEAD_DIM
B_WIDTH = 512
B_BLOCKS = 8
B_CONV = 4
RG_C = 8.0
C_HEADS = 16
C_HEAD_DIM = 64
C_WIDTH = C_HEADS * C_HEAD_DIM
D_FF = 2816
FFN_CONV = 3
ROPE_THETA = 10000.0
ALPHA = (2 * DEPTH) ** 0.25
LN_EPS = 1e-5
NEG = -1e30
LOG2E = 1.4426950408889634

LANES = 128
SUBLANES = 8
BF16_ROWS = 16
MXU_DIM = 256
VMEM_LIMIT = 56 * 1024 * 1024
BIAS_WIDTH = (C_HEADS // 2) * LANES
FFN_CHUNK = 256
LN_ROWS = 256
ATTN_BLOCK = 512


def _params(sem, flags=None):
    return pltpu.CompilerParams(dimension_semantics=sem, vmem_limit_bytes=VMEM_LIMIT, flags=flags)


def _row_tile(t, pref):
    if t <= pref:
        return t
    tm = pref
    while t % tm:
        tm //= 2
    return tm


def _modulate(x, sc, sh):
    return (x * (1.0 + sc) + sh).astype(BF16)


def _sigmoid(x):
    return 1.0 / (1.0 + jnp.exp(-x))


def _softplus(x):
    return jnp.maximum(x, 0.0) + jnp.log1p(jnp.exp(-jnp.abs(x)))


def _log_sigmoid(x):
    return jnp.minimum(x, 0.0) - jnp.log1p(jnp.exp(-jnp.abs(x)))


def _split3(x):
    hi = x.astype(BF16)
    r1 = x - hi.astype(F32)
    mid = r1.astype(BF16)
    lo = (r1 - mid.astype(F32)).astype(BF16)
    return hi, mid, lo


def _dot(a, b):
    return jnp.dot(a, b, preferred_element_type=F32)


def _dot_nt(a, b):
    return lax.dot_general(a, b, (((1,), (1,)), ((), ())), preferred_element_type=F32)


def _mods_kernel(c_ref, w_ref, b_ref, o_ref):
    c = c_ref[...]
    s = (c * _sigmoid(c)).astype(BF16)
    o_ref[0] = _dot(s, w_ref[0].astype(BF16)) + b_ref[0]


def _mods(c_all, w_ada, b_ada):
    rows, d = c_all.shape
    n = w_ada.shape[-1]
    tn = 1536
    return pl.pallas_call(
        _mods_kernel,
        grid=(DEPTH, n // tn),
        in_specs=[pl.BlockSpec((rows, d), lambda l, j: (0, 0)),
                  pl.BlockSpec((1, d, tn), lambda l, j: (l, 0, j)),
                  pl.BlockSpec((1, 1, tn), lambda l, j: (l, 0, j))],
        out_specs=pl.BlockSpec((1, rows, tn), lambda l, j: (l, 0, j)),
        out_shape=jax.ShapeDtypeStruct((DEPTH, rows, n), F32),
        compiler_params=_params(("arbitrary", "arbitrary")),
        name="mods",
    )(c_all, w_ada, b_ada.reshape(DEPTH, 1, n))


def _rope_slab(x, cos, sin_signed, first_half):
    fwd = pltpu.roll(x, LANES - A_HEAD_DIM // 2, 1)
    bwd = pltpu.roll(x, A_HEAD_DIM // 2, 1)
    partner = jnp.where(first_half, fwd, bwd)
    return x * cos + partner * sin_signed


def _proj_ab_kernel(x_ref, sc_ref, sh_ref, w_ref, wvt_ref, cos_ref, sin_ref,
                    q_ref, kb_ref, vt_ref, k_ref, v_ref, xg_ref, *, q_scale):
    u = _modulate(x_ref[0], sc_ref[0], sh_ref[0])
    pr = _dot(u, w_ref[...])
    cos = cos_ref[...]
    sin = sin_ref[...]
    lane = lax.broadcasted_iota(jnp.int32, (1, LANES), 1)
    first_half = (lane & (A_HEAD_DIM - 1)) < (A_HEAD_DIM // 2)
    for h in range(A_HEADS):
        sl = slice(h * LANES, (h + 1) * LANES)
        q = _rope_slab(pr[:, sl], cos, sin, first_half)
        q_ref[0, :, sl] = (q * q_scale).astype(BF16)
        k = _rope_slab(pr[:, A_WIDTH + h * LANES:A_WIDTH + (h + 1) * LANES], cos, sin, first_half)
        k_ref[0, :, h, :] = k
        kb_ref[0, :, sl] = k.astype(BF16)
    for h in range(A_HEADS):
        v_ref[0, :, h, :] = pr[:, 2 * A_WIDTH + h * LANES:2 * A_WIDTH + (h + 1) * LANES]
    vt_ref[0, 0] = _dot_nt(wvt_ref[...], u).astype(BF16)
    xg_ref[0] = pr[:, 3 * A_WIDTH:]


def _proj_ab(x, sc, sh, w, wvt, cos, sin):
    b, t, d = x.shape
    tm = _row_tile(t, ATTN_BLOCK)
    nt = t // tm
    n = w.shape[1]
    q_scale = A_HEAD_DIM ** -0.5 * LOG2E
    row = lambda width: pl.BlockSpec((1, tm, width), lambda bi, i: (bi, i, 0))
    const = lambda shape: pl.BlockSpec(shape, lambda bi, i: tuple(0 for _ in shape))
    heads = pl.BlockSpec((1, tm, A_HEADS, LANES), lambda bi, i: (bi, i, 0, 0))
    return pl.pallas_call(
        functools.partial(_proj_ab_kernel, q_scale=q_scale),
        grid=(b, nt),
        in_specs=[row(d),
                  pl.BlockSpec((1, 1, d), lambda bi, i: (bi, 0, 0)),
                  pl.BlockSpec((1, 1, d), lambda bi, i: (bi, 0, 0)),
                  const((d, n)), const((A_WIDTH, d)),
                  pl.BlockSpec((tm, LANES), lambda bi, i: (i, 0)),
                  pl.BlockSpec((tm, LANES), lambda bi, i: (i, 0))],
        out_specs=[row(A_WIDTH), row(A_WIDTH),
                   pl.BlockSpec((1, 1, A_WIDTH, tm), lambda bi, i: (bi, i, 0, 0)),
                   heads, heads, row(2 * B_WIDTH)],
        out_shape=[jax.ShapeDtypeStruct((b, t, A_WIDTH), BF16),
                   jax.ShapeDtypeStruct((b, t, A_WIDTH), BF16),
                   jax.ShapeDtypeStruct((b, nt, A_WIDTH, tm), BF16),
                   jax.ShapeDtypeStruct((b, t, A_HEADS, LANES), F32),
                   jax.ShapeDtypeStruct((b, t, A_HEADS, LANES), F32),
                   jax.ShapeDtypeStruct((b, t, 2 * B_WIDTH), F32)],
        compiler_params=_params(("arbitrary", "arbitrary")),
        name="proj_ab",
    )(x, sc, sh, w, wvt, cos, sin)


def _bias_placement():
    e = np.zeros((3 * LANES, BIAS_WIDTH), np.float32)
    for piece in range(3):
        for h in range(C_HEADS):
            e[piece * LANES + h, (h // 2) * LANES + 3 * (h % 2) + piece] = 1.0
    return jnp.asarray(e, BF16)


def _proj_c_kernel(x_ref, sc_ref, sh_ref, w_ref, wvt_ref, wft_ref, bfr_ref, bfc_ref, place_ref,
                   q_ref, kb_ref, bias_ref, vt_ref, k_ref, v_ref, lf_ref, lft_ref, run_ref, *, q_scale):
    i = pl.program_id(1)
    tm = x_ref.shape[1]

    @pl.when(i == 0)
    def _():
        run_ref[...] = jnp.zeros_like(run_ref)

    u = _modulate(x_ref[0], sc_ref[0], sh_ref[0])
    pr = _dot(u, w_ref[...])
    q_ref[0] = (pr[:, :C_WIDTH] * q_scale).astype(BF16)
    k = pr[:, C_WIDTH:2 * C_WIDTH]
    k_ref[0] = k
    kb_ref[0] = k.astype(BF16)
    v_ref[0] = pr[:, 2 * C_WIDTH:3 * C_WIDTH]
    vt_ref[0, 0] = _dot_nt(wvt_ref[...], u).astype(BF16)
    lf = _log_sigmoid(pr[:, 3 * C_WIDTH:] + bfr_ref[...])
    lf_ref[0] = lf[:, :C_HEADS]
    lft_ref[0] = _log_sigmoid(_dot_nt(wft_ref[...], u) + bfc_ref[...])
    r = lax.broadcasted_iota(jnp.int32, (tm, tm), 0)
    c = lax.broadcasted_iota(jnp.int32, (tm, tm), 1)
    lower = jnp.where(c <= r, 1.0, 0.0).astype(BF16)
    hi, mid, lo = _split3(lf)
    cum = (_dot(lower, hi) + _dot(lower, mid)) + _dot(lower, lo) + run_ref[...]
    pieces = jnp.concatenate(_split3(cum * (-LOG2E)), axis=1)
    bias_ref[0] = _dot(pieces, place_ref[...]).astype(BF16)
    run_ref[...] = run_ref[...] + jnp.sum(lf, axis=0, keepdims=True)


def _proj_c(x, sc, sh, w, wvt, wft, bf_row, bf_col, place):
    b, t, d = x.shape
    tm = _row_tile(t, ATTN_BLOCK)
    nt = t // tm
    n = w.shape[1]
    q_scale = C_HEAD_DIM ** -0.5 * LOG2E
    row = lambda width: pl.BlockSpec((1, tm, width), lambda bi, i: (bi, i, 0))
    const = lambda shape: pl.BlockSpec(shape, lambda bi, i: tuple(0 for _ in shape))
    return pl.pallas_call(
        functools.partial(_proj_c_kernel, q_scale=q_scale),
        grid=(b, nt),
        in_specs=[row(d),
                  pl.BlockSpec((1, 1, d), lambda bi, i: (bi, 0, 0)),
                  pl.BlockSpec((1, 1, d), lambda bi, i: (bi, 0, 0)),
                  const((d, n)), const((C_WIDTH, d)), const((C_HEADS, d)),
                  const((1, LANES)), const((C_HEADS, 1)), const((3 * LANES, BIAS_WIDTH))],
        out_specs=[row(C_WIDTH), row(C_WIDTH), row(BIAS_WIDTH),
                   pl.BlockSpec((1, 1, C_WIDTH, tm), lambda bi, i: (bi, i, 0, 0)),
                   row(C_WIDTH), row(C_WIDTH), row(C_HEADS),
                   pl.BlockSpec((1, C_HEADS, tm), lambda bi, i: (bi, 0, i))],
        out_shape=[jax.ShapeDtypeStruct((b, t, C_WIDTH), BF16),
                   jax.ShapeDtypeStruct((b, t, C_WIDTH), BF16),
                   jax.ShapeDtypeStruct((b, t, BIAS_WIDTH), BF16),
                   jax.ShapeDtypeStruct((b, nt, C_WIDTH, tm), BF16),
                   jax.ShapeDtypeStruct((b, t, C_WIDTH), F32),
                   jax.ShapeDtypeStruct((b, t, C_WIDTH), F32),
                   jax.ShapeDtypeStruct((b, t, C_HEADS), F32),
                   jax.ShapeDtypeStruct((b, C_HEADS, t), F32)],
        scratch_shapes=[pltpu.VMEM((1, LANES), F32)],
        compiler_params=_params(("arbitrary", "arbitrary")),
        name="proj_c",
    )(x, sc, sh, w, wvt, wft, bf_row, bf_col, place)


def _lambda(lam_ref, lam_init):
    lq1, lk1, lq2, lk2 = (lam_ref[r:r + 1, :] for r in range(4))
    return (jnp.exp(jnp.sum(lq1 * lk1, axis=1, keepdims=True))
            - jnp.exp(jnp.sum(lq2 * lk2, axis=1, keepdims=True)) + lam_init)


def _attn_kernel(*refs, mode, lam_init):
    if mode == "A":
        q_ref, k_ref, vt_ref, lam_ref, gain_ref, o_ref, qc_ref, m_ref, acc_ref, s0_ref, s1_ref, c0_ref, c1_ref = refs
        bias_ref = None
        d_val = 2 * A_HEAD_DIM
    else:
        q_ref, k_ref, bias_ref, vt_ref, o_ref, qc_ref, m_ref, acc_ref, s0_ref, s1_ref, c0_ref, c1_ref = refs
        d_val = C_HEAD_DIM
    tq = q_ref.shape[1]
    tk = k_ref.shape[2]
    qi = pl.program_id(2)
    lane = lax.broadcasted_iota(jnp.int32, (1, LANES), 1)
    low = lane < (LANES // 2)
    q = q_ref[0]
    zero = jnp.zeros_like(q)
    for a in range(2):
        qa = jnp.where(low, q, zero) if a == 0 else jnp.where(low, zero, q)
        if bias_ref is not None:
            pick = jnp.where((lane >= 3 * a) & (lane < 3 * a + 3), 1.0, 0.0).astype(BF16)
            qa = jnp.concatenate([qa, jnp.broadcast_to(pick, (tq, LANES))], axis=1)
        qc_ref[a] = qa
    m_ref[...] = jnp.full(m_ref.shape, NEG, F32)
    acc_ref[...] = jnp.zeros(acc_ref.shape, F32)
    ones = jnp.ones((BF16_ROWS, tk), BF16)

    units = [(a, slice(n * MXU_DIM, (n + 1) * MXU_DIM)) for a in range(2) for n in range(tq // MXU_DIM)]

    def score_chain(j, u):
        a, cs = units[u]
        kc = k_ref[0, j]
        if bias_ref is not None:
            kc = jnp.concatenate([kc, bias_ref[0, j]], axis=1)
        return _dot_nt(kc, qc_ref[a, cs, :])

    def value_chain(j, u, st, cmax, masked):
        a, cs = units[u]
        if masked:
            key = lax.broadcasted_iota(jnp.int32, (tk, MXU_DIM), 0)
            qry = lax.broadcasted_iota(jnp.int32, (tk, MXU_DIM), 1) + cs.start
            keep = ((key >> CHUNK_SHIFT) <= (qry >> CHUNK_SHIFT)) if mode == "A" else (key <= qry)
            st = jnp.where(keep, st, NEG)
            cmax = jnp.max(st, axis=0, keepdims=True)
        m_prev = m_ref[a, :, cs]
        m_new = jnp.maximum(m_prev, cmax)
        alpha = jnp.exp2(m_prev - m_new)
        pt = jnp.exp2(st - m_new).astype(BF16)
        vt = vt_ref[0, j]
        va = vt if mode == "A" else vt[a * d_val:(a + 1) * d_val]
        va = jnp.concatenate([va, ones], axis=0)
        acc_ref[a, :, cs] = alpha * acc_ref[a, :, cs] + _dot(va, pt)
        m_ref[a, :, cs] = m_new

    def stage(j_scores, dst, j_values, src, masked=False):
        for t in range(len(units) + 1):
            if j_scores is not None and t < len(units):
                st = score_chain(j_scores, t)
                dst[0][t] = st
                dst[1][t] = jnp.max(st, axis=0, keepdims=True)
            if j_values is not None and t >= 1:
                value_chain(j_values, t - 1, src[0][t - 1], src[1][t - 1], masked)

    buf0 = (s0_ref, c0_ref)
    buf1 = (s1_ref, c1_ref)
    stage(0, buf0, None, None)

    def pair(j):
        stage(j + 1, buf1, j, buf0)
        stage(j + 2, buf0, j + 1, buf1)

    def quad(p, carry):
        pair(4 * p)
        pair(4 * p + 2)
        return carry

    lax.fori_loop(0, qi >> 2, quad, 0)

    @pl.when((qi & 2) == 2)
    def _():
        pair(qi & ~3)

    @pl.when((qi & 1) == 1)
    def _():
        stage(qi, buf1, qi - 1, buf0)
        stage(None, None, qi, buf1, masked=True)

    @pl.when((qi & 1) == 0)
    def _():
        stage(None, None, qi, buf0, masked=True)

    outs = []
    for a in range(2):
        acc = acc_ref[a]
        outs.append(acc[:d_val] * (1.0 / acc[d_val:d_val + 1]))
    if mode == "A":
        ot = outs[0] - _lambda(lam_ref, lam_init) * outs[1]
        ot = ot * lax.rsqrt(jnp.mean(ot * ot, axis=0, keepdims=True) + LN_EPS)
        ot = ot * (gain_ref[0] * (1.0 - lam_init))
    else:
        ot = jnp.concatenate(outs, axis=0)
    o_ref[0] = ot.T.astype(o_ref.dtype)


def _attention(q, k, vt, mode, lam=None, gain=None, bias=None, lam_init=0.0):
    b, t, width = q.shape
    nk, tk = vt.shape[1], vt.shape[3]
    tq = tk
    groups = width // LANES
    d_aug = (2 * A_HEAD_DIM if mode == "A" else C_HEAD_DIM) + BF16_ROWS
    kspec = pl.BlockSpec((1, nk, tk, LANES), lambda bi, g, i: (bi, 0, 0, g))
    in_specs = [pl.BlockSpec((1, tq, LANES), lambda bi, g, i: (bi, i, g)), kspec]
    args = [q, k.reshape(b, nk, tk, width)]
    if mode == "C":
        in_specs.append(kspec)
        args.append(bias.reshape(b, nk, tk, groups * LANES))
    in_specs.append(pl.BlockSpec((1, nk, LANES, tk), lambda bi, g, i: (bi, 0, g, 0)))
    args.append(vt)
    if mode == "A":
        in_specs += [pl.BlockSpec((4, A_HEAD_DIM), lambda bi, g, i: (0, 0)),
                     pl.BlockSpec((1, LANES, 1), lambda bi, g, i: (g, 0, 0))]
        args += [lam, gain]
    return pl.pallas_call(
        functools.partial(_attn_kernel, mode=mode, lam_init=lam_init),
        grid=(b, groups, t // tq),
        in_specs=in_specs,
        out_specs=pl.BlockSpec((1, tq, LANES), lambda bi, g, i: (bi, i, g)),
        out_shape=jax.ShapeDtypeStruct((b, t, width), BF16),
        scratch_shapes=[pltpu.VMEM((2, tq, LANES if mode == "A" else 2 * LANES), BF16),
                        pltpu.VMEM((2, 1, tq), F32),
                        pltpu.VMEM((2, d_aug, tq), F32),
                        pltpu.VMEM((2 * tq // MXU_DIM, tk, MXU_DIM), F32),
                        pltpu.VMEM((2 * tq // MXU_DIM, tk, MXU_DIM), F32),
                        pltpu.VMEM((2 * tq // MXU_DIM, 1, MXU_DIM), F32),
                        pltpu.VMEM((2 * tq // MXU_DIM, 1, MXU_DIM), F32)],
        compiler_params=_params(("arbitrary", "arbitrary", "arbitrary")),
        name="attn_" + mode,
    )(*args)


def _diff_finish(o0, o1, lam, gain, lam_init):
    o = o0 - lam * o1
    o = o * lax.rsqrt(jnp.mean(o * o, axis=-1, keepdims=True) + LN_EPS)
    return o * gain * (1.0 - lam_init)


def _cached_attn_kernel(*refs, mode, past, lam_init):
    if mode == "A":
        q_ref, kn_ref, vn_ref, ck_ref, cv_ref, lam_ref, gain_ref, o_ref = refs
    else:
        q_ref, kn_ref, vn_ref, ck_ref, cv_ref, clf_ref, nlf_ref, o_ref = refs
    t = q_ref.shape[1]
    width = o_ref.shape[2]
    groups = width // LANES
    lane = lax.broadcasted_iota(jnp.int32, (1, LANES), 1)
    low = lane < (LANES // 2)
    row = lax.broadcasted_iota(jnp.int32, (t, t), 0)
    col = lax.broadcasted_iota(jnp.int32, (t, t), 1)
    if mode == "A":
        keep = ((past + col) >> CHUNK_SHIFT) <= ((past + row) >> CHUNK_SHIFT)
        lam = _lambda(lam_ref, lam_init)
    else:
        keep = col <= row
        r = lax.broadcasted_iota(jnp.int32, (past, past), 0)
        c = lax.broadcasted_iota(jnp.int32, (past, past), 1)
        upper = jnp.where(r <= c, 1.0, 0.0).astype(BF16)
        clf = clf_ref[0]
        hi, mid, lo = _split3(clf)
        cum_c = (_dot(hi, upper) + _dot(mid, upper)) + _dot(lo, upper)
        upper_n = jnp.where(row <= col, 1.0, 0.0).astype(BF16)
        hi, mid, lo = _split3(nlf_ref[0])
        cum_n = ((_dot(hi, upper_n) + _dot(mid, upper_n)) + _dot(lo, upper_n)
                 + jnp.sum(clf, axis=1, keepdims=True))
        bias_c = cum_c * (-LOG2E)
        bias_n = cum_n * (-LOG2E)
    for g in range(groups):
        sl = slice(g * LANES, (g + 1) * LANES)
        q = q_ref[0, :, sl]
        kn = kn_ref[0, :, sl]
        vn = vn_ref[0, :, sl].astype(BF16)
        if mode == "A":
            kc = ck_ref[0, 0, :, g, :].astype(BF16)
            vc = cv_ref[0, 0, :, g, :].astype(BF16)
        else:
            kc = ck_ref[0, :, sl].astype(BF16)
            vc = cv_ref[0, :, sl].astype(BF16)
        outs = []
        for a in range(2):
            qa = jnp.where(low, q, jnp.zeros_like(q)) if a == 0 else jnp.where(low, jnp.zeros_like(q), q)
            s_c = _dot_nt(qa, kc)
            s_n = _dot_nt(qa, kn)
            if mode == "C":
                h = 2 * g + a
                s_c = s_c + bias_c[h:h + 1, :]
                s_n = s_n + bias_n[h:h + 1, :]
            s_n = jnp.where(keep, s_n, NEG)
            m = jnp.maximum(jnp.max(s_c, axis=1, keepdims=True), jnp.max(s_n, axis=1, keepdims=True))
            p_c = jnp.exp2(s_c - m)
            p_n = jnp.exp2(s_n - m)
            l = jnp.sum(p_c, axis=1, keepdims=True) + jnp.sum(p_n, axis=1, keepdims=True)
            acc = _dot(p_c.astype(BF16), vc) + _dot(p_n.astype(BF16), vn)
            outs.append(acc * (1.0 / l))
        if mode == "A":
            o = _diff_finish(outs[0], outs[1], lam, gain_ref[g], lam_init)
        else:
            o = jnp.where(low, outs[0], outs[1])
        o_ref[0, :, sl] = o.astype(o_ref.dtype)


def _cached_attention(q, kn, vn, cache_k, cache_v, mode, lam=None, gain=None, cache_lft=None, new_lft=None,
                      lam_init=0.0):
    b, t, width = q.shape
    new = pl.BlockSpec((1, t, width), lambda bi: (bi, 0, 0))
    if mode == "A":
        _, _, past, heads, dh = cache_k.shape
        old = pl.BlockSpec((1, 1, past, heads, dh), lambda bi: (0, bi, 0, 0, 0))
    else:
        past = cache_k.shape[1]
        old = pl.BlockSpec((1, past, width), lambda bi: (bi, 0, 0))
    in_specs = [new, new, new, old, old]
    args = [q, kn, vn, cache_k, cache_v]
    if mode == "A":
        in_specs += [pl.BlockSpec((4, A_HEAD_DIM), lambda bi: (0, 0)),
                     pl.BlockSpec((A_HEADS, 1, LANES), lambda bi: (0, 0, 0))]
        args += [lam, gain]
    else:
        in_specs += [pl.BlockSpec((1, C_HEADS, past), lambda bi: (bi, 0, 0)),
                     pl.BlockSpec((1, C_HEADS, t), lambda bi: (bi, 0, 0))]
        args += [cache_lft, new_lft]
    return pl.pallas_call(
        functools.partial(_cached_attn_kernel, mode=mode, past=past, lam_init=lam_init),
        grid=(b,),
        in_specs=in_specs,
        out_specs=new,
        out_shape=jax.ShapeDtypeStruct((b, t, width), BF16),
        compiler_params=_params(("arbitrary",)),
        name="cached_attn_" + mode,
    )(*args)


def _gelu_tanh(x):
    return 0.5 * x * (1.0 + jnp.tanh(math.sqrt(2.0 / math.pi) * (x + 0.044715 * (x * x * x))))


def _causal_conv(x, tail, w_ref, b_ref):
    width = w_ref.shape[0]
    tm = x.shape[0]
    cat = jnp.concatenate([tail, x], axis=0)
    y = None
    for j in range(width):
        back = width - 1 - j
        src = cat if back == 0 else pltpu.roll(cat, back, 0)
        term = src[SUBLANES:SUBLANES + tm] * w_ref[j:j + 1, :]
        y = term if y is None else y + term
    return y + b_ref[...]


def _rglru_kernel(xg_ref, h0_ref, cb_ref, cw_ref, cbias_ref, wa_ref, ba_ref, wx_ref, bx_ref, sp_ref,
                  y_ref, hl_ref, ct_ref, hc_ref, tail_ref):
    i = pl.program_id(1)
    tm = xg_ref.shape[1]

    @pl.when(i == 0)
    def _():
        hc_ref[...] = h0_ref[0]
        tail_ref[...] = cb_ref[0]

    x = xg_ref[0, :, :B_WIDTH]
    gate_in = xg_ref[0, :, B_WIDTH:]
    xc = _causal_conv(x, tail_ref[...], cw_ref, cbias_ref)
    xcb = xc.astype(BF16)
    r = _sigmoid(_dot(xcb, wa_ref[...]) + ba_ref[...])
    ig = _sigmoid(_dot(xcb, wx_ref[...]) + bx_ref[...])
    log_a = (-RG_C) * r * sp_ref[...]
    a = jnp.exp(log_a)
    th = jnp.tanh(log_a)
    bx = jnp.sqrt((-2.0 * th) / (1.0 - th)) * (ig * xc)
    row = lax.broadcasted_iota(jnp.int32, (tm, 1), 0)
    s = 1
    while s < tm:
        valid = row >= s
        a_sh = pltpu.roll(a, s, 0)
        b_sh = pltpu.roll(bx, s, 0)
        bx = jnp.where(valid, a * b_sh + bx, bx)
        a = jnp.where(valid, a * a_sh, a)
        s *= 2
    h = a * hc_ref[...] + bx
    y_ref[0] = (h * _gelu_tanh(gate_in)).astype(y_ref.dtype)
    h_tail = h[tm - SUBLANES:, :]
    x_tail = x[tm - SUBLANES:, :]
    hc_ref[...] = h_tail[SUBLANES - 1:, :]
    tail_ref[...] = x_tail
    hl_ref[0] = h_tail
    ct_ref[0] = x_tail


def _rglru(xg, h0, conv_tail, cw, cbias, wa, ba, wx, bx, sp, tm_pref=256):
    b, t, _ = xg.shape
    tm = _row_tile(t, tm_pref)
    wspec = lambda shape: pl.BlockSpec(shape, lambda bi, i: tuple(0 for _ in shape))
    return pl.pallas_call(
        _rglru_kernel,
        grid=(b, t // tm),
        in_specs=[pl.BlockSpec((1, tm, 2 * B_WIDTH), lambda bi, i: (bi, i, 0)),
                  pl.BlockSpec((1, 1, B_WIDTH), lambda bi, i: (bi, 0, 0)),
                  pl.BlockSpec((1, SUBLANES, B_WIDTH), lambda bi, i: (bi, 0, 0)),
                  wspec((B_CONV, B_WIDTH)), wspec((1, B_WIDTH)),
                  wspec((B_WIDTH, B_WIDTH)), wspec((1, B_WIDTH)),
                  wspec((B_WIDTH, B_WIDTH)), wspec((1, B_WIDTH)), wspec((1, B_WIDTH))],
        out_specs=[pl.BlockSpec((1, tm, B_WIDTH), lambda bi, i: (bi, i, 0)),
                   pl.BlockSpec((1, SUBLANES, B_WIDTH), lambda bi, i: (bi, 0, 0)),
                   pl.BlockSpec((1, SUBLANES, B_WIDTH), lambda bi, i: (bi, 0, 0))],
        out_shape=[jax.ShapeDtypeStruct((b, t, B_WIDTH), BF16),
                   jax.ShapeDtypeStruct((b, SUBLANES, B_WIDTH), F32),
                   jax.ShapeDtypeStruct((b, SUBLANES, B_WIDTH), F32)],
        scratch_shapes=[pltpu.VMEM((1, B_WIDTH), F32), pltpu.VMEM((SUBLANES, B_WIDTH), F32)],
        compiler_params=_params(("arbitrary", "arbitrary")),
        name="rglru",
    )(xg, h0, conv_tail, cw, cbias, wa, ba, wx, bx, sp)


def _proj_ln_kernel(*refs, n_in):
    h_refs = refs[:n_in]
    w_ref, x_ref, gate_ref, g_ref, b_ref, o_ref = refs[n_in:]
    tm = x_ref.shape[1]
    rows = [slice(r, min(r + LN_ROWS, tm)) for r in range(0, tm, LN_ROWS)]

    def matmul(rs):
        hs = [r[0, rs, :] for r in h_refs]
        return _dot(hs[0] if n_in == 1 else jnp.concatenate(hs, axis=-1), w_ref[...])

    def norm(rs, proj):
        y = ALPHA * x_ref[0, rs, :] + gate_ref[0] * proj
        mu = jnp.mean(y, axis=-1, keepdims=True)
        yc = y - mu
        var = jnp.mean(yc * yc, axis=-1, keepdims=True)
        o_ref[0, rs, :] = yc * lax.rsqrt(var + LN_EPS) * g_ref[...] + b_ref[...]

    pending = matmul(rows[0])
    for r in range(len(rows)):
        nxt = matmul(rows[r + 1]) if r + 1 < len(rows) else None
        norm(rows[r], pending)
        pending = nxt


def _proj_ln(hs, w, x, gate, ln_g, ln_b, tm_pref=512):
    b, t, d = x.shape
    tm = _row_tile(t, tm_pref)
    k = w.shape[0]
    in_specs = [pl.BlockSpec((1, tm, h.shape[2]), lambda bi, i: (bi, i, 0)) for h in hs]
    in_specs += [pl.BlockSpec((k, d), lambda bi, i: (0, 0)),
                 pl.BlockSpec((1, tm, d), lambda bi, i: (bi, i, 0)),
                 pl.BlockSpec((1, 1, d), lambda bi, i: (bi, 0, 0)),
                 pl.BlockSpec((1, d), lambda bi, i: (0, 0)),
                 pl.BlockSpec((1, d), lambda bi, i: (0, 0))]
    return pl.pallas_call(
        functools.partial(_proj_ln_kernel, n_in=len(hs)),
        grid=(b, t // tm),
        in_specs=in_specs,
        out_specs=pl.BlockSpec((1, tm, d), lambda bi, i: (bi, i, 0)),
        out_shape=jax.ShapeDtypeStruct((b, t, d), F32),
        compiler_params=_params(("arbitrary", "arbitrary")),
        name="proj_ln",
    )(*hs, w, x, gate, ln_g, ln_b)


def _ffn_up_kernel(x_ref, sc_ref, sh_ref, w_ref, cb_ref, cw_ref, cbias_ref, h_ref, ct_ref, tail_ref):
    i = pl.program_id(1)
    tm = x_ref.shape[1]

    @pl.when(i == 0)
    def _():
        tail_ref[...] = cb_ref[0]

    u = _modulate(x_ref[0], sc_ref[0], sh_ref[0])
    chunks = [slice(c, min(c + FFN_CHUNK, D_FF)) for c in range(0, D_FF, FFN_CHUNK)]

    def matmuls(cs):
        gs = slice(D_FF + cs.start, D_FF + cs.stop)
        return _dot(u, w_ref[:, cs]), _dot(u, w_ref[:, gs])

    def gate(cs, a, g):
        gc = _causal_conv(g, tail_ref[:, cs], cw_ref.at[:, cs], cbias_ref.at[:, cs])
        silu = gc * (0.5 * jnp.tanh(0.5 * gc) + 0.5)
        h_ref[0, :, cs] = (a * silu).astype(h_ref.dtype)
        g_tail = g[tm - SUBLANES:, :]
        tail_ref[:, cs] = g_tail
        ct_ref[0, :, cs] = g_tail

    pending = matmuls(chunks[0])
    for c in range(len(chunks)):
        nxt = matmuls(chunks[c + 1]) if c + 1 < len(chunks) else None
        gate(chunks[c], *pending)
        pending = nxt


def _ffn_up(x, sc, sh, w, conv_tail, cw, cbias, tm_pref=256):
    b, t, d = x.shape
    tm = _row_tile(t, tm_pref)
    return pl.pallas_call(
        _ffn_up_kernel,
        grid=(b, t // tm),
        in_specs=[pl.BlockSpec((1, tm, d), lambda bi, i: (bi, i, 0)),
                  pl.BlockSpec((1, 1, d), lambda bi, i: (bi, 0, 0)),
                  pl.BlockSpec((1, 1, d), lambda bi, i: (bi, 0, 0)),
                  pl.BlockSpec((d, 2 * D_FF), lambda bi, i: (0, 0)),
                  pl.BlockSpec((1, SUBLANES, D_FF), lambda bi, i: (bi, 0, 0)),
                  pl.BlockSpec((FFN_CONV, D_FF), lambda bi, i: (0, 0)),
                  pl.BlockSpec((1, D_FF), lambda bi, i: (0, 0))],
        out_specs=[pl.BlockSpec((1, tm, D_FF), lambda bi, i: (bi, i, 0)),
                   pl.BlockSpec((1, SUBLANES, D_FF), lambda bi, i: (bi, 0, 0))],
        out_shape=[jax.ShapeDtypeStruct((b, t, D_FF), BF16),
                   jax.ShapeDtypeStruct((b, SUBLANES, D_FF), F32)],
        scratch_shapes=[pltpu.VMEM((SUBLANES, D_FF), F32)],
        compiler_params=_params(("arbitrary", "arbitrary")),
        name="ffn_up",
    )(x, sc, sh, w, conv_tail, cw, cbias)


def _pad_tail(buf):
    return jnp.pad(buf, ((0, 0), (SUBLANES - buf.shape[1], 0), (0, 0)))


def _rope_tables(past, t):
    half = A_HEAD_DIM // 2
    inv = ROPE_THETA ** (-jnp.arange(0, A_HEAD_DIM, 2, dtype=F32) / A_HEAD_DIM)
    pos = (past + jnp.arange(t, dtype=jnp.int32)).astype(F32)
    ang = pos[:, None] * inv[None, :]
    cos = jnp.tile(jnp.cos(ang), (1, LANES // half))
    sin = jnp.sin(ang)
    sin_signed = jnp.tile(jnp.concatenate([-sin, sin], axis=1), (1, LANES // A_HEAD_DIM))
    return cos, sin_signed


def _block_diag(w):
    n, i, o = w.shape
    return jnp.einsum("nio,nm->nimo", w, jnp.eye(n, dtype=w.dtype)).reshape(n * i, n * o)


def _prepare(p):
    w = {}
    w_in_ab = p["w_in_ab"][0]
    w["in_ab"] = w_in_ab.astype(BF16)
    w["in_ab_vt"] = w_in_ab[:, 2 * A_WIDTH:3 * A_WIDTH].T.astype(BF16)
    w["out_ab"] = p["w_out_ab"][0].astype(BF16)
    w["lam"] = jnp.stack([p["lam_q1"][0], p["lam_k1"][0], p["lam_q2"][0], p["lam_k2"][0]])
    w["gain_row"] = p["attn_gain"][0].reshape(A_HEADS, 1, LANES)
    w["gain_col"] = p["attn_gain"][0].reshape(A_HEADS, LANES, 1)
    w["b_conv_w"] = p["b_conv_w"][0]
    w["b_conv_b"] = p["b_conv_b"][0].reshape(1, B_WIDTH)
    w["rg_a"] = _block_diag(p["w_rg_a"][0]).astype(BF16)
    w["rg_x"] = _block_diag(p["w_rg_x"][0]).astype(BF16)
    w["b_rg_a"] = p["b_rg_a"][0].reshape(1, B_WIDTH)
    w["b_rg_x"] = p["b_rg_x"][0].reshape(1, B_WIDTH)
    w["rg_L"] = p["rg_L"][0].reshape(1, B_WIDTH)
    w_in_c = p["w_in_c"][0]
    w["in_c"] = jnp.pad(w_in_c, ((0, 0), (0, LANES - C_HEADS))).astype(BF16)
    w["in_c_vt"] = w_in_c[:, 2 * C_WIDTH:3 * C_WIDTH].T.astype(BF16)
    w["in_c_ft"] = w_in_c[:, 3 * C_WIDTH:].T.astype(BF16)
    w["bf_row"] = jnp.pad(p["b_f"][0], (0, LANES - C_HEADS)).reshape(1, LANES)
    w["bf_col"] = p["b_f"][0].reshape(C_HEADS, 1)
    w["place"] = _bias_placement()
    w["out_c"] = p["w_out_c"][0].astype(BF16)
    w["up"] = [p["w_up"][i].astype(BF16) for i in range(DEPTH)]
    w["down"] = [p["w_down"][i].astype(BF16) for i in range(DEPTH)]
    w["ffn_conv_w"] = [p["ffn_conv_w"][i] for i in range(DEPTH)]
    w["ffn_conv_b"] = [p["ffn_conv_b"][i].reshape(1, D_FF) for i in range(DEPTH)]
    w["ln1_g"] = [p["ln1_g"][i].reshape(1, D_MODEL) for i in range(DEPTH)]
    w["ln1_b"] = [p["ln1_b"][i].reshape(1, D_MODEL) for i in range(DEPTH)]
    w["ln2_g"] = [p["ln2_g"][i].reshape(1, D_MODEL) for i in range(DEPTH)]
    w["ln2_b"] = [p["ln2_b"][i].reshape(1, D_MODEL) for i in range(DEPTH)]
    return w


def _softplus_kernel(x_ref, o_ref):
    o_ref[...] = _softplus(-x_ref[...])


def _trunk(x, mods, w, sp, cache_a_k=None, cache_a_v=None, state_b_h=None, state_b_conv=None,
           cache_c_k=None, cache_c_v=None, cache_c_logf=None, state_ffn_conv=None):
    b, t, d = x.shape
    cached = cache_a_k is not None
    past = cache_a_k.shape[2] if cached else 0
    outs = {}
    for i in range(DEPTH):
        sh1, sc1, g1, sh2, sc2, g2 = [m[:, None, :] for m in jnp.split(mods[i], 6, axis=-1)]
        if i % 2 == 0:
            lam_init = 0.8 - 0.6 * math.exp(-0.3 * i)
            cos, sin = _rope_tables(past, t)
            q, kb, vt, k32, v32, xg = _proj_ab(x, sc1, sh1, w["in_ab"], w["in_ab_vt"], cos, sin)
            if cached:
                o = _cached_attention(q, kb, v32.reshape(b, t, A_WIDTH), cache_a_k, cache_a_v,
                                      "A", lam=w["lam"], gain=w["gain_row"], lam_init=lam_init)
                h0 = state_b_h[0][:, None, :]
                ctail = _pad_tail(state_b_conv[0])
            else:
                o = _attention(q, kb, vt, "A", lam=w["lam"], gain=w["gain_col"], lam_init=lam_init)
                h0 = jnp.zeros((b, 1, B_WIDTH), F32)
                ctail = jnp.zeros((b, SUBLANES, B_WIDTH), F32)
            yb, h_tail, x_tail = _rglru(xg, h0, ctail, w["b_conv_w"], w["b_conv_b"], w["rg_a"], w["b_rg_a"],
                                        w["rg_x"], w["b_rg_x"], sp)
            outs["a_k"] = k32[None]
            outs["a_v"] = v32[None]
            outs["b_h"] = h_tail[:, SUBLANES - 1, :][None]
            outs["b_conv"] = x_tail[:, SUBLANES - (B_CONV - 1):, :][None]
            x = _proj_ln([o, yb], w["out_ab"], x, g1, w["ln1_g"][i], w["ln1_b"][i])
        else:
            q, kb, bias, vt, k32, v32, lf, lft = _proj_c(x, sc1, sh1, w["in_c"], w["in_c_vt"], w["in_c_ft"],
                                                         w["bf_row"], w["bf_col"], w["place"])
            if cached:
                o = _cached_attention(q, kb, v32, cache_c_k[0].reshape(b, past, C_WIDTH),
                                      cache_c_v[0].reshape(b, past, C_WIDTH),
                                      "C", cache_lft=jnp.swapaxes(cache_c_logf[0], 1, 2), new_lft=lft)
            else:
                o = _attention(q, kb, vt, "C", bias=bias)
            outs["c_k"] = k32.reshape(1, b, t, C_HEADS, C_HEAD_DIM)
            outs["c_v"] = v32.reshape(1, b, t, C_HEADS, C_HEAD_DIM)
            outs["c_logf"] = lf[None]
            x = _proj_ln([o], w["out_c"], x, g1, w["ln1_g"][i], w["ln1_b"][i])
        ftail = _pad_tail(state_ffn_conv[i]) if cached else jnp.zeros((b, SUBLANES, D_FF), F32)
        hmid, g_tail = _ffn_up(x, sc2, sh2, w["up"][i], ftail, w["ffn_conv_w"][i], w["ffn_conv_b"][i])
        outs.setdefault("ffn", []).append(g_tail[:, SUBLANES - (FFN_CONV - 1):, :])
        x = _proj_ln([hmid], w["down"][i], x, g2, w["ln2_g"][i], w["ln2_b"][i])
    return (x, outs["a_k"], outs["a_v"], outs["b_h"], outs["b_conv"],
            outs["c_k"], outs["c_v"], outs["c_logf"], jnp.stack(outs["ffn"]))


def kernel(x_prompt, x_sample, c_prompt, c_sample, cache_a_k, cache_a_v, state_b_h, state_b_conv, cache_c_k, cache_c_v, cache_c_logf, state_ffn_conv, w_ada, b_ada, ln1_g, ln1_b, ln2_g, ln2_b, w_in_ab, lam_q1, lam_k1, lam_q2, lam_k2, attn_gain, b_conv_w, b_conv_b, w_rg_a, b_rg_a, w_rg_x, b_rg_x, rg_L, w_out_ab, w_in_c, b_f, w_out_c, w_up, ffn_conv_w, ffn_conv_b, w_down):
    p = dict(w_in_ab=w_in_ab, lam_q1=lam_q1, lam_k1=lam_k1, lam_q2=lam_q2, lam_k2=lam_k2, attn_gain=attn_gain,
             b_conv_w=b_conv_w, b_conv_b=b_conv_b, w_rg_a=w_rg_a, b_rg_a=b_rg_a, w_rg_x=w_rg_x, b_rg_x=b_rg_x,
             rg_L=rg_L, w_out_ab=w_out_ab, w_in_c=w_in_c, b_f=b_f, w_out_c=w_out_c, w_up=w_up,
             ffn_conv_w=ffn_conv_w, ffn_conv_b=ffn_conv_b, w_down=w_down,
             ln1_g=ln1_g, ln1_b=ln1_b, ln2_g=ln2_g, ln2_b=ln2_b)
    w = _prepare(p)
    bp = c_prompt.shape[0]
    bs = c_sample.shape[0]
    rows = -(-(bp + bs) // 16) * 16
    c_all = jnp.pad(jnp.concatenate([c_prompt, c_sample], axis=0), ((0, rows - bp - bs), (0, 0)))
    mods = _mods(c_all, w_ada, b_ada)
    sp = pl.pallas_call(_softplus_kernel, out_shape=jax.ShapeDtypeStruct((1, B_WIDTH), F32),
                        name="softplus")(w["rg_L"])
    res_p = _trunk(x_prompt, mods[:, :bp], w, sp)
    res_s = _trunk(x_sample, mods[:, bp:bp + bs], w, sp, cache_a_k, cache_a_v, state_b_h, state_b_conv,
                   cache_c_k, cache_c_v, cache_c_logf, state_ffn_conv)
    return (res_p[0], res_s[0]) + res_p[1:] + res_s[1:]
```

```python
import functools
import math

import numpy as np
import jax
import jax.numpy as jnp
from jax import lax
from jax.experimental import pallas as pl
from jax.experimental.pallas import tpu as pltpu

F32 = jnp.float32
BF16 = jnp.bfloat16

D_MODEL = 1024
DEPTH = 2
CHUNK = 64
CHUNK_SHIFT = 6
A_HEADS = 4
A_HEAD_DIM = 64
A_WIDTH = A_HEADS * 2 * A_HEAD_DIM
B_WIDTH = 512
B_BLOCKS = 8
B_CONV = 4
RG_C = 8.0
C_HEADS = 16
C_HEAD_DIM = 64
C_WIDTH = C_HEADS * C_HEAD_DIM
D_FF = 2816
FFN_CONV = 3
ROPE_THETA = 10000.0
ALPHA = (2 * DEPTH) ** 0.25
LN_EPS = 1e-5
NEG = -1e30
LOG2E = 1.4426950408889634

LANES = 128
SUBLANES = 8
BF16_ROWS = 16
MXU_DIM = 256
VMEM_LIMIT = 56 * 1024 * 1024
BIAS_WIDTH = (C_HEADS // 2) * LANES
FFN_CHUNK = 256
LN_ROWS = 256
ATTN_BLOCK = 512


def _params(sem, flags=None):
    return pltpu.CompilerParams(dimension_semantics=sem, vmem_limit_bytes=VMEM_LIMIT, flags=flags)


def _row_tile(t, pref):
    if t <= pref:
        return t
    tm = pref
    while t % tm:
        tm //= 2
    return tm


def _modulate(x, sc, sh):
    return (x * (1.0 + sc) + sh).astype(BF16)


def _sigmoid(x):
    return 1.0 / (1.0 + jnp.exp(-x))


def _softplus(x):
    return jnp.maximum(x, 0.0) + jnp.log1p(jnp.exp(-jnp.abs(x)))


def _log_sigmoid(x):
    return jnp.minimum(x, 0.0) - jnp.log1p(jnp.exp(-jnp.abs(x)))


def _split3(x):
    hi = x.astype(BF16)
    r1 = x - hi.astype(F32)
    mid = r1.astype(BF16)
    lo = (r1 - mid.astype(F32)).astype(BF16)
    return hi, mid, lo


def _dot(a, b):
    return jnp.dot(a, b, preferred_element_type=F32)


def _dot_nt(a, b):
    return lax.dot_general(a, b, (((1,), (1,)), ((), ())), preferred_element_type=F32)


def _mods_kernel(c_ref, w_ref, b_ref, o_ref):
    c = c_ref[...]
    s = (c * _sigmoid(c)).astype(BF16)
    o_ref[0] = _dot(s, w_ref[0].astype(BF16)) + b_ref[0]


def _mods(c_all, w_ada, b_ada):
    rows, d = c_all.shape
    n = w_ada.shape[-1]
    tn = 1536
    return pl.pallas_call(
        _mods_kernel,
        grid=(DEPTH, n // tn),
        in_specs=[pl.BlockSpec((rows, d), lambda l, j: (0, 0)),
                  pl.BlockSpec((1, d, tn), lambda l, j: (l, 0, j)),
                  pl.BlockSpec((1, 1, tn), lambda l, j: (l, 0, j))],
        out_specs=pl.BlockSpec((1, rows, tn), lambda l, j: (l, 0, j)),
        out_shape=jax.ShapeDtypeStruct((DEPTH, rows, n), F32),
        compiler_params=_params(("arbitrary", "arbitrary")),
        name="mods",
    )(c_all, w_ada, b_ada.reshape(DEPTH, 1, n))


def _rope_slab(x, cos, sin_signed, first_half):
    fwd = pltpu.roll(x, LANES - A_HEAD_DIM // 2, 1)
    bwd = pltpu.roll(x, A_HEAD_DIM // 2, 1)
    partner = jnp.where(first_half, fwd, bwd)
    return x * cos + partner * sin_signed


def _proj_ab_kernel(x_ref, sc_ref, sh_ref, w_ref, cos_ref, sin_ref,
                    q_ref, kb_ref, vt_ref, k_ref, v_ref, xg_ref, *, q_scale):
    u = _modulate(x_ref[0], sc_ref[0], sh_ref[0])
    pr = _dot(u, w_ref[...])
    cos = cos_ref[...]
    sin = sin_ref[...]
    lane = lax.broadcasted_iota(jnp.int32, (1, LANES), 1)
    first_half = (lane & (A_HEAD_DIM - 1)) < (A_HEAD_DIM // 2)
    for h in range(A_HEADS):
        sl = slice(h * LANES, (h + 1) * LANES)
        q = _rope_slab(pr[:, sl], cos, sin, first_half)
        q_ref[0, :, sl] = (q * q_scale).astype(BF16)
        k = _rope_slab(pr[:, A_WIDTH + h * LANES:A_WIDTH + (h + 1) * LANES], cos, sin, first_half)
        k_ref[0, :, h, :] = k
        kb_ref[0, :, sl] = k.astype(BF16)
    for h in range(A_HEADS):
        v_ref[0, :, h, :] = pr[:, 2 * A_WIDTH + h * LANES:2 * A_WIDTH + (h + 1) * LANES]
    vt_ref[0, 0] = pr[:, 2 * A_WIDTH:3 * A_WIDTH].T.astype(BF16)
    xg_ref[0] = pr[:, 3 * A_WIDTH:]


def _proj_ab(x, sc, sh, w, cos, sin):
    b, t, d = x.shape
    tm = _row_tile(t, ATTN_BLOCK)
    nt = t // tm
    n = w.shape[1]
    q_scale = A_HEAD_DIM ** -0.5 * LOG2E
    row = lambda width: pl.BlockSpec((1, tm, width), lambda bi, i: (bi, i, 0))
    const = lambda shape: pl.BlockSpec(shape, lambda bi, i: tuple(0 for _ in shape))
    heads = pl.BlockSpec((1, tm, A_HEADS, LANES), lambda bi, i: (bi, i, 0, 0))
    return pl.pallas_call(
        functools.partial(_proj_ab_kernel, q_scale=q_scale),
        grid=(b, nt),
        in_specs=[row(d),
                  pl.BlockSpec((1, 1, d), lambda bi, i: (bi, 0, 0)),
                  pl.BlockSpec((1, 1, d), lambda bi, i: (bi, 0, 0)),
                  const((d, n)),
                  pl.BlockSpec((tm, LANES), lambda bi, i: (i, 0)),
                  pl.BlockSpec((tm, LANES), lambda bi, i: (i, 0))],
        out_specs=[row(A_WIDTH), row(A_WIDTH),
                   pl.BlockSpec((1, 1, A_WIDTH, tm), lambda bi, i: (bi, i, 0, 0)),
                   heads, heads, row(2 * B_WIDTH)],
        out_shape=[jax.ShapeDtypeStruct((b, t, A_WIDTH), BF16),
                   jax.ShapeDtypeStruct((b, t, A_WIDTH), BF16),
                   jax.ShapeDtypeStruct((b, nt, A_WIDTH, tm), BF16),
                   jax.ShapeDtypeStruct((b, t, A_HEADS, LANES), F32),
                   jax.ShapeDtypeStruct((b, t, A_HEADS, LANES), F32),
                   jax.ShapeDtypeStruct((b, t, 2 * B_WIDTH), F32)],
        compiler_params=_params(("arbitrary", "arbitrary")),
        name="proj_ab",
    )(x, sc, sh, w, cos, sin)


def _bias_placement():
    e = np.zeros((3 * LANES, BIAS_WIDTH), np.float32)
    for piece in range(3):
        for h in range(C_HEADS):
            e[piece * LANES + h, (h // 2) * LANES + 3 * (h % 2) + piece] = 1.0
    return jnp.asarray(e, BF16)


def _proj_c_kernel(x_ref, sc_ref, sh_ref, w_ref, wft_ref, bfr_ref, bfc_ref, place_ref,
                   q_ref, kb_ref, bias_ref, vt_ref, k_ref, v_ref, lf_ref, lft_ref, run_ref, *, q_scale):
    i = pl.program_id(1)
    tm = x_ref.shape[1]

    @pl.when(i == 0)
    def _():
        run_ref[...] = jnp.zeros_like(run_ref)

    u = _modulate(x_ref[0], sc_ref[0], sh_ref[0])
    pr = _dot(u, w_ref[...])
    q_ref[0] = (pr[:, :C_WIDTH] * q_scale).astype(BF16)
    k = pr[:, C_WIDTH:2 * C_WIDTH]
    k_ref[0] = k
    kb_ref[0] = k.astype(BF16)
    v = pr[:, 2 * C_WIDTH:3 * C_WIDTH]
    v_ref[0] = v
    vt_ref[0, 0] = v.T.astype(BF16)
    lf = _log_sigmoid(pr[:, 3 * C_WIDTH:] + bfr_ref[...])
    lf_ref[0] = lf[:, :C_HEADS]
    lft_ref[0] = _log_sigmoid(_dot_nt(wft_ref[...], u) + bfc_ref[...])
    r = lax.broadcasted_iota(jnp.int32, (tm, tm), 0)
    c = lax.broadcasted_iota(jnp.int32, (tm, tm), 1)
    lower = jnp.where(c <= r, 1.0, 0.0).astype(BF16)
    hi, mid, lo = _split3(lf)
    cum = (_dot(lower, hi) + _dot(lower, mid)) + _dot(lower, lo) + run_ref[...]
    pieces = jnp.concatenate(_split3(cum * (-LOG2E)), axis=1)
    bias_ref[0] = _dot(pieces, place_ref[...]).astype(BF16)
    run_ref[...] = run_ref[...] + jnp.sum(lf, axis=0, keepdims=True)


def _proj_c(x, sc, sh, w, wft, bf_row, bf_col, place):
    b, t, d = x.shape
    tm = _row_tile(t, ATTN_BLOCK)
    nt = t // tm
    n = w.shape[1]
    q_scale = C_HEAD_DIM ** -0.5 * LOG2E
    row = lambda width: pl.BlockSpec((1, tm, width), lambda bi, i: (bi, i, 0))
    const = lambda shape: pl.BlockSpec(shape, lambda bi, i: tuple(0 for _ in shape))
    return pl.pallas_call(
        functools.partial(_proj_c_kernel, q_scale=q_scale),
        grid=(b, nt),
        in_specs=[row(d),
                  pl.BlockSpec((1, 1, d), lambda bi, i: (bi, 0, 0)),
                  pl.BlockSpec((1, 1, d), lambda bi, i: (bi, 0, 0)),
                  const((d, n)), const((C_HEADS, d)),
                  const((1, LANES)), const((C_HEADS, 1)), const((3 * LANES, BIAS_WIDTH))],
        out_specs=[row(C_WIDTH), row(C_WIDTH), row(BIAS_WIDTH),
                   pl.BlockSpec((1, 1, C_WIDTH, tm), lambda bi, i: (bi, i, 0, 0)),
                   row(C_WIDTH), row(C_WIDTH), row(C_HEADS),
                   pl.BlockSpec((1, C_HEADS, tm), lambda bi, i: (bi, 0, i))],
        out_shape=[jax.ShapeDtypeStruct((b, t, C_WIDTH), BF16),
                   jax.ShapeDtypeStruct((b, t, C_WIDTH), BF16),
                   jax.ShapeDtypeStruct((b, t, BIAS_WIDTH), BF16),
                   jax.ShapeDtypeStruct((b, nt, C_WIDTH, tm), BF16),
                   jax.ShapeDtypeStruct((b, t, C_WIDTH), F32),
                   jax.ShapeDtypeStruct((b, t, C_WIDTH), F32),
                   jax.ShapeDtypeStruct((b, t, C_HEADS), F32),
                   jax.ShapeDtypeStruct((b, C_HEADS, t), F32)],
        scratch_shapes=[pltpu.VMEM((1, LANES), F32)],
        compiler_params=_params(("arbitrary", "arbitrary")),
        name="proj_c",
    )(x, sc, sh, w, wft, bf_row, bf_col, place)


def _lambda(lam_ref, lam_init):
    lq1, lk1, lq2, lk2 = (lam_ref[r:r + 1, :] for r in range(4))
    return (jnp.exp(jnp.sum(lq1 * lk1, axis=1, keepdims=True))
            - jnp.exp(jnp.sum(lq2 * lk2, axis=1, keepdims=True)) + lam_init)


def _attn_kernel(*refs, mode, lam_init):
    if mode == "A":
        q_ref, k_ref, vt_ref, lam_ref, gain_ref, o_ref, qc_ref, m_ref, acc_ref, s0_ref, s1_ref, c0_ref, c1_ref = refs
        bias_ref = None
        d_val = 2 * A_HEAD_DIM
    else:
        q_ref, k_ref, bias_ref, vt_ref, o_ref, qc_ref, m_ref, acc_ref, s0_ref, s1_ref, c0_ref, c1_ref = refs
        d_val = C_HEAD_DIM
    tq = q_ref.shape[1]
    tk = k_ref.shape[2]
    qi = pl.program_id(2)
    lane = lax.broadcasted_iota(jnp.int32, (1, LANES), 1)
    low = lane < (LANES // 2)
    q = q_ref[0]
    zero = jnp.zeros_like(q)
    for a in range(2):
        qa = jnp.where(low, q, zero) if a == 0 else jnp.where(low, zero, q)
        if bias_ref is not None:
            pick = jnp.where((lane >= 3 * a) & (lane < 3 * a + 3), 1.0, 0.0).astype(BF16)
            qa = jnp.concatenate([qa, jnp.broadcast_to(pick, (tq, LANES))], axis=1)
        qc_ref[a] = qa
    m_ref[...] = jnp.full(m_ref.shape, NEG, F32)
    acc_ref[...] = jnp.zeros(acc_ref.shape, F32)
    ones = jnp.ones((BF16_ROWS, tk), BF16)

    units = [(a, slice(n * MXU_DIM, (n + 1) * MXU_DIM)) for a in range(2) for n in range(tq // MXU_DIM)]

    def score_chain(j, u):
        a, cs = units[u]
        kc = k_ref[0, j]
        if bias_ref is not None:
            kc = jnp.concatenate([kc, bias_ref[0, j]], axis=1)
        return _dot_nt(kc, qc_ref[a, cs, :])

    def value_chain(j, u, st, cmax, key_off):
        a, cs = units[u]
        masked = key_off is not None
        if masked:
            key = lax.broadcasted_iota(jnp.int32, (tk, MXU_DIM), 0) + key_off
            qry = lax.broadcasted_iota(jnp.int32, (tk, MXU_DIM), 1) + cs.start
            keep = ((key >> CHUNK_SHIFT) <= (qry >> CHUNK_SHIFT)) if mode == "A" else (key <= qry)
            st = jnp.where(keep, st, NEG)
            cmax = jnp.max(st, axis=0, keepdims=True)
        m_prev = m_ref[a, :, cs]
        m_new = jnp.maximum(m_prev, cmax)
        alpha = jnp.exp2(m_prev - m_new)
        pt = jnp.exp2(st - m_new).astype(BF16)
        vt = vt_ref[0, j]
        va = vt if mode == "A" else vt[a * d_val:(a + 1) * d_val]
        va = jnp.concatenate([va, ones], axis=0)
        acc_ref[a, :, cs] = alpha * acc_ref[a, :, cs] + _dot(va, pt)
        m_ref[a, :, cs] = m_new

    def stage(j_scores, dst, j_values, src, key_off=None):
        for t in range(len(units) + 1):
            if j_scores is not None and t < len(units):
                st = score_chain(j_scores, t)
                dst[0][t] = st
                dst[1][t] = jnp.max(st, axis=0, keepdims=True)
            if j_values is not None and t >= 1:
                value_chain(j_values, t - 1, src[0][t - 1], src[1][t - 1], key_off)

    buf0 = (s0_ref, c0_ref)
    buf1 = (s1_ref, c1_ref)
    n_diag = tq // tk
    n_full = qi * n_diag
    stage(0, buf0, None, None)

    def pair(j):
        stage(j + 1, buf1, j, buf0)
        stage(j + 2, buf0, j + 1, buf1)

    def quad(p, carry):
        pair(4 * p)
        pair(4 * p + 2)
        return carry

    lax.fori_loop(0, n_full >> 2, quad, 0)

    @pl.when((n_full & 2) == 2)
    def _():
        pair(n_full & ~3)

    for d in range(0, n_diag, 2):
        j = n_full + d
        stage(j + 1, buf1, j, buf0, key_off=d * tk)
        stage(j + 2 if d + 2 < n_diag else None, buf0, j + 1, buf1, key_off=(d + 1) * tk)

    outs = []
    for a in range(2):
        acc = acc_ref[a]
        outs.append(acc[:d_val] * (1.0 / acc[d_val:d_val + 1]))
    if mode == "A":
        ot = outs[0] - _lambda(lam_ref, lam_init) * outs[1]
        ot = ot * lax.rsqrt(jnp.mean(ot * ot, axis=0, keepdims=True) + LN_EPS)
        ot = ot * (gain_ref[0] * (1.0 - lam_init))
    else:
        ot = jnp.concatenate(outs, axis=0)
    o_ref[0] = ot.T.astype(o_ref.dtype)


def _attention(q, k, vt, mode, lam=None, gain=None, bias=None, lam_init=0.0):
    b, t, width = q.shape
    nk, tk = vt.shape[1], vt.shape[3]
    tq = 2 * tk
    assert t % tq == 0
    groups = width // LANES
    d_aug = (2 * A_HEAD_DIM if mode == "A" else C_HEAD_DIM) + BF16_ROWS
    kspec = pl.BlockSpec((1, nk, tk, LANES), lambda bi, g, i: (bi, 0, 0, g))
    in_specs = [pl.BlockSpec((1, tq, LANES), lambda bi, g, i: (bi, i, g)), kspec]
    args = [q, k.reshape(b, nk, tk, width)]
    if mode == "C":
        in_specs.append(kspec)
        args.append(bias.reshape(b, nk, tk, groups * LANES))
    in_specs.append(pl.BlockSpec((1, nk, LANES, tk), lambda bi, g, i: (bi, 0, g, 0)))
    args.append(vt)
    if mode == "A":
        in_specs += [pl.BlockSpec((4, A_HEAD_DIM), lambda bi, g, i: (0, 0)),
                     pl.BlockSpec((1, LANES, 1), lambda bi, g, i: (g, 0, 0))]
        args += [lam, gain]
    return pl.pallas_call(
        functools.partial(_attn_kernel, mode=mode, lam_init=lam_init),
        grid=(b, groups, t // tq),
        in_specs=in_specs,
        out_specs=pl.BlockSpec((1, tq, LANES), lambda bi, g, i: (bi, i, g)),
        out_shape=jax.ShapeDtypeStruct((b, t, width), BF16),
        scratch_shapes=[pltpu.VMEM((2, tq, LANES if mode == "A" else 2 * LANES), BF16),
                        pltpu.VMEM((2, 1, tq), F32),
                        pltpu.VMEM((2, d_aug, tq), F32),
                        pltpu.VMEM((2 * tq // MXU_DIM, tk, MXU_DIM), F32),
                        pltpu.VMEM((2 * tq // MXU_DIM, tk, MXU_DIM), F32),
                        pltpu.VMEM((2 * tq // MXU_DIM, 1, MXU_DIM), F32),
                        pltpu.VMEM((2 * tq // MXU_DIM, 1, MXU_DIM), F32)],
        compiler_params=_params(("arbitrary", "arbitrary", "arbitrary")),
        name="attn_" + mode,
    )(*args)


def _diff_finish(o0, o1, lam, gain, lam_init):
    o = o0 - lam * o1
    o = o * lax.rsqrt(jnp.mean(o * o, axis=-1, keepdims=True) + LN_EPS)
    return o * gain * (1.0 - lam_init)


def _cached_attn_kernel(*refs, mode, past, lam_init):
    if mode == "A":
        q_ref, kn_ref, vn_ref, ck_ref, cv_ref, lam_ref, gain_ref, o_ref = refs
    else:
        q_ref, kn_ref, vn_ref, ck_ref, cv_ref, clf_ref, nlf_ref, o_ref = refs
    t = q_ref.shape[1]
    width = o_ref.shape[2]
    groups = width // LANES
    lane = lax.broadcasted_iota(jnp.int32, (1, LANES), 1)
    low = lane < (LANES // 2)
    row = lax.broadcasted_iota(jnp.int32, (t, t), 0)
    col = lax.broadcasted_iota(jnp.int32, (t, t), 1)
    if mode == "A":
        keep = ((past + col) >> CHUNK_SHIFT) <= ((past + row) >> CHUNK_SHIFT)
        lam = _lambda(lam_ref, lam_init)
    else:
        keep = col <= row
        r = lax.broadcasted_iota(jnp.int32, (past, past), 0)
        c = lax.broadcasted_iota(jnp.int32, (past, past), 1)
        upper = jnp.where(r <= c, 1.0, 0.0).astype(BF16)
        clf = clf_ref[0]
        hi, mid, lo = _split3(clf)
        cum_c = (_dot(hi, upper) + _dot(mid, upper)) + _dot(lo, upper)
        upper_n = jnp.where(row <= col, 1.0, 0.0).astype(BF16)
        hi, mid, lo = _split3(nlf_ref[0])
        cum_n = ((_dot(hi, upper_n) + _dot(mid, upper_n)) + _dot(lo, upper_n)
                 + jnp.sum(clf, axis=1, keepdims=True))
        bias_c = cum_c * (-LOG2E)
        bias_n = cum_n * (-LOG2E)
    for g in range(groups):
        sl = slice(g * LANES, (g + 1) * LANES)
        q = q_ref[0, :, sl]
        kn = kn_ref[0, :, sl]
        vn = vn_ref[0, :, sl].astype(BF16)
        if mode == "A":
            kc = ck_ref[0, 0, :, g, :].astype(BF16)
            vc = cv_ref[0, 0, :, g, :].astype(BF16)
        else:
            kc = ck_ref[0, :, sl].astype(BF16)
            vc = cv_ref[0, :, sl].astype(BF16)
        outs = []
        for a in range(2):
            qa = jnp.where(low, q, jnp.zeros_like(q)) if a == 0 else jnp.where(low, jnp.zeros_like(q), q)
            s_c = _dot_nt(qa, kc)
            s_n = _dot_nt(qa, kn)
            if mode == "C":
                h = 2 * g + a
                s_c = s_c + bias_c[h:h + 1, :]
                s_n = s_n + bias_n[h:h + 1, :]
            s_n = jnp.where(keep, s_n, NEG)
            m = jnp.maximum(jnp.max(s_c, axis=1, keepdims=True), jnp.max(s_n, axis=1, keepdims=True))
            p_c = jnp.exp2(s_c - m)
            p_n = jnp.exp2(s_n - m)
            l = jnp.sum(p_c, axis=1, keepdims=True) + jnp.sum(p_n, axis=1, keepdims=True)
            acc = _dot(p_c.astype(BF16), vc) + _dot(p_n.astype(BF16), vn)
            outs.append(acc * (1.0 / l))
        if mode == "A":
            o = _diff_finish(outs[0], outs[1], lam, gain_ref[g], lam_init)
        else:
            o = jnp.where(low, outs[0], outs[1])
        o_ref[0, :, sl] = o.astype(o_ref.dtype)


def _cached_attention(q, kn, vn, cache_k, cache_v, mode, lam=None, gain=None, cache_lft=None, new_lft=None,
                      lam_init=0.0):
    b, t, width = q.shape
    new = pl.BlockSpec((1, t, width), lambda bi: (bi, 0, 0))
    if mode == "A":
        _, _, past, heads, dh = cache_k.shape
        old = pl.BlockSpec((1, 1, past, heads, dh), lambda bi: (0, bi, 0, 0, 0))
    else:
        past = cache_k.shape[1]
        old = pl.BlockSpec((1, past, width), lambda bi: (bi, 0, 0))
    in_specs = [new, new, new, old, old]
    args = [q, kn, vn, cache_k, cache_v]
    if mode == "A":
        in_specs += [pl.BlockSpec((4, A_HEAD_DIM), lambda bi: (0, 0)),
                     pl.BlockSpec((A_HEADS, 1, LANES), lambda bi: (0, 0, 0))]
        args += [lam, gain]
    else:
        in_specs += [pl.BlockSpec((1, C_HEADS, past), lambda bi: (bi, 0, 0)),
                     pl.BlockSpec((1, C_HEADS, t), lambda bi: (bi, 0, 0))]
        args += [cache_lft, new_lft]
    return pl.pallas_call(
        functools.partial(_cached_attn_kernel, mode=mode, past=past, lam_init=lam_init),
        grid=(b,),
        in_specs=in_specs,
        out_specs=new,
        out_shape=jax.ShapeDtypeStruct((b, t, width), BF16),
        compiler_params=_params(("arbitrary",)),
        name="cached_attn_" + mode,
    )(*args)


def _gelu_tanh(x):
    return 0.5 * x * (1.0 + jnp.tanh(math.sqrt(2.0 / math.pi) * (x + 0.044715 * (x * x * x))))


def _causal_conv(x, tail, w_ref, b_ref):
    width = w_ref.shape[0]
    tm = x.shape[0]
    cat = jnp.concatenate([tail, x], axis=0)
    y = None
    for j in range(width):
        back = width - 1 - j
        src = cat if back == 0 else pltpu.roll(cat, back, 0)
        term = src[SUBLANES:SUBLANES + tm] * w_ref[j:j + 1, :]
        y = term if y is None else y + term
    return y + b_ref[...]


def _rglru_kernel(xg_ref, h0_ref, cb_ref, cw_ref, cbias_ref, wa_ref, ba_ref, wx_ref, bx_ref, sp_ref,
                  y_ref, hl_ref, ct_ref, hc_ref, tail_ref):
    i = pl.program_id(1)
    tm = xg_ref.shape[1]

    @pl.when(i == 0)
    def _():
        hc_ref[...] = h0_ref[0]
        tail_ref[...] = cb_ref[0]

    x = xg_ref[0, :, :B_WIDTH]
    gate_in = xg_ref[0, :, B_WIDTH:]
    xc = _causal_conv(x, tail_ref[...], cw_ref, cbias_ref)
    xcb = xc.astype(BF16)
    r = _sigmoid(_dot(xcb, wa_ref[...]) + ba_ref[...])
    ig = _sigmoid(_dot(xcb, wx_ref[...]) + bx_ref[...])
    log_a = (-RG_C) * r * sp_ref[...]
    a = jnp.exp(log_a)
    th = jnp.tanh(log_a)
    bx = jnp.sqrt((-2.0 * th) / (1.0 - th)) * (ig * xc)
    row = lax.broadcasted_iota(jnp.int32, (tm, 1), 0)
    s = 1
    while s < tm:
        valid = row >= s
        a_sh = pltpu.roll(a, s, 0)
        b_sh = pltpu.roll(bx, s, 0)
        bx = jnp.where(valid, a * b_sh + bx, bx)
        a = jnp.where(valid, a * a_sh, a)
        s *= 2
    h = a * hc_ref[...] + bx
    y_ref[0] = (h * _gelu_tanh(gate_in)).astype(y_ref.dtype)
    h_tail = h[tm - SUBLANES:, :]
    x_tail = x[tm - SUBLANES:, :]
    hc_ref[...] = h_tail[SUBLANES - 1:, :]
    tail_ref[...] = x_tail
    hl_ref[0] = h_tail
    ct_ref[0] = x_tail


def _rglru(xg, h0, conv_tail, cw, cbias, wa, ba, wx, bx, sp, tm_pref=256):
    b, t, _ = xg.shape
    tm = _row_tile(t, tm_pref)
    wspec = lambda shape: pl.BlockSpec(shape, lambda bi, i: tuple(0 for _ in shape))
    return pl.pallas_call(
        _rglru_kernel,
        grid=(b, t // tm),
        in_specs=[pl.BlockSpec((1, tm, 2 * B_WIDTH), lambda bi, i: (bi, i, 0)),
                  pl.BlockSpec((1, 1, B_WIDTH), lambda bi, i: (bi, 0, 0)),
                  pl.BlockSpec((1, SUBLANES, B_WIDTH), lambda bi, i: (bi, 0, 0)),
                  wspec((B_CONV, B_WIDTH)), wspec((1, B_WIDTH)),
                  wspec((B_WIDTH, B_WIDTH)), wspec((1, B_WIDTH)),
                  wspec((B_WIDTH, B_WIDTH)), wspec((1, B_WIDTH)), wspec((1, B_WIDTH))],
        out_specs=[pl.BlockSpec((1, tm, B_WIDTH), lambda bi, i: (bi, i, 0)),
                   pl.BlockSpec((1, SUBLANES, B_WIDTH), lambda bi, i: (bi, 0, 0)),
                   pl.BlockSpec((1, SUBLANES, B_WIDTH), lambda bi, i: (bi, 0, 0))],
        out_shape=[jax.ShapeDtypeStruct((b, t, B_WIDTH), BF16),
                   jax.ShapeDtypeStruct((b, SUBLANES, B_WIDTH), F32),
                   jax.ShapeDtypeStruct((b, SUBLANES, B_WIDTH), F32)],
        scratch_shapes=[pltpu.VMEM((1, B_WIDTH), F32), pltpu.VMEM((SUBLANES, B_WIDTH), F32)],
        compiler_params=_params(("arbitrary", "arbitrary")),
        name="rglru",
    )(xg, h0, conv_tail, cw, cbias, wa, ba, wx, bx, sp)


def _proj_ln_kernel(*refs, n_in):
    h_refs = refs[:n_in]
    w_ref, x_ref, gate_ref, g_ref, b_ref, o_ref = refs[n_in:]
    tm = x_ref.shape[1]
    rows = [slice(r, min(r + LN_ROWS, tm)) for r in range(0, tm, LN_ROWS)]

    def matmul(rs):
        hs = [r[0, rs, :] for r in h_refs]
        return _dot(hs[0] if n_in == 1 else jnp.concatenate(hs, axis=-1), w_ref[...])

    def norm(rs, proj):
        y = ALPHA * x_ref[0, rs, :] + gate_ref[0] * proj
        mu = jnp.mean(y, axis=-1, keepdims=True)
        yc = y - mu
        var = jnp.mean(yc * yc, axis=-1, keepdims=True)
        o_ref[0, rs, :] = yc * lax.rsqrt(var + LN_EPS) * g_ref[...] + b_ref[...]

    pending = matmul(rows[0])
    for r in range(len(rows)):
        nxt = matmul(rows[r + 1]) if r + 1 < len(rows) else None
        norm(rows[r], pending)
        pending = nxt


def _proj_ln(hs, w, x, gate, ln_g, ln_b, tm_pref=512):
    b, t, d = x.shape
    tm = _row_tile(t, tm_pref)
    k = w.shape[0]
    in_specs = [pl.BlockSpec((1, tm, h.shape[2]), lambda bi, i: (bi, i, 0)) for h in hs]
    in_specs += [pl.BlockSpec((k, d), lambda bi, i: (0, 0)),
                 pl.BlockSpec((1, tm, d), lambda bi, i: (bi, i, 0)),
                 pl.BlockSpec((1, 1, d), lambda bi, i: (bi, 0, 0)),
                 pl.BlockSpec((1, d), lambda bi, i: (0, 0)),
                 pl.BlockSpec((1, d), lambda bi, i: (0, 0))]
    return pl.pallas_call(
        functools.partial(_proj_ln_kernel, n_in=len(hs)),
        grid=(b, t // tm),
        in_specs=in_specs,
        out_specs=pl.BlockSpec((1, tm, d), lambda bi, i: (bi, i, 0)),
        out_shape=jax.ShapeDtypeStruct((b, t, d), F32),
        compiler_params=_params(("arbitrary", "arbitrary")),
        name="proj_ln",
    )(*hs, w, x, gate, ln_g, ln_b)


def _ffn_up_kernel(x_ref, sc_ref, sh_ref, w_ref, cb_ref, cw_ref, cbias_ref, h_ref, ct_ref, tail_ref):
    i = pl.program_id(1)
    tm = x_ref.shape[1]

    @pl.when(i == 0)
    def _():
        tail_ref[...] = cb_ref[0]

    u = _modulate(x_ref[0], sc_ref[0], sh_ref[0])
    chunks = [slice(c, min(c + FFN_CHUNK, D_FF)) for c in range(0, D_FF, FFN_CHUNK)]

    def matmuls(cs):
        gs = slice(D_FF + cs.start, D_FF + cs.stop)
        return _dot(u, w_ref[:, cs]), _dot(u, w_ref[:, gs])

    def gate(cs, a, g):
        gc = _causal_conv(g, tail_ref[:, cs], cw_ref.at[:, cs], cbias_ref.at[:, cs])
        silu = gc * (0.5 * jnp.tanh(0.5 * gc) + 0.5)
        h_ref[0, :, cs] = (a * silu).astype(h_ref.dtype)
        g_tail = g[tm - SUBLANES:, :]
        tail_ref[:, cs] = g_tail
        ct_ref[0, :, cs] = g_tail

    pending = matmuls(chunks[0])
    for c in range(len(chunks)):
        nxt = matmuls(chunks[c + 1]) if c + 1 < len(chunks) else None
        gate(chunks[c], *pending)
        pending = nxt


def _ffn_up(x, sc, sh, w, conv_tail, cw, cbias, tm_pref=256):
    b, t, d = x.shape
    tm = _row_tile(t, tm_pref)
    return pl.pallas_call(
        _ffn_up_kernel,
        grid=(b, t // tm),
        in_specs=[pl.BlockSpec((1, tm, d), lambda bi, i: (bi, i, 0)),
                  pl.BlockSpec((1, 1, d), lambda bi, i: (bi, 0, 0)),
                  pl.BlockSpec((1, 1, d), lambda bi, i: (bi, 0, 0)),
                  pl.BlockSpec((d, 2 * D_FF), lambda bi, i: (0, 0)),
                  pl.BlockSpec((1, SUBLANES, D_FF), lambda bi, i: (bi, 0, 0)),
                  pl.BlockSpec((FFN_CONV, D_FF), lambda bi, i: (0, 0)),
                  pl.BlockSpec((1, D_FF), lambda bi, i: (0, 0))],
        out_specs=[pl.BlockSpec((1, tm, D_FF), lambda bi, i: (bi, i, 0)),
                   pl.BlockSpec((1, SUBLANES, D_FF), lambda bi, i: (bi, 0, 0))],
        out_shape=[jax.ShapeDtypeStruct((b, t, D_FF), BF16),
                   jax.ShapeDtypeStruct((b, SUBLANES, D_FF), F32)],
        scratch_shapes=[pltpu.VMEM((SUBLANES, D_FF), F32)],
        compiler_params=_params(("arbitrary", "arbitrary")),
        name="ffn_up",
    )(x, sc, sh, w, conv_tail, cw, cbias)


def _pad_tail(buf):
    return jnp.pad(buf, ((0, 0), (SUBLANES - buf.shape[1], 0), (0, 0)))


def _rope_tables(past, t):
    half = A_HEAD_DIM // 2
    inv = ROPE_THETA ** (-jnp.arange(0, A_HEAD_DIM, 2, dtype=F32) / A_HEAD_DIM)
    pos = (past + jnp.arange(t, dtype=jnp.int32)).astype(F32)
    ang = pos[:, None] * inv[None, :]
    cos = jnp.tile(jnp.cos(ang), (1, LANES // half))
    sin = jnp.sin(ang)
    sin_signed = jnp.tile(jnp.concatenate([-sin, sin], axis=1), (1, LANES // A_HEAD_DIM))
    return cos, sin_signed


def _block_diag(w):
    n, i, o = w.shape
    return jnp.einsum("nio,nm->nimo", w, jnp.eye(n, dtype=w.dtype)).reshape(n * i, n * o)


def _prepare(p):
    w = {}
    w_in_ab = p["w_in_ab"][0]
    w["in_ab"] = w_in_ab.astype(BF16)
    w["out_ab"] = p["w_out_ab"][0].astype(BF16)
    w["lam"] = jnp.stack([p["lam_q1"][0], p["lam_k1"][0], p["lam_q2"][0], p["lam_k2"][0]])
    w["gain_row"] = p["attn_gain"][0].reshape(A_HEADS, 1, LANES)
    w["gain_col"] = p["attn_gain"][0].reshape(A_HEADS, LANES, 1)
    w["b_conv_w"] = p["b_conv_w"][0]
    w["b_conv_b"] = p["b_conv_b"][0].reshape(1, B_WIDTH)
    w["rg_a"] = _block_diag(p["w_rg_a"][0]).astype(BF16)
    w["rg_x"] = _block_diag(p["w_rg_x"][0]).astype(BF16)
    w["b_rg_a"] = p["b_rg_a"][0].reshape(1, B_WIDTH)
    w["b_rg_x"] = p["b_rg_x"][0].reshape(1, B_WIDTH)
    w["rg_L"] = p["rg_L"][0].reshape(1, B_WIDTH)
    w_in_c = p["w_in_c"][0]
    w["in_c"] = jnp.pad(w_in_c, ((0, 0), (0, LANES - C_HEADS))).astype(BF16)
    w["in_c_ft"] = w_in_c[:, 3 * C_WIDTH:].T.astype(BF16)
    w["bf_row"] = jnp.pad(p["b_f"][0], (0, LANES - C_HEADS)).reshape(1, LANES)
    w["bf_col"] = p["b_f"][0].reshape(C_HEADS, 1)
    w["place"] = _bias_placement()
    w["out_c"] = p["w_out_c"][0].astype(BF16)
    w["up"] = [p["w_up"][i].astype(BF16) for i in range(DEPTH)]
    w["down"] = [p["w_down"][i].astype(BF16) for i in range(DEPTH)]
    w["ffn_conv_w"] = [p["ffn_conv_w"][i] for i in range(DEPTH)]
    w["ffn_conv_b"] = [p["ffn_conv_b"][i].reshape(1, D_FF) for i in range(DEPTH)]
    w["ln1_g"] = [p["ln1_g"][i].reshape(1, D_MODEL) for i in range(DEPTH)]
    w["ln1_b"] = [p["ln1_b"][i].reshape(1, D_MODEL) for i in range(DEPTH)]
    w["ln2_g"] = [p["ln2_g"][i].reshape(1, D_MODEL) for i in range(DEPTH)]
    w["ln2_b"] = [p["ln2_b"][i].reshape(1, D_MODEL) for i in range(DEPTH)]
    return w


def _softplus_kernel(x_ref, o_ref):
    o_ref[...] = _softplus(-x_ref[...])


def _trunk(x, mods, w, sp, cache_a_k=None, cache_a_v=None, state_b_h=None, state_b_conv=None,
           cache_c_k=None, cache_c_v=None, cache_c_logf=None, state_ffn_conv=None):
    b, t, d = x.shape
    cached = cache_a_k is not None
    past = cache_a_k.shape[2] if cached else 0
    outs = {}
    for i in range(DEPTH):
        sh1, sc1, g1, sh2, sc2, g2 = [m[:, None, :] for m in jnp.split(mods[i], 6, axis=-1)]
        if i % 2 == 0:
            lam_init = 0.8 - 0.6 * math.exp(-0.3 * i)
            cos, sin = _rope_tables(past, t)
            q, kb, vt, k32, v32, xg = _proj_ab(x, sc1, sh1, w["in_ab"], cos, sin)
            if cached:
                o = _cached_attention(q, kb, v32.reshape(b, t, A_WIDTH), cache_a_k, cache_a_v,
                                      "A", lam=w["lam"], gain=w["gain_row"], lam_init=lam_init)
                h0 = state_b_h[0][:, None, :]
                ctail = _pad_tail(state_b_conv[0])
            else:
                o = _attention(q, kb, vt, "A", lam=w["lam"], gain=w["gain_col"], lam_init=lam_init)
                h0 = jnp.zeros((b, 1, B_WIDTH), F32)
                ctail = jnp.zeros((b, SUBLANES, B_WIDTH), F32)
            yb, h_tail, x_tail = _rglru(xg, h0, ctail, w["b_conv_w"], w["b_conv_b"], w["rg_a"], w["b_rg_a"],
                                        w["rg_x"], w["b_rg_x"], sp)
            outs["a_k"] = k32[None]
            outs["a_v"] = v32[None]
            outs["b_h"] = h_tail[:, SUBLANES - 1, :][None]
            outs["b_conv"] = x_tail[:, SUBLANES - (B_CONV - 1):, :][None]
            x = _proj_ln([o, yb], w["out_ab"], x, g1, w["ln1_g"][i], w["ln1_b"][i])
        else:
            q, kb, bias, vt, k32, v32, lf, lft = _proj_c(x, sc1, sh1, w["in_c"], w["in_c_ft"],
                                                         w["bf_row"], w["bf_col"], w["place"])
            if cached:
                o = _cached_attention(q, kb, v32, cache_c_k[0].reshape(b, past, C_WIDTH),
                                      cache_c_v[0].reshape(b, past, C_WIDTH),
                                      "C", cache_lft=jnp.swapaxes(cache_c_logf[0], 1, 2), new_lft=lft)
            else:
                o = _attention(q, kb, vt, "C", bias=bias)
            outs["c_k"] = k32.reshape(1, b, t, C_HEADS, C_HEAD_DIM)
            outs["c_v"] = v32.reshape(1, b, t, C_HEADS, C_HEAD_DIM)
            outs["c_logf"] = lf[None]
            x = _proj_ln([o], w["out_c"], x, g1, w["ln1_g"][i], w["ln1_b"][i])
        ftail = _pad_tail(state_ffn_conv[i]) if cached else jnp.zeros((b, SUBLANES, D_FF), F32)
        hmid, g_tail = _ffn_up(x, sc2, sh2, w["up"][i], ftail, w["ffn_conv_w"][i], w["ffn_conv_b"][i])
        outs.setdefault("ffn", []).append(g_tail[:, SUBLANES - (FFN_CONV - 1):, :])
        x = _proj_ln([hmid], w["down"][i], x, g2, w["ln2_g"][i], w["ln2_b"][i])
    return (x, outs["a_k"], outs["a_v"], outs["b_h"], outs["b_conv"],
            outs["c_k"], outs["c_v"], outs["c_logf"], jnp.stack(outs["ffn"]))


def kernel(x_prompt, x_sample, c_prompt, c_sample, cache_a_k, cache_a_v, state_b_h, state_b_conv, cache_c_k, cache_c_v, cache_c_logf, state_ffn_conv, w_ada, b_ada, ln1_g, ln1_b, ln2_g, ln2_b, w_in_ab, lam_q1, lam_k1, lam_q2, lam_k2, attn_gain, b_conv_w, b_conv_b, w_rg_a, b_rg_a, w_rg_x, b_rg_x, rg_L, w_out_ab, w_in_c, b_f, w_out_c, w_up, ffn_conv_w, ffn_conv_b, w_down):
    p = dict(w_in_ab=w_in_ab, lam_q1=lam_q1, lam_k1=lam_k1, lam_q2=lam_q2, lam_k2=lam_k2, attn_gain=attn_gain,
             b_conv_w=b_conv_w, b_conv_b=b_conv_b, w_rg_a=w_rg_a, b_rg_a=b_rg_a, w_rg_x=w_rg_x, b_rg_x=b_rg_x,
             rg_L=rg_L, w_out_ab=w_out_ab, w_in_c=w_in_c, b_f=b_f, w_out_c=w_out_c, w_up=w_up,
             ffn_conv_w=ffn_conv_w, ffn_conv_b=ffn_conv_b, w_down=w_down,
             ln1_g=ln1_g, ln1_b=ln1_b, ln2_g=ln2_g, ln2_b=ln2_b)
    w = _prepare(p)
    bp = c_prompt.shape[0]
    bs = c_sample.shape[0]
    rows = -(-(bp + bs) // 16) * 16
    c_all = jnp.pad(jnp.concatenate([c_prompt, c_sample], axis=0), ((0, rows - bp - bs), (0, 0)))
    mods = _mods(c_all, w_ada, b_ada)
    sp = pl.pallas_call(_softplus_kernel, out_shape=jax.ShapeDtypeStruct((1, B_WIDTH), F32),
                        name="softplus")(w["rg_L"])
    res_p = _trunk(x_prompt, mods[:, :bp], w, sp)
    res_s = _trunk(x_sample, mods[:, bp:bp + bs], w, sp, cache_a_k, cache_a_v, state_b_h, state_b_conv,
                   cache_c_k, cache_c_v, cache_c_logf, state_ffn_conv)
    return (res_p[0], res_s[0]) + res_p[1:] + res_s[1:]
```

```python
import functools
import math

import numpy as np
import jax
import jax.numpy as jnp
from jax import lax
from jax.experimental import pallas as pl
from jax.experimental.pallas import tpu as pltpu

F32 = jnp.float32
BF16 = jnp.bfloat16

D_MODEL = 1024
DEPTH = 2
CHUNK = 64
CHUNK_SHIFT = 6
A_HEADS = 4
A_HEAD_DIM = 64
A_WIDTH = A_HEADS * 2 * A_HEAD_DIM
B_WIDTH = 512
B_BLOCKS = 8
B_CONV = 4
RG_C = 8.0
C_HEADS = 16
C_HEAD_DIM = 64
C_WIDTH = C_HEADS * C_HEAD_DIM
D_FF = 2816
FFN_CONV = 3
ROPE_THETA = 10000.0
ALPHA = (2 * DEPTH) ** 0.25
LN_EPS = 1e-5
NEG = -1e30
LOG2E = 1.4426950408889634

LANES = 128
SUBLANES = 8
BF16_ROWS = 16
MXU_DIM = 256
VMEM_LIMIT = 56 * 1024 * 1024
BIAS_WIDTH = (C_HEADS // 2) * LANES
FFN_CHUNK = 256
LN_ROWS = 256
SKIP_GAP = 160.0
NORM_SLACK = 1.03
ATTN_BLOCK = 512


def _params(sem, flags=None):
    return pltpu.CompilerParams(dimension_semantics=sem, vmem_limit_bytes=VMEM_LIMIT, flags=flags)


def _row_tile(t, pref):
    if t <= pref:
        return t
    tm = pref
    while t % tm:
        tm //= 2
    return tm


def _modulate(x, sc, sh):
    return (x * (1.0 + sc) + sh).astype(BF16)


def _sigmoid(x):
    return 1.0 / (1.0 + jnp.exp(-x))


def _softplus(x):
    return jnp.maximum(x, 0.0) + jnp.log1p(jnp.exp(-jnp.abs(x)))


def _log_sigmoid(x):
    return jnp.minimum(x, 0.0) - jnp.log1p(jnp.exp(-jnp.abs(x)))


def _split3(x):
    hi = x.astype(BF16)
    r1 = x - hi.astype(F32)
    mid = r1.astype(BF16)
    lo = (r1 - mid.astype(F32)).astype(BF16)
    return hi, mid, lo


def _dot(a, b):
    return jnp.dot(a, b, preferred_element_type=F32)


def _dot_nt(a, b):
    return lax.dot_general(a, b, (((1,), (1,)), ((), ())), preferred_element_type=F32)


def _mods_kernel(c_ref, w_ref, b_ref, o_ref):
    c = c_ref[...]
    s = (c * _sigmoid(c)).astype(BF16)
    o_ref[0] = _dot(s, w_ref[0].astype(BF16)) + b_ref[0]


def _mods(c_all, w_ada, b_ada):
    rows, d = c_all.shape
    n = w_ada.shape[-1]
    tn = 1536
    return pl.pallas_call(
        _mods_kernel,
        grid=(DEPTH, n // tn),
        in_specs=[pl.BlockSpec((rows, d), lambda l, j: (0, 0)),
                  pl.BlockSpec((1, d, tn), lambda l, j: (l, 0, j)),
                  pl.BlockSpec((1, 1, tn), lambda l, j: (l, 0, j))],
        out_specs=pl.BlockSpec((1, rows, tn), lambda l, j: (l, 0, j)),
        out_shape=jax.ShapeDtypeStruct((DEPTH, rows, n), F32),
        compiler_params=_params(("arbitrary", "arbitrary")),
        name="mods",
    )(c_all, w_ada, b_ada.reshape(DEPTH, 1, n))


def _rope_slab(x, cos, sin_signed, first_half):
    fwd = pltpu.roll(x, LANES - A_HEAD_DIM // 2, 1)
    bwd = pltpu.roll(x, A_HEAD_DIM // 2, 1)
    partner = jnp.where(first_half, fwd, bwd)
    return x * cos + partner * sin_signed


def _proj_ab_kernel(x_ref, sc_ref, sh_ref, w_ref, cos_ref, sin_ref,
                    q_ref, kb_ref, vt_ref, k_ref, v_ref, xg_ref, *, q_scale):
    u = _modulate(x_ref[0], sc_ref[0], sh_ref[0])
    pr = _dot(u, w_ref[...])
    cos = cos_ref[...]
    sin = sin_ref[...]
    lane = lax.broadcasted_iota(jnp.int32, (1, LANES), 1)
    first_half = (lane & (A_HEAD_DIM - 1)) < (A_HEAD_DIM // 2)
    for h in range(A_HEADS):
        sl = slice(h * LANES, (h + 1) * LANES)
        q = _rope_slab(pr[:, sl], cos, sin, first_half)
        q_ref[0, :, sl] = (q * q_scale).astype(BF16)
        k = _rope_slab(pr[:, A_WIDTH + h * LANES:A_WIDTH + (h + 1) * LANES], cos, sin, first_half)
        k_ref[0, :, h, :] = k
        kb_ref[0, :, sl] = k.astype(BF16)
    for h in range(A_HEADS):
        v_ref[0, :, h, :] = pr[:, 2 * A_WIDTH + h * LANES:2 * A_WIDTH + (h + 1) * LANES]
    vt_ref[0, 0] = pr[:, 2 * A_WIDTH:3 * A_WIDTH].T.astype(BF16)
    xg_ref[0] = pr[:, 3 * A_WIDTH:]


def _proj_ab(x, sc, sh, w, cos, sin):
    b, t, d = x.shape
    tm = _row_tile(t, ATTN_BLOCK)
    nt = t // tm
    n = w.shape[1]
    q_scale = A_HEAD_DIM ** -0.5 * LOG2E
    row = lambda width: pl.BlockSpec((1, tm, width), lambda bi, i: (bi, i, 0))
    const = lambda shape: pl.BlockSpec(shape, lambda bi, i: tuple(0 for _ in shape))
    heads = pl.BlockSpec((1, tm, A_HEADS, LANES), lambda bi, i: (bi, i, 0, 0))
    return pl.pallas_call(
        functools.partial(_proj_ab_kernel, q_scale=q_scale),
        grid=(b, nt),
        in_specs=[row(d),
                  pl.BlockSpec((1, 1, d), lambda bi, i: (bi, 0, 0)),
                  pl.BlockSpec((1, 1, d), lambda bi, i: (bi, 0, 0)),
                  const((d, n)),
                  pl.BlockSpec((tm, LANES), lambda bi, i: (i, 0)),
                  pl.BlockSpec((tm, LANES), lambda bi, i: (i, 0))],
        out_specs=[row(A_WIDTH), row(A_WIDTH),
                   pl.BlockSpec((1, 1, A_WIDTH, tm), lambda bi, i: (bi, i, 0, 0)),
                   heads, heads, row(2 * B_WIDTH)],
        out_shape=[jax.ShapeDtypeStruct((b, t, A_WIDTH), BF16),
                   jax.ShapeDtypeStruct((b, t, A_WIDTH), BF16),
                   jax.ShapeDtypeStruct((b, nt, A_WIDTH, tm), BF16),
                   jax.ShapeDtypeStruct((b, t, A_HEADS, LANES), F32),
                   jax.ShapeDtypeStruct((b, t, A_HEADS, LANES), F32),
                   jax.ShapeDtypeStruct((b, t, 2 * B_WIDTH), F32)],
        compiler_params=_params(("arbitrary", "arbitrary")),
        name="proj_ab",
    )(x, sc, sh, w, cos, sin)


def _bias_placement():
    e = np.zeros((3 * LANES, BIAS_WIDTH), np.float32)
    for piece in range(3):
        for h in range(C_HEADS):
            e[piece * LANES + h, (h // 2) * LANES + 3 * (h % 2) + piece] = 1.0
    return jnp.asarray(e, BF16)


def _head_norm_max(x, seg):
    n2 = _dot((x * x).astype(BF16), seg)
    return jnp.sqrt(jnp.max(n2, axis=0, keepdims=True))


def _proj_c_kernel(x_ref, sc_ref, sh_ref, w_ref, wft_ref, bfr_ref, bfc_ref, place_ref, seg_ref,
                   q_ref, kb_ref, bias_ref, vt_ref, k_ref, v_ref, lf_ref, lft_ref, bound_ref, run_ref, *, q_scale):
    i = pl.program_id(1)
    tm = x_ref.shape[1]

    @pl.when(i == 0)
    def _():
        run_ref[...] = jnp.zeros_like(run_ref)

    u = _modulate(x_ref[0], sc_ref[0], sh_ref[0])
    pr = _dot(u, w_ref[...])
    qs = pr[:, :C_WIDTH] * q_scale
    q_ref[0] = qs.astype(BF16)
    k = pr[:, C_WIDTH:2 * C_WIDTH]
    k_ref[0] = k
    kb_ref[0] = k.astype(BF16)
    v = pr[:, 2 * C_WIDTH:3 * C_WIDTH]
    v_ref[0] = v
    vt_ref[0, 0] = v.T.astype(BF16)
    lf = _log_sigmoid(pr[:, 3 * C_WIDTH:] + bfr_ref[...])
    lf_ref[0] = lf[:, :C_HEADS]
    lft_ref[0] = _log_sigmoid(_dot_nt(wft_ref[...], u) + bfc_ref[...])
    r = lax.broadcasted_iota(jnp.int32, (tm, tm), 0)
    c = lax.broadcasted_iota(jnp.int32, (tm, tm), 1)
    lower = jnp.where(c <= r, 1.0, 0.0).astype(BF16)
    hi, mid, lo = _split3(lf)
    cum = (_dot(lower, hi) + _dot(lower, mid)) + _dot(lower, lo) + run_ref[...]
    bias = cum * (-LOG2E)
    pieces = jnp.concatenate(_split3(bias), axis=1)
    bias_ref[0] = _dot(pieces, place_ref[...]).astype(BF16)
    seg = seg_ref[...]
    bound_ref[0, 0] = jnp.concatenate(
        [_head_norm_max(qs, seg), _head_norm_max(k, seg),
         jnp.max(bias, axis=0, keepdims=True), jnp.min(bias, axis=0, keepdims=True),
         jnp.zeros((SUBLANES - 4, LANES), F32)], axis=0)
    run_ref[...] = run_ref[...] + jnp.sum(lf, axis=0, keepdims=True)


def _proj_c(x, sc, sh, w, wft, bf_row, bf_col, place, seg):
    b, t, d = x.shape
    tm = _row_tile(t, ATTN_BLOCK)
    nt = t // tm
    n = w.shape[1]
    q_scale = C_HEAD_DIM ** -0.5 * LOG2E
    row = lambda width: pl.BlockSpec((1, tm, width), lambda bi, i: (bi, i, 0))
    const = lambda shape: pl.BlockSpec(shape, lambda bi, i: tuple(0 for _ in shape))
    return pl.pallas_call(
        functools.partial(_proj_c_kernel, q_scale=q_scale),
        grid=(b, nt),
        in_specs=[row(d),
                  pl.BlockSpec((1, 1, d), lambda bi, i: (bi, 0, 0)),
                  pl.BlockSpec((1, 1, d), lambda bi, i: (bi, 0, 0)),
                  const((d, n)), const((C_HEADS, d)),
                  const((1, LANES)), const((C_HEADS, 1)), const((3 * LANES, BIAS_WIDTH)), const((C_WIDTH, LANES))],
        out_specs=[row(C_WIDTH), row(C_WIDTH), row(BIAS_WIDTH),
                   pl.BlockSpec((1, 1, C_WIDTH, tm), lambda bi, i: (bi, i, 0, 0)),
                   row(C_WIDTH), row(C_WIDTH), row(C_HEADS),
                   pl.BlockSpec((1, C_HEADS, tm), lambda bi, i: (bi, 0, i)),
                   pl.BlockSpec((1, 1, SUBLANES, LANES), lambda bi, i: (bi, i, 0, 0))],
        out_shape=[jax.ShapeDtypeStruct((b, t, C_WIDTH), BF16),
                   jax.ShapeDtypeStruct((b, t, C_WIDTH), BF16),
                   jax.ShapeDtypeStruct((b, t, BIAS_WIDTH), BF16),
                   jax.ShapeDtypeStruct((b, nt, C_WIDTH, tm), BF16),
                   jax.ShapeDtypeStruct((b, t, C_WIDTH), F32),
                   jax.ShapeDtypeStruct((b, t, C_WIDTH), F32),
                   jax.ShapeDtypeStruct((b, t, C_HEADS), F32),
                   jax.ShapeDtypeStruct((b, C_HEADS, t), F32),
                   jax.ShapeDtypeStruct((b, nt, SUBLANES, LANES), F32)],
        scratch_shapes=[pltpu.VMEM((1, LANES), F32)],
        compiler_params=_params(("arbitrary", "arbitrary")),
        name="proj_c",
    )(x, sc, sh, w, wft, bf_row, bf_col, place, seg)


def _lambda(lam_ref, lam_init):
    lq1, lk1, lq2, lk2 = (lam_ref[r:r + 1, :] for r in range(4))
    return (jnp.exp(jnp.sum(lq1 * lk1, axis=1, keepdims=True))
            - jnp.exp(jnp.sum(lq2 * lk2, axis=1, keepdims=True)) + lam_init)


def _attn_kernel(*refs, mode, lam_init):
    first_ref, refs = refs[0], refs[1:]
    if mode == "A":
        q_ref, k_ref, vt_ref, lam_ref, gain_ref, o_ref, qc_ref, m_ref, acc_ref, s0_ref, s1_ref, c0_ref, c1_ref = refs
        bias_ref = None
        d_val = 2 * A_HEAD_DIM
    else:
        q_ref, k_ref, bias_ref, vt_ref, o_ref, qc_ref, m_ref, acc_ref, s0_ref, s1_ref, c0_ref, c1_ref = refs
        d_val = C_HEAD_DIM
    tq = q_ref.shape[1]
    tk = k_ref.shape[2]
    qi = pl.program_id(2)
    lane = lax.broadcasted_iota(jnp.int32, (1, LANES), 1)
    low = lane < (LANES // 2)
    q = q_ref[0]
    zero = jnp.zeros_like(q)
    for a in range(2):
        qa = jnp.where(low, q, zero) if a == 0 else jnp.where(low, zero, q)
        if bias_ref is not None:
            pick = jnp.where((lane >= 3 * a) & (lane < 3 * a + 3), 1.0, 0.0).astype(BF16)
            qa = jnp.concatenate([qa, jnp.broadcast_to(pick, (tq, LANES))], axis=1)
        qc_ref[a] = qa
    m_ref[...] = jnp.full(m_ref.shape, NEG, F32)
    acc_ref[...] = jnp.zeros(acc_ref.shape, F32)
    ones = jnp.ones((BF16_ROWS, tk), BF16)

    units = [(a, slice(n * MXU_DIM, (n + 1) * MXU_DIM)) for a in range(2) for n in range(tq // MXU_DIM)]

    def score_chain(j, u):
        a, cs = units[u]
        kc = k_ref[0, j]
        if bias_ref is not None:
            kc = jnp.concatenate([kc, bias_ref[0, j]], axis=1)
        return _dot_nt(kc, qc_ref[a, cs, :])

    def value_chain(j, u, st, cmax, key_off):
        a, cs = units[u]
        masked = key_off is not None
        if masked:
            key = lax.broadcasted_iota(jnp.int32, (tk, MXU_DIM), 0) + key_off
            qry = lax.broadcasted_iota(jnp.int32, (tk, MXU_DIM), 1) + cs.start
            keep = ((key >> CHUNK_SHIFT) <= (qry >> CHUNK_SHIFT)) if mode == "A" else (key <= qry)
            st = jnp.where(keep, st, NEG)
            cmax = jnp.max(st, axis=0, keepdims=True)
        m_prev = m_ref[a, :, cs]
        m_new = jnp.maximum(m_prev, cmax)
        alpha = jnp.exp2(m_prev - m_new)
        pt = jnp.exp2(st - m_new).astype(BF16)
        vt = vt_ref[0, j]
        va = vt if mode == "A" else vt[a * d_val:(a + 1) * d_val]
        va = jnp.concatenate([va, ones], axis=0)
        acc_ref[a, :, cs] = alpha * acc_ref[a, :, cs] + _dot(va, pt)
        m_ref[a, :, cs] = m_new

    def stage(j_scores, dst, j_values, src, key_off=None):
        for t in range(len(units) + 1):
            if j_scores is not None and t < len(units):
                st = score_chain(j_scores, t)
                dst[0][t] = st
                dst[1][t] = jnp.max(st, axis=0, keepdims=True)
            if j_values is not None and t >= 1:
                value_chain(j_values, t - 1, src[0][t - 1], src[1][t - 1], key_off)

    buf0 = (s0_ref, c0_ref)
    buf1 = (s1_ref, c1_ref)
    n_diag = tq // tk
    n_full = qi * n_diag
    j0 = first_ref[pl.program_id(0), pl.program_id(1), qi]
    n_vis = n_full - j0
    stage(j0, buf0, None, None)

    def pair(j):
        stage(j + 1, buf1, j, buf0)
        stage(j + 2, buf0, j + 1, buf1)

    def quad(p, carry):
        pair(j0 + 4 * p)
        pair(j0 + 4 * p + 2)
        return carry

    lax.fori_loop(0, n_vis >> 2, quad, 0)

    @pl.when((n_vis & 2) == 2)
    def _():
        pair(j0 + (n_vis & ~3))

    for d in range(0, n_diag, 2):
        j = n_full + d
        stage(j + 1, buf1, j, buf0, key_off=d * tk)
        stage(j + 2 if d + 2 < n_diag else None, buf0, j + 1, buf1, key_off=(d + 1) * tk)

    outs = []
    for a in range(2):
        acc = acc_ref[a]
        outs.append(acc[:d_val] * (1.0 / acc[d_val:d_val + 1]))
    if mode == "A":
        ot = outs[0] - _lambda(lam_ref, lam_init) * outs[1]
        ot = ot * lax.rsqrt(jnp.mean(ot * ot, axis=0, keepdims=True) + LN_EPS)
        ot = ot * (gain_ref[0] * (1.0 - lam_init))
    else:
        ot = jnp.concatenate(outs, axis=0)
    o_ref[0] = ot.T.astype(o_ref.dtype)


def _attention(q, k, vt, first, mode, lam=None, gain=None, bias=None, lam_init=0.0):
    b, t, width = q.shape
    nk, tk = vt.shape[1], vt.shape[3]
    tq = 2 * tk
    assert t % tq == 0
    groups = width // LANES
    d_aug = (2 * A_HEAD_DIM if mode == "A" else C_HEAD_DIM) + BF16_ROWS
    kspec = pl.BlockSpec((1, nk, tk, LANES), lambda bi, g, i, f: (bi, 0, 0, g))
    in_specs = [pl.BlockSpec((1, tq, LANES), lambda bi, g, i, f: (bi, i, g)), kspec]
    args = [q, k.reshape(b, nk, tk, width)]
    if mode == "C":
        in_specs.append(kspec)
        args.append(bias.reshape(b, nk, tk, groups * LANES))
    in_specs.append(pl.BlockSpec((1, nk, LANES, tk), lambda bi, g, i, f: (bi, 0, g, 0)))
    args.append(vt)
    if mode == "A":
        in_specs += [pl.BlockSpec((4, A_HEAD_DIM), lambda bi, g, i, f: (0, 0)),
                     pl.BlockSpec((1, LANES, 1), lambda bi, g, i, f: (g, 0, 0))]
        args += [lam, gain]
    return pl.pallas_call(
        functools.partial(_attn_kernel, mode=mode, lam_init=lam_init),
        grid_spec=pltpu.PrefetchScalarGridSpec(
            num_scalar_prefetch=1,
            grid=(b, groups, t // tq),
            in_specs=in_specs,
            out_specs=pl.BlockSpec((1, tq, LANES), lambda bi, g, i, f: (bi, i, g)),
            scratch_shapes=[pltpu.VMEM((2, tq, LANES if mode == "A" else 2 * LANES), BF16),
                            pltpu.VMEM((2, 1, tq), F32),
                            pltpu.VMEM((2, d_aug, tq), F32),
                            pltpu.VMEM((2 * tq // MXU_DIM, tk, MXU_DIM), F32),
                            pltpu.VMEM((2 * tq // MXU_DIM, tk, MXU_DIM), F32),
                            pltpu.VMEM((2 * tq // MXU_DIM, 1, MXU_DIM), F32),
                            pltpu.VMEM((2 * tq // MXU_DIM, 1, MXU_DIM), F32)]),
        out_shape=jax.ShapeDtypeStruct((b, t, width), BF16),
        compiler_params=_params(("arbitrary", "arbitrary", "arbitrary")),
        name="attn_" + mode,
    )(first, *args)


def _first_visible_block(bounds, tq_tiles):
    qn, kn, bmax, bmin = (bounds[:, :, r, :C_HEADS] for r in range(4))
    b, tiles, _ = qn.shape
    nq = tiles // tq_tiles
    blk = lambda x, f: f(x.reshape(b, nq, tq_tiles, C_HEADS), axis=2)
    qn_q = blk(qn, jnp.max) * NORM_SLACK
    own = blk(bmin, jnp.min) - qn_q * blk(kn, jnp.max) * NORM_SLACK
    best = qn_q[:, :, None, :] * (kn * NORM_SLACK)[:, None, :, :] + bmax[:, None, :, :]
    dead = best < (own[:, :, None, :] - SKIP_GAP)
    dead = jnp.all(dead.reshape(b, nq, tiles, C_HEADS // 2, 2), axis=-1)
    lead = jnp.sum(jnp.cumprod(dead.astype(jnp.int32), axis=2), axis=2)
    lead = jnp.minimum(lead, (jnp.arange(nq, dtype=jnp.int32) * tq_tiles)[None, :, None])
    return jnp.transpose(lead & ~1, (0, 2, 1)).astype(jnp.int32)


def _diff_finish(o0, o1, lam, gain, lam_init):
    o = o0 - lam * o1
    o = o * lax.rsqrt(jnp.mean(o * o, axis=-1, keepdims=True) + LN_EPS)
    return o * gain * (1.0 - lam_init)


def _cached_attn_kernel(*refs, mode, past, lam_init):
    if mode == "A":
        q_ref, kn_ref, vn_ref, ck_ref, cv_ref, lam_ref, gain_ref, o_ref = refs
    else:
        q_ref, kn_ref, vn_ref, ck_ref, cv_ref, clf_ref, nlf_ref, o_ref = refs
    t = q_ref.shape[1]
    width = o_ref.shape[2]
    groups = width // LANES
    lane = lax.broadcasted_iota(jnp.int32, (1, LANES), 1)
    low = lane < (LANES // 2)
    row = lax.broadcasted_iota(jnp.int32, (t, t), 0)
    col = lax.broadcasted_iota(jnp.int32, (t, t), 1)
    if mode == "A":
        keep = ((past + col) >> CHUNK_SHIFT) <= ((past + row) >> CHUNK_SHIFT)
        lam = _lambda(lam_ref, lam_init)
    else:
        keep = col <= row
        r = lax.broadcasted_iota(jnp.int32, (past, past), 0)
        c = lax.broadcasted_iota(jnp.int32, (past, past), 1)
        upper = jnp.where(r <= c, 1.0, 0.0).astype(BF16)
        clf = clf_ref[0]
        hi, mid, lo = _split3(clf)
        cum_c = (_dot(hi, upper) + _dot(mid, upper)) + _dot(lo, upper)
        upper_n = jnp.where(row <= col, 1.0, 0.0).astype(BF16)
        hi, mid, lo = _split3(nlf_ref[0])
        cum_n = ((_dot(hi, upper_n) + _dot(mid, upper_n)) + _dot(lo, upper_n)
                 + jnp.sum(clf, axis=1, keepdims=True))
        bias_c = cum_c * (-LOG2E)
        bias_n = cum_n * (-LOG2E)
    for g in range(groups):
        sl = slice(g * LANES, (g + 1) * LANES)
        q = q_ref[0, :, sl]
        kn = kn_ref[0, :, sl]
        vn = vn_ref[0, :, sl].astype(BF16)
        if mode == "A":
            kc = ck_ref[0, 0, :, g, :].astype(BF16)
            vc = cv_ref[0, 0, :, g, :].astype(BF16)
        else:
            kc = ck_ref[0, :, sl].astype(BF16)
            vc = cv_ref[0, :, sl].astype(BF16)
        outs = []
        for a in range(2):
            qa = jnp.where(low, q, jnp.zeros_like(q)) if a == 0 else jnp.where(low, jnp.zeros_like(q), q)
            s_c = _dot_nt(qa, kc)
            s_n = _dot_nt(qa, kn)
            if mode == "C":
                h = 2 * g + a
                s_c = s_c + bias_c[h:h + 1, :]
                s_n = s_n + bias_n[h:h + 1, :]
            s_n = jnp.where(keep, s_n, NEG)
            m = jnp.maximum(jnp.max(s_c, axis=1, keepdims=True), jnp.max(s_n, axis=1, keepdims=True))
            p_c = jnp.exp2(s_c - m)
            p_n = jnp.exp2(s_n - m)
            l = jnp.sum(p_c, axis=1, keepdims=True) + jnp.sum(p_n, axis=1, keepdims=True)
            acc = _dot(p_c.astype(BF16), vc) + _dot(p_n.astype(BF16), vn)
            outs.append(acc * (1.0 / l))
        if mode == "A":
            o = _diff_finish(outs[0], outs[1], lam, gain_ref[g], lam_init)
        else:
            o = jnp.where(low, outs[0], outs[1])
        o_ref[0, :, sl] = o.astype(o_ref.dtype)


def _cached_attention(q, kn, vn, cache_k, cache_v, mode, lam=None, gain=None, cache_lft=None, new_lft=None,
                      lam_init=0.0):
    b, t, width = q.shape
    new = pl.BlockSpec((1, t, width), lambda bi: (bi, 0, 0))
    if mode == "A":
        _, _, past, heads, dh = cache_k.shape
        old = pl.BlockSpec((1, 1, past, heads, dh), lambda bi: (0, bi, 0, 0, 0))
    else:
        past = cache_k.shape[1]
        old = pl.BlockSpec((1, past, width), lambda bi: (bi, 0, 0))
    in_specs = [new, new, new, old, old]
    args = [q, kn, vn, cache_k, cache_v]
    if mode == "A":
        in_specs += [pl.BlockSpec((4, A_HEAD_DIM), lambda bi: (0, 0)),
                     pl.BlockSpec((A_HEADS, 1, LANES), lambda bi: (0, 0, 0))]
        args += [lam, gain]
    else:
        in_specs += [pl.BlockSpec((1, C_HEADS, past), lambda bi: (bi, 0, 0)),
                     pl.BlockSpec((1, C_HEADS, t), lambda bi: (bi, 0, 0))]
        args += [cache_lft, new_lft]
    return pl.pallas_call(
        functools.partial(_cached_attn_kernel, mode=mode, past=past, lam_init=lam_init),
        grid=(b,),
        in_specs=in_specs,
        out_specs=new,
        out_shape=jax.ShapeDtypeStruct((b, t, width), BF16),
        compiler_params=_params(("arbitrary",)),
        name="cached_attn_" + mode,
    )(*args)


def _gelu_tanh(x):
    return 0.5 * x * (1.0 + jnp.tanh(math.sqrt(2.0 / math.pi) * (x + 0.044715 * (x * x * x))))


def _causal_conv(x, tail, w_ref, b_ref):
    width = w_ref.shape[0]
    tm = x.shape[0]
    cat = jnp.concatenate([tail, x], axis=0)
    y = None
    for j in range(width):
        back = width - 1 - j
        src = cat if back == 0 else pltpu.roll(cat, back, 0)
        term = src[SUBLANES:SUBLANES + tm] * w_ref[j:j + 1, :]
        y = term if y is None else y + term
    return y + b_ref[...]


def _rglru_kernel(xg_ref, h0_ref, cb_ref, cw_ref, cbias_ref, wa_ref, ba_ref, wx_ref, bx_ref, sp_ref,
                  y_ref, hl_ref, ct_ref, hc_ref, tail_ref):
    i = pl.program_id(1)
    tm = xg_ref.shape[1]

    @pl.when(i == 0)
    def _():
        hc_ref[...] = h0_ref[0]
        tail_ref[...] = cb_ref[0]

    x = xg_ref[0, :, :B_WIDTH]
    gate_in = xg_ref[0, :, B_WIDTH:]
    xc = _causal_conv(x, tail_ref[...], cw_ref, cbias_ref)
    xcb = xc.astype(BF16)
    r = _sigmoid(_dot(xcb, wa_ref[...]) + ba_ref[...])
    ig = _sigmoid(_dot(xcb, wx_ref[...]) + bx_ref[...])
    log_a = (-RG_C) * r * sp_ref[...]
    a = jnp.exp(log_a)
    th = jnp.tanh(log_a)
    bx = jnp.sqrt((-2.0 * th) / (1.0 - th)) * (ig * xc)
    row = lax.broadcasted_iota(jnp.int32, (tm, 1), 0)
    s = 1
    while s < tm:
        valid = row >= s
        a_sh = pltpu.roll(a, s, 0)
        b_sh = pltpu.roll(bx, s, 0)
        bx = jnp.where(valid, a * b_sh + bx, bx)
        a = jnp.where(valid, a * a_sh, a)
        s *= 2
    h = a * hc_ref[...] + bx
    y_ref[0] = (h * _gelu_tanh(gate_in)).astype(y_ref.dtype)
    h_tail = h[tm - SUBLANES:, :]
    x_tail = x[tm - SUBLANES:, :]
    hc_ref[...] = h_tail[SUBLANES - 1:, :]
    tail_ref[...] = x_tail
    hl_ref[0] = h_tail
    ct_ref[0] = x_tail


def _rglru(xg, h0, conv_tail, cw, cbias, wa, ba, wx, bx, sp, tm_pref=256):
    b, t, _ = xg.shape
    tm = _row_tile(t, tm_pref)
    wspec = lambda shape: pl.BlockSpec(shape, lambda bi, i: tuple(0 for _ in shape))
    return pl.pallas_call(
        _rglru_kernel,
        grid=(b, t // tm),
        in_specs=[pl.BlockSpec((1, tm, 2 * B_WIDTH), lambda bi, i: (bi, i, 0)),
                  pl.BlockSpec((1, 1, B_WIDTH), lambda bi, i: (bi, 0, 0)),
                  pl.BlockSpec((1, SUBLANES, B_WIDTH), lambda bi, i: (bi, 0, 0)),
                  wspec((B_CONV, B_WIDTH)), wspec((1, B_WIDTH)),
                  wspec((B_WIDTH, B_WIDTH)), wspec((1, B_WIDTH)),
                  wspec((B_WIDTH, B_WIDTH)), wspec((1, B_WIDTH)), wspec((1, B_WIDTH))],
        out_specs=[pl.BlockSpec((1, tm, B_WIDTH), lambda bi, i: (bi, i, 0)),
                   pl.BlockSpec((1, SUBLANES, B_WIDTH), lambda bi, i: (bi, 0, 0)),
                   pl.BlockSpec((1, SUBLANES, B_WIDTH), lambda bi, i: (bi, 0, 0))],
        out_shape=[jax.ShapeDtypeStruct((b, t, B_WIDTH), BF16),
                   jax.ShapeDtypeStruct((b, SUBLANES, B_WIDTH), F32),
                   jax.ShapeDtypeStruct((b, SUBLANES, B_WIDTH), F32)],
        scratch_shapes=[pltpu.VMEM((1, B_WIDTH), F32), pltpu.VMEM((SUBLANES, B_WIDTH), F32)],
        compiler_params=_params(("arbitrary", "arbitrary")),
        name="rglru",
    )(xg, h0, conv_tail, cw, cbias, wa, ba, wx, bx, sp)


def _proj_ln_kernel(*refs, n_in):
    h_refs = refs[:n_in]
    w_ref, x_ref, gate_ref, g_ref, b_ref, o_ref = refs[n_in:]
    tm = x_ref.shape[1]
    rows = [slice(r, min(r + LN_ROWS, tm)) for r in range(0, tm, LN_ROWS)]

    def matmul(rs):
        hs = [r[0, rs, :] for r in h_refs]
        return _dot(hs[0] if n_in == 1 else jnp.concatenate(hs, axis=-1), w_ref[...])

    def norm(rs, proj):
        y = ALPHA * x_ref[0, rs, :] + gate_ref[0] * proj
        mu = jnp.mean(y, axis=-1, keepdims=True)
        yc = y - mu
        var = jnp.mean(yc * yc, axis=-1, keepdims=True)
        o_ref[0, rs, :] = yc * lax.rsqrt(var + LN_EPS) * g_ref[...] + b_ref[...]

    pending = matmul(rows[0])
    for r in range(len(rows)):
        nxt = matmul(rows[r + 1]) if r + 1 < len(rows) else None
        norm(rows[r], pending)
        pending = nxt


def _proj_ln(hs, w, x, gate, ln_g, ln_b, tm_pref=512):
    b, t, d = x.shape
    tm = _row_tile(t, tm_pref)
    k = w.shape[0]
    in_specs = [pl.BlockSpec((1, tm, h.shape[2]), lambda bi, i: (bi, i, 0)) for h in hs]
    in_specs += [pl.BlockSpec((k, d), lambda bi, i: (0, 0)),
                 pl.BlockSpec((1, tm, d), lambda bi, i: (bi, i, 0)),
                 pl.BlockSpec((1, 1, d), lambda bi, i: (bi, 0, 0)),
                 pl.BlockSpec((1, d), lambda bi, i: (0, 0)),
                 pl.BlockSpec((1, d), lambda bi, i: (0, 0))]
    return pl.pallas_call(
        functools.partial(_proj_ln_kernel, n_in=len(hs)),
        grid=(b, t // tm),
        in_specs=in_specs,
        out_specs=pl.BlockSpec((1, tm, d), lambda bi, i: (bi, i, 0)),
        out_shape=jax.ShapeDtypeStruct((b, t, d), F32),
        compiler_params=_params(("arbitrary", "arbitrary")),
        name="proj_ln",
    )(*hs, w, x, gate, ln_g, ln_b)


def _ffn_up_kernel(x_ref, sc_ref, sh_ref, w_ref, cb_ref, cw_ref, cbias_ref, h_ref, ct_ref, tail_ref):
    i = pl.program_id(1)
    tm = x_ref.shape[1]

    @pl.when(i == 0)
    def _():
        tail_ref[...] = cb_ref[0]

    u = _modulate(x_ref[0], sc_ref[0], sh_ref[0])
    chunks = [slice(c, min(c + FFN_CHUNK, D_FF)) for c in range(0, D_FF, FFN_CHUNK)]

    def matmuls(cs):
        gs = slice(D_FF + cs.start, D_FF + cs.stop)
        return _dot(u, w_ref[:, cs]), _dot(u, w_ref[:, gs])

    def gate(cs, a, g):
        gc = _causal_conv(g, tail_ref[:, cs], cw_ref.at[:, cs], cbias_ref.at[:, cs])
        silu = gc * (0.5 * jnp.tanh(0.5 * gc) + 0.5)
        h_ref[0, :, cs] = (a * silu).astype(h_ref.dtype)
        g_tail = g[tm - SUBLANES:, :]
        tail_ref[:, cs] = g_tail
        ct_ref[0, :, cs] = g_tail

    pending = matmuls(chunks[0])
    for c in range(len(chunks)):
        nxt = matmuls(chunks[c + 1]) if c + 1 < len(chunks) else None
        gate(chunks[c], *pending)
        pending = nxt


def _ffn_up(x, sc, sh, w, conv_tail, cw, cbias, tm_pref=256):
    b, t, d = x.shape
    tm = _row_tile(t, tm_pref)
    return pl.pallas_call(
        _ffn_up_kernel,
        grid=(b, t // tm),
        in_specs=[pl.BlockSpec((1, tm, d), lambda bi, i: (bi, i, 0)),
                  pl.BlockSpec((1, 1, d), lambda bi, i: (bi, 0, 0)),
                  pl.BlockSpec((1, 1, d), lambda bi, i: (bi, 0, 0)),
                  pl.BlockSpec((d, 2 * D_FF), lambda bi, i: (0, 0)),
                  pl.BlockSpec((1, SUBLANES, D_FF), lambda bi, i: (bi, 0, 0)),
                  pl.BlockSpec((FFN_CONV, D_FF), lambda bi, i: (0, 0)),
                  pl.BlockSpec((1, D_FF), lambda bi, i: (0, 0))],
        out_specs=[pl.BlockSpec((1, tm, D_FF), lambda bi, i: (bi, i, 0)),
                   pl.BlockSpec((1, SUBLANES, D_FF), lambda bi, i: (bi, 0, 0))],
        out_shape=[jax.ShapeDtypeStruct((b, t, D_FF), BF16),
                   jax.ShapeDtypeStruct((b, SUBLANES, D_FF), F32)],
        scratch_shapes=[pltpu.VMEM((SUBLANES, D_FF), F32)],
        compiler_params=_params(("arbitrary", "arbitrary")),
        name="ffn_up",
    )(x, sc, sh, w, conv_tail, cw, cbias)


def _pad_tail(buf):
    return jnp.pad(buf, ((0, 0), (SUBLANES - buf.shape[1], 0), (0, 0)))


def _rope_tables(past, t):
    half = A_HEAD_DIM // 2
    inv = ROPE_THETA ** (-jnp.arange(0, A_HEAD_DIM, 2, dtype=F32) / A_HEAD_DIM)
    pos = (past + jnp.arange(t, dtype=jnp.int32)).astype(F32)
    ang = pos[:, None] * inv[None, :]
    cos = jnp.tile(jnp.cos(ang), (1, LANES // half))
    sin = jnp.sin(ang)
    sin_signed = jnp.tile(jnp.concatenate([-sin, sin], axis=1), (1, LANES // A_HEAD_DIM))
    return cos, sin_signed


def _block_diag(w):
    n, i, o = w.shape
    return jnp.einsum("nio,nm->nimo", w, jnp.eye(n, dtype=w.dtype)).reshape(n * i, n * o)


def _prepare(p):
    w = {}
    w_in_ab = p["w_in_ab"][0]
    w["in_ab"] = w_in_ab.astype(BF16)
    w["out_ab"] = p["w_out_ab"][0].astype(BF16)
    w["lam"] = jnp.stack([p["lam_q1"][0], p["lam_k1"][0], p["lam_q2"][0], p["lam_k2"][0]])
    w["gain_row"] = p["attn_gain"][0].reshape(A_HEADS, 1, LANES)
    w["gain_col"] = p["attn_gain"][0].reshape(A_HEADS, LANES, 1)
    w["b_conv_w"] = p["b_conv_w"][0]
    w["b_conv_b"] = p["b_conv_b"][0].reshape(1, B_WIDTH)
    w["rg_a"] = _block_diag(p["w_rg_a"][0]).astype(BF16)
    w["rg_x"] = _block_diag(p["w_rg_x"][0]).astype(BF16)
    w["b_rg_a"] = p["b_rg_a"][0].reshape(1, B_WIDTH)
    w["b_rg_x"] = p["b_rg_x"][0].reshape(1, B_WIDTH)
    w["rg_L"] = p["rg_L"][0].reshape(1, B_WIDTH)
    w_in_c = p["w_in_c"][0]
    w["in_c"] = jnp.pad(w_in_c, ((0, 0), (0, LANES - C_HEADS))).astype(BF16)
    w["in_c_ft"] = w_in_c[:, 3 * C_WIDTH:].T.astype(BF16)
    w["bf_row"] = jnp.pad(p["b_f"][0], (0, LANES - C_HEADS)).reshape(1, LANES)
    w["bf_col"] = p["b_f"][0].reshape(C_HEADS, 1)
    w["place"] = _bias_placement()
    w["seg"] = jnp.asarray(np.repeat(np.eye(LANES, dtype=np.float32)[:C_HEADS], C_HEAD_DIM, axis=0), BF16)
    w["out_c"] = p["w_out_c"][0].astype(BF16)
    w["up"] = [p["w_up"][i].astype(BF16) for i in range(DEPTH)]
    w["down"] = [p["w_down"][i].astype(BF16) for i in range(DEPTH)]
    w["ffn_conv_w"] = [p["ffn_conv_w"][i] for i in range(DEPTH)]
    w["ffn_conv_b"] = [p["ffn_conv_b"][i].reshape(1, D_FF) for i in range(DEPTH)]
    w["ln1_g"] = [p["ln1_g"][i].reshape(1, D_MODEL) for i in range(DEPTH)]
    w["ln1_b"] = [p["ln1_b"][i].reshape(1, D_MODEL) for i in range(DEPTH)]
    w["ln2_g"] = [p["ln2_g"][i].reshape(1, D_MODEL) for i in range(DEPTH)]
    w["ln2_b"] = [p["ln2_b"][i].reshape(1, D_MODEL) for i in range(DEPTH)]
    return w


def _softplus_kernel(x_ref, o_ref):
    o_ref[...] = _softplus(-x_ref[...])


def _trunk(x, mods, w, sp, cache_a_k=None, cache_a_v=None, state_b_h=None, state_b_conv=None,
           cache_c_k=None, cache_c_v=None, cache_c_logf=None, state_ffn_conv=None):
    b, t, d = x.shape
    cached = cache_a_k is not None
    past = cache_a_k.shape[2] if cached else 0
    outs = {}
    for i in range(DEPTH):
        sh1, sc1, g1, sh2, sc2, g2 = [m[:, None, :] for m in jnp.split(mods[i], 6, axis=-1)]
        if i % 2 == 0:
            lam_init = 0.8 - 0.6 * math.exp(-0.3 * i)
            cos, sin = _rope_tables(past, t)
            q, kb, vt, k32, v32, xg = _proj_ab(x, sc1, sh1, w["in_ab"], cos, sin)
            if cached:
                o = _cached_attention(q, kb, v32.reshape(b, t, A_WIDTH), cache_a_k, cache_a_v,
                                      "A", lam=w["lam"], gain=w["gain_row"], lam_init=lam_init)
                h0 = state_b_h[0][:, None, :]
                ctail = _pad_tail(state_b_conv[0])
            else:
                first = jnp.zeros((b, A_HEADS, t // (2 * ATTN_BLOCK)), jnp.int32)
                o = _attention(q, kb, vt, first, "A", lam=w["lam"], gain=w["gain_col"], lam_init=lam_init)
                h0 = jnp.zeros((b, 1, B_WIDTH), F32)
                ctail = jnp.zeros((b, SUBLANES, B_WIDTH), F32)
            yb, h_tail, x_tail = _rglru(xg, h0, ctail, w["b_conv_w"], w["b_conv_b"], w["rg_a"], w["b_rg_a"],
                                        w["rg_x"], w["b_rg_x"], sp)
            outs["a_k"] = k32[None]
            outs["a_v"] = v32[None]
            outs["b_h"] = h_tail[:, SUBLANES - 1, :][None]
            outs["b_conv"] = x_tail[:, SUBLANES - (B_CONV - 1):, :][None]
            x = _proj_ln([o, yb], w["out_ab"], x, g1, w["ln1_g"][i], w["ln1_b"][i])
        else:
            q, kb, bias, vt, k32, v32, lf, lft, bounds = _proj_c(x, sc1, sh1, w["in_c"], w["in_c_ft"],
                                                                 w["bf_row"], w["bf_col"], w["place"], w["seg"])
            if cached:
                o = _cached_attention(q, kb, v32, cache_c_k[0].reshape(b, past, C_WIDTH),
                                      cache_c_v[0].reshape(b, past, C_WIDTH),
                                      "C", cache_lft=jnp.swapaxes(cache_c_logf[0], 1, 2), new_lft=lft)
            else:
                o = _attention(q, kb, vt, _first_visible_block(bounds, 2), "C", bias=bias)
            outs["c_k"] = k32.reshape(1, b, t, C_HEADS, C_HEAD_DIM)
            outs["c_v"] = v32.reshape(1, b, t, C_HEADS, C_HEAD_DIM)
            outs["c_logf"] = lf[None]
            x = _proj_ln([o], w["out_c"], x, g1, w["ln1_g"][i], w["ln1_b"][i])
        ftail = _pad_tail(state_ffn_conv[i]) if cached else jnp.zeros((b, SUBLANES, D_FF), F32)
        hmid, g_tail = _ffn_up(x, sc2, sh2, w["up"][i], ftail, w["ffn_conv_w"][i], w["ffn_conv_b"][i])
        outs.setdefault("ffn", []).append(g_tail[:, SUBLANES - (FFN_CONV - 1):, :])
        x = _proj_ln([hmid], w["down"][i], x, g2, w["ln2_g"][i], w["ln2_b"][i])
    return (x, outs["a_k"], outs["a_v"], outs["b_h"], outs["b_conv"],
            outs["c_k"], outs["c_v"], outs["c_logf"], jnp.stack(outs["ffn"]))


def kernel(x_prompt, x_sample, c_prompt, c_sample, cache_a_k, cache_a_v, state_b_h, state_b_conv, cache_c_k, cache_c_v, cache_c_logf, state_ffn_conv, w_ada, b_ada, ln1_g, ln1_b, ln2_g, ln2_b, w_in_ab, lam_q1, lam_k1, lam_q2, lam_k2, attn_gain, b_conv_w, b_conv_b, w_rg_a, b_rg_a, w_rg_x, b_rg_x, rg_L, w_out_ab, w_in_c, b_f, w_out_c, w_up, ffn_conv_w, ffn_conv_b, w_down):
    p = dict(w_in_ab=w_in_ab, lam_q1=lam_q1, lam_k1=lam_k1, lam_q2=lam_q2, lam_k2=lam_k2, attn_gain=attn_gain,
             b_conv_w=b_conv_w, b_conv_b=b_conv_b, w_rg_a=w_rg_a, b_rg_a=b_rg_a, w_rg_x=w_rg_x, b_rg_x=b_rg_x,
             rg_L=rg_L, w_out_ab=w_out_ab, w_in_c=w_in_c, b_f=b_f, w_out_c=w_out_c, w_up=w_up,
             ffn_conv_w=ffn_conv_w, ffn_conv_b=ffn_conv_b, w_down=w_down,
             ln1_g=ln1_g, ln1_b=ln1_b, ln2_g=ln2_g, ln2_b=ln2_b)
    w = _prepare(p)
    bp = c_prompt.shape[0]
    bs = c_sample.shape[0]
    rows = -(-(bp + bs) // 16) * 16
    c_all = jnp.pad(jnp.concatenate([c_prompt, c_sample], axis=0), ((0, rows - bp - bs), (0, 0)))
    mods = _mods(c_all, w_ada, b_ada)
    sp = pl.pallas_call(_softplus_kernel, out_shape=jax.ShapeDtypeStruct((1, B_WIDTH), F32),
                        name="softplus")(w["rg_L"])
    res_p = _trunk(x_prompt, mods[:, :bp], w, sp)
    res_s = _trunk(x_sample, mods[:, bp:bp + bs], w, sp, cache_a_k, cache_a_v, state_b_h, state_b_conv,
                   cache_c_k, cache_c_v, cache_c_logf, state_ffn_conv)
    return (res_p[0], res_s[0]) + res_p[1:] + res_s[1:]
```

```python
import functools
import math

import numpy as np
import jax
import jax.numpy as jnp
from jax import lax
from jax.experimental import pallas as pl
from jax.experimental.pallas import tpu as pltpu

F32 = jnp.float32
BF16 = jnp.bfloat16

D_MODEL = 1024
DEPTH = 2
CHUNK = 64
CHUNK_SHIFT = 6
A_HEADS = 4
A_HEAD_DIM = 64
A_WIDTH = A_HEADS * 2 * A_HEAD_DIM
B_WIDTH = 512
B_BLOCKS = 8
B_CONV = 4
RG_C = 8.0
C_HEADS = 16
C_HEAD_DIM = 64
C_WIDTH = C_HEADS * C_HEAD_DIM
D_FF = 2816
FFN_CONV = 3
ROPE_THETA = 10000.0
ALPHA = (2 * DEPTH) ** 0.25
LN_EPS = 1e-5
NEG = -1e30
LOG2E = 1.4426950408889634

LANES = 128
SUBLANES = 8
BF16_ROWS = 16
MXU_DIM = 256
VMEM_LIMIT = 56 * 1024 * 1024
BIAS_WIDTH = (C_HEADS // 2) * LANES
FFN_CHUNK = 256
LN_ROWS = 256
SKIP_GAP = 160.0
NORM_SLACK = 1.03
ATTN_BLOCK = 512


def _params(sem, flags=None):
    return pltpu.CompilerParams(dimension_semantics=sem, vmem_limit_bytes=VMEM_LIMIT, flags=flags)


def _row_tile(t, pref):
    if t <= pref:
        return t
    tm = pref
    while t % tm:
        tm //= 2
    return tm


def _modulate(x, sc, sh):
    return (x * (1.0 + sc) + sh).astype(BF16)


def _sigmoid(x):
    return 1.0 / (1.0 + jnp.exp(-x))


def _softplus(x):
    return jnp.maximum(x, 0.0) + jnp.log1p(jnp.exp(-jnp.abs(x)))


def _log_sigmoid(x):
    return jnp.minimum(x, 0.0) - jnp.log1p(jnp.exp(-jnp.abs(x)))


def _split3(x):
    hi = x.astype(BF16)
    r1 = x - hi.astype(F32)
    mid = r1.astype(BF16)
    lo = (r1 - mid.astype(F32)).astype(BF16)
    return hi, mid, lo


def _dot(a, b):
    return jnp.dot(a, b, preferred_element_type=F32)


def _dot_nt(a, b):
    return lax.dot_general(a, b, (((1,), (1,)), ((), ())), preferred_element_type=F32)


def _mods_kernel(c_ref, w_ref, b_ref, o_ref):
    c = c_ref[...]
    s = (c * _sigmoid(c)).astype(BF16)
    o_ref[0] = _dot(s, w_ref[0].astype(BF16)) + b_ref[0]


def _mods(c_all, w_ada, b_ada):
    rows, d = c_all.shape
    n = w_ada.shape[-1]
    tn = 1536
    return pl.pallas_call(
        _mods_kernel,
        grid=(DEPTH, n // tn),
        in_specs=[pl.BlockSpec((rows, d), lambda l, j: (0, 0)),
                  pl.BlockSpec((1, d, tn), lambda l, j: (l, 0, j)),
                  pl.BlockSpec((1, 1, tn), lambda l, j: (l, 0, j))],
        out_specs=pl.BlockSpec((1, rows, tn), lambda l, j: (l, 0, j)),
        out_shape=jax.ShapeDtypeStruct((DEPTH, rows, n), F32),
        compiler_params=_params(("arbitrary", "arbitrary")),
        name="mods",
    )(c_all, w_ada, b_ada.reshape(DEPTH, 1, n))


def _rope_slab(x, cos, sin_signed, first_half):
    fwd = pltpu.roll(x, LANES - A_HEAD_DIM // 2, 1)
    bwd = pltpu.roll(x, A_HEAD_DIM // 2, 1)
    partner = jnp.where(first_half, fwd, bwd)
    return x * cos + partner * sin_signed


def _proj_ab_kernel(x_ref, sc_ref, sh_ref, w_ref, cos_ref, sin_ref,
                    q_ref, kb_ref, vt_ref, k_ref, v_ref, xg_ref, *, q_scale):
    u = _modulate(x_ref[0], sc_ref[0], sh_ref[0])
    pr = _dot(u, w_ref[...])
    cos = cos_ref[...]
    sin = sin_ref[...]
    lane = lax.broadcasted_iota(jnp.int32, (1, LANES), 1)
    first_half = (lane & (A_HEAD_DIM - 1)) < (A_HEAD_DIM // 2)
    for h in range(A_HEADS):
        sl = slice(h * LANES, (h + 1) * LANES)
        q = _rope_slab(pr[:, sl], cos, sin, first_half)
        q_ref[0, :, sl] = (q * q_scale).astype(BF16)
        k = _rope_slab(pr[:, A_WIDTH + h * LANES:A_WIDTH + (h + 1) * LANES], cos, sin, first_half)
        k_ref[0, :, h, :] = k
        kb_ref[0, :, sl] = k.astype(BF16)
    for h in range(A_HEADS):
        v_ref[0, :, h, :] = pr[:, 2 * A_WIDTH + h * LANES:2 * A_WIDTH + (h + 1) * LANES]
    vt_ref[0, 0] = pr[:, 2 * A_WIDTH:3 * A_WIDTH].T.astype(BF16)
    xg_ref[0] = pr[:, 3 * A_WIDTH:]


def _proj_ab(x, sc, sh, w, cos, sin):
    b, t, d = x.shape
    tm = _row_tile(t, ATTN_BLOCK)
    nt = t // tm
    n = w.shape[1]
    q_scale = A_HEAD_DIM ** -0.5 * LOG2E
    row = lambda width: pl.BlockSpec((1, tm, width), lambda bi, i: (bi, i, 0))
    const = lambda shape: pl.BlockSpec(shape, lambda bi, i: tuple(0 for _ in shape))
    heads = pl.BlockSpec((1, tm, A_HEADS, LANES), lambda bi, i: (bi, i, 0, 0))
    return pl.pallas_call(
        functools.partial(_proj_ab_kernel, q_scale=q_scale),
        grid=(b, nt),
        in_specs=[row(d),
                  pl.BlockSpec((1, 1, d), lambda bi, i: (bi, 0, 0)),
                  pl.BlockSpec((1, 1, d), lambda bi, i: (bi, 0, 0)),
                  const((d, n)),
                  pl.BlockSpec((tm, LANES), lambda bi, i: (i, 0)),
                  pl.BlockSpec((tm, LANES), lambda bi, i: (i, 0))],
        out_specs=[row(A_WIDTH), row(A_WIDTH),
                   pl.BlockSpec((1, 1, A_WIDTH, tm), lambda bi, i: (bi, i, 0, 0)),
                   heads, heads, row(2 * B_WIDTH)],
        out_shape=[jax.ShapeDtypeStruct((b, t, A_WIDTH), BF16),
                   jax.ShapeDtypeStruct((b, t, A_WIDTH), BF16),
                   jax.ShapeDtypeStruct((b, nt, A_WIDTH, tm), BF16),
                   jax.ShapeDtypeStruct((b, t, A_HEADS, LANES), F32),
                   jax.ShapeDtypeStruct((b, t, A_HEADS, LANES), F32),
                   jax.ShapeDtypeStruct((b, t, 2 * B_WIDTH), F32)],
        compiler_params=_params(("arbitrary", "arbitrary")),
        name="proj_ab",
    )(x, sc, sh, w, cos, sin)


def _bias_placement():
    e = np.zeros((3 * LANES, BIAS_WIDTH), np.float32)
    for piece in range(3):
        for h in range(C_HEADS):
            e[piece * LANES + h, (h // 2) * LANES + 3 * (h % 2) + piece] = 1.0
    return jnp.asarray(e, BF16)


def _head_norm_max(x, seg):
    n2 = _dot((x * x).astype(BF16), seg)
    return jnp.sqrt(jnp.max(n2, axis=0, keepdims=True))


def _proj_c_kernel(x_ref, sc_ref, sh_ref, w_ref, wft_ref, bfr_ref, bfc_ref, place_ref, seg_ref,
                   q_ref, kb_ref, bias_ref, vt_ref, k_ref, v_ref, lf_ref, lft_ref, bound_ref, run_ref, *, q_scale):
    i = pl.program_id(1)
    tm = x_ref.shape[1]

    @pl.when(i == 0)
    def _():
        run_ref[...] = jnp.zeros_like(run_ref)

    u = _modulate(x_ref[0], sc_ref[0], sh_ref[0])
    pr = _dot(u, w_ref[...])
    qs = pr[:, :C_WIDTH] * q_scale
    q_ref[0] = qs.astype(BF16)
    k = pr[:, C_WIDTH:2 * C_WIDTH]
    k_ref[0] = k
    kb_ref[0] = k.astype(BF16)
    v = pr[:, 2 * C_WIDTH:3 * C_WIDTH]
    v_ref[0] = v
    vt_ref[0, 0] = v.T.astype(BF16)
    lf = _log_sigmoid(pr[:, 3 * C_WIDTH:] + bfr_ref[...])
    lf_ref[0] = lf[:, :C_HEADS]
    lft_ref[0] = _log_sigmoid(_dot_nt(wft_ref[...], u) + bfc_ref[...])
    r = lax.broadcasted_iota(jnp.int32, (tm, tm), 0)
    c = lax.broadcasted_iota(jnp.int32, (tm, tm), 1)
    lower = jnp.where(c <= r, 1.0, 0.0).astype(BF16)
    hi, mid, lo = _split3(lf)
    cum = (_dot(lower, hi) + _dot(lower, mid)) + _dot(lower, lo) + run_ref[...]
    bias = cum * (-LOG2E)
    pieces = jnp.concatenate(_split3(bias), axis=1)
    bias_ref[0] = _dot(pieces, place_ref[...]).astype(BF16)
    seg = seg_ref[...]
    bound_ref[0, 0] = jnp.concatenate(
        [_head_norm_max(qs, seg), _head_norm_max(k, seg),
         jnp.max(bias, axis=0, keepdims=True), jnp.min(bias, axis=0, keepdims=True),
         jnp.zeros((SUBLANES - 4, LANES), F32)], axis=0)
    run_ref[...] = run_ref[...] + jnp.sum(lf, axis=0, keepdims=True)


def _proj_c(x, sc, sh, w, wft, bf_row, bf_col, place, seg):
    b, t, d = x.shape
    tm = _row_tile(t, ATTN_BLOCK)
    nt = t // tm
    n = w.shape[1]
    q_scale = C_HEAD_DIM ** -0.5 * LOG2E
    row = lambda width: pl.BlockSpec((1, tm, width), lambda bi, i: (bi, i, 0))
    const = lambda shape: pl.BlockSpec(shape, lambda bi, i: tuple(0 for _ in shape))
    return pl.pallas_call(
        functools.partial(_proj_c_kernel, q_scale=q_scale),
        grid=(b, nt),
        in_specs=[row(d),
                  pl.BlockSpec((1, 1, d), lambda bi, i: (bi, 0, 0)),
                  pl.BlockSpec((1, 1, d), lambda bi, i: (bi, 0, 0)),
                  const((d, n)), const((C_HEADS, d)),
                  const((1, LANES)), const((C_HEADS, 1)), const((3 * LANES, BIAS_WIDTH)), const((C_WIDTH, LANES))],
        out_specs=[row(C_WIDTH), row(C_WIDTH), row(BIAS_WIDTH),
                   pl.BlockSpec((1, 1, C_WIDTH, tm), lambda bi, i: (bi, i, 0, 0)),
                   row(C_WIDTH), row(C_WIDTH), row(C_HEADS),
                   pl.BlockSpec((1, C_HEADS, tm), lambda bi, i: (bi, 0, i)),
                   pl.BlockSpec((1, 1, SUBLANES, LANES), lambda bi, i: (bi, i, 0, 0))],
        out_shape=[jax.ShapeDtypeStruct((b, t, C_WIDTH), BF16),
                   jax.ShapeDtypeStruct((b, t, C_WIDTH), BF16),
                   jax.ShapeDtypeStruct((b, t, BIAS_WIDTH), BF16),
                   jax.ShapeDtypeStruct((b, nt, C_WIDTH, tm), BF16),
                   jax.ShapeDtypeStruct((b, t, C_WIDTH), F32),
                   jax.ShapeDtypeStruct((b, t, C_WIDTH), F32),
                   jax.ShapeDtypeStruct((b, t, C_HEADS), F32),
                   jax.ShapeDtypeStruct((b, C_HEADS, t), F32),
                   jax.ShapeDtypeStruct((b, nt, SUBLANES, LANES), F32)],
        scratch_shapes=[pltpu.VMEM((1, LANES), F32)],
        compiler_params=_params(("arbitrary", "arbitrary")),
        name="proj_c",
    )(x, sc, sh, w, wft, bf_row, bf_col, place, seg)


def _lambda(lam_ref, lam_init):
    lq1, lk1, lq2, lk2 = (lam_ref[r:r + 1, :] for r in range(4))
    return (jnp.exp(jnp.sum(lq1 * lk1, axis=1, keepdims=True))
            - jnp.exp(jnp.sum(lq2 * lk2, axis=1, keepdims=True)) + lam_init)


def _attn_kernel(*refs, mode, lam_init):
    first_ref, refs = refs[0], refs[1:]
    if mode == "A":
        q_ref, k_ref, vt_ref, lam_ref, gain_ref, o_ref, qc_ref, m_ref, acc_ref, s0_ref, s1_ref, c0_ref, c1_ref = refs
        bias_ref = None
        d_val = 2 * A_HEAD_DIM
    else:
        q_ref, k_ref, bias_ref, vt_ref, o_ref, qc_ref, m_ref, acc_ref, s0_ref, s1_ref, c0_ref, c1_ref = refs
        d_val = C_HEAD_DIM
    tq = q_ref.shape[1]
    tk = k_ref.shape[2]
    qi = pl.program_id(2)
    lane = lax.broadcasted_iota(jnp.int32, (1, LANES), 1)
    low = lane < (LANES // 2)
    q = q_ref[0]
    zero = jnp.zeros_like(q)
    for a in range(2):
        qa = jnp.where(low, q, zero) if a == 0 else jnp.where(low, zero, q)
        if bias_ref is not None:
            pick = jnp.where((lane >= 3 * a) & (lane < 3 * a + 3), 1.0, 0.0).astype(BF16)
            qa = jnp.concatenate([qa, jnp.broadcast_to(pick, (tq, LANES))], axis=1)
        qc_ref[a] = qa
    m_ref[...] = jnp.full(m_ref.shape, NEG, F32)
    acc_ref[...] = jnp.zeros(acc_ref.shape, F32)
    ones = jnp.ones((BF16_ROWS, tk), BF16)

    units = [(a, slice(n * MXU_DIM, (n + 1) * MXU_DIM)) for a in range(2) for n in range(tq // MXU_DIM)]

    def score_chain(j, u):
        a, cs = units[u]
        kc = k_ref[0, j]
        if bias_ref is not None:
            kc = jnp.concatenate([kc, bias_ref[0, j]], axis=1)
        return _dot_nt(kc, qc_ref[a, cs, :])

    def value_chain(j, u, st, cmax, key_off):
        a, cs = units[u]
        masked = key_off is not None
        if masked:
            key = lax.broadcasted_iota(jnp.int32, (tk, MXU_DIM), 0) + key_off
            qry = lax.broadcasted_iota(jnp.int32, (tk, MXU_DIM), 1) + cs.start
            keep = ((key >> CHUNK_SHIFT) <= (qry >> CHUNK_SHIFT)) if mode == "A" else (key <= qry)
            st = jnp.where(keep, st, NEG)
            cmax = jnp.max(st, axis=0, keepdims=True)
        m_prev = m_ref[a, :, cs]
        m_new = jnp.maximum(m_prev, cmax)
        alpha = jnp.exp2(m_prev - m_new)
        pt = jnp.exp2(st - m_new).astype(BF16)
        vt = vt_ref[0, j]
        va = vt if mode == "A" else vt[a * d_val:(a + 1) * d_val]
        va = jnp.concatenate([va, ones], axis=0)
        acc_ref[a, :, cs] = alpha * acc_ref[a, :, cs] + _dot(va, pt)
        m_ref[a, :, cs] = m_new

    def stage(j_scores, dst, j_values, src, key_off=None):
        for t in range(len(units) + 1):
            if j_scores is not None and t < len(units):
                st = score_chain(j_scores, t)
                dst[0][t] = st
                dst[1][t] = jnp.max(st, axis=0, keepdims=True)
            if j_values is not None and t >= 1:
                value_chain(j_values, t - 1, src[0][t - 1], src[1][t - 1], key_off)

    buf0 = (s0_ref, c0_ref)
    buf1 = (s1_ref, c1_ref)
    n_diag = tq // tk
    n_full = qi * n_diag
    j0 = first_ref[pl.program_id(0), pl.program_id(1), qi]
    n_vis = n_full - j0
    stage(j0, buf0, None, None)

    def pair(j):
        stage(j + 1, buf1, j, buf0)
        stage(j + 2, buf0, j + 1, buf1)

    def quad(p, carry):
        pair(j0 + 4 * p)
        pair(j0 + 4 * p + 2)
        return carry

    lax.fori_loop(0, n_vis >> 2, quad, 0)

    @pl.when((n_vis & 2) == 2)
    def _():
        pair(j0 + (n_vis & ~3))

    for d in range(0, n_diag, 2):
        j = n_full + d
        stage(j + 1, buf1, j, buf0, key_off=d * tk)
        stage(j + 2 if d + 2 < n_diag else None, buf0, j + 1, buf1, key_off=(d + 1) * tk)

    outs = []
    for a in range(2):
        acc = acc_ref[a]
        outs.append(acc[:d_val] * (1.0 / acc[d_val:d_val + 1]))
    if mode == "A":
        ot = outs[0] - _lambda(lam_ref, lam_init) * outs[1]
        ot = ot * lax.rsqrt(jnp.mean(ot * ot, axis=0, keepdims=True) + LN_EPS)
        ot = ot * (gain_ref[0] * (1.0 - lam_init))
    else:
        ot = jnp.concatenate(outs, axis=0)
    o_ref[0] = ot.T.astype(o_ref.dtype)


def _attention(q, k, vt, first, mode, lam=None, gain=None, bias=None, lam_init=0.0):
    b, t, width = q.shape
    nk, tk = vt.shape[1], vt.shape[3]
    tq = 2 * tk
    assert t % tq == 0
    groups = width // LANES
    d_aug = (2 * A_HEAD_DIM if mode == "A" else C_HEAD_DIM) + BF16_ROWS
    kspec = pl.BlockSpec((1, nk, tk, LANES), lambda bi, g, i, f: (bi, 0, 0, g))
    in_specs = [pl.BlockSpec((1, tq, LANES), lambda bi, g, i, f: (bi, i, g)), kspec]
    args = [q, k.reshape(b, nk, tk, width)]
    if mode == "C":
        in_specs.append(kspec)
        args.append(bias.reshape(b, nk, tk, groups * LANES))
    in_specs.append(pl.BlockSpec((1, nk, LANES, tk), lambda bi, g, i, f: (bi, 0, g, 0)))
    args.append(vt)
    if mode == "A":
        in_specs += [pl.BlockSpec((4, A_HEAD_DIM), lambda bi, g, i, f: (0, 0)),
                     pl.BlockSpec((1, LANES, 1), lambda bi, g, i, f: (g, 0, 0))]
        args += [lam, gain]
    return pl.pallas_call(
        functools.partial(_attn_kernel, mode=mode, lam_init=lam_init),
        grid_spec=pltpu.PrefetchScalarGridSpec(
            num_scalar_prefetch=1,
            grid=(b, groups, t // tq),
            in_specs=in_specs,
            out_specs=pl.BlockSpec((1, tq, LANES), lambda bi, g, i, f: (bi, i, g)),
            scratch_shapes=[pltpu.VMEM((2, tq, LANES if mode == "A" else 2 * LANES), BF16),
                            pltpu.VMEM((2, 1, tq), F32),
                            pltpu.VMEM((2, d_aug, tq), F32),
                            pltpu.VMEM((2 * tq // MXU_DIM, tk, MXU_DIM), F32),
                            pltpu.VMEM((2 * tq // MXU_DIM, tk, MXU_DIM), F32),
                            pltpu.VMEM((2 * tq // MXU_DIM, 1, MXU_DIM), F32),
                            pltpu.VMEM((2 * tq // MXU_DIM, 1, MXU_DIM), F32)]),
        out_shape=jax.ShapeDtypeStruct((b, t, width), BF16),
        compiler_params=_params(("arbitrary", "arbitrary", "arbitrary")),
        name="attn_" + mode,
    )(first, *args)


def _first_visible_block(bounds, tq_tiles):
    qn, kn, bmax, bmin = (bounds[:, :, r, :C_HEADS] for r in range(4))
    b, tiles, _ = qn.shape
    nq = tiles // tq_tiles
    blk = lambda x, f: f(x.reshape(b, nq, tq_tiles, C_HEADS), axis=2)
    qn_q = blk(qn, jnp.max) * NORM_SLACK
    own = blk(bmin, jnp.min) - qn_q * blk(kn, jnp.max) * NORM_SLACK
    best = qn_q[:, :, None, :] * (kn * NORM_SLACK)[:, None, :, :] + bmax[:, None, :, :]
    dead = best < (own[:, :, None, :] - SKIP_GAP)
    block = jnp.arange(tiles, dtype=jnp.int32)[None, None, :, None]
    lead = jnp.min(jnp.where(dead, tiles, block), axis=2)
    lead = jnp.min(lead.reshape(b, nq, C_HEADS // 2, 2), axis=-1)
    lead = jnp.minimum(lead, (jnp.arange(nq, dtype=jnp.int32) * tq_tiles)[None, :, None])
    return jnp.transpose(lead - lead % 2, (0, 2, 1)).astype(jnp.int32)


def _diff_finish(o0, o1, lam, gain, lam_init):
    o = o0 - lam * o1
    o = o * lax.rsqrt(jnp.mean(o * o, axis=-1, keepdims=True) + LN_EPS)
    return o * gain * (1.0 - lam_init)


def _cached_attn_kernel(*refs, mode, past, lam_init):
    if mode == "A":
        q_ref, kn_ref, vn_ref, ck_ref, cv_ref, lam_ref, gain_ref, o_ref = refs
    else:
        q_ref, kn_ref, vn_ref, ck_ref, cv_ref, clf_ref, nlf_ref, o_ref = refs
    t = q_ref.shape[1]
    width = o_ref.shape[2]
    groups = width // LANES
    lane = lax.broadcasted_iota(jnp.int32, (1, LANES), 1)
    low = lane < (LANES // 2)
    row = lax.broadcasted_iota(jnp.int32, (t, t), 0)
    col = lax.broadcasted_iota(jnp.int32, (t, t), 1)
    if mode == "A":
        keep = ((past + col) >> CHUNK_SHIFT) <= ((past + row) >> CHUNK_SHIFT)
        lam = _lambda(lam_ref, lam_init)
    else:
        keep = col <= row
        r = lax.broadcasted_iota(jnp.int32, (past, past), 0)
        c = lax.broadcasted_iota(jnp.int32, (past, past), 1)
        upper = jnp.where(r <= c, 1.0, 0.0).astype(BF16)
        clf = clf_ref[0]
        hi, mid, lo = _split3(clf)
        cum_c = (_dot(hi, upper) + _dot(mid, upper)) + _dot(lo, upper)
        upper_n = jnp.where(row <= col, 1.0, 0.0).astype(BF16)
        hi, mid, lo = _split3(nlf_ref[0])
        cum_n = ((_dot(hi, upper_n) + _dot(mid, upper_n)) + _dot(lo, upper_n)
                 + jnp.sum(clf, axis=1, keepdims=True))
        bias_c = cum_c * (-LOG2E)
        bias_n = cum_n * (-LOG2E)
    for g in range(groups):
        sl = slice(g * LANES, (g + 1) * LANES)
        q = q_ref[0, :, sl]
        kn = kn_ref[0, :, sl]
        vn = vn_ref[0, :, sl].astype(BF16)
        if mode == "A":
            kc = ck_ref[0, 0, :, g, :].astype(BF16)
            vc = cv_ref[0, 0, :, g, :].astype(BF16)
        else:
            kc = ck_ref[0, :, sl].astype(BF16)
            vc = cv_ref[0, :, sl].astype(BF16)
        outs = []
        for a in range(2):
            qa = jnp.where(low, q, jnp.zeros_like(q)) if a == 0 else jnp.where(low, jnp.zeros_like(q), q)
            s_c = _dot_nt(qa, kc)
            s_n = _dot_nt(qa, kn)
            if mode == "C":
                h = 2 * g + a
                s_c = s_c + bias_c[h:h + 1, :]
                s_n = s_n + bias_n[h:h + 1, :]
            s_n = jnp.where(keep, s_n, NEG)
            m = jnp.maximum(jnp.max(s_c, axis=1, keepdims=True), jnp.max(s_n, axis=1, keepdims=True))
            p_c = jnp.exp2(s_c - m)
            p_n = jnp.exp2(s_n - m)
            l = jnp.sum(p_c, axis=1, keepdims=True) + jnp.sum(p_n, axis=1, keepdims=True)
            acc = _dot(p_c.astype(BF16), vc) + _dot(p_n.astype(BF16), vn)
            outs.append(acc * (1.0 / l))
        if mode == "A":
            o = _diff_finish(outs[0], outs[1], lam, gain_ref[g], lam_init)
        else:
            o = jnp.where(low, outs[0], outs[1])
        o_ref[0, :, sl] = o.astype(o_ref.dtype)


def _cached_attention(q, kn, vn, cache_k, cache_v, mode, lam=None, gain=None, cache_lft=None, new_lft=None,
                      lam_init=0.0):
    b, t, width = q.shape
    new = pl.BlockSpec((1, t, width), lambda bi: (bi, 0, 0))
    if mode == "A":
        _, _, past, heads, dh = cache_k.shape
        old = pl.BlockSpec((1, 1, past, heads, dh), lambda bi: (0, bi, 0, 0, 0))
    else:
        past = cache_k.shape[1]
        old = pl.BlockSpec((1, past, width), lambda bi: (bi, 0, 0))
    in_specs = [new, new, new, old, old]
    args = [q, kn, vn, cache_k, cache_v]
    if mode == "A":
        in_specs += [pl.BlockSpec((4, A_HEAD_DIM), lambda bi: (0, 0)),
                     pl.BlockSpec((A_HEADS, 1, LANES), lambda bi: (0, 0, 0))]
        args += [lam, gain]
    else:
        in_specs += [pl.BlockSpec((1, C_HEADS, past), lambda bi: (bi, 0, 0)),
                     pl.BlockSpec((1, C_HEADS, t), lambda bi: (bi, 0, 0))]
        args += [cache_lft, new_lft]
    return pl.pallas_call(
        functools.partial(_cached_attn_kernel, mode=mode, past=past, lam_init=lam_init),
        grid=(b,),
        in_specs=in_specs,
        out_specs=new,
        out_shape=jax.ShapeDtypeStruct((b, t, width), BF16),
        compiler_params=_params(("arbitrary",)),
        name="cached_attn_" + mode,
    )(*args)


def _gelu_tanh(x):
    return 0.5 * x * (1.0 + jnp.tanh(math.sqrt(2.0 / math.pi) * (x + 0.044715 * (x * x * x))))


def _causal_conv(x, tail, w_ref, b_ref):
    width = w_ref.shape[0]
    tm = x.shape[0]
    cat = jnp.concatenate([tail, x], axis=0)
    y = None
    for j in range(width):
        back = width - 1 - j
        src = cat if back == 0 else pltpu.roll(cat, back, 0)
        term = src[SUBLANES:SUBLANES + tm] * w_ref[j:j + 1, :]
        y = term if y is None else y + term
    return y + b_ref[...]


def _rglru_kernel(xg_ref, h0_ref, cb_ref, cw_ref, cbias_ref, wa_ref, ba_ref, wx_ref, bx_ref, sp_ref,
                  y_ref, hl_ref, ct_ref, hc_ref, tail_ref):
    i = pl.program_id(1)
    tm = xg_ref.shape[1]

    @pl.when(i == 0)
    def _():
        hc_ref[...] = h0_ref[0]
        tail_ref[...] = cb_ref[0]

    x = xg_ref[0, :, :B_WIDTH]
    gate_in = xg_ref[0, :, B_WIDTH:]
    xc = _causal_conv(x, tail_ref[...], cw_ref, cbias_ref)
    xcb = xc.astype(BF16)
    r = _sigmoid(_dot(xcb, wa_ref[...]) + ba_ref[...])
    ig = _sigmoid(_dot(xcb, wx_ref[...]) + bx_ref[...])
    log_a = (-RG_C) * r * sp_ref[...]
    a = jnp.exp(log_a)
    th = jnp.tanh(log_a)
    bx = jnp.sqrt((-2.0 * th) / (1.0 - th)) * (ig * xc)
    row = lax.broadcasted_iota(jnp.int32, (tm, 1), 0)
    s = 1
    while s < tm:
        valid = row >= s
        a_sh = pltpu.roll(a, s, 0)
        b_sh = pltpu.roll(bx, s, 0)
        bx = jnp.where(valid, a * b_sh + bx, bx)
        a = jnp.where(valid, a * a_sh, a)
        s *= 2
    h = a * hc_ref[...] + bx
    y_ref[0] = (h * _gelu_tanh(gate_in)).astype(y_ref.dtype)
    h_tail = h[tm - SUBLANES:, :]
    x_tail = x[tm - SUBLANES:, :]
    hc_ref[...] = h_tail[SUBLANES - 1:, :]
    tail_ref[...] = x_tail
    hl_ref[0] = h_tail
    ct_ref[0] = x_tail


def _rglru(xg, h0, conv_tail, cw, cbias, wa, ba, wx, bx, sp, tm_pref=256):
    b, t, _ = xg.shape
    tm = _row_tile(t, tm_pref)
    wspec = lambda shape: pl.BlockSpec(shape, lambda bi, i: tuple(0 for _ in shape))
    return pl.pallas_call(
        _rglru_kernel,
        grid=(b, t // tm),
        in_specs=[pl.BlockSpec((1, tm, 2 * B_WIDTH), lambda bi, i: (bi, i, 0)),
                  pl.BlockSpec((1, 1, B_WIDTH), lambda bi, i: (bi, 0, 0)),
                  pl.BlockSpec((1, SUBLANES, B_WIDTH), lambda bi, i: (bi, 0, 0)),
                  wspec((B_CONV, B_WIDTH)), wspec((1, B_WIDTH)),
                  wspec((B_WIDTH, B_WIDTH)), wspec((1, B_WIDTH)),
                  wspec((B_WIDTH, B_WIDTH)), wspec((1, B_WIDTH)), wspec((1, B_WIDTH))],
        out_specs=[pl.BlockSpec((1, tm, B_WIDTH), lambda bi, i: (bi, i, 0)),
                   pl.BlockSpec((1, SUBLANES, B_WIDTH), lambda bi, i: (bi, 0, 0)),
                   pl.BlockSpec((1, SUBLANES, B_WIDTH), lambda bi, i: (bi, 0, 0))],
        out_shape=[jax.ShapeDtypeStruct((b, t, B_WIDTH), BF16),
                   jax.ShapeDtypeStruct((b, SUBLANES, B_WIDTH), F32),
                   jax.ShapeDtypeStruct((b, SUBLANES, B_WIDTH), F32)],
        scratch_shapes=[pltpu.VMEM((1, B_WIDTH), F32), pltpu.VMEM((SUBLANES, B_WIDTH), F32)],
        compiler_params=_params(("arbitrary", "arbitrary")),
        name="rglru",
    )(xg, h0, conv_tail, cw, cbias, wa, ba, wx, bx, sp)


def _proj_ln_kernel(*refs, n_in):
    h_refs = refs[:n_in]
    w_ref, x_ref, gate_ref, g_ref, b_ref, o_ref = refs[n_in:]
    tm = x_ref.shape[1]
    rows = [slice(r, min(r + LN_ROWS, tm)) for r in range(0, tm, LN_ROWS)]

    def matmul(rs):
        hs = [r[0, rs, :] for r in h_refs]
        return _dot(hs[0] if n_in == 1 else jnp.concatenate(hs, axis=-1), w_ref[...])

    def norm(rs, proj):
        y = ALPHA * x_ref[0, rs, :] + gate_ref[0] * proj
        mu = jnp.mean(y, axis=-1, keepdims=True)
        yc = y - mu
        var = jnp.mean(yc * yc, axis=-1, keepdims=True)
        o_ref[0, rs, :] = yc * lax.rsqrt(var + LN_EPS) * g_ref[...] + b_ref[...]

    pending = matmul(rows[0])
    for r in range(len(rows)):
        nxt = matmul(rows[r + 1]) if r + 1 < len(rows) else None
        norm(rows[r], pending)
        pending = nxt


def _proj_ln(hs, w, x, gate, ln_g, ln_b, tm_pref=512):
    b, t, d = x.shape
    tm = _row_tile(t, tm_pref)
    k = w.shape[0]
    in_specs = [pl.BlockSpec((1, tm, h.shape[2]), lambda bi, i: (bi, i, 0)) for h in hs]
    in_specs += [pl.BlockSpec((k, d), lambda bi, i: (0, 0)),
                 pl.BlockSpec((1, tm, d), lambda bi, i: (bi, i, 0)),
                 pl.BlockSpec((1, 1, d), lambda bi, i: (bi, 0, 0)),
                 pl.BlockSpec((1, d), lambda bi, i: (0, 0)),
                 pl.BlockSpec((1, d), lambda bi, i: (0, 0))]
    return pl.pallas_call(
        functools.partial(_proj_ln_kernel, n_in=len(hs)),
        grid=(b, t // tm),
        in_specs=in_specs,
        out_specs=pl.BlockSpec((1, tm, d), lambda bi, i: (bi, i, 0)),
        out_shape=jax.ShapeDtypeStruct((b, t, d), F32),
        compiler_params=_params(("arbitrary", "arbitrary")),
        name="proj_ln",
    )(*hs, w, x, gate, ln_g, ln_b)


def _ffn_up_kernel(x_ref, sc_ref, sh_ref, w_ref, cb_ref, cw_ref, cbias_ref, h_ref, ct_ref, tail_ref):
    i = pl.program_id(1)
    tm = x_ref.shape[1]

    @pl.when(i == 0)
    def _():
        tail_ref[...] = cb_ref[0]

    u = _modulate(x_ref[0], sc_ref[0], sh_ref[0])
    chunks = [slice(c, min(c + FFN_CHUNK, D_FF)) for c in range(0, D_FF, FFN_CHUNK)]

    def matmuls(cs):
        gs = slice(D_FF + cs.start, D_FF + cs.stop)
        return _dot(u, w_ref[:, cs]), _dot(u, w_ref[:, gs])

    def gate(cs, a, g):
        gc = _causal_conv(g, tail_ref[:, cs], cw_ref.at[:, cs], cbias_ref.at[:, cs])
        silu = gc * (0.5 * jnp.tanh(0.5 * gc) + 0.5)
        h_ref[0, :, cs] = (a * silu).astype(h_ref.dtype)
        g_tail = g[tm - SUBLANES:, :]
        tail_ref[:, cs] = g_tail
        ct_ref[0, :, cs] = g_tail

    pending = matmuls(chunks[0])
    for c in range(len(chunks)):
        nxt = matmuls(chunks[c + 1]) if c + 1 < len(chunks) else None
        gate(chunks[c], *pending)
        pending = nxt


def _ffn_up(x, sc, sh, w, conv_tail, cw, cbias, tm_pref=256):
    b, t, d = x.shape
    tm = _row_tile(t, tm_pref)
    return pl.pallas_call(
        _ffn_up_kernel,
        grid=(b, t // tm),
        in_specs=[pl.BlockSpec((1, tm, d), lambda bi, i: (bi, i, 0)),
                  pl.BlockSpec((1, 1, d), lambda bi, i: (bi, 0, 0)),
                  pl.BlockSpec((1, 1, d), lambda bi, i: (bi, 0, 0)),
                  pl.BlockSpec((d, 2 * D_FF), lambda bi, i: (0, 0)),
                  pl.BlockSpec((1, SUBLANES, D_FF), lambda bi, i: (bi, 0, 0)),
                  pl.BlockSpec((FFN_CONV, D_FF), lambda bi, i: (0, 0)),
                  pl.BlockSpec((1, D_FF), lambda bi, i: (0, 0))],
        out_specs=[pl.BlockSpec((1, tm, D_FF), lambda bi, i: (bi, i, 0)),
                   pl.BlockSpec((1, SUBLANES, D_FF), lambda bi, i: (bi, 0, 0))],
        out_shape=[jax.ShapeDtypeStruct((b, t, D_FF), BF16),
                   jax.ShapeDtypeStruct((b, SUBLANES, D_FF), F32)],
        scratch_shapes=[pltpu.VMEM((SUBLANES, D_FF), F32)],
        compiler_params=_params(("arbitrary", "arbitrary")),
        name="ffn_up",
    )(x, sc, sh, w, conv_tail, cw, cbias)


def _pad_tail(buf):
    return jnp.pad(buf, ((0, 0), (SUBLANES - buf.shape[1], 0), (0, 0)))


def _rope_tables(past, t):
    half = A_HEAD_DIM // 2
    inv = ROPE_THETA ** (-jnp.arange(0, A_HEAD_DIM, 2, dtype=F32) / A_HEAD_DIM)
    pos = (past + jnp.arange(t, dtype=jnp.int32)).astype(F32)
    ang = pos[:, None] * inv[None, :]
    cos = jnp.tile(jnp.cos(ang), (1, LANES // half))
    sin = jnp.sin(ang)
    sin_signed = jnp.tile(jnp.concatenate([-sin, sin], axis=1), (1, LANES // A_HEAD_DIM))
    return cos, sin_signed


def _block_diag(w):
    n, i, o = w.shape
    return jnp.einsum("nio,nm->nimo", w, jnp.eye(n, dtype=w.dtype)).reshape(n * i, n * o)


def _prepare(p):
    w = {}
    w_in_ab = p["w_in_ab"][0]
    w["in_ab"] = w_in_ab.astype(BF16)
    w["out_ab"] = p["w_out_ab"][0].astype(BF16)
    w["lam"] = jnp.stack([p["lam_q1"][0], p["lam_k1"][0], p["lam_q2"][0], p["lam_k2"][0]])
    w["gain_row"] = p["attn_gain"][0].reshape(A_HEADS, 1, LANES)
    w["gain_col"] = p["attn_gain"][0].reshape(A_HEADS, LANES, 1)
    w["b_conv_w"] = p["b_conv_w"][0]
    w["b_conv_b"] = p["b_conv_b"][0].reshape(1, B_WIDTH)
    w["rg_a"] = _block_diag(p["w_rg_a"][0]).astype(BF16)
    w["rg_x"] = _block_diag(p["w_rg_x"][0]).astype(BF16)
    w["b_rg_a"] = p["b_rg_a"][0].reshape(1, B_WIDTH)
    w["b_rg_x"] = p["b_rg_x"][0].reshape(1, B_WIDTH)
    w["rg_L"] = p["rg_L"][0].reshape(1, B_WIDTH)
    w_in_c = p["w_in_c"][0]
    w["in_c"] = jnp.pad(w_in_c, ((0, 0), (0, LANES - C_HEADS))).astype(BF16)
    w["in_c_ft"] = w_in_c[:, 3 * C_WIDTH:].T.astype(BF16)
    w["bf_row"] = jnp.pad(p["b_f"][0], (0, LANES - C_HEADS)).reshape(1, LANES)
    w["bf_col"] = p["b_f"][0].reshape(C_HEADS, 1)
    w["place"] = _bias_placement()
    w["seg"] = jnp.asarray(np.repeat(np.eye(LANES, dtype=np.float32)[:C_HEADS], C_HEAD_DIM, axis=0), BF16)
    w["out_c"] = p["w_out_c"][0].astype(BF16)
    w["up"] = [p["w_up"][i].astype(BF16) for i in range(DEPTH)]
    w["down"] = [p["w_down"][i].astype(BF16) for i in range(DEPTH)]
    w["ffn_conv_w"] = [p["ffn_conv_w"][i] for i in range(DEPTH)]
    w["ffn_conv_b"] = [p["ffn_conv_b"][i].reshape(1, D_FF) for i in range(DEPTH)]
    w["ln1_g"] = [p["ln1_g"][i].reshape(1, D_MODEL) for i in range(DEPTH)]
    w["ln1_b"] = [p["ln1_b"][i].reshape(1, D_MODEL) for i in range(DEPTH)]
    w["ln2_g"] = [p["ln2_g"][i].reshape(1, D_MODEL) for i in range(DEPTH)]
    w["ln2_b"] = [p["ln2_b"][i].reshape(1, D_MODEL) for i in range(DEPTH)]
    return w


def _softplus_kernel(x_ref, o_ref):
    o_ref[...] = _softplus(-x_ref[...])


def _trunk(x, mods, w, sp, cache_a_k=None, cache_a_v=None, state_b_h=None, state_b_conv=None,
           cache_c_k=None, cache_c_v=None, cache_c_logf=None, state_ffn_conv=None):
    b, t, d = x.shape
    cached = cache_a_k is not None
    past = cache_a_k.shape[2] if cached else 0
    outs = {}
    for i in range(DEPTH):
        sh1, sc1, g1, sh2, sc2, g2 = [m[:, None, :] for m in jnp.split(mods[i], 6, axis=-1)]
        if i % 2 == 0:
            lam_init = 0.8 - 0.6 * math.exp(-0.3 * i)
            cos, sin = _rope_tables(past, t)
            q, kb, vt, k32, v32, xg = _proj_ab(x, sc1, sh1, w["in_ab"], cos, sin)
            if cached:
                o = _cached_attention(q, kb, v32.reshape(b, t, A_WIDTH), cache_a_k, cache_a_v,
                                      "A", lam=w["lam"], gain=w["gain_row"], lam_init=lam_init)
                h0 = state_b_h[0][:, None, :]
                ctail = _pad_tail(state_b_conv[0])
            else:
                first = jnp.zeros((b, A_HEADS, t // (2 * ATTN_BLOCK)), jnp.int32)
                o = _attention(q, kb, vt, first, "A", lam=w["lam"], gain=w["gain_col"], lam_init=lam_init)
                h0 = jnp.zeros((b, 1, B_WIDTH), F32)
                ctail = jnp.zeros((b, SUBLANES, B_WIDTH), F32)
            yb, h_tail, x_tail = _rglru(xg, h0, ctail, w["b_conv_w"], w["b_conv_b"], w["rg_a"], w["b_rg_a"],
                                        w["rg_x"], w["b_rg_x"], sp)
            outs["a_k"] = k32[None]
            outs["a_v"] = v32[None]
            outs["b_h"] = h_tail[:, SUBLANES - 1, :][None]
            outs["b_conv"] = x_tail[:, SUBLANES - (B_CONV - 1):, :][None]
            x = _proj_ln([o, yb], w["out_ab"], x, g1, w["ln1_g"][i], w["ln1_b"][i])
        else:
            q, kb, bias, vt, k32, v32, lf, lft, bounds = _proj_c(x, sc1, sh1, w["in_c"], w["in_c_ft"],
                                                                 w["bf_row"], w["bf_col"], w["place"], w["seg"])
            if cached:
                o = _cached_attention(q, kb, v32, cache_c_k[0].reshape(b, past, C_WIDTH),
                                      cache_c_v[0].reshape(b, past, C_WIDTH),
                                      "C", cache_lft=jnp.swapaxes(cache_c_logf[0], 1, 2), new_lft=lft)
            else:
                o = _attention(q, kb, vt, _first_visible_block(bounds, 2), "C", bias=bias)
            outs["c_k"] = k32.reshape(1, b, t, C_HEADS, C_HEAD_DIM)
            outs["c_v"] = v32.reshape(1, b, t, C_HEADS, C_HEAD_DIM)
            outs["c_logf"] = lf[None]
            x = _proj_ln([o], w["out_c"], x, g1, w["ln1_g"][i], w["ln1_b"][i])
        ftail = _pad_tail(state_ffn_conv[i]) if cached else jnp.zeros((b, SUBLANES, D_FF), F32)
        hmid, g_tail = _ffn_up(x, sc2, sh2, w["up"][i], ftail, w["ffn_conv_w"][i], w["ffn_conv_b"][i])
        outs.setdefault("ffn", []).append(g_tail[:, SUBLANES - (FFN_CONV - 1):, :])
        x = _proj_ln([hmid], w["down"][i], x, g2, w["ln2_g"][i], w["ln2_b"][i])
    return (x, outs["a_k"], outs["a_v"], outs["b_h"], outs["b_conv"],
            outs["c_k"], outs["c_v"], outs["c_logf"], jnp.stack(outs["ffn"]))


def kernel(x_prompt, x_sample, c_prompt, c_sample, cache_a_k, cache_a_v, state_b_h, state_b_conv, cache_c_k, cache_c_v, cache_c_logf, state_ffn_conv, w_ada, b_ada, ln1_g, ln1_b, ln2_g, ln2_b, w_in_ab, lam_q1, lam_k1, lam_q2, lam_k2, attn_gain, b_conv_w, b_conv_b, w_rg_a, b_rg_a, w_rg_x, b_rg_x, rg_L, w_out_ab, w_in_c, b_f, w_out_c, w_up, ffn_conv_w, ffn_conv_b, w_down):
    p = dict(w_in_ab=w_in_ab, lam_q1=lam_q1, lam_k1=lam_k1, lam_q2=lam_q2, lam_k2=lam_k2, attn_gain=attn_gain,
             b_conv_w=b_conv_w, b_conv_b=b_conv_b, w_rg_a=w_rg_a, b_rg_a=b_rg_a, w_rg_x=w_rg_x, b_rg_x=b_rg_x,
             rg_L=rg_L, w_out_ab=w_out_ab, w_in_c=w_in_c, b_f=b_f, w_out_c=w_out_c, w_up=w_up,
             ffn_conv_w=ffn_conv_w, ffn_conv_b=ffn_conv_b, w_down=w_down,
             ln1_g=ln1_g, ln1_b=ln1_b, ln2_g=ln2_g, ln2_b=ln2_b)
    w = _prepare(p)
    bp = c_prompt.shape[0]
    bs = c_sample.shape[0]
    rows = -(-(bp + bs) // 16) * 16
    c_all = jnp.pad(jnp.concatenate([c_prompt, c_sample], axis=0), ((0, rows - bp - bs), (0, 0)))
    mods = _mods(c_all, w_ada, b_ada)
    sp = pl.pallas_call(_softplus_kernel, out_shape=jax.ShapeDtypeStruct((1, B_WIDTH), F32),
                        name="softplus")(w["rg_L"])
    res_p = _trunk(x_prompt, mods[:, :bp], w, sp)
    res_s = _trunk(x_sample, mods[:, bp:bp + bs], w, sp, cache_a_k, cache_a_v, state_b_h, state_b_conv,
                   cache_c_k, cache_c_v, cache_c_logf, state_ffn_conv)
    return (res_p[0], res_s[0]) + res_p[1:] + res_s[1:]
```

```python
import functools
import math

import numpy as np
import jax
import jax.numpy as jnp
from jax import lax
from jax.experimental import pallas as pl
from jax.experimental.pallas import tpu as pltpu

F32 = jnp.float32
BF16 = jnp.bfloat16

D_MODEL = 1024
DEPTH = 2
CHUNK = 64
CHUNK_SHIFT = 6
A_HEADS = 4
A_HEAD_DIM = 64
A_WIDTH = A_HEADS * 2 * A_HEAD_DIM
B_WIDTH = 512
B_BLOCKS = 8
B_CONV = 4
RG_C = 8.0
C_HEADS = 16
C_HEAD_DIM = 64
C_WIDTH = C_HEADS * C_HEAD_DIM
D_FF = 2816
FFN_CONV = 3
ROPE_THETA = 10000.0
ALPHA = (2 * DEPTH) ** 0.25
LN_EPS = 1e-5
NEG = -1e30
LOG2E = 1.4426950408889634

LANES = 128
SUBLANES = 8
BF16_ROWS = 16
MXU_DIM = 256
VMEM_LIMIT = 56 * 1024 * 1024
BIAS_WIDTH = (C_HEADS // 2) * LANES
FFN_CHUNK = 256
LN_ROWS = 256
SKIP_GAP = 160.0
NORM_SLACK = 1.03
ATTN_BLOCK = 512


def _params(sem, flags=None):
    return pltpu.CompilerParams(dimension_semantics=sem, vmem_limit_bytes=VMEM_LIMIT, flags=flags)


def _row_tile(t, pref):
    if t <= pref:
        return t
    tm = pref
    while t % tm:
        tm //= 2
    return tm


def _modulate(x, sc, sh):
    return (x * (1.0 + sc) + sh).astype(BF16)


def _sigmoid(x):
    return 1.0 / (1.0 + jnp.exp(-x))


def _softplus(x):
    return jnp.maximum(x, 0.0) + jnp.log1p(jnp.exp(-jnp.abs(x)))


def _log_sigmoid(x):
    return jnp.minimum(x, 0.0) - jnp.log1p(jnp.exp(-jnp.abs(x)))


def _split3(x):
    hi = x.astype(BF16)
    r1 = x - hi.astype(F32)
    mid = r1.astype(BF16)
    lo = (r1 - mid.astype(F32)).astype(BF16)
    return hi, mid, lo


def _dot(a, b):
    return jnp.dot(a, b, preferred_element_type=F32)


def _dot_nt(a, b):
    return lax.dot_general(a, b, (((1,), (1,)), ((), ())), preferred_element_type=F32)


def _mods_kernel(c_ref, w_ref, b_ref, o_ref):
    c = c_ref[...]
    s = (c * _sigmoid(c)).astype(BF16)
    o_ref[0] = _dot(s, w_ref[0].astype(BF16)) + b_ref[0]


def _mods(c_all, w_ada, b_ada):
    rows, d = c_all.shape
    n = w_ada.shape[-1]
    tn = 1536
    return pl.pallas_call(
        _mods_kernel,
        grid=(DEPTH, n // tn),
        in_specs=[pl.BlockSpec((rows, d), lambda l, j: (0, 0)),
                  pl.BlockSpec((1, d, tn), lambda l, j: (l, 0, j)),
                  pl.BlockSpec((1, 1, tn), lambda l, j: (l, 0, j))],
        out_specs=pl.BlockSpec((1, rows, tn), lambda l, j: (l, 0, j)),
        out_shape=jax.ShapeDtypeStruct((DEPTH, rows, n), F32),
        compiler_params=_params(("arbitrary", "arbitrary")),
        name="mods",
    )(c_all, w_ada, b_ada.reshape(DEPTH, 1, n))


def _rope_slab(x, cos, sin_signed, first_half):
    fwd = pltpu.roll(x, LANES - A_HEAD_DIM // 2, 1)
    bwd = pltpu.roll(x, A_HEAD_DIM // 2, 1)
    partner = jnp.where(first_half, fwd, bwd)
    return x * cos + partner * sin_signed


def _proj_ab_kernel(x_ref, sc_ref, sh_ref, w_ref, cos_ref, sin_ref,
                    q_ref, kb_ref, vt_ref, k_ref, v_ref, xg_ref, *, q_scale):
    u = _modulate(x_ref[0], sc_ref[0], sh_ref[0])
    pr = _dot(u, w_ref[...])
    cos = cos_ref[...]
    sin = sin_ref[...]
    lane = lax.broadcasted_iota(jnp.int32, (1, LANES), 1)
    first_half = (lane & (A_HEAD_DIM - 1)) < (A_HEAD_DIM // 2)
    for h in range(A_HEADS):
        sl = slice(h * LANES, (h + 1) * LANES)
        q = _rope_slab(pr[:, sl], cos, sin, first_half)
        q_ref[0, :, sl] = (q * q_scale).astype(BF16)
        k = _rope_slab(pr[:, A_WIDTH + h * LANES:A_WIDTH + (h + 1) * LANES], cos, sin, first_half)
        k_ref[0, :, h, :] = k
        kb_ref[0, :, sl] = k.astype(BF16)
    for h in range(A_HEADS):
        v_ref[0, :, h, :] = pr[:, 2 * A_WIDTH + h * LANES:2 * A_WIDTH + (h + 1) * LANES]
    vt_ref[0, 0] = pr[:, 2 * A_WIDTH:3 * A_WIDTH].T.astype(BF16)
    xg_ref[0] = pr[:, 3 * A_WIDTH:]


def _proj_ab(x, sc, sh, w, cos, sin):
    b, t, d = x.shape
    tm = _row_tile(t, ATTN_BLOCK)
    nt = t // tm
    n = w.shape[1]
    q_scale = A_HEAD_DIM ** -0.5 * LOG2E
    row = lambda width: pl.BlockSpec((1, tm, width), lambda bi, i: (bi, i, 0))
    const = lambda shape: pl.BlockSpec(shape, lambda bi, i: tuple(0 for _ in shape))
    heads = pl.BlockSpec((1, tm, A_HEADS, LANES), lambda bi, i: (bi, i, 0, 0))
    return pl.pallas_call(
        functools.partial(_proj_ab_kernel, q_scale=q_scale),
        grid=(b, nt),
        in_specs=[row(d),
                  pl.BlockSpec((1, 1, d), lambda bi, i: (bi, 0, 0)),
                  pl.BlockSpec((1, 1, d), lambda bi, i: (bi, 0, 0)),
                  const((d, n)),
                  pl.BlockSpec((tm, LANES), lambda bi, i: (i, 0)),
                  pl.BlockSpec((tm, LANES), lambda bi, i: (i, 0))],
        out_specs=[row(A_WIDTH), row(A_WIDTH),
                   pl.BlockSpec((1, 1, A_WIDTH, tm), lambda bi, i: (bi, i, 0, 0)),
                   heads, heads, row(2 * B_WIDTH)],
        out_shape=[jax.ShapeDtypeStruct((b, t, A_WIDTH), BF16),
                   jax.ShapeDtypeStruct((b, t, A_WIDTH), BF16),
                   jax.ShapeDtypeStruct((b, nt, A_WIDTH, tm), BF16),
                   jax.ShapeDtypeStruct((b, t, A_HEADS, LANES), F32),
                   jax.ShapeDtypeStruct((b, t, A_HEADS, LANES), F32),
                   jax.ShapeDtypeStruct((b, t, 2 * B_WIDTH), F32)],
        compiler_params=_params(("arbitrary", "arbitrary")),
        name="proj_ab",
    )(x, sc, sh, w, cos, sin)


def _bias_placement():
    e = np.zeros((3 * LANES, BIAS_WIDTH), np.float32)
    for piece in range(3):
        for h in range(C_HEADS):
            e[piece * LANES + h, (h // 2) * LANES + 3 * (h % 2) + piece] = 1.0
    return jnp.asarray(e, BF16)


def _head_norm_max(x, seg):
    n2 = _dot((x * x).astype(BF16), seg)
    return jnp.sqrt(jnp.max(n2, axis=0, keepdims=True))


def _proj_c_kernel(x_ref, sc_ref, sh_ref, w_ref, wft_ref, bfr_ref, bfc_ref, place_ref, seg_ref,
                   q_ref, kb_ref, bias_ref, vt_ref, k_ref, v_ref, lf_ref, lft_ref, bound_ref, run_ref, *, q_scale):
    i = pl.program_id(1)
    tm = x_ref.shape[1]

    @pl.when(i == 0)
    def _():
        run_ref[...] = jnp.zeros_like(run_ref)

    u = _modulate(x_ref[0], sc_ref[0], sh_ref[0])
    pr = _dot(u, w_ref[...])
    qs = pr[:, :C_WIDTH] * q_scale
    q_ref[0] = qs.astype(BF16)
    k = pr[:, C_WIDTH:2 * C_WIDTH]
    k_ref[0] = k
    kb_ref[0] = k.astype(BF16)
    v = pr[:, 2 * C_WIDTH:3 * C_WIDTH]
    v_ref[0] = v
    vt_ref[0, 0] = v.T.astype(BF16)
    lf = _log_sigmoid(pr[:, 3 * C_WIDTH:] + bfr_ref[...])
    lf_ref[0] = lf[:, :C_HEADS]
    lft_ref[0] = _log_sigmoid(_dot_nt(wft_ref[...], u) + bfc_ref[...])
    r = lax.broadcasted_iota(jnp.int32, (tm, tm), 0)
    c = lax.broadcasted_iota(jnp.int32, (tm, tm), 1)
    lower = jnp.where(c <= r, 1.0, 0.0).astype(BF16)
    hi, mid, lo = _split3(lf)
    cum = (_dot(lower, hi) + _dot(lower, mid)) + _dot(lower, lo) + run_ref[...]
    bias = cum * (-LOG2E)
    pieces = jnp.concatenate(_split3(bias), axis=1)
    bias_ref[0] = _dot(pieces, place_ref[...]).astype(BF16)
    seg = seg_ref[...]
    bound_ref[0, 0] = jnp.concatenate(
        [_head_norm_max(qs, seg), _head_norm_max(k, seg),
         jnp.max(bias, axis=0, keepdims=True), jnp.min(bias, axis=0, keepdims=True),
         jnp.zeros((SUBLANES - 4, LANES), F32)], axis=0)
    run_ref[...] = run_ref[...] + jnp.sum(lf, axis=0, keepdims=True)


def _proj_c(x, sc, sh, w, wft, bf_row, bf_col, place, seg):
    b, t, d = x.shape
    tm = _row_tile(t, ATTN_BLOCK)
    nt = t // tm
    n = w.shape[1]
    q_scale = C_HEAD_DIM ** -0.5 * LOG2E
    row = lambda width: pl.BlockSpec((1, tm, width), lambda bi, i: (bi, i, 0))
    const = lambda shape: pl.BlockSpec(shape, lambda bi, i: tuple(0 for _ in shape))
    return pl.pallas_call(
        functools.partial(_proj_c_kernel, q_scale=q_scale),
        grid=(b, nt),
        in_specs=[row(d),
                  pl.BlockSpec((1, 1, d), lambda bi, i: (bi, 0, 0)),
                  pl.BlockSpec((1, 1, d), lambda bi, i: (bi, 0, 0)),
                  const((d, n)), const((C_HEADS, d)),
                  const((1, LANES)), const((C_HEADS, 1)), const((3 * LANES, BIAS_WIDTH)), const((C_WIDTH, LANES))],
        out_specs=[row(C_WIDTH), row(C_WIDTH), row(BIAS_WIDTH),
                   pl.BlockSpec((1, 1, C_WIDTH, tm), lambda bi, i: (bi, i, 0, 0)),
                   row(C_WIDTH), row(C_WIDTH), row(C_HEADS),
                   pl.BlockSpec((1, C_HEADS, tm), lambda bi, i: (bi, 0, i)),
                   pl.BlockSpec((1, 1, SUBLANES, LANES), lambda bi, i: (bi, i, 0, 0))],
        out_shape=[jax.ShapeDtypeStruct((b, t, C_WIDTH), BF16),
                   jax.ShapeDtypeStruct((b, t, C_WIDTH), BF16),
                   jax.ShapeDtypeStruct((b, t, BIAS_WIDTH), BF16),
                   jax.ShapeDtypeStruct((b, nt, C_WIDTH, tm), BF16),
                   jax.ShapeDtypeStruct((b, t, C_WIDTH), F32),
                   jax.ShapeDtypeStruct((b, t, C_WIDTH), F32),
                   jax.ShapeDtypeStruct((b, t, C_HEADS), F32),
                   jax.ShapeDtypeStruct((b, C_HEADS, t), F32),
                   jax.ShapeDtypeStruct((b, nt, SUBLANES, LANES), F32)],
        scratch_shapes=[pltpu.VMEM((1, LANES), F32)],
        compiler_params=_params(("arbitrary", "arbitrary")),
        name="proj_c",
    )(x, sc, sh, w, wft, bf_row, bf_col, place, seg)


def _lambda(lam_ref, lam_init):
    lq1, lk1, lq2, lk2 = (lam_ref[r:r + 1, :] for r in range(4))
    return (jnp.exp(jnp.sum(lq1 * lk1, axis=1, keepdims=True))
            - jnp.exp(jnp.sum(lq2 * lk2, axis=1, keepdims=True)) + lam_init)


def _attn_kernel(*refs, mode, lam_init):
    first_ref, refs = refs[0], refs[1:]
    if mode == "A":
        q_ref, k_ref, vt_ref, lam_ref, gain_ref, o_ref, qc_ref, m_ref, acc_ref, s0_ref, s1_ref, c0_ref, c1_ref = refs
        bias_ref = None
        d_val = 2 * A_HEAD_DIM
    else:
        q_ref, k_ref, bias_ref, vt_ref, o_ref, qc_ref, m_ref, acc_ref, s0_ref, s1_ref, c0_ref, c1_ref = refs
        d_val = C_HEAD_DIM
    tq = q_ref.shape[1]
    tk = k_ref.shape[2]
    qi = pl.program_id(2)
    lane = lax.broadcasted_iota(jnp.int32, (1, LANES), 1)
    low = lane < (LANES // 2)
    q = q_ref[0]
    zero = jnp.zeros_like(q)
    for a in range(2):
        qa = jnp.where(low, q, zero) if a == 0 else jnp.where(low, zero, q)
        if bias_ref is not None:
            pick = jnp.where((lane >= 3 * a) & (lane < 3 * a + 3), 1.0, 0.0).astype(BF16)
            qa = jnp.concatenate([qa, jnp.broadcast_to(pick, (tq, LANES))], axis=1)
        qc_ref[a] = qa
    m_ref[...] = jnp.full(m_ref.shape, NEG, F32)
    acc_ref[...] = jnp.zeros(acc_ref.shape, F32)
    ones = jnp.ones((BF16_ROWS, tk), BF16)

    units = [(a, slice(n * MXU_DIM, (n + 1) * MXU_DIM)) for a in range(2) for n in range(tq // MXU_DIM)]

    def score_chain(j, u):
        a, cs = units[u]
        kc = k_ref[0, j]
        if bias_ref is not None:
            kc = jnp.concatenate([kc, bias_ref[0, j]], axis=1)
        return _dot_nt(kc, qc_ref[a, cs, :])

    def value_chain(j, u, st, cmax, key_off):
        a, cs = units[u]
        masked = key_off is not None
        if masked:
            key = lax.broadcasted_iota(jnp.int32, (tk, MXU_DIM), 0) + key_off
            qry = lax.broadcasted_iota(jnp.int32, (tk, MXU_DIM), 1) + cs.start
            keep = ((key >> CHUNK_SHIFT) <= (qry >> CHUNK_SHIFT)) if mode == "A" else (key <= qry)
            st = jnp.where(keep, st, NEG)
            cmax = jnp.max(st, axis=0, keepdims=True)
        m_prev = m_ref[a, :, cs]
        m_new = jnp.maximum(m_prev, cmax)
        alpha = jnp.exp2(m_prev - m_new)
        pt = jnp.exp2(st - m_new).astype(BF16)
        vt = vt_ref[0, j]
        va = vt if mode == "A" else vt[a * d_val:(a + 1) * d_val]
        va = jnp.concatenate([va, ones], axis=0)
        acc_ref[a, :, cs] = alpha * acc_ref[a, :, cs] + _dot(va, pt)
        m_ref[a, :, cs] = m_new

    def stage(j_scores, dst, j_values, src, key_off=None):
        for t in range(len(units) + 1):
            if j_scores is not None and t < len(units):
                st = score_chain(j_scores, t)
                dst[0][t] = st
                dst[1][t] = jnp.max(st, axis=0, keepdims=True)
            if j_values is not None and t >= 1:
                value_chain(j_values, t - 1, src[0][t - 1], src[1][t - 1], key_off)

    buf0 = (s0_ref, c0_ref)
    buf1 = (s1_ref, c1_ref)
    n_diag = tq // tk
    n_full = qi * n_diag
    j0 = first_ref[pl.program_id(0), pl.program_id(1), qi]
    n_vis = n_full - j0
    stage(j0, buf0, None, None)

    def pair(j):
        stage(j + 1, buf1, j, buf0)
        stage(j + 2, buf0, j + 1, buf1)

    def quad(p, carry):
        pair(j0 + 4 * p)
        pair(j0 + 4 * p + 2)
        return carry

    lax.fori_loop(0, n_vis >> 2, quad, 0)

    @pl.when((n_vis & 2) == 2)
    def _():
        pair(j0 + (n_vis & ~3))

    for d in range(0, n_diag, 2):
        j = n_full + d
        stage(j + 1, buf1, j, buf0, key_off=d * tk)
        stage(j + 2 if d + 2 < n_diag else None, buf0, j + 1, buf1, key_off=(d + 1) * tk)

    outs = []
    for a in range(2):
        acc = acc_ref[a]
        outs.append(acc[:d_val] * (1.0 / acc[d_val:d_val + 1]))
    if mode == "A":
        ot = outs[0] - _lambda(lam_ref, lam_init) * outs[1]
        ot = ot * lax.rsqrt(jnp.mean(ot * ot, axis=0, keepdims=True) + LN_EPS)
        ot = ot * (gain_ref[0] * (1.0 - lam_init))
    else:
        ot = jnp.concatenate(outs, axis=0)
    o_ref[0] = ot.T.astype(o_ref.dtype)


def _attention(q, k, vt, first, mode, lam=None, gain=None, bias=None, lam_init=0.0):
    b, t, width = q.shape
    nk, tk = vt.shape[1], vt.shape[3]
    tq = 2 * tk
    assert t % tq == 0
    groups = width // LANES
    d_aug = (2 * A_HEAD_DIM if mode == "A" else C_HEAD_DIM) + BF16_ROWS
    kspec = pl.BlockSpec((1, nk, tk, LANES), lambda bi, g, i, f: (bi, 0, 0, g))
    in_specs = [pl.BlockSpec((1, tq, LANES), lambda bi, g, i, f: (bi, i, g)), kspec]
    args = [q, k.reshape(b, nk, tk, width)]
    if mode == "C":
        in_specs.append(kspec)
        args.append(bias.reshape(b, nk, tk, groups * LANES))
    in_specs.append(pl.BlockSpec((1, nk, LANES, tk), lambda bi, g, i, f: (bi, 0, g, 0)))
    args.append(vt)
    if mode == "A":
        in_specs += [pl.BlockSpec((4, A_HEAD_DIM), lambda bi, g, i, f: (0, 0)),
                     pl.BlockSpec((1, LANES, 1), lambda bi, g, i, f: (g, 0, 0))]
        args += [lam, gain]
    return pl.pallas_call(
        functools.partial(_attn_kernel, mode=mode, lam_init=lam_init),
        grid_spec=pltpu.PrefetchScalarGridSpec(
            num_scalar_prefetch=1,
            grid=(b, groups, t // tq),
            in_specs=in_specs,
            out_specs=pl.BlockSpec((1, tq, LANES), lambda bi, g, i, f: (bi, i, g)),
            scratch_shapes=[pltpu.VMEM((2, tq, LANES if mode == "A" else 2 * LANES), BF16),
                            pltpu.VMEM((2, 1, tq), F32),
                            pltpu.VMEM((2, d_aug, tq), F32),
                            pltpu.VMEM((2 * tq // MXU_DIM, tk, MXU_DIM), F32),
                            pltpu.VMEM((2 * tq // MXU_DIM, tk, MXU_DIM), F32),
                            pltpu.VMEM((2 * tq // MXU_DIM, 1, MXU_DIM), F32),
                            pltpu.VMEM((2 * tq // MXU_DIM, 1, MXU_DIM), F32)]),
        out_shape=jax.ShapeDtypeStruct((b, t, width), BF16),
        compiler_params=_params(("arbitrary", "arbitrary", "arbitrary")),
        name="attn_" + mode,
    )(first, *args)


def _first_visible_block(bounds, tq_tiles):
    qn, kn, bmax, bmin = (bounds[:, :, r, :C_HEADS] for r in range(4))
    b, tiles, _ = qn.shape
    nq = tiles // tq_tiles
    blk = lambda x, f: f(x.reshape(b, nq, tq_tiles, C_HEADS), axis=2)
    qn_q = blk(qn, jnp.max) * NORM_SLACK
    own = blk(bmin, jnp.min) - qn_q * blk(kn, jnp.max) * NORM_SLACK
    best = qn_q[:, :, None, :] * (kn * NORM_SLACK)[:, None, :, :] + bmax[:, None, :, :]
    dead = best < (own[:, :, None, :] - SKIP_GAP)
    block = jnp.arange(tiles, dtype=jnp.int32)[None, None, :, None]
    lead = jnp.min(jnp.where(dead, tiles, block), axis=2)
    lead = jnp.min(lead.reshape(b, nq, C_HEADS // 2, 2), axis=-1)
    lead = jnp.minimum(lead, (jnp.arange(nq, dtype=jnp.int32) * tq_tiles)[None, :, None])
    return jnp.transpose(lead - lead % 2, (0, 2, 1)).astype(jnp.int32)


def _diff_finish(o0, o1, lam, gain, lam_init):
    o = o0 - lam * o1
    o = o * lax.rsqrt(jnp.mean(o * o, axis=-1, keepdims=True) + LN_EPS)
    return o * gain * (1.0 - lam_init)


def _cached_attn_kernel(*refs, mode, past, lam_init):
    if mode == "A":
        q_ref, kn_ref, vn_ref, ck_ref, cv_ref, lam_ref, gain_ref, o_ref = refs
    else:
        q_ref, kn_ref, vn_ref, ck_ref, cv_ref, clf_ref, nlf_ref, o_ref = refs
    t = q_ref.shape[1]
    width = o_ref.shape[2]
    groups = width // LANES
    lane = lax.broadcasted_iota(jnp.int32, (1, LANES), 1)
    low = lane < (LANES // 2)
    row = lax.broadcasted_iota(jnp.int32, (t, t), 0)
    col = lax.broadcasted_iota(jnp.int32, (t, t), 1)
    if mode == "A":
        keep = ((past + col) >> CHUNK_SHIFT) <= ((past + row) >> CHUNK_SHIFT)
        lam = _lambda(lam_ref, lam_init)
    else:
        keep = col <= row
        r = lax.broadcasted_iota(jnp.int32, (past, past), 0)
        c = lax.broadcasted_iota(jnp.int32, (past, past), 1)
        upper = jnp.where(r <= c, 1.0, 0.0).astype(BF16)
        clf = clf_ref[0]
        hi, mid, lo = _split3(clf)
        cum_c = (_dot(hi, upper) + _dot(mid, upper)) + _dot(lo, upper)
        upper_n = jnp.where(row <= col, 1.0, 0.0).astype(BF16)
        hi, mid, lo = _split3(nlf_ref[0])
        cum_n = ((_dot(hi, upper_n) + _dot(mid, upper_n)) + _dot(lo, upper_n)
                 + jnp.sum(clf, axis=1, keepdims=True))
        bias_c = cum_c * (-LOG2E)
        bias_n = cum_n * (-LOG2E)
    keep2 = jnp.concatenate([keep, keep], axis=0)
    first_rows = lax.broadcasted_iota(jnp.int32, (2 * t, 1), 0) < t
    for g in range(groups):
        sl = slice(g * LANES, (g + 1) * LANES)
        q = q_ref[0, :, sl]
        kn = kn_ref[0, :, sl]
        vn = vn_ref[0, :, sl].astype(BF16)
        if mode == "A":
            kc = ck_ref[0, 0, :, g, :].astype(BF16)
            vc = cv_ref[0, 0, :, g, :].astype(BF16)
        else:
            kc = ck_ref[0, :, sl].astype(BF16)
            vc = cv_ref[0, :, sl].astype(BF16)
        zero = jnp.zeros_like(q)
        q2 = jnp.concatenate([jnp.where(low, q, zero), jnp.where(low, zero, q)], axis=0)
        s_c = _dot_nt(q2, kc)
        s_n = _dot_nt(q2, kn)
        if mode == "C":
            h = 2 * g
            s_c = s_c + jnp.where(first_rows, bias_c[h:h + 1, :], bias_c[h + 1:h + 2, :])
            s_n = s_n + jnp.where(first_rows, bias_n[h:h + 1, :], bias_n[h + 1:h + 2, :])
        s_n = jnp.where(keep2, s_n, NEG)
        m = jnp.maximum(jnp.max(s_c, axis=1, keepdims=True), jnp.max(s_n, axis=1, keepdims=True))
        p_c = jnp.exp2(s_c - m)
        p_n = jnp.exp2(s_n - m)
        l = jnp.sum(p_c, axis=1, keepdims=True) + jnp.sum(p_n, axis=1, keepdims=True)
        acc = (_dot(p_c.astype(BF16), vc) + _dot(p_n.astype(BF16), vn)) * (1.0 / l)
        outs = [acc[:t], acc[t:]]
        if mode == "A":
            o = _diff_finish(outs[0], outs[1], lam, gain_ref[g], lam_init)
        else:
            o = jnp.where(low, outs[0], outs[1])
        o_ref[0, :, sl] = o.astype(o_ref.dtype)


def _cached_attention(q, kn, vn, cache_k, cache_v, mode, lam=None, gain=None, cache_lft=None, new_lft=None,
                      lam_init=0.0):
    b, t, width = q.shape
    new = pl.BlockSpec((1, t, width), lambda bi: (bi, 0, 0))
    if mode == "A":
        _, _, past, heads, dh = cache_k.shape
        old = pl.BlockSpec((1, 1, past, heads, dh), lambda bi: (0, bi, 0, 0, 0))
    else:
        past = cache_k.shape[1]
        old = pl.BlockSpec((1, past, width), lambda bi: (bi, 0, 0))
    in_specs = [new, new, new, old, old]
    args = [q, kn, vn, cache_k, cache_v]
    if mode == "A":
        in_specs += [pl.BlockSpec((4, A_HEAD_DIM), lambda bi: (0, 0)),
                     pl.BlockSpec((A_HEADS, 1, LANES), lambda bi: (0, 0, 0))]
        args += [lam, gain]
    else:
        in_specs += [pl.BlockSpec((1, C_HEADS, past), lambda bi: (bi, 0, 0)),
                     pl.BlockSpec((1, C_HEADS, t), lambda bi: (bi, 0, 0))]
        args += [cache_lft, new_lft]
    return pl.pallas_call(
        functools.partial(_cached_attn_kernel, mode=mode, past=past, lam_init=lam_init),
        grid=(b,),
        in_specs=in_specs,
        out_specs=new,
        out_shape=jax.ShapeDtypeStruct((b, t, width), BF16),
        compiler_params=_params(("arbitrary",)),
        name="cached_attn_" + mode,
    )(*args)


def _gelu_tanh(x):
    return 0.5 * x * (1.0 + jnp.tanh(math.sqrt(2.0 / math.pi) * (x + 0.044715 * (x * x * x))))


def _causal_conv(x, tail, w_ref, b_ref):
    width = w_ref.shape[0]
    tm = x.shape[0]
    cat = jnp.concatenate([tail, x], axis=0)
    y = None
    for j in range(width):
        back = width - 1 - j
        src = cat if back == 0 else pltpu.roll(cat, back, 0)
        term = src[SUBLANES:SUBLANES + tm] * w_ref[j:j + 1, :]
        y = term if y is None else y + term
    return y + b_ref[...]


def _rglru_kernel(xg_ref, h0_ref, cb_ref, cw_ref, cbias_ref, wa_ref, ba_ref, wx_ref, bx_ref, sp_ref,
                  y_ref, hl_ref, ct_ref, hc_ref, tail_ref):
    i = pl.program_id(1)
    tm = xg_ref.shape[1]

    @pl.when(i == 0)
    def _():
        hc_ref[...] = h0_ref[0]
        tail_ref[...] = cb_ref[0]

    x = xg_ref[0, :, :B_WIDTH]
    gate_in = xg_ref[0, :, B_WIDTH:]
    xc = _causal_conv(x, tail_ref[...], cw_ref, cbias_ref)
    xcb = xc.astype(BF16)
    r = _sigmoid(_dot(xcb, wa_ref[...]) + ba_ref[...])
    ig = _sigmoid(_dot(xcb, wx_ref[...]) + bx_ref[...])
    log_a = (-RG_C) * r * sp_ref[...]
    a = jnp.exp(log_a)
    th = jnp.tanh(log_a)
    bx = jnp.sqrt((-2.0 * th) / (1.0 - th)) * (ig * xc)
    pos = lax.broadcasted_iota(jnp.int32, (tm, 1), 0) & (SUBLANES - 1)
    s = 1
    while s < SUBLANES:
        valid = pos >= s
        a_sh = pltpu.roll(a, s, 0)
        b_sh = pltpu.roll(bx, s, 0)
        bx = jnp.where(valid, a * b_sh + bx, bx)
        a = jnp.where(valid, a * a_sh, a)
        s *= 2
    carry = hc_ref[...]
    groups = []
    for g in range(tm // SUBLANES):
        rows = slice(g * SUBLANES, (g + 1) * SUBLANES)
        hg = a[rows] * carry + bx[rows]
        groups.append(hg)
        carry = hg[SUBLANES - 1:, :]
    h = jnp.concatenate(groups, axis=0)
    y_ref[0] = (h * _gelu_tanh(gate_in)).astype(y_ref.dtype)
    h_tail = h[tm - SUBLANES:, :]
    x_tail = x[tm - SUBLANES:, :]
    hc_ref[...] = h_tail[SUBLANES - 1:, :]
    tail_ref[...] = x_tail
    hl_ref[0] = h_tail
    ct_ref[0] = x_tail


def _rglru(xg, h0, conv_tail, cw, cbias, wa, ba, wx, bx, sp, tm_pref=256):
    b, t, _ = xg.shape
    tm = _row_tile(t, tm_pref)
    wspec = lambda shape: pl.BlockSpec(shape, lambda bi, i: tuple(0 for _ in shape))
    return pl.pallas_call(
        _rglru_kernel,
        grid=(b, t // tm),
        in_specs=[pl.BlockSpec((1, tm, 2 * B_WIDTH), lambda bi, i: (bi, i, 0)),
                  pl.BlockSpec((1, 1, B_WIDTH), lambda bi, i: (bi, 0, 0)),
                  pl.BlockSpec((1, SUBLANES, B_WIDTH), lambda bi, i: (bi, 0, 0)),
                  wspec((B_CONV, B_WIDTH)), wspec((1, B_WIDTH)),
                  wspec((B_WIDTH, B_WIDTH)), wspec((1, B_WIDTH)),
                  wspec((B_WIDTH, B_WIDTH)), wspec((1, B_WIDTH)), wspec((1, B_WIDTH))],
        out_specs=[pl.BlockSpec((1, tm, B_WIDTH), lambda bi, i: (bi, i, 0)),
                   pl.BlockSpec((1, SUBLANES, B_WIDTH), lambda bi, i: (bi, 0, 0)),
                   pl.BlockSpec((1, SUBLANES, B_WIDTH), lambda bi, i: (bi, 0, 0))],
        out_shape=[jax.ShapeDtypeStruct((b, t, B_WIDTH), BF16),
                   jax.ShapeDtypeStruct((b, SUBLANES, B_WIDTH), F32),
                   jax.ShapeDtypeStruct((b, SUBLANES, B_WIDTH), F32)],
        scratch_shapes=[pltpu.VMEM((1, B_WIDTH), F32), pltpu.VMEM((SUBLANES, B_WIDTH), F32)],
        compiler_params=_params(("arbitrary", "arbitrary")),
        name="rglru",
    )(xg, h0, conv_tail, cw, cbias, wa, ba, wx, bx, sp)


def _proj_ln_kernel(*refs, n_in):
    h_refs = refs[:n_in]
    w_ref, x_ref, gate_ref, g_ref, b_ref, o_ref = refs[n_in:]
    tm = x_ref.shape[1]
    rows = [slice(r, min(r + LN_ROWS, tm)) for r in range(0, tm, LN_ROWS)]

    def matmul(rs):
        hs = [r[0, rs, :] for r in h_refs]
        return _dot(hs[0] if n_in == 1 else jnp.concatenate(hs, axis=-1), w_ref[...])

    def norm(rs, proj):
        y = ALPHA * x_ref[0, rs, :] + gate_ref[0] * proj
        mu = jnp.mean(y, axis=-1, keepdims=True)
        yc = y - mu
        var = jnp.mean(yc * yc, axis=-1, keepdims=True)
        o_ref[0, rs, :] = yc * lax.rsqrt(var + LN_EPS) * g_ref[...] + b_ref[...]

    pending = matmul(rows[0])
    for r in range(len(rows)):
        nxt = matmul(rows[r + 1]) if r + 1 < len(rows) else None
        norm(rows[r], pending)
        pending = nxt


def _proj_ln(hs, w, x, gate, ln_g, ln_b, tm_pref=1024):
    b, t, d = x.shape
    tm = _row_tile(t, tm_pref)
    k = w.shape[0]
    gate_spec = (pl.BlockSpec((1, 1, d), lambda bi, i: (bi, 0, 0)) if gate.shape[1] == 1
                 else pl.BlockSpec((1, tm, d), lambda bi, i: (bi, i, 0)))
    in_specs = [pl.BlockSpec((1, tm, h.shape[2]), lambda bi, i: (bi, i, 0)) for h in hs]
    in_specs += [pl.BlockSpec((k, d), lambda bi, i: (0, 0)),
                 pl.BlockSpec((1, tm, d), lambda bi, i: (bi, i, 0)),
                 gate_spec,
                 pl.BlockSpec((1, d), lambda bi, i: (0, 0)),
                 pl.BlockSpec((1, d), lambda bi, i: (0, 0))]
    return pl.pallas_call(
        functools.partial(_proj_ln_kernel, n_in=len(hs)),
        grid=(b, t // tm),
        in_specs=in_specs,
        out_specs=pl.BlockSpec((1, tm, d), lambda bi, i: (bi, i, 0)),
        out_shape=jax.ShapeDtypeStruct((b, t, d), F32),
        compiler_params=_params(("arbitrary", "arbitrary")),
        name="proj_ln",
    )(*hs, w, x, gate, ln_g, ln_b)


def _ffn_up_kernel(x_ref, sc_ref, sh_ref, w_ref, cb_ref, cw_ref, cbias_ref, h_ref, ct_ref, tail_ref):
    i = pl.program_id(1)
    tm = x_ref.shape[1]

    @pl.when(i == 0)
    def _():
        tail_ref[...] = cb_ref[0]

    u = _modulate(x_ref[0], sc_ref[0], sh_ref[0])
    chunks = [slice(c, min(c + FFN_CHUNK, D_FF)) for c in range(0, D_FF, FFN_CHUNK)]

    def matmuls(cs):
        gs = slice(D_FF + cs.start, D_FF + cs.stop)
        return _dot(u, w_ref[:, cs]), _dot(u, w_ref[:, gs])

    def gate(cs, a, g):
        gc = _causal_conv(g, tail_ref[:, cs], cw_ref.at[:, cs], cbias_ref.at[:, cs])
        silu = gc * (0.5 * jnp.tanh(0.5 * gc) + 0.5)
        h_ref[0, :, cs] = (a * silu).astype(h_ref.dtype)
        g_tail = g[tm - SUBLANES:, :]
        tail_ref[:, cs] = g_tail
        ct_ref[0, :, cs] = g_tail

    pending = matmuls(chunks[0])
    for c in range(len(chunks)):
        nxt = matmuls(chunks[c + 1]) if c + 1 < len(chunks) else None
        gate(chunks[c], *pending)
        pending = nxt


def _ffn_up(x, sc, sh, w, conv_tail, cw, cbias, tm_pref=256):
    b, t, d = x.shape
    tm = _row_tile(t, tm_pref)
    return pl.pallas_call(
        _ffn_up_kernel,
        grid=(b, t // tm),
        in_specs=[pl.BlockSpec((1, tm, d), lambda bi, i: (bi, i, 0)),
                  pl.BlockSpec((1, 1, d), lambda bi, i: (bi, 0, 0)),
                  pl.BlockSpec((1, 1, d), lambda bi, i: (bi, 0, 0)),
                  pl.BlockSpec((d, 2 * D_FF), lambda bi, i: (0, 0)),
                  pl.BlockSpec((1, SUBLANES, D_FF), lambda bi, i: (bi, 0, 0)),
                  pl.BlockSpec((FFN_CONV, D_FF), lambda bi, i: (0, 0)),
                  pl.BlockSpec((1, D_FF), lambda bi, i: (0, 0))],
        out_specs=[pl.BlockSpec((1, tm, D_FF), lambda bi, i: (bi, i, 0)),
                   pl.BlockSpec((1, SUBLANES, D_FF), lambda bi, i: (bi, 0, 0))],
        out_shape=[jax.ShapeDtypeStruct((b, t, D_FF), BF16),
                   jax.ShapeDtypeStruct((b, SUBLANES, D_FF), F32)],
        scratch_shapes=[pltpu.VMEM((SUBLANES, D_FF), F32)],
        compiler_params=_params(("arbitrary", "arbitrary")),
        name="ffn_up",
    )(x, sc, sh, w, conv_tail, cw, cbias)


def _pad_tail(buf):
    return jnp.pad(buf, ((0, 0), (SUBLANES - buf.shape[1], 0), (0, 0)))


def _rope_tables(past, t):
    half = A_HEAD_DIM // 2
    inv = ROPE_THETA ** (-jnp.arange(0, A_HEAD_DIM, 2, dtype=F32) / A_HEAD_DIM)
    pos = (past + jnp.arange(t, dtype=jnp.int32)).astype(F32)
    ang = pos[:, None] * inv[None, :]
    cos = jnp.tile(jnp.cos(ang), (1, LANES // half))
    sin = jnp.sin(ang)
    sin_signed = jnp.tile(jnp.concatenate([-sin, sin], axis=1), (1, LANES // A_HEAD_DIM))
    return cos, sin_signed


def _block_diag(w):
    n, i, o = w.shape
    return jnp.einsum("nio,nm->nimo", w, jnp.eye(n, dtype=w.dtype)).reshape(n * i, n * o)


def _prepare(p):
    w = {}
    w_in_ab = p["w_in_ab"][0]
    w["in_ab"] = w_in_ab.astype(BF16)
    w["out_ab"] = p["w_out_ab"][0].astype(BF16)
    w["lam"] = jnp.stack([p["lam_q1"][0], p["lam_k1"][0], p["lam_q2"][0], p["lam_k2"][0]])
    w["gain_row"] = p["attn_gain"][0].reshape(A_HEADS, 1, LANES)
    w["gain_col"] = p["attn_gain"][0].reshape(A_HEADS, LANES, 1)
    w["b_conv_w"] = p["b_conv_w"][0]
    w["b_conv_b"] = p["b_conv_b"][0].reshape(1, B_WIDTH)
    w["rg_a"] = _block_diag(p["w_rg_a"][0]).astype(BF16)
    w["rg_x"] = _block_diag(p["w_rg_x"][0]).astype(BF16)
    w["b_rg_a"] = p["b_rg_a"][0].reshape(1, B_WIDTH)
    w["b_rg_x"] = p["b_rg_x"][0].reshape(1, B_WIDTH)
    w["rg_L"] = p["rg_L"][0].reshape(1, B_WIDTH)
    w_in_c = p["w_in_c"][0]
    w["in_c"] = jnp.pad(w_in_c, ((0, 0), (0, LANES - C_HEADS))).astype(BF16)
    w["in_c_ft"] = w_in_c[:, 3 * C_WIDTH:].T.astype(BF16)
    w["bf_row"] = jnp.pad(p["b_f"][0], (0, LANES - C_HEADS)).reshape(1, LANES)
    w["bf_col"] = p["b_f"][0].reshape(C_HEADS, 1)
    w["place"] = _bias_placement()
    w["seg"] = jnp.asarray(np.repeat(np.eye(LANES, dtype=np.float32)[:C_HEADS], C_HEAD_DIM, axis=0), BF16)
    w["out_c"] = p["w_out_c"][0].astype(BF16)
    w["up"] = [p["w_up"][i].astype(BF16) for i in range(DEPTH)]
    w["down"] = [p["w_down"][i].astype(BF16) for i in range(DEPTH)]
    w["ffn_conv_w"] = [p["ffn_conv_w"][i] for i in range(DEPTH)]
    w["ffn_conv_b"] = [p["ffn_conv_b"][i].reshape(1, D_FF) for i in range(DEPTH)]
    w["ln1_g"] = [p["ln1_g"][i].reshape(1, D_MODEL) for i in range(DEPTH)]
    w["ln1_b"] = [p["ln1_b"][i].reshape(1, D_MODEL) for i in range(DEPTH)]
    w["ln2_g"] = [p["ln2_g"][i].reshape(1, D_MODEL) for i in range(DEPTH)]
    w["ln2_b"] = [p["ln2_b"][i].reshape(1, D_MODEL) for i in range(DEPTH)]
    return w


def _softplus_kernel(x_ref, o_ref):
    o_ref[...] = _softplus(-x_ref[...])


def _proj_ln_rows(hs, w, x, gate, ln_g, ln_b):
    b, t, d = x.shape
    flat = lambda a: a.reshape(1, b * t, a.shape[2])
    gate_rows = jnp.broadcast_to(gate, (b, t, d))
    return _proj_ln([flat(h) for h in hs], w, flat(x), flat(gate_rows), ln_g, ln_b).reshape(b, t, d)


def _trunk(x, mods, w, sp, cache_a_k=None, cache_a_v=None, state_b_h=None, state_b_conv=None,
           cache_c_k=None, cache_c_v=None, cache_c_logf=None, state_ffn_conv=None):
    b, t, d = x.shape
    cached = cache_a_k is not None
    proj_ln = _proj_ln_rows if cached else _proj_ln
    past = cache_a_k.shape[2] if cached else 0
    outs = {}
    for i in range(DEPTH):
        sh1, sc1, g1, sh2, sc2, g2 = [m[:, None, :] for m in jnp.split(mods[i], 6, axis=-1)]
        if i % 2 == 0:
            lam_init = 0.8 - 0.6 * math.exp(-0.3 * i)
            cos, sin = _rope_tables(past, t)
            q, kb, vt, k32, v32, xg = _proj_ab(x, sc1, sh1, w["in_ab"], cos, sin)
            if cached:
                o = _cached_attention(q, kb, v32.reshape(b, t, A_WIDTH), cache_a_k, cache_a_v,
                                      "A", lam=w["lam"], gain=w["gain_row"], lam_init=lam_init)
                h0 = state_b_h[0][:, None, :]
                ctail = _pad_tail(state_b_conv[0])
            else:
                first = jnp.zeros((b, A_HEADS, t // (2 * ATTN_BLOCK)), jnp.int32)
                o = _attention(q, kb, vt, first, "A", lam=w["lam"], gain=w["gain_col"], lam_init=lam_init)
                h0 = jnp.zeros((b, 1, B_WIDTH), F32)
                ctail = jnp.zeros((b, SUBLANES, B_WIDTH), F32)
            yb, h_tail, x_tail = _rglru(xg, h0, ctail, w["b_conv_w"], w["b_conv_b"], w["rg_a"], w["b_rg_a"],
                                        w["rg_x"], w["b_rg_x"], sp)
            outs["a_k"] = k32[None]
            outs["a_v"] = v32[None]
            outs["b_h"] = h_tail[:, SUBLANES - 1, :][None]
            outs["b_conv"] = x_tail[:, SUBLANES - (B_CONV - 1):, :][None]
            x = proj_ln([o, yb], w["out_ab"], x, g1, w["ln1_g"][i], w["ln1_b"][i])
        else:
            q, kb, bias, vt, k32, v32, lf, lft, bounds = _proj_c(x, sc1, sh1, w["in_c"], w["in_c_ft"],
                                                                 w["bf_row"], w["bf_col"], w["place"], w["seg"])
            if cached:
                o = _cached_attention(q, kb, v32, cache_c_k[0].reshape(b, past, C_WIDTH),
                                      cache_c_v[0].reshape(b, past, C_WIDTH),
                                      "C", cache_lft=jnp.swapaxes(cache_c_logf[0], 1, 2), new_lft=lft)
            else:
                o = _attention(q, kb, vt, _first_visible_block(bounds, 2), "C", bias=bias)
            outs["c_k"] = k32.reshape(1, b, t, C_HEADS, C_HEAD_DIM)
            outs["c_v"] = v32.reshape(1, b, t, C_HEADS, C_HEAD_DIM)
            outs["c_logf"] = lf[None]
            x = proj_ln([o], w["out_c"], x, g1, w["ln1_g"][i], w["ln1_b"][i])
        ftail = _pad_tail(state_ffn_conv[i]) if cached else jnp.zeros((b, SUBLANES, D_FF), F32)
        hmid, g_tail = _ffn_up(x, sc2, sh2, w["up"][i], ftail, w["ffn_conv_w"][i], w["ffn_conv_b"][i])
        outs.setdefault("ffn", []).append(g_tail[:, SUBLANES - (FFN_CONV - 1):, :])
        x = proj_ln([hmid], w["down"][i], x, g2, w["ln2_g"][i], w["ln2_b"][i])
    return (x, outs["a_k"], outs["a_v"], outs["b_h"], outs["b_conv"],
            outs["c_k"], outs["c_v"], outs["c_logf"], jnp.stack(outs["ffn"]))


def kernel(x_prompt, x_sample, c_prompt, c_sample, cache_a_k, cache_a_v, state_b_h, state_b_conv, cache_c_k, cache_c_v, cache_c_logf, state_ffn_conv, w_ada, b_ada, ln1_g, ln1_b, ln2_g, ln2_b, w_in_ab, lam_q1, lam_k1, lam_q2, lam_k2, attn_gain, b_conv_w, b_conv_b, w_rg_a, b_rg_a, w_rg_x, b_rg_x, rg_L, w_out_ab, w_in_c, b_f, w_out_c, w_up, ffn_conv_w, ffn_conv_b, w_down):
    p = dict(w_in_ab=w_in_ab, lam_q1=lam_q1, lam_k1=lam_k1, lam_q2=lam_q2, lam_k2=lam_k2, attn_gain=attn_gain,
             b_conv_w=b_conv_w, b_conv_b=b_conv_b, w_rg_a=w_rg_a, b_rg_a=b_rg_a, w_rg_x=w_rg_x, b_rg_x=b_rg_x,
             rg_L=rg_L, w_out_ab=w_out_ab, w_in_c=w_in_c, b_f=b_f, w_out_c=w_out_c, w_up=w_up,
             ffn_conv_w=ffn_conv_w, ffn_conv_b=ffn_conv_b, w_down=w_down,
             ln1_g=ln1_g, ln1_b=ln1_b, ln2_g=ln2_g, ln2_b=ln2_b)
    w = _prepare(p)
    bp = c_prompt.shape[0]
    bs = c_sample.shape[0]
    rows = -(-(bp + bs) // 16) * 16
    c_all = jnp.pad(jnp.concatenate([c_prompt, c_sample], axis=0), ((0, rows - bp - bs), (0, 0)))
    mods = _mods(c_all, w_ada, b_ada)
    sp = pl.pallas_call(_softplus_kernel, out_shape=jax.ShapeDtypeStruct((1, B_WIDTH), F32),
                        name="softplus")(w["rg_L"])
    res_p = _trunk(x_prompt, mods[:, :bp], w, sp)
    res_s = _trunk(x_sample, mods[:, bp:bp + bs], w, sp, cache_a_k, cache_a_v, state_b_h, state_b_conv,
                   cache_c_k, cache_c_v, cache_c_logf, state_ffn_conv)
    return (res_p[0], res_s[0]) + res_p[1:] + res_s[1:]
```

```python
import functools
import math

import numpy as np
import jax
import jax.numpy as jnp
from jax import lax
from jax.experimental import pallas as pl
from jax.experimental.pallas import tpu as pltpu

F32 = jnp.float32
BF16 = jnp.bfloat16

D_MODEL = 1024
DEPTH = 2
CHUNK = 64
CHUNK_SHIFT = 6
A_HEADS = 4
A_HEAD_DIM = 64
A_WIDTH = A_HEADS * 2 * A_HEAD_DIM
B_WIDTH = 512
B_BLOCKS = 8
B_CONV = 4
RG_C = 8.0
C_HEADS = 16
C_HEAD_DIM = 64
C_WIDTH = C_HEADS * C_HEAD_DIM
D_FF = 2816
FFN_CONV = 3
ROPE_THETA = 10000.0
ALPHA = (2 * DEPTH) ** 0.25
LN_EPS = 1e-5
NEG = -1e30
LOG2E = 1.4426950408889634

LANES = 128
SUBLANES = 8
BF16_ROWS = 16
MXU_DIM = 256
VMEM_LIMIT = 56 * 1024 * 1024
BIAS_WIDTH = (C_HEADS // 2) * LANES
FFN_CHUNK = 256
LN_ROWS = 256
SKIP_GAP = 160.0
NORM_SLACK = 1.03
ATTN_BLOCK = 512


def _params(sem, flags=None):
    return pltpu.CompilerParams(dimension_semantics=sem, vmem_limit_bytes=VMEM_LIMIT, flags=flags)


def _row_tile(t, pref):
    if t <= pref:
        return t
    tm = pref
    while t % tm:
        tm //= 2
    return tm


def _modulate(x, sc, sh):
    return (x * (1.0 + sc) + sh).astype(BF16)


def _sigmoid(x):
    return 1.0 / (1.0 + jnp.exp(-x))


def _softplus(x):
    return jnp.maximum(x, 0.0) + jnp.log1p(jnp.exp(-jnp.abs(x)))


def _log_sigmoid(x):
    return jnp.minimum(x, 0.0) - jnp.log1p(jnp.exp(-jnp.abs(x)))


def _split3(x):
    hi = x.astype(BF16)
    r1 = x - hi.astype(F32)
    mid = r1.astype(BF16)
    lo = (r1 - mid.astype(F32)).astype(BF16)
    return hi, mid, lo


def _dot(a, b):
    return jnp.dot(a, b, preferred_element_type=F32)


def _dot_nt(a, b):
    return lax.dot_general(a, b, (((1,), (1,)), ((), ())), preferred_element_type=F32)


def _mods_kernel(c_ref, w_ref, b_ref, o_ref):
    c = c_ref[...]
    s = (c * _sigmoid(c)).astype(BF16)
    o_ref[0] = _dot(s, w_ref[0].astype(BF16)) + b_ref[0]


def _mods(c_all, w_ada, b_ada):
    rows, d = c_all.shape
    n = w_ada.shape[-1]
    tn = 1536
    return pl.pallas_call(
        _mods_kernel,
        grid=(DEPTH, n // tn),
        in_specs=[pl.BlockSpec((rows, d), lambda l, j: (0, 0)),
                  pl.BlockSpec((1, d, tn), lambda l, j: (l, 0, j)),
                  pl.BlockSpec((1, 1, tn), lambda l, j: (l, 0, j))],
        out_specs=pl.BlockSpec((1, rows, tn), lambda l, j: (l, 0, j)),
        out_shape=jax.ShapeDtypeStruct((DEPTH, rows, n), F32),
        compiler_params=_params(("arbitrary", "arbitrary")),
        name="mods",
    )(c_all, w_ada, b_ada.reshape(DEPTH, 1, n))


def _rope_slab(x, cos, sin_signed, first_half):
    fwd = pltpu.roll(x, LANES - A_HEAD_DIM // 2, 1)
    bwd = pltpu.roll(x, A_HEAD_DIM // 2, 1)
    partner = jnp.where(first_half, fwd, bwd)
    return x * cos + partner * sin_signed


def _proj_ab_kernel(x_ref, sc_ref, sh_ref, w_ref, cos_ref, sin_ref,
                    q_ref, kb_ref, vt_ref, k_ref, v_ref, xg_ref, *, q_scale):
    u = _modulate(x_ref[0], sc_ref[0], sh_ref[0])
    pr = _dot(u, w_ref[...])
    cos = cos_ref[...]
    sin = sin_ref[...]
    lane = lax.broadcasted_iota(jnp.int32, (1, LANES), 1)
    first_half = (lane & (A_HEAD_DIM - 1)) < (A_HEAD_DIM // 2)
    for h in range(A_HEADS):
        sl = slice(h * LANES, (h + 1) * LANES)
        q = _rope_slab(pr[:, sl], cos, sin, first_half)
        q_ref[0, :, sl] = (q * q_scale).astype(BF16)
        k = _rope_slab(pr[:, A_WIDTH + h * LANES:A_WIDTH + (h + 1) * LANES], cos, sin, first_half)
        k_ref[0, :, h, :] = k
        kb_ref[0, :, sl] = k.astype(BF16)
    for h in range(A_HEADS):
        v_ref[0, :, h, :] = pr[:, 2 * A_WIDTH + h * LANES:2 * A_WIDTH + (h + 1) * LANES]
    vt_ref[0, 0] = pr[:, 2 * A_WIDTH:3 * A_WIDTH].T.astype(BF16)
    xg_ref[0] = pr[:, 3 * A_WIDTH:]


def _proj_ab(x, sc, sh, w, cos, sin):
    b, t, d = x.shape
    tm = _row_tile(t, ATTN_BLOCK)
    nt = t // tm
    n = w.shape[1]
    q_scale = A_HEAD_DIM ** -0.5 * LOG2E
    row = lambda width: pl.BlockSpec((1, tm, width), lambda bi, i: (bi, i, 0))
    const = lambda shape: pl.BlockSpec(shape, lambda bi, i: tuple(0 for _ in shape))
    heads = pl.BlockSpec((1, tm, A_HEADS, LANES), lambda bi, i: (bi, i, 0, 0))
    return pl.pallas_call(
        functools.partial(_proj_ab_kernel, q_scale=q_scale),
        grid=(b, nt),
        in_specs=[row(d),
                  pl.BlockSpec((1, 1, d), lambda bi, i: (bi, 0, 0)),
                  pl.BlockSpec((1, 1, d), lambda bi, i: (bi, 0, 0)),
                  const((d, n)),
                  pl.BlockSpec((tm, LANES), lambda bi, i: (i, 0)),
                  pl.BlockSpec((tm, LANES), lambda bi, i: (i, 0))],
        out_specs=[row(A_WIDTH), row(A_WIDTH),
                   pl.BlockSpec((1, 1, A_WIDTH, tm), lambda bi, i: (bi, i, 0, 0)),
                   heads, heads, row(2 * B_WIDTH)],
        out_shape=[jax.ShapeDtypeStruct((b, t, A_WIDTH), BF16),
                   jax.ShapeDtypeStruct((b, t, A_WIDTH), BF16),
                   jax.ShapeDtypeStruct((b, nt, A_WIDTH, tm), BF16),
                   jax.ShapeDtypeStruct((b, t, A_HEADS, LANES), F32),
                   jax.ShapeDtypeStruct((b, t, A_HEADS, LANES), F32),
                   jax.ShapeDtypeStruct((b, t, 2 * B_WIDTH), F32)],
        compiler_params=_params(("arbitrary", "arbitrary")),
        name="proj_ab",
    )(x, sc, sh, w, cos, sin)


def _bias_placement():
    e = np.zeros((3 * LANES, BIAS_WIDTH), np.float32)
    for piece in range(3):
        for h in range(C_HEADS):
            e[piece * LANES + h, (h // 2) * LANES + 3 * (h % 2) + piece] = 1.0
    return jnp.asarray(e, BF16)


def _head_norm_max(x, seg):
    n2 = _dot((x * x).astype(BF16), seg)
    return jnp.sqrt(jnp.max(n2, axis=0, keepdims=True))


def _proj_c_kernel(x_ref, sc_ref, sh_ref, w_ref, wft_ref, bfr_ref, bfc_ref, place_ref, seg_ref,
                   q_ref, kb_ref, bias_ref, vt_ref, k_ref, v_ref, lf_ref, lft_ref, bound_ref, run_ref, *, q_scale):
    i = pl.program_id(1)
    tm = x_ref.shape[1]

    @pl.when(i == 0)
    def _():
        run_ref[...] = jnp.zeros_like(run_ref)

    u = _modulate(x_ref[0], sc_ref[0], sh_ref[0])
    pr = _dot(u, w_ref[...])
    qs = pr[:, :C_WIDTH] * q_scale
    q_ref[0] = qs.astype(BF16)
    k = pr[:, C_WIDTH:2 * C_WIDTH]
    k_ref[0] = k
    kb_ref[0] = k.astype(BF16)
    v = pr[:, 2 * C_WIDTH:3 * C_WIDTH]
    v_ref[0] = v
    vt_ref[0, 0] = v.T.astype(BF16)
    lf = _log_sigmoid(pr[:, 3 * C_WIDTH:] + bfr_ref[...])
    lf_ref[0] = lf[:, :C_HEADS]
    lft_ref[0] = _log_sigmoid(_dot_nt(wft_ref[...], u) + bfc_ref[...])
    r = lax.broadcasted_iota(jnp.int32, (tm, tm), 0)
    c = lax.broadcasted_iota(jnp.int32, (tm, tm), 1)
    lower = jnp.where(c <= r, 1.0, 0.0).astype(BF16)
    hi, mid, lo = _split3(lf)
    cum = (_dot(lower, hi) + _dot(lower, mid)) + _dot(lower, lo) + run_ref[...]
    bias = cum * (-LOG2E)
    pieces = jnp.concatenate(_split3(bias), axis=1)
    bias_ref[0] = _dot(pieces, place_ref[...]).astype(BF16)
    seg = seg_ref[...]
    bound_ref[0, 0] = jnp.concatenate(
        [_head_norm_max(qs, seg), _head_norm_max(k, seg),
         jnp.max(bias, axis=0, keepdims=True), jnp.min(bias, axis=0, keepdims=True),
         jnp.zeros((SUBLANES - 4, LANES), F32)], axis=0)
    run_ref[...] = run_ref[...] + jnp.sum(lf, axis=0, keepdims=True)


def _proj_c(x, sc, sh, w, wft, bf_row, bf_col, place, seg):
    b, t, d = x.shape
    tm = _row_tile(t, ATTN_BLOCK)
    nt = t // tm
    n = w.shape[1]
    q_scale = C_HEAD_DIM ** -0.5 * LOG2E
    row = lambda width: pl.BlockSpec((1, tm, width), lambda bi, i: (bi, i, 0))
    const = lambda shape: pl.BlockSpec(shape, lambda bi, i: tuple(0 for _ in shape))
    return pl.pallas_call(
        functools.partial(_proj_c_kernel, q_scale=q_scale),
        grid=(b, nt),
        in_specs=[row(d),
                  pl.BlockSpec((1, 1, d), lambda bi, i: (bi, 0, 0)),
                  pl.BlockSpec((1, 1, d), lambda bi, i: (bi, 0, 0)),
                  const((d, n)), const((C_HEADS, d)),
                  const((1, LANES)), const((C_HEADS, 1)), const((3 * LANES, BIAS_WIDTH)), const((C_WIDTH, LANES))],
        out_specs=[row(C_WIDTH), row(C_WIDTH), row(BIAS_WIDTH),
                   pl.BlockSpec((1, 1, C_WIDTH, tm), lambda bi, i: (bi, i, 0, 0)),
                   row(C_WIDTH), row(C_WIDTH), row(C_HEADS),
                   pl.BlockSpec((1, C_HEADS, tm), lambda bi, i: (bi, 0, i)),
                   pl.BlockSpec((1, 1, SUBLANES, LANES), lambda bi, i: (bi, i, 0, 0))],
        out_shape=[jax.ShapeDtypeStruct((b, t, C_WIDTH), BF16),
                   jax.ShapeDtypeStruct((b, t, C_WIDTH), BF16),
                   jax.ShapeDtypeStruct((b, t, BIAS_WIDTH), BF16),
                   jax.ShapeDtypeStruct((b, nt, C_WIDTH, tm), BF16),
                   jax.ShapeDtypeStruct((b, t, C_WIDTH), F32),
                   jax.ShapeDtypeStruct((b, t, C_WIDTH), F32),
                   jax.ShapeDtypeStruct((b, t, C_HEADS), F32),
                   jax.ShapeDtypeStruct((b, C_HEADS, t), F32),
                   jax.ShapeDtypeStruct((b, nt, SUBLANES, LANES), F32)],
        scratch_shapes=[pltpu.VMEM((1, LANES), F32)],
        compiler_params=_params(("arbitrary", "arbitrary")),
        name="proj_c",
    )(x, sc, sh, w, wft, bf_row, bf_col, place, seg)


def _lambda(lam_ref, lam_init):
    lq1, lk1, lq2, lk2 = (lam_ref[r:r + 1, :] for r in range(4))
    return (jnp.exp(jnp.sum(lq1 * lk1, axis=1, keepdims=True))
            - jnp.exp(jnp.sum(lq2 * lk2, axis=1, keepdims=True)) + lam_init)


def _attn_kernel(*refs, mode, lam_init):
    first_ref, refs = refs[0], refs[1:]
    if mode == "A":
        q_ref, k_ref, vt_ref, lam_ref, gain_ref, o_ref, qc_ref, m_ref, acc_ref, s0_ref, s1_ref, c0_ref, c1_ref = refs
        bias_ref = None
        d_val = 2 * A_HEAD_DIM
    else:
        q_ref, k_ref, bias_ref, vt_ref, o_ref, qc_ref, m_ref, acc_ref, s0_ref, s1_ref, c0_ref, c1_ref = refs
        d_val = C_HEAD_DIM
    tq = q_ref.shape[1]
    tk = k_ref.shape[2]
    qi = pl.program_id(2)
    lane = lax.broadcasted_iota(jnp.int32, (1, LANES), 1)
    low = lane < (LANES // 2)
    q = q_ref[0]
    zero = jnp.zeros_like(q)
    for a in range(2):
        qa = jnp.where(low, q, zero) if a == 0 else jnp.where(low, zero, q)
        if bias_ref is not None:
            pick = jnp.where((lane >= 3 * a) & (lane < 3 * a + 3), 1.0, 0.0).astype(BF16)
            qa = jnp.concatenate([qa, jnp.broadcast_to(pick, (tq, LANES))], axis=1)
        qc_ref[a] = qa
    m_ref[...] = jnp.full(m_ref.shape, NEG, F32)
    acc_ref[...] = jnp.zeros(acc_ref.shape, F32)
    ones = jnp.ones((BF16_ROWS, tk), BF16)

    units = [(a, slice(n * MXU_DIM, (n + 1) * MXU_DIM)) for a in range(2) for n in range(tq // MXU_DIM)]

    def score_chain(j, u):
        a, cs = units[u]
        kc = k_ref[0, j]
        if bias_ref is not None:
            kc = jnp.concatenate([kc, bias_ref[0, j]], axis=1)
        return _dot_nt(kc, qc_ref[a, cs, :])

    def value_chain(j, u, st, cmax, key_off):
        a, cs = units[u]
        masked = key_off is not None
        if masked:
            key = lax.broadcasted_iota(jnp.int32, (tk, MXU_DIM), 0) + key_off
            qry = lax.broadcasted_iota(jnp.int32, (tk, MXU_DIM), 1) + cs.start
            keep = ((key >> CHUNK_SHIFT) <= (qry >> CHUNK_SHIFT)) if mode == "A" else (key <= qry)
            st = jnp.where(keep, st, NEG)
            cmax = jnp.max(st, axis=0, keepdims=True)
        m_prev = m_ref[a, :, cs]
        m_new = jnp.maximum(m_prev, cmax)
        alpha = jnp.exp2(m_prev - m_new)
        pt = jnp.exp2(st - m_new).astype(BF16)
        vt = vt_ref[0, j]
        va = vt if mode == "A" else vt[a * d_val:(a + 1) * d_val]
        va = jnp.concatenate([va, ones], axis=0)
        acc_ref[a, :, cs] = alpha * acc_ref[a, :, cs] + _dot(va, pt)
        m_ref[a, :, cs] = m_new

    def stage(j_scores, dst, j_values, src, key_off=None):
        for t in range(len(units) + 1):
            if j_scores is not None and t < len(units):
                st = score_chain(j_scores, t)
                dst[0][t] = st
                dst[1][t] = jnp.max(st, axis=0, keepdims=True)
            if j_values is not None and t >= 1:
                value_chain(j_values, t - 1, src[0][t - 1], src[1][t - 1], key_off)

    buf0 = (s0_ref, c0_ref)
    buf1 = (s1_ref, c1_ref)
    n_diag = tq // tk
    n_full = qi * n_diag
    j0 = first_ref[pl.program_id(0), pl.program_id(1), qi]
    n_vis = n_full - j0
    stage(j0, buf0, None, None)

    def pair(j):
        stage(j + 1, buf1, j, buf0)
        stage(j + 2, buf0, j + 1, buf1)

    def quad(p, carry):
        pair(j0 + 4 * p)
        pair(j0 + 4 * p + 2)
        return carry

    lax.fori_loop(0, n_vis >> 2, quad, 0)

    @pl.when((n_vis & 2) == 2)
    def _():
        pair(j0 + (n_vis & ~3))

    for d in range(0, n_diag, 2):
        j = n_full + d
        stage(j + 1, buf1, j, buf0, key_off=d * tk)
        stage(j + 2 if d + 2 < n_diag else None, buf0, j + 1, buf1, key_off=(d + 1) * tk)

    outs = []
    for a in range(2):
        acc = acc_ref[a]
        outs.append(acc[:d_val] * (1.0 / acc[d_val:d_val + 1]))
    if mode == "A":
        ot = outs[0] - _lambda(lam_ref, lam_init) * outs[1]
        ot = ot * lax.rsqrt(jnp.mean(ot * ot, axis=0, keepdims=True) + LN_EPS)
        ot = ot * (gain_ref[0] * (1.0 - lam_init))
    else:
        ot = jnp.concatenate(outs, axis=0)
    o_ref[0] = ot.T.astype(o_ref.dtype)


def _attention(q, k, vt, first, mode, lam=None, gain=None, bias=None, lam_init=0.0):
    b, t, width = q.shape
    nk, tk = vt.shape[1], vt.shape[3]
    tq = 2 * tk
    assert t % tq == 0
    groups = width // LANES
    d_aug = (2 * A_HEAD_DIM if mode == "A" else C_HEAD_DIM) + BF16_ROWS
    kspec = pl.BlockSpec((1, nk, tk, LANES), lambda bi, g, i, f: (bi, 0, 0, g))
    in_specs = [pl.BlockSpec((1, tq, LANES), lambda bi, g, i, f: (bi, i, g)), kspec]
    args = [q, k.reshape(b, nk, tk, width)]
    if mode == "C":
        in_specs.append(kspec)
        args.append(bias.reshape(b, nk, tk, groups * LANES))
    in_specs.append(pl.BlockSpec((1, nk, LANES, tk), lambda bi, g, i, f: (bi, 0, g, 0)))
    args.append(vt)
    if mode == "A":
        in_specs += [pl.BlockSpec((4, A_HEAD_DIM), lambda bi, g, i, f: (0, 0)),
                     pl.BlockSpec((1, LANES, 1), lambda bi, g, i, f: (g, 0, 0))]
        args += [lam, gain]
    return pl.pallas_call(
        functools.partial(_attn_kernel, mode=mode, lam_init=lam_init),
        grid_spec=pltpu.PrefetchScalarGridSpec(
            num_scalar_prefetch=1,
            grid=(b, groups, t // tq),
            in_specs=in_specs,
            out_specs=pl.BlockSpec((1, tq, LANES), lambda bi, g, i, f: (bi, i, g)),
            scratch_shapes=[pltpu.VMEM((2, tq, LANES if mode == "A" else 2 * LANES), BF16),
                            pltpu.VMEM((2, 1, tq), F32),
                            pltpu.VMEM((2, d_aug, tq), F32),
                            pltpu.VMEM((2 * tq // MXU_DIM, tk, MXU_DIM), F32),
                            pltpu.VMEM((2 * tq // MXU_DIM, tk, MXU_DIM), F32),
                            pltpu.VMEM((2 * tq // MXU_DIM, 1, MXU_DIM), F32),
                            pltpu.VMEM((2 * tq // MXU_DIM, 1, MXU_DIM), F32)]),
        out_shape=jax.ShapeDtypeStruct((b, t, width), BF16),
        compiler_params=_params(("arbitrary", "arbitrary", "arbitrary")),
        name="attn_" + mode,
    )(first, *args)


def _first_visible_block(bounds, tq_tiles):
    qn, kn, bmax, bmin = (bounds[:, :, r, :C_HEADS] for r in range(4))
    b, tiles, _ = qn.shape
    nq = tiles // tq_tiles
    blk = lambda x, f: f(x.reshape(b, nq, tq_tiles, C_HEADS), axis=2)
    qn_q = blk(qn, jnp.max) * NORM_SLACK
    own = blk(bmin, jnp.min) - qn_q * blk(kn, jnp.max) * NORM_SLACK
    best = qn_q[:, :, None, :] * (kn * NORM_SLACK)[:, None, :, :] + bmax[:, None, :, :]
    dead = best < (own[:, :, None, :] - SKIP_GAP)
    block = jnp.arange(tiles, dtype=jnp.int32)[None, None, :, None]
    lead = jnp.min(jnp.where(dead, tiles, block), axis=2)
    lead = jnp.min(lead.reshape(b, nq, C_HEADS // 2, 2), axis=-1)
    lead = jnp.minimum(lead, (jnp.arange(nq, dtype=jnp.int32) * tq_tiles)[None, :, None])
    return jnp.transpose(lead - lead % 2, (0, 2, 1)).astype(jnp.int32)


def _diff_finish(o0, o1, lam, gain, lam_init):
    o = o0 - lam * o1
    o = o * lax.rsqrt(jnp.mean(o * o, axis=-1, keepdims=True) + LN_EPS)
    return o * gain * (1.0 - lam_init)


def _cached_attn_kernel(*refs, mode, past, lam_init):
    if mode == "A":
        q_ref, kn_ref, vn_ref, ck_ref, cv_ref, lam_ref, gain_ref, o_ref = refs
    else:
        q_ref, kn_ref, vn_ref, ck_ref, cv_ref, clf_ref, nlf_ref, o_ref = refs
    t = q_ref.shape[1]
    width = o_ref.shape[2]
    groups = width // LANES
    lane = lax.broadcasted_iota(jnp.int32, (1, LANES), 1)
    low = lane < (LANES // 2)
    row = lax.broadcasted_iota(jnp.int32, (t, t), 0)
    col = lax.broadcasted_iota(jnp.int32, (t, t), 1)
    if mode == "A":
        keep = ((past + col) >> CHUNK_SHIFT) <= ((past + row) >> CHUNK_SHIFT)
        lam = _lambda(lam_ref, lam_init)
    else:
        keep = col <= row
        r = lax.broadcasted_iota(jnp.int32, (past, past), 0)
        c = lax.broadcasted_iota(jnp.int32, (past, past), 1)
        upper = jnp.where(r <= c, 1.0, 0.0).astype(BF16)
        clf = clf_ref[0]
        hi, mid, lo = _split3(clf)
        cum_c = (_dot(hi, upper) + _dot(mid, upper)) + _dot(lo, upper)
        upper_n = jnp.where(row <= col, 1.0, 0.0).astype(BF16)
        hi, mid, lo = _split3(nlf_ref[0])
        cum_n = ((_dot(hi, upper_n) + _dot(mid, upper_n)) + _dot(lo, upper_n)
                 + jnp.sum(clf, axis=1, keepdims=True))
        bias_c = cum_c * (-LOG2E)
        bias_n = cum_n * (-LOG2E)
    keep2 = jnp.concatenate([keep, keep], axis=0)
    first_rows = lax.broadcasted_iota(jnp.int32, (2 * t, 1), 0) < t
    for g in range(groups):
        sl = slice(g * LANES, (g + 1) * LANES)
        q = q_ref[0, :, sl]
        kn = kn_ref[0, :, sl]
        vn = vn_ref[0, :, sl].astype(BF16)
        if mode == "A":
            kc = ck_ref[0, 0, :, g, :].astype(BF16)
            vc = cv_ref[0, 0, :, g, :].astype(BF16)
        else:
            kc = ck_ref[0, :, sl].astype(BF16)
            vc = cv_ref[0, :, sl].astype(BF16)
        zero = jnp.zeros_like(q)
        q2 = jnp.concatenate([jnp.where(low, q, zero), jnp.where(low, zero, q)], axis=0)
        s_c = _dot_nt(q2, kc)
        s_n = _dot_nt(q2, kn)
        if mode == "C":
            h = 2 * g
            s_c = s_c + jnp.where(first_rows, bias_c[h:h + 1, :], bias_c[h + 1:h + 2, :])
            s_n = s_n + jnp.where(first_rows, bias_n[h:h + 1, :], bias_n[h + 1:h + 2, :])
        s_n = jnp.where(keep2, s_n, NEG)
        m = jnp.maximum(jnp.max(s_c, axis=1, keepdims=True), jnp.max(s_n, axis=1, keepdims=True))
        p_c = jnp.exp2(s_c - m)
        p_n = jnp.exp2(s_n - m)
        l = jnp.sum(p_c, axis=1, keepdims=True) + jnp.sum(p_n, axis=1, keepdims=True)
        acc = (_dot(p_c.astype(BF16), vc) + _dot(p_n.astype(BF16), vn)) * (1.0 / l)
        outs = [acc[:t], acc[t:]]
        if mode == "A":
            o = _diff_finish(outs[0], outs[1], lam, gain_ref[g], lam_init)
        else:
            o = jnp.where(low, outs[0], outs[1])
        o_ref[0, :, sl] = o.astype(o_ref.dtype)


def _cached_attention(q, kn, vn, cache_k, cache_v, mode, lam=None, gain=None, cache_lft=None, new_lft=None,
                      lam_init=0.0):
    b, t, width = q.shape
    new = pl.BlockSpec((1, t, width), lambda bi: (bi, 0, 0))
    if mode == "A":
        _, _, past, heads, dh = cache_k.shape
        old = pl.BlockSpec((1, 1, past, heads, dh), lambda bi: (0, bi, 0, 0, 0))
    else:
        past = cache_k.shape[1]
        old = pl.BlockSpec((1, past, width), lambda bi: (bi, 0, 0))
    in_specs = [new, new, new, old, old]
    args = [q, kn, vn, cache_k, cache_v]
    if mode == "A":
        in_specs += [pl.BlockSpec((4, A_HEAD_DIM), lambda bi: (0, 0)),
                     pl.BlockSpec((A_HEADS, 1, LANES), lambda bi: (0, 0, 0))]
        args += [lam, gain]
    else:
        in_specs += [pl.BlockSpec((1, C_HEADS, past), lambda bi: (bi, 0, 0)),
                     pl.BlockSpec((1, C_HEADS, t), lambda bi: (bi, 0, 0))]
        args += [cache_lft, new_lft]
    return pl.pallas_call(
        functools.partial(_cached_attn_kernel, mode=mode, past=past, lam_init=lam_init),
        grid=(b,),
        in_specs=in_specs,
        out_specs=new,
        out_shape=jax.ShapeDtypeStruct((b, t, width), BF16),
        compiler_params=_params(("arbitrary",)),
        name="cached_attn_" + mode,
    )(*args)


def _gelu_tanh(x):
    return 0.5 * x * (1.0 + jnp.tanh(math.sqrt(2.0 / math.pi) * (x + 0.044715 * (x * x * x))))


def _causal_conv(x, tail, w_ref, b_ref):
    width = w_ref.shape[0]
    tm = x.shape[0]
    cat = jnp.concatenate([tail, x], axis=0)
    y = None
    for j in range(width):
        back = width - 1 - j
        src = cat if back == 0 else pltpu.roll(cat, back, 0)
        term = src[SUBLANES:SUBLANES + tm] * w_ref[j:j + 1, :]
        y = term if y is None else y + term
    return y + b_ref[...]


def _rglru_kernel(xg_ref, h0_ref, cb_ref, cw_ref, cbias_ref, wa_ref, ba_ref, wx_ref, bx_ref, sp_ref,
                  y_ref, hl_ref, ct_ref, hc_ref, tail_ref):
    i = pl.program_id(1)
    tm = xg_ref.shape[1]

    @pl.when(i == 0)
    def _():
        hc_ref[...] = h0_ref[0]
        tail_ref[...] = cb_ref[0]

    x = xg_ref[0, :, :B_WIDTH]
    gate_in = xg_ref[0, :, B_WIDTH:]
    xc = _causal_conv(x, tail_ref[...], cw_ref, cbias_ref)
    xcb = xc.astype(BF16)
    r = _sigmoid(_dot(xcb, wa_ref[...]) + ba_ref[...])
    ig = _sigmoid(_dot(xcb, wx_ref[...]) + bx_ref[...])
    log_a = (-RG_C) * r * sp_ref[...]
    a = jnp.exp(log_a)
    th = jnp.tanh(log_a)
    bx = jnp.sqrt((-2.0 * th) / (1.0 - th)) * (ig * xc)
    pos = lax.broadcasted_iota(jnp.int32, (tm, 1), 0) & (SUBLANES - 1)
    s = 1
    while s < SUBLANES:
        valid = pos >= s
        a_sh = pltpu.roll(a, s, 0)
        b_sh = pltpu.roll(bx, s, 0)
        bx = jnp.where(valid, a * b_sh + bx, bx)
        a = jnp.where(valid, a * a_sh, a)
        s *= 2
    carry = hc_ref[...]
    groups = []
    for g in range(tm // SUBLANES):
        rows = slice(g * SUBLANES, (g + 1) * SUBLANES)
        hg = a[rows] * carry + bx[rows]
        groups.append(hg)
        carry = hg[SUBLANES - 1:, :]
    h = jnp.concatenate(groups, axis=0)
    y_ref[0] = (h * _gelu_tanh(gate_in)).astype(y_ref.dtype)
    h_tail = h[tm - SUBLANES:, :]
    x_tail = x[tm - SUBLANES:, :]
    hc_ref[...] = h_tail[SUBLANES - 1:, :]
    tail_ref[...] = x_tail
    hl_ref[0] = h_tail
    ct_ref[0] = x_tail


def _rglru(xg, h0, conv_tail, cw, cbias, wa, ba, wx, bx, sp, tm_pref=256):
    b, t, _ = xg.shape
    tm = _row_tile(t, tm_pref)
    wspec = lambda shape: pl.BlockSpec(shape, lambda bi, i: tuple(0 for _ in shape))
    return pl.pallas_call(
        _rglru_kernel,
        grid=(b, t // tm),
        in_specs=[pl.BlockSpec((1, tm, 2 * B_WIDTH), lambda bi, i: (bi, i, 0)),
                  pl.BlockSpec((1, 1, B_WIDTH), lambda bi, i: (bi, 0, 0)),
                  pl.BlockSpec((1, SUBLANES, B_WIDTH), lambda bi, i: (bi, 0, 0)),
                  wspec((B_CONV, B_WIDTH)), wspec((1, B_WIDTH)),
                  wspec((B_WIDTH, B_WIDTH)), wspec((1, B_WIDTH)),
                  wspec((B_WIDTH, B_WIDTH)), wspec((1, B_WIDTH)), wspec((1, B_WIDTH))],
        out_specs=[pl.BlockSpec((1, tm, B_WIDTH), lambda bi, i: (bi, i, 0)),
                   pl.BlockSpec((1, SUBLANES, B_WIDTH), lambda bi, i: (bi, 0, 0)),
                   pl.BlockSpec((1, SUBLANES, B_WIDTH), lambda bi, i: (bi, 0, 0))],
        out_shape=[jax.ShapeDtypeStruct((b, t, B_WIDTH), BF16),
                   jax.ShapeDtypeStruct((b, SUBLANES, B_WIDTH), F32),
                   jax.ShapeDtypeStruct((b, SUBLANES, B_WIDTH), F32)],
        scratch_shapes=[pltpu.VMEM((1, B_WIDTH), F32), pltpu.VMEM((SUBLANES, B_WIDTH), F32)],
        compiler_params=_params(("arbitrary", "arbitrary")),
        name="rglru",
    )(xg, h0, conv_tail, cw, cbias, wa, ba, wx, bx, sp)


def _proj_ln_kernel(*refs, n_in):
    h_refs = refs[:n_in]
    w_ref, x_ref, gate_ref, g_ref, b_ref, o_ref = refs[n_in:]
    tm = x_ref.shape[1]
    rows = [slice(r, min(r + LN_ROWS, tm)) for r in range(0, tm, LN_ROWS)]

    def matmul(rs):
        hs = [r[0, rs, :] for r in h_refs]
        return _dot(hs[0] if n_in == 1 else jnp.concatenate(hs, axis=-1), w_ref[0])

    def norm(rs, proj):
        y = ALPHA * x_ref[0, rs, :] + gate_ref[0] * proj
        mu = jnp.mean(y, axis=-1, keepdims=True)
        yc = y - mu
        var = jnp.mean(yc * yc, axis=-1, keepdims=True)
        o_ref[0, rs, :] = yc * lax.rsqrt(var + LN_EPS) * g_ref[...] + b_ref[...]

    pending = matmul(rows[0])
    for r in range(len(rows)):
        nxt = matmul(rows[r + 1]) if r + 1 < len(rows) else None
        norm(rows[r], pending)
        pending = nxt


def _proj_ln(hs, w, layer, x, gate, ln_g, ln_b, tm_pref=1024):
    b, t, d = x.shape
    tm = _row_tile(t, tm_pref)
    k = w.shape[1]
    gate_spec = (pl.BlockSpec((1, 1, d), lambda bi, i: (bi, 0, 0)) if gate.shape[1] == 1
                 else pl.BlockSpec((1, tm, d), lambda bi, i: (bi, i, 0)))
    in_specs = [pl.BlockSpec((1, tm, h.shape[2]), lambda bi, i: (bi, i, 0)) for h in hs]
    in_specs += [pl.BlockSpec((1, k, d), lambda bi, i: (layer, 0, 0)),
                 pl.BlockSpec((1, tm, d), lambda bi, i: (bi, i, 0)),
                 gate_spec,
                 pl.BlockSpec((1, d), lambda bi, i: (0, 0)),
                 pl.BlockSpec((1, d), lambda bi, i: (0, 0))]
    return pl.pallas_call(
        functools.partial(_proj_ln_kernel, n_in=len(hs)),
        grid=(b, t // tm),
        in_specs=in_specs,
        out_specs=pl.BlockSpec((1, tm, d), lambda bi, i: (bi, i, 0)),
        out_shape=jax.ShapeDtypeStruct((b, t, d), F32),
        compiler_params=_params(("arbitrary", "arbitrary")),
        name="proj_ln",
    )(*hs, w, x, gate, ln_g, ln_b)


def _ffn_up_kernel(x_ref, sc_ref, sh_ref, w_ref, cb_ref, cw_ref, cbias_ref, h_ref, ct_ref, tail_ref):
    i = pl.program_id(1)
    tm = x_ref.shape[1]

    @pl.when(i == 0)
    def _():
        tail_ref[...] = cb_ref[0]

    u = _modulate(x_ref[0], sc_ref[0], sh_ref[0])
    chunks = [slice(c, min(c + FFN_CHUNK, D_FF)) for c in range(0, D_FF, FFN_CHUNK)]

    def matmuls(cs):
        gs = slice(D_FF + cs.start, D_FF + cs.stop)
        return _dot(u, w_ref[0, :, cs]), _dot(u, w_ref[0, :, gs])

    def gate(cs, a, g):
        gc = _causal_conv(g, tail_ref[:, cs], cw_ref.at[:, cs], cbias_ref.at[:, cs])
        silu = gc * (0.5 * jnp.tanh(0.5 * gc) + 0.5)
        h_ref[0, :, cs] = (a * silu).astype(h_ref.dtype)
        g_tail = g[tm - SUBLANES:, :]
        tail_ref[:, cs] = g_tail
        ct_ref[0, :, cs] = g_tail

    pending = matmuls(chunks[0])
    for c in range(len(chunks)):
        nxt = matmuls(chunks[c + 1]) if c + 1 < len(chunks) else None
        gate(chunks[c], *pending)
        pending = nxt


def _ffn_up(x, sc, sh, w, layer, conv_tail, cw, cbias, tm_pref=256):
    b, t, d = x.shape
    tm = _row_tile(t, tm_pref)
    return pl.pallas_call(
        _ffn_up_kernel,
        grid=(b, t // tm),
        in_specs=[pl.BlockSpec((1, tm, d), lambda bi, i: (bi, i, 0)),
                  pl.BlockSpec((1, 1, d), lambda bi, i: (bi, 0, 0)),
                  pl.BlockSpec((1, 1, d), lambda bi, i: (bi, 0, 0)),
                  pl.BlockSpec((1, d, 2 * D_FF), lambda bi, i: (layer, 0, 0)),
                  pl.BlockSpec((1, SUBLANES, D_FF), lambda bi, i: (bi, 0, 0)),
                  pl.BlockSpec((FFN_CONV, D_FF), lambda bi, i: (0, 0)),
                  pl.BlockSpec((1, D_FF), lambda bi, i: (0, 0))],
        out_specs=[pl.BlockSpec((1, tm, D_FF), lambda bi, i: (bi, i, 0)),
                   pl.BlockSpec((1, SUBLANES, D_FF), lambda bi, i: (bi, 0, 0))],
        out_shape=[jax.ShapeDtypeStruct((b, t, D_FF), BF16),
                   jax.ShapeDtypeStruct((b, SUBLANES, D_FF), F32)],
        scratch_shapes=[pltpu.VMEM((SUBLANES, D_FF), F32)],
        compiler_params=_params(("arbitrary", "arbitrary")),
        name="ffn_up",
    )(x, sc, sh, w, conv_tail, cw, cbias)


def _pad_tail(buf):
    return jnp.pad(buf, ((0, 0), (SUBLANES - buf.shape[1], 0), (0, 0)))


def _rope_tables(past, t):
    half = A_HEAD_DIM // 2
    inv = ROPE_THETA ** (-jnp.arange(0, A_HEAD_DIM, 2, dtype=F32) / A_HEAD_DIM)
    pos = (past + jnp.arange(t, dtype=jnp.int32)).astype(F32)
    ang = pos[:, None] * inv[None, :]
    cos = jnp.tile(jnp.cos(ang), (1, LANES // half))
    sin = jnp.sin(ang)
    sin_signed = jnp.tile(jnp.concatenate([-sin, sin], axis=1), (1, LANES // A_HEAD_DIM))
    return cos, sin_signed


def _block_diag(w):
    n, i, o = w.shape
    return jnp.einsum("nio,nm->nimo", w, jnp.eye(n, dtype=w.dtype)).reshape(n * i, n * o)


def _prepare(p):
    w = {}
    w_in_ab = p["w_in_ab"][0]
    w["in_ab"] = w_in_ab.astype(BF16)
    w["lam"] = jnp.stack([p["lam_q1"][0], p["lam_k1"][0], p["lam_q2"][0], p["lam_k2"][0]])
    w["gain_row"] = p["attn_gain"][0].reshape(A_HEADS, 1, LANES)
    w["gain_col"] = p["attn_gain"][0].reshape(A_HEADS, LANES, 1)
    w["b_conv_w"] = p["b_conv_w"][0]
    w["b_conv_b"] = p["b_conv_b"][0].reshape(1, B_WIDTH)
    w["rg_a"] = _block_diag(p["w_rg_a"][0]).astype(BF16)
    w["rg_x"] = _block_diag(p["w_rg_x"][0]).astype(BF16)
    w["b_rg_a"] = p["b_rg_a"][0].reshape(1, B_WIDTH)
    w["b_rg_x"] = p["b_rg_x"][0].reshape(1, B_WIDTH)
    w["rg_L"] = p["rg_L"][0].reshape(1, B_WIDTH)
    w_in_c = p["w_in_c"][0]
    w["in_c"] = jnp.pad(w_in_c, ((0, 0), (0, LANES - C_HEADS))).astype(BF16)
    w["in_c_ft"] = w_in_c[:, 3 * C_WIDTH:].T.astype(BF16)
    w["bf_row"] = jnp.pad(p["b_f"][0], (0, LANES - C_HEADS)).reshape(1, LANES)
    w["bf_col"] = p["b_f"][0].reshape(C_HEADS, 1)
    w["place"] = _bias_placement()
    w["seg"] = jnp.asarray(np.repeat(np.eye(LANES, dtype=np.float32)[:C_HEADS], C_HEAD_DIM, axis=0), BF16)
    w["up"] = p["w_up"].astype(BF16)
    w["down"] = p["w_down"].astype(BF16)
    w["out_ab"] = p["w_out_ab"].astype(BF16)
    w["out_c"] = p["w_out_c"].astype(BF16)
    w["ffn_conv_w"] = [p["ffn_conv_w"][i] for i in range(DEPTH)]
    w["ffn_conv_b"] = [p["ffn_conv_b"][i].reshape(1, D_FF) for i in range(DEPTH)]
    w["ln1_g"] = [p["ln1_g"][i].reshape(1, D_MODEL) for i in range(DEPTH)]
    w["ln1_b"] = [p["ln1_b"][i].reshape(1, D_MODEL) for i in range(DEPTH)]
    w["ln2_g"] = [p["ln2_g"][i].reshape(1, D_MODEL) for i in range(DEPTH)]
    w["ln2_b"] = [p["ln2_b"][i].reshape(1, D_MODEL) for i in range(DEPTH)]
    return w


def _softplus_kernel(x_ref, o_ref):
    o_ref[...] = _softplus(-x_ref[...])


def _proj_ln_rows(hs, w, layer, x, gate, ln_g, ln_b):
    b, t, d = x.shape
    flat = lambda a: a.reshape(1, b * t, a.shape[2])
    gate_rows = jnp.broadcast_to(gate, (b, t, d))
    return _proj_ln([flat(h) for h in hs], w, layer, flat(x), flat(gate_rows), ln_g, ln_b).reshape(b, t, d)


def _trunk(x, mods, w, sp, cache_a_k=None, cache_a_v=None, state_b_h=None, state_b_conv=None,
           cache_c_k=None, cache_c_v=None, cache_c_logf=None, state_ffn_conv=None):
    b, t, d = x.shape
    cached = cache_a_k is not None
    proj_ln = _proj_ln_rows if cached else _proj_ln
    past = cache_a_k.shape[2] if cached else 0
    outs = {}
    for i in range(DEPTH):
        sh1, sc1, g1, sh2, sc2, g2 = [m[:, None, :] for m in jnp.split(mods[i], 6, axis=-1)]
        if i % 2 == 0:
            lam_init = 0.8 - 0.6 * math.exp(-0.3 * i)
            cos, sin = _rope_tables(past, t)
            q, kb, vt, k32, v32, xg = _proj_ab(x, sc1, sh1, w["in_ab"], cos, sin)
            if cached:
                o = _cached_attention(q, kb, v32.reshape(b, t, A_WIDTH), cache_a_k, cache_a_v,
                                      "A", lam=w["lam"], gain=w["gain_row"], lam_init=lam_init)
                h0 = state_b_h[0][:, None, :]
                ctail = _pad_tail(state_b_conv[0])
            else:
                first = jnp.zeros((b, A_HEADS, t // (2 * ATTN_BLOCK)), jnp.int32)
                o = _attention(q, kb, vt, first, "A", lam=w["lam"], gain=w["gain_col"], lam_init=lam_init)
                h0 = jnp.zeros((b, 1, B_WIDTH), F32)
                ctail = jnp.zeros((b, SUBLANES, B_WIDTH), F32)
            yb, h_tail, x_tail = _rglru(xg, h0, ctail, w["b_conv_w"], w["b_conv_b"], w["rg_a"], w["b_rg_a"],
                                        w["rg_x"], w["b_rg_x"], sp)
            outs["a_k"] = k32[None]
            outs["a_v"] = v32[None]
            outs["b_h"] = h_tail[:, SUBLANES - 1, :][None]
            outs["b_conv"] = x_tail[:, SUBLANES - (B_CONV - 1):, :][None]
            x = proj_ln([o, yb], w["out_ab"], i // 2, x, g1, w["ln1_g"][i], w["ln1_b"][i])
        else:
            q, kb, bias, vt, k32, v32, lf, lft, bounds = _proj_c(x, sc1, sh1, w["in_c"], w["in_c_ft"],
                                                                 w["bf_row"], w["bf_col"], w["place"], w["seg"])
            if cached:
                o = _cached_attention(q, kb, v32, cache_c_k[0].reshape(b, past, C_WIDTH),
                                      cache_c_v[0].reshape(b, past, C_WIDTH),
                                      "C", cache_lft=jnp.swapaxes(cache_c_logf[0], 1, 2), new_lft=lft)
            else:
                o = _attention(q, kb, vt, _first_visible_block(bounds, 2), "C", bias=bias)
            outs["c_k"] = k32.reshape(1, b, t, C_HEADS, C_HEAD_DIM)
            outs["c_v"] = v32.reshape(1, b, t, C_HEADS, C_HEAD_DIM)
            outs["c_logf"] = lf[None]
            x = proj_ln([o], w["out_c"], i // 2, x, g1, w["ln1_g"][i], w["ln1_b"][i])
        ftail = _pad_tail(state_ffn_conv[i]) if cached else jnp.zeros((b, SUBLANES, D_FF), F32)
        hmid, g_tail = _ffn_up(x, sc2, sh2, w["up"], i, ftail, w["ffn_conv_w"][i], w["ffn_conv_b"][i])
        outs.setdefault("ffn", []).append(g_tail[:, SUBLANES - (FFN_CONV - 1):, :])
        x = proj_ln([hmid], w["down"], i, x, g2, w["ln2_g"][i], w["ln2_b"][i])
    return (x, outs["a_k"], outs["a_v"], outs["b_h"], outs["b_conv"],
            outs["c_k"], outs["c_v"], outs["c_logf"], jnp.stack(outs["ffn"]))


def kernel(x_prompt, x_sample, c_prompt, c_sample, cache_a_k, cache_a_v, state_b_h, state_b_conv, cache_c_k, cache_c_v, cache_c_logf, state_ffn_conv, w_ada, b_ada, ln1_g, ln1_b, ln2_g, ln2_b, w_in_ab, lam_q1, lam_k1, lam_q2, lam_k2, attn_gain, b_conv_w, b_conv_b, w_rg_a, b_rg_a, w_rg_x, b_rg_x, rg_L, w_out_ab, w_in_c, b_f, w_out_c, w_up, ffn_conv_w, ffn_conv_b, w_down):
    p = dict(w_in_ab=w_in_ab, lam_q1=lam_q1, lam_k1=lam_k1, lam_q2=lam_q2, lam_k2=lam_k2, attn_gain=attn_gain,
             b_conv_w=b_conv_w, b_conv_b=b_conv_b, w_rg_a=w_rg_a, b_rg_a=b_rg_a, w_rg_x=w_rg_x, b_rg_x=b_rg_x,
             rg_L=rg_L, w_out_ab=w_out_ab, w_in_c=w_in_c, b_f=b_f, w_out_c=w_out_c, w_up=w_up,
             ffn_conv_w=ffn_conv_w, ffn_conv_b=ffn_conv_b, w_down=w_down,
             ln1_g=ln1_g, ln1_b=ln1_b, ln2_g=ln2_g, ln2_b=ln2_b)
    w = _prepare(p)
    bp = c_prompt.shape[0]
    bs = c_sample.shape[0]
    rows = -(-(bp + bs) // 16) * 16
    c_all = jnp.pad(jnp.concatenate([c_prompt, c_sample], axis=0), ((0, rows - bp - bs), (0, 0)))
    mods = _mods(c_all, w_ada, b_ada)
    sp = pl.pallas_call(_softplus_kernel, out_shape=jax.ShapeDtypeStruct((1, B_WIDTH), F32),
                        name="softplus")(w["rg_L"])
    res_p = _trunk(x_prompt, mods[:, :bp], w, sp)
    res_s = _trunk(x_sample, mods[:, bp:bp + bs], w, sp, cache_a_k, cache_a_v, state_b_h, state_b_conv,
                   cache_c_k, cache_c_v, cache_c_logf, state_ffn_conv)
    return (res_p[0], res_s[0]) + res_p[1:] + res_s[1:]
```

```python
import functools
import math

import numpy as np
import jax
import jax.numpy as jnp
from jax import lax
from jax.experimental import pallas as pl
from jax.experimental.pallas import tpu as pltpu

F32 = jnp.float32
BF16 = jnp.bfloat16

D_MODEL = 1024
DEPTH = 2
CHUNK = 64
CHUNK_SHIFT = 6
A_HEADS = 4
A_HEAD_DIM = 64
A_WIDTH = A_HEADS * 2 * A_HEAD_DIM
B_WIDTH = 512
B_BLOCKS = 8
B_CONV = 4
RG_C = 8.0
C_HEADS = 16
C_HEAD_DIM = 64
C_WIDTH = C_HEADS * C_HEAD_DIM
D_FF = 2816
FFN_CONV = 3
ROPE_THETA = 10000.0
ALPHA = (2 * DEPTH) ** 0.25
LN_EPS = 1e-5
NEG = -1e30
LOG2E = 1.4426950408889634

LANES = 128
SUBLANES = 8
BF16_ROWS = 16
MXU_DIM = 256
VMEM_LIMIT = 56 * 1024 * 1024
BIAS_WIDTH = (C_HEADS // 2) * LANES
FFN_CHUNK = 256
LN_ROWS = 256
SKIP_GAP = 160.0
NORM_SLACK = 1.03
SHORT_WINDOWS = (4, 6)
ATTN_BLOCK = 512


def _params(sem, flags=None):
    return pltpu.CompilerParams(dimension_semantics=sem, vmem_limit_bytes=VMEM_LIMIT, flags=flags)


def _row_tile(t, pref):
    if t <= pref:
        return t
    tm = pref
    while t % tm:
        tm //= 2
    return tm


def _modulate(x, sc, sh):
    return (x * (1.0 + sc) + sh).astype(BF16)


def _sigmoid(x):
    return 1.0 / (1.0 + jnp.exp(-x))


def _softplus(x):
    return jnp.maximum(x, 0.0) + jnp.log1p(jnp.exp(-jnp.abs(x)))


def _log_sigmoid(x):
    return jnp.minimum(x, 0.0) - jnp.log1p(jnp.exp(-jnp.abs(x)))


def _split3(x):
    hi = x.astype(BF16)
    r1 = x - hi.astype(F32)
    mid = r1.astype(BF16)
    lo = (r1 - mid.astype(F32)).astype(BF16)
    return hi, mid, lo


def _dot(a, b):
    return jnp.dot(a, b, preferred_element_type=F32)


def _dot_nt(a, b):
    return lax.dot_general(a, b, (((1,), (1,)), ((), ())), preferred_element_type=F32)


def _mods_kernel(c_ref, w_ref, b_ref, o_ref):
    c = c_ref[...]
    s = (c * _sigmoid(c)).astype(BF16)
    o_ref[0] = _dot(s, w_ref[0].astype(BF16)) + b_ref[0]


def _mods(c_all, w_ada, b_ada):
    rows, d = c_all.shape
    n = w_ada.shape[-1]
    tn = 1536
    return pl.pallas_call(
        _mods_kernel,
        grid=(DEPTH, n // tn),
        in_specs=[pl.BlockSpec((rows, d), lambda l, j: (0, 0)),
                  pl.BlockSpec((1, d, tn), lambda l, j: (l, 0, j)),
                  pl.BlockSpec((1, 1, tn), lambda l, j: (l, 0, j))],
        out_specs=pl.BlockSpec((1, rows, tn), lambda l, j: (l, 0, j)),
        out_shape=jax.ShapeDtypeStruct((DEPTH, rows, n), F32),
        compiler_params=_params(("arbitrary", "arbitrary")),
        name="mods",
    )(c_all, w_ada, b_ada.reshape(DEPTH, 1, n))


def _rope_slab(x, cos, sin_signed, first_half):
    fwd = pltpu.roll(x, LANES - A_HEAD_DIM // 2, 1)
    bwd = pltpu.roll(x, A_HEAD_DIM // 2, 1)
    partner = jnp.where(first_half, fwd, bwd)
    return x * cos + partner * sin_signed


def _proj_ab_kernel(x_ref, sc_ref, sh_ref, w_ref, cos_ref, sin_ref,
                    q_ref, kb_ref, vt_ref, k_ref, v_ref, xg_ref, *, q_scale):
    u = _modulate(x_ref[0], sc_ref[0], sh_ref[0])
    pr = _dot(u, w_ref[...])
    cos = cos_ref[...]
    sin = sin_ref[...]
    lane = lax.broadcasted_iota(jnp.int32, (1, LANES), 1)
    first_half = (lane & (A_HEAD_DIM - 1)) < (A_HEAD_DIM // 2)
    for h in range(A_HEADS):
        sl = slice(h * LANES, (h + 1) * LANES)
        q = _rope_slab(pr[:, sl], cos, sin, first_half)
        q_ref[0, :, sl] = (q * q_scale).astype(BF16)
        k = _rope_slab(pr[:, A_WIDTH + h * LANES:A_WIDTH + (h + 1) * LANES], cos, sin, first_half)
        k_ref[0, :, h, :] = k
        kb_ref[0, :, sl] = k.astype(BF16)
    for h in range(A_HEADS):
        v_ref[0, :, h, :] = pr[:, 2 * A_WIDTH + h * LANES:2 * A_WIDTH + (h + 1) * LANES]
    vt_ref[0, 0] = pr[:, 2 * A_WIDTH:3 * A_WIDTH].T.astype(BF16)
    xg_ref[0] = pr[:, 3 * A_WIDTH:]


def _proj_ab(x, sc, sh, w, cos, sin):
    b, t, d = x.shape
    tm = _row_tile(t, ATTN_BLOCK)
    nt = t // tm
    n = w.shape[1]
    q_scale = A_HEAD_DIM ** -0.5 * LOG2E
    row = lambda width: pl.BlockSpec((1, tm, width), lambda bi, i: (bi, i, 0))
    const = lambda shape: pl.BlockSpec(shape, lambda bi, i: tuple(0 for _ in shape))
    heads = pl.BlockSpec((1, tm, A_HEADS, LANES), lambda bi, i: (bi, i, 0, 0))
    return pl.pallas_call(
        functools.partial(_proj_ab_kernel, q_scale=q_scale),
        grid=(b, nt),
        in_specs=[row(d),
                  pl.BlockSpec((1, 1, d), lambda bi, i: (bi, 0, 0)),
                  pl.BlockSpec((1, 1, d), lambda bi, i: (bi, 0, 0)),
                  const((d, n)),
                  pl.BlockSpec((tm, LANES), lambda bi, i: (i, 0)),
                  pl.BlockSpec((tm, LANES), lambda bi, i: (i, 0))],
        out_specs=[row(A_WIDTH), row(A_WIDTH),
                   pl.BlockSpec((1, 1, A_WIDTH, tm), lambda bi, i: (bi, i, 0, 0)),
                   heads, heads, row(2 * B_WIDTH)],
        out_shape=[jax.ShapeDtypeStruct((b, t, A_WIDTH), BF16),
                   jax.ShapeDtypeStruct((b, t, A_WIDTH), BF16),
                   jax.ShapeDtypeStruct((b, nt, A_WIDTH, tm), BF16),
                   jax.ShapeDtypeStruct((b, t, A_HEADS, LANES), F32),
                   jax.ShapeDtypeStruct((b, t, A_HEADS, LANES), F32),
                   jax.ShapeDtypeStruct((b, t, 2 * B_WIDTH), F32)],
        compiler_params=_params(("arbitrary", "arbitrary")),
        name="proj_ab",
    )(x, sc, sh, w, cos, sin)


def _bias_placement():
    e = np.zeros((3 * LANES, BIAS_WIDTH), np.float32)
    for piece in range(3):
        for h in range(C_HEADS):
            e[piece * LANES + h, (h // 2) * LANES + 3 * (h % 2) + piece] = 1.0
    return jnp.asarray(e, BF16)


def _head_norm_max(x, seg):
    n2 = _dot((x * x).astype(BF16), seg)
    return jnp.sqrt(jnp.max(n2, axis=0, keepdims=True))


def _proj_c_kernel(x_ref, sc_ref, sh_ref, w_ref, wft_ref, bfr_ref, bfc_ref, place_ref, seg_ref,
                   q_ref, kb_ref, bias_ref, vt_ref, k_ref, v_ref, lf_ref, lft_ref, bound_ref, run_ref, *, q_scale):
    i = pl.program_id(1)
    tm = x_ref.shape[1]

    @pl.when(i == 0)
    def _():
        run_ref[...] = jnp.zeros_like(run_ref)

    u = _modulate(x_ref[0], sc_ref[0], sh_ref[0])
    pr = _dot(u, w_ref[...])
    qs = pr[:, :C_WIDTH] * q_scale
    q_ref[0] = qs.astype(BF16)
    k = pr[:, C_WIDTH:2 * C_WIDTH]
    k_ref[0] = k
    kb_ref[0] = k.astype(BF16)
    v = pr[:, 2 * C_WIDTH:3 * C_WIDTH]
    v_ref[0] = v
    vt_ref[0, 0] = v.T.astype(BF16)
    lf = _log_sigmoid(pr[:, 3 * C_WIDTH:] + bfr_ref[...])
    lf_ref[0] = lf[:, :C_HEADS]
    lft_ref[0] = _log_sigmoid(_dot_nt(wft_ref[...], u) + bfc_ref[...])
    r = lax.broadcasted_iota(jnp.int32, (tm, tm), 0)
    c = lax.broadcasted_iota(jnp.int32, (tm, tm), 1)
    lower = jnp.where(c <= r, 1.0, 0.0).astype(BF16)
    hi, mid, lo = _split3(lf)
    cum = (_dot(lower, hi) + _dot(lower, mid)) + _dot(lower, lo) + run_ref[...]
    bias = cum * (-LOG2E)
    pieces = jnp.concatenate(_split3(bias), axis=1)
    bias_ref[0] = _dot(pieces, place_ref[...]).astype(BF16)
    seg = seg_ref[...]
    bound_ref[0, 0] = jnp.concatenate(
        [_head_norm_max(qs, seg), _head_norm_max(k, seg),
         jnp.max(bias, axis=0, keepdims=True), jnp.min(bias, axis=0, keepdims=True),
         jnp.zeros((SUBLANES - 4, LANES), F32)], axis=0)
    run_ref[...] = run_ref[...] + jnp.sum(lf, axis=0, keepdims=True)


def _proj_c(x, sc, sh, w, wft, bf_row, bf_col, place, seg):
    b, t, d = x.shape
    tm = _row_tile(t, ATTN_BLOCK)
    nt = t // tm
    n = w.shape[1]
    q_scale = C_HEAD_DIM ** -0.5 * LOG2E
    row = lambda width: pl.BlockSpec((1, tm, width), lambda bi, i: (bi, i, 0))
    const = lambda shape: pl.BlockSpec(shape, lambda bi, i: tuple(0 for _ in shape))
    return pl.pallas_call(
        functools.partial(_proj_c_kernel, q_scale=q_scale),
        grid=(b, nt),
        in_specs=[row(d),
                  pl.BlockSpec((1, 1, d), lambda bi, i: (bi, 0, 0)),
                  pl.BlockSpec((1, 1, d), lambda bi, i: (bi, 0, 0)),
                  const((d, n)), const((C_HEADS, d)),
                  const((1, LANES)), const((C_HEADS, 1)), const((3 * LANES, BIAS_WIDTH)), const((C_WIDTH, LANES))],
        out_specs=[row(C_WIDTH), row(C_WIDTH), row(BIAS_WIDTH),
                   pl.BlockSpec((1, 1, C_WIDTH, tm), lambda bi, i: (bi, i, 0, 0)),
                   row(C_WIDTH), row(C_WIDTH), row(C_HEADS),
                   pl.BlockSpec((1, C_HEADS, tm), lambda bi, i: (bi, 0, i)),
                   pl.BlockSpec((1, 1, SUBLANES, LANES), lambda bi, i: (bi, i, 0, 0))],
        out_shape=[jax.ShapeDtypeStruct((b, t, C_WIDTH), BF16),
                   jax.ShapeDtypeStruct((b, t, C_WIDTH), BF16),
                   jax.ShapeDtypeStruct((b, t, BIAS_WIDTH), BF16),
                   jax.ShapeDtypeStruct((b, nt, C_WIDTH, tm), BF16),
                   jax.ShapeDtypeStruct((b, t, C_WIDTH), F32),
                   jax.ShapeDtypeStruct((b, t, C_WIDTH), F32),
                   jax.ShapeDtypeStruct((b, t, C_HEADS), F32),
                   jax.ShapeDtypeStruct((b, C_HEADS, t), F32),
                   jax.ShapeDtypeStruct((b, nt, SUBLANES, LANES), F32)],
        scratch_shapes=[pltpu.VMEM((1, LANES), F32)],
        compiler_params=_params(("arbitrary", "arbitrary")),
        name="proj_c",
    )(x, sc, sh, w, wft, bf_row, bf_col, place, seg)


def _lambda(lam_ref, lam_init):
    lq1, lk1, lq2, lk2 = (lam_ref[r:r + 1, :] for r in range(4))
    return (jnp.exp(jnp.sum(lq1 * lk1, axis=1, keepdims=True))
            - jnp.exp(jnp.sum(lq2 * lk2, axis=1, keepdims=True)) + lam_init)


def _attn_kernel(*refs, mode, lam_init):
    first_ref, refs = refs[0], refs[1:]
    if mode == "A":
        q_ref, k_ref, vt_ref, lam_ref, gain_ref, o_ref, qc_ref, m_ref, acc_ref, s0_ref, s1_ref, c0_ref, c1_ref = refs
        bias_ref = None
        d_val = 2 * A_HEAD_DIM
    else:
        q_ref, k_ref, bias_ref, vt_ref, o_ref, qc_ref, m_ref, acc_ref, s0_ref, s1_ref, c0_ref, c1_ref = refs
        d_val = C_HEAD_DIM
    tq = q_ref.shape[1]
    tk = k_ref.shape[2]
    qi = pl.program_id(2)
    lane = lax.broadcasted_iota(jnp.int32, (1, LANES), 1)
    low = lane < (LANES // 2)
    q = q_ref[0]
    zero = jnp.zeros_like(q)
    for a in range(2):
        qa = jnp.where(low, q, zero) if a == 0 else jnp.where(low, zero, q)
        if bias_ref is not None:
            pick = jnp.where((lane >= 3 * a) & (lane < 3 * a + 3), 1.0, 0.0).astype(BF16)
            qa = jnp.concatenate([qa, jnp.broadcast_to(pick, (tq, LANES))], axis=1)
        qc_ref[a] = qa
    m_ref[...] = jnp.full(m_ref.shape, NEG, F32)
    acc_ref[...] = jnp.zeros(acc_ref.shape, F32)
    ones = jnp.ones((BF16_ROWS, tk), BF16)

    units = [(a, slice(n * MXU_DIM, (n + 1) * MXU_DIM)) for a in range(2) for n in range(tq // MXU_DIM)]

    def score_chain(j, u):
        a, cs = units[u]
        kc = k_ref[0, j]
        if bias_ref is not None:
            kc = jnp.concatenate([kc, bias_ref[0, j]], axis=1)
        return _dot_nt(kc, qc_ref[a, cs, :])

    def value_chain(j, u, st, cmax, key_off):
        a, cs = units[u]
        masked = key_off is not None
        if masked:
            key = lax.broadcasted_iota(jnp.int32, (tk, MXU_DIM), 0) + key_off
            qry = lax.broadcasted_iota(jnp.int32, (tk, MXU_DIM), 1) + cs.start
            keep = ((key >> CHUNK_SHIFT) <= (qry >> CHUNK_SHIFT)) if mode == "A" else (key <= qry)
            st = jnp.where(keep, st, NEG)
            cmax = jnp.max(st, axis=0, keepdims=True)
        m_prev = m_ref[a, :, cs]
        m_new = jnp.maximum(m_prev, cmax)
        alpha = jnp.exp2(m_prev - m_new)
        pt = jnp.exp2(st - m_new).astype(BF16)
        vt = vt_ref[0, j]
        va = vt if mode == "A" else vt[a * d_val:(a + 1) * d_val]
        va = jnp.concatenate([va, ones], axis=0)
        acc_ref[a, :, cs] = alpha * acc_ref[a, :, cs] + _dot(va, pt)
        m_ref[a, :, cs] = m_new

    def stage(j_scores, dst, j_values, src, key_off=None):
        for t in range(len(units) + 1):
            if j_scores is not None and t < len(units):
                st = score_chain(j_scores, t)
                dst[0][t] = st
                dst[1][t] = jnp.max(st, axis=0, keepdims=True)
            if j_values is not None and t >= 1:
                value_chain(j_values, t - 1, src[0][t - 1], src[1][t - 1], key_off)

    buf0 = (s0_ref, c0_ref)
    buf1 = (s1_ref, c1_ref)
    n_diag = tq // tk
    n_full = qi * n_diag
    j0 = first_ref[pl.program_id(0), pl.program_id(1), qi]
    n_vis = n_full - j0
    def pair(j):
        stage(j + 1, buf1, j, buf0)
        stage(j + 2, buf0, j + 1, buf1)

    def diagonal():
        for d in range(0, n_diag, 2):
            j = n_full + d
            stage(j + 1, buf1, j, buf0, key_off=d * tk)
            stage(j + 2 if d + 2 < n_diag else None, buf0, j + 1, buf1, key_off=(d + 1) * tk)

    def any_window():
        stage(j0, buf0, None, None)

        def quad(p, carry):
            pair(j0 + 4 * p)
            pair(j0 + 4 * p + 2)
            return carry

        lax.fori_loop(0, n_vis >> 2, quad, 0)

        @pl.when((n_vis & 2) == 2)
        def _():
            pair(j0 + (n_vis & ~3))

        diagonal()

    if mode == "C":
        def straight(window):
            stage(j0, buf0, None, None)
            for d in range(0, window, 2):
                pair(j0 + d)
            diagonal()

        for window in SHORT_WINDOWS:
            pl.when(n_vis == window)(functools.partial(straight, window))

        @pl.when(functools.reduce(jnp.logical_and, [n_vis != window for window in SHORT_WINDOWS]))
        def _():
            any_window()
    else:
        any_window()

    outs = []
    for a in range(2):
        acc = acc_ref[a]
        outs.append(acc[:d_val] * (1.0 / acc[d_val:d_val + 1]))
    if mode == "A":
        ot = outs[0] - _lambda(lam_ref, lam_init) * outs[1]
        ot = ot * lax.rsqrt(jnp.mean(ot * ot, axis=0, keepdims=True) + LN_EPS)
        ot = ot * (gain_ref[0] * (1.0 - lam_init))
    else:
        ot = jnp.concatenate(outs, axis=0)
    o_ref[0] = ot.T.astype(o_ref.dtype)


def _attention(q, k, vt, first, mode, lam=None, gain=None, bias=None, lam_init=0.0):
    b, t, width = q.shape
    nk, tk = vt.shape[1], vt.shape[3]
    tq = 2 * tk
    assert t % tq == 0
    groups = width // LANES
    d_aug = (2 * A_HEAD_DIM if mode == "A" else C_HEAD_DIM) + BF16_ROWS
    kspec = pl.BlockSpec((1, nk, tk, LANES), lambda bi, g, i, f: (bi, 0, 0, g))
    in_specs = [pl.BlockSpec((1, tq, LANES), lambda bi, g, i, f: (bi, i, g)), kspec]
    args = [q, k.reshape(b, nk, tk, width)]
    if mode == "C":
        in_specs.append(kspec)
        args.append(bias.reshape(b, nk, tk, groups * LANES))
    in_specs.append(pl.BlockSpec((1, nk, LANES, tk), lambda bi, g, i, f: (bi, 0, g, 0)))
    args.append(vt)
    if mode == "A":
        in_specs += [pl.BlockSpec((4, A_HEAD_DIM), lambda bi, g, i, f: (0, 0)),
                     pl.BlockSpec((1, LANES, 1), lambda bi, g, i, f: (g, 0, 0))]
        args += [lam, gain]
    return pl.pallas_call(
        functools.partial(_attn_kernel, mode=mode, lam_init=lam_init),
        grid_spec=pltpu.PrefetchScalarGridSpec(
            num_scalar_prefetch=1,
            grid=(b, groups, t // tq),
            in_specs=in_specs,
            out_specs=pl.BlockSpec((1, tq, LANES), lambda bi, g, i, f: (bi, i, g)),
            scratch_shapes=[pltpu.VMEM((2, tq, LANES if mode == "A" else 2 * LANES), BF16),
                            pltpu.VMEM((2, 1, tq), F32),
                            pltpu.VMEM((2, d_aug, tq), F32),
                            pltpu.VMEM((2 * tq // MXU_DIM, tk, MXU_DIM), F32),
                            pltpu.VMEM((2 * tq // MXU_DIM, tk, MXU_DIM), F32),
                            pltpu.VMEM((2 * tq // MXU_DIM, 1, MXU_DIM), F32),
                            pltpu.VMEM((2 * tq // MXU_DIM, 1, MXU_DIM), F32)]),
        out_shape=jax.ShapeDtypeStruct((b, t, width), BF16),
        compiler_params=_params(("arbitrary", "arbitrary", "arbitrary")),
        name="attn_" + mode,
    )(first, *args)


def _first_visible_block(bounds, tq_tiles):
    qn, kn, bmax, bmin = (bounds[:, :, r, :C_HEADS] for r in range(4))
    b, tiles, _ = qn.shape
    nq = tiles // tq_tiles
    blk = lambda x, f: f(x.reshape(b, nq, tq_tiles, C_HEADS), axis=2)
    qn_q = blk(qn, jnp.max) * NORM_SLACK
    own = blk(bmin, jnp.min) - qn_q * blk(kn, jnp.max) * NORM_SLACK
    best = qn_q[:, :, None, :] * (kn * NORM_SLACK)[:, None, :, :] + bmax[:, None, :, :]
    dead = best < (own[:, :, None, :] - SKIP_GAP)
    block = jnp.arange(tiles, dtype=jnp.int32)[None, None, :, None]
    lead = jnp.min(jnp.where(dead, tiles, block), axis=2)
    lead = jnp.min(lead.reshape(b, nq, C_HEADS // 2, 2), axis=-1)
    lead = jnp.minimum(lead, (jnp.arange(nq, dtype=jnp.int32) * tq_tiles)[None, :, None])
    return jnp.transpose(lead - lead % 2, (0, 2, 1)).astype(jnp.int32)


def _diff_finish(o0, o1, lam, gain, lam_init):
    o = o0 - lam * o1
    o = o * lax.rsqrt(jnp.mean(o * o, axis=-1, keepdims=True) + LN_EPS)
    return o * gain * (1.0 - lam_init)


def _cached_attn_kernel(*refs, mode, past, lam_init):
    if mode == "A":
        q_ref, kn_ref, vn_ref, ck_ref, cv_ref, lam_ref, gain_ref, o_ref = refs
    else:
        q_ref, kn_ref, vn_ref, ck_ref, cv_ref, clf_ref, nlf_ref, o_ref = refs
    t = q_ref.shape[1]
    width = o_ref.shape[2]
    groups = width // LANES
    lane = lax.broadcasted_iota(jnp.int32, (1, LANES), 1)
    low = lane < (LANES // 2)
    row = lax.broadcasted_iota(jnp.int32, (t, t), 0)
    col = lax.broadcasted_iota(jnp.int32, (t, t), 1)
    if mode == "A":
        keep = ((past + col) >> CHUNK_SHIFT) <= ((past + row) >> CHUNK_SHIFT)
        lam = _lambda(lam_ref, lam_init)
    else:
        keep = col <= row
        r = lax.broadcasted_iota(jnp.int32, (past, past), 0)
        c = lax.broadcasted_iota(jnp.int32, (past, past), 1)
        upper = jnp.where(r <= c, 1.0, 0.0).astype(BF16)
        clf = clf_ref[0]
        hi, mid, lo = _split3(clf)
        cum_c = (_dot(hi, upper) + _dot(mid, upper)) + _dot(lo, upper)
        upper_n = jnp.where(row <= col, 1.0, 0.0).astype(BF16)
        hi, mid, lo = _split3(nlf_ref[0])
        cum_n = ((_dot(hi, upper_n) + _dot(mid, upper_n)) + _dot(lo, upper_n)
                 + jnp.sum(clf, axis=1, keepdims=True))
        bias_c = cum_c * (-LOG2E)
        bias_n = cum_n * (-LOG2E)
    keep2 = jnp.concatenate([keep, keep], axis=0)
    first_rows = lax.broadcasted_iota(jnp.int32, (2 * t, 1), 0) < t
    for g in range(groups):
        sl = slice(g * LANES, (g + 1) * LANES)
        q = q_ref[0, :, sl]
        kn = kn_ref[0, :, sl]
        vn = vn_ref[0, :, sl].astype(BF16)
        if mode == "A":
            kc = ck_ref[0, 0, :, g, :].astype(BF16)
            vc = cv_ref[0, 0, :, g, :].astype(BF16)
        else:
            kc = ck_ref[0, :, sl].astype(BF16)
            vc = cv_ref[0, :, sl].astype(BF16)
        zero = jnp.zeros_like(q)
        q2 = jnp.concatenate([jnp.where(low, q, zero), jnp.where(low, zero, q)], axis=0)
        s_c = _dot_nt(q2, kc)
        s_n = _dot_nt(q2, kn)
        if mode == "C":
            h = 2 * g
            s_c = s_c + jnp.where(first_rows, bias_c[h:h + 1, :], bias_c[h + 1:h + 2, :])
            s_n = s_n + jnp.where(first_rows, bias_n[h:h + 1, :], bias_n[h + 1:h + 2, :])
        s_n = jnp.where(keep2, s_n, NEG)
        m = jnp.maximum(jnp.max(s_c, axis=1, keepdims=True), jnp.max(s_n, axis=1, keepdims=True))
        p_c = jnp.exp2(s_c - m)
        p_n = jnp.exp2(s_n - m)
        l = jnp.sum(p_c, axis=1, keepdims=True) + jnp.sum(p_n, axis=1, keepdims=True)
        acc = (_dot(p_c.astype(BF16), vc) + _dot(p_n.astype(BF16), vn)) * (1.0 / l)
        outs = [acc[:t], acc[t:]]
        if mode == "A":
            o = _diff_finish(outs[0], outs[1], lam, gain_ref[g], lam_init)
        else:
            o = jnp.where(low, outs[0], outs[1])
        o_ref[0, :, sl] = o.astype(o_ref.dtype)


def _cached_attention(q, kn, vn, cache_k, cache_v, mode, lam=None, gain=None, cache_lft=None, new_lft=None,
                      lam_init=0.0):
    b, t, width = q.shape
    new = pl.BlockSpec((1, t, width), lambda bi: (bi, 0, 0))
    if mode == "A":
        _, _, past, heads, dh = cache_k.shape
        old = pl.BlockSpec((1, 1, past, heads, dh), lambda bi: (0, bi, 0, 0, 0))
    else:
        past = cache_k.shape[1]
        old = pl.BlockSpec((1, past, width), lambda bi: (bi, 0, 0))
    in_specs = [new, new, new, old, old]
    args = [q, kn, vn, cache_k, cache_v]
    if mode == "A":
        in_specs += [pl.BlockSpec((4, A_HEAD_DIM), lambda bi: (0, 0)),
                     pl.BlockSpec((A_HEADS, 1, LANES), lambda bi: (0, 0, 0))]
        args += [lam, gain]
    else:
        in_specs += [pl.BlockSpec((1, C_HEADS, past), lambda bi: (bi, 0, 0)),
                     pl.BlockSpec((1, C_HEADS, t), lambda bi: (bi, 0, 0))]
        args += [cache_lft, new_lft]
    return pl.pallas_call(
        functools.partial(_cached_attn_kernel, mode=mode, past=past, lam_init=lam_init),
        grid=(b,),
        in_specs=in_specs,
        out_specs=new,
        out_shape=jax.ShapeDtypeStruct((b, t, width), BF16),
        compiler_params=_params(("arbitrary",)),
        name="cached_attn_" + mode,
    )(*args)


def _gelu_tanh(x):
    return 0.5 * x * (1.0 + jnp.tanh(math.sqrt(2.0 / math.pi) * (x + 0.044715 * (x * x * x))))


def _causal_conv(x, tail, w_ref, b_ref):
    width = w_ref.shape[0]
    tm = x.shape[0]
    cat = jnp.concatenate([tail, x], axis=0)
    y = None
    for j in range(width):
        back = width - 1 - j
        src = cat if back == 0 else pltpu.roll(cat, back, 0)
        term = src[SUBLANES:SUBLANES + tm] * w_ref[j:j + 1, :]
        y = term if y is None else y + term
    return y + b_ref[...]


def _rglru_kernel(xg_ref, h0_ref, cb_ref, cw_ref, cbias_ref, wa_ref, ba_ref, wx_ref, bx_ref, sp_ref,
                  y_ref, hl_ref, ct_ref, hc_ref, tail_ref):
    i = pl.program_id(1)
    tm = xg_ref.shape[1]

    @pl.when(i == 0)
    def _():
        hc_ref[...] = h0_ref[0]
        tail_ref[...] = cb_ref[0]

    x = xg_ref[0, :, :B_WIDTH]
    gate_in = xg_ref[0, :, B_WIDTH:]
    xc = _causal_conv(x, tail_ref[...], cw_ref, cbias_ref)
    xcb = xc.astype(BF16)
    r = _sigmoid(_dot(xcb, wa_ref[...]) + ba_ref[...])
    ig = _sigmoid(_dot(xcb, wx_ref[...]) + bx_ref[...])
    log_a = (-RG_C) * r * sp_ref[...]
    a = jnp.exp(log_a)
    th = jnp.tanh(log_a)
    bx = jnp.sqrt((-2.0 * th) / (1.0 - th)) * (ig * xc)
    pos = lax.broadcasted_iota(jnp.int32, (tm, 1), 0) & (SUBLANES - 1)
    s = 1
    while s < SUBLANES:
        valid = pos >= s
        a_sh = pltpu.roll(a, s, 0)
        b_sh = pltpu.roll(bx, s, 0)
        bx = jnp.where(valid, a * b_sh + bx, bx)
        a = jnp.where(valid, a * a_sh, a)
        s *= 2
    carry = hc_ref[...]
    groups = []
    for g in range(tm // SUBLANES):
        rows = slice(g * SUBLANES, (g + 1) * SUBLANES)
        hg = a[rows] * carry + bx[rows]
        groups.append(hg)
        carry = hg[SUBLANES - 1:, :]
    h = jnp.concatenate(groups, axis=0)
    y_ref[0] = (h * _gelu_tanh(gate_in)).astype(y_ref.dtype)
    h_tail = h[tm - SUBLANES:, :]
    x_tail = x[tm - SUBLANES:, :]
    hc_ref[...] = h_tail[SUBLANES - 1:, :]
    tail_ref[...] = x_tail
    hl_ref[0] = h_tail
    ct_ref[0] = x_tail


def _rglru(xg, h0, conv_tail, cw, cbias, wa, ba, wx, bx, sp, tm_pref=256):
    b, t, _ = xg.shape
    tm = _row_tile(t, tm_pref)
    wspec = lambda shape: pl.BlockSpec(shape, lambda bi, i: tuple(0 for _ in shape))
    return pl.pallas_call(
        _rglru_kernel,
        grid=(b, t // tm),
        in_specs=[pl.BlockSpec((1, tm, 2 * B_WIDTH), lambda bi, i: (bi, i, 0)),
                  pl.BlockSpec((1, 1, B_WIDTH), lambda bi, i: (bi, 0, 0)),
                  pl.BlockSpec((1, SUBLANES, B_WIDTH), lambda bi, i: (bi, 0, 0)),
                  wspec((B_CONV, B_WIDTH)), wspec((1, B_WIDTH)),
                  wspec((B_WIDTH, B_WIDTH)), wspec((1, B_WIDTH)),
                  wspec((B_WIDTH, B_WIDTH)), wspec((1, B_WIDTH)), wspec((1, B_WIDTH))],
        out_specs=[pl.BlockSpec((1, tm, B_WIDTH), lambda bi, i: (bi, i, 0)),
                   pl.BlockSpec((1, SUBLANES, B_WIDTH), lambda bi, i: (bi, 0, 0)),
                   pl.BlockSpec((1, SUBLANES, B_WIDTH), lambda bi, i: (bi, 0, 0))],
        out_shape=[jax.ShapeDtypeStruct((b, t, B_WIDTH), BF16),
                   jax.ShapeDtypeStruct((b, SUBLANES, B_WIDTH), F32),
                   jax.ShapeDtypeStruct((b, SUBLANES, B_WIDTH), F32)],
        scratch_shapes=[pltpu.VMEM((1, B_WIDTH), F32), pltpu.VMEM((SUBLANES, B_WIDTH), F32)],
        compiler_params=_params(("arbitrary", "arbitrary")),
        name="rglru",
    )(xg, h0, conv_tail, cw, cbias, wa, ba, wx, bx, sp)


def _proj_ln_kernel(*refs, n_in):
    h_refs = refs[:n_in]
    w_ref, x_ref, gate_ref, g_ref, b_ref, o_ref = refs[n_in:]
    tm = x_ref.shape[1]
    rows = [slice(r, min(r + LN_ROWS, tm)) for r in range(0, tm, LN_ROWS)]

    def matmul(rs):
        hs = [r[0, rs, :] for r in h_refs]
        return _dot(hs[0] if n_in == 1 else jnp.concatenate(hs, axis=-1), w_ref[0])

    def norm(rs, proj):
        y = ALPHA * x_ref[0, rs, :] + gate_ref[0] * proj
        mu = jnp.mean(y, axis=-1, keepdims=True)
        yc = y - mu
        var = jnp.mean(yc * yc, axis=-1, keepdims=True)
        o_ref[0, rs, :] = yc * lax.rsqrt(var + LN_EPS) * g_ref[...] + b_ref[...]

    pending = matmul(rows[0])
    for r in range(len(rows)):
        nxt = matmul(rows[r + 1]) if r + 1 < len(rows) else None
        norm(rows[r], pending)
        pending = nxt


def _proj_ln(hs, w, layer, x, gate, ln_g, ln_b, tm_pref=1024):
    b, t, d = x.shape
    tm = _row_tile(t, tm_pref)
    k = w.shape[1]
    gate_spec = (pl.BlockSpec((1, 1, d), lambda bi, i: (bi, 0, 0)) if gate.shape[1] == 1
                 else pl.BlockSpec((1, tm, d), lambda bi, i: (bi, i, 0)))
    in_specs = [pl.BlockSpec((1, tm, h.shape[2]), lambda bi, i: (bi, i, 0)) for h in hs]
    in_specs += [pl.BlockSpec((1, k, d), lambda bi, i: (layer, 0, 0)),
                 pl.BlockSpec((1, tm, d), lambda bi, i: (bi, i, 0)),
                 gate_spec,
                 pl.BlockSpec((1, d), lambda bi, i: (0, 0)),
                 pl.BlockSpec((1, d), lambda bi, i: (0, 0))]
    return pl.pallas_call(
        functools.partial(_proj_ln_kernel, n_in=len(hs)),
        grid=(b, t // tm),
        in_specs=in_specs,
        out_specs=pl.BlockSpec((1, tm, d), lambda bi, i: (bi, i, 0)),
        out_shape=jax.ShapeDtypeStruct((b, t, d), F32),
        compiler_params=_params(("arbitrary", "arbitrary")),
        name="proj_ln",
    )(*hs, w, x, gate, ln_g, ln_b)


def _ffn_up_kernel(x_ref, sc_ref, sh_ref, w_ref, cb_ref, cw_ref, cbias_ref, h_ref, ct_ref, tail_ref):
    i = pl.program_id(1)
    tm = x_ref.shape[1]

    @pl.when(i == 0)
    def _():
        tail_ref[...] = cb_ref[0]

    u = _modulate(x_ref[0], sc_ref[0], sh_ref[0])
    chunks = [slice(c, min(c + FFN_CHUNK, D_FF)) for c in range(0, D_FF, FFN_CHUNK)]

    def matmuls(cs):
        gs = slice(D_FF + cs.start, D_FF + cs.stop)
        return _dot(u, w_ref[0, :, cs]), _dot(u, w_ref[0, :, gs])

    def gate(cs, a, g):
        gc = _causal_conv(g, tail_ref[:, cs], cw_ref.at[:, cs], cbias_ref.at[:, cs])
        silu = gc * (0.5 * jnp.tanh(0.5 * gc) + 0.5)
        h_ref[0, :, cs] = (a * silu).astype(h_ref.dtype)
        g_tail = g[tm - SUBLANES:, :]
        tail_ref[:, cs] = g_tail
        ct_ref[0, :, cs] = g_tail

    pending = matmuls(chunks[0])
    for c in range(len(chunks)):
        nxt = matmuls(chunks[c + 1]) if c + 1 < len(chunks) else None
        gate(chunks[c], *pending)
        pending = nxt


def _ffn_up(x, sc, sh, w, layer, conv_tail, cw, cbias, tm_pref=256):
    b, t, d = x.shape
    tm = _row_tile(t, tm_pref)
    return pl.pallas_call(
        _ffn_up_kernel,
        grid=(b, t // tm),
        in_specs=[pl.BlockSpec((1, tm, d), lambda bi, i: (bi, i, 0)),
                  pl.BlockSpec((1, 1, d), lambda bi, i: (bi, 0, 0)),
                  pl.BlockSpec((1, 1, d), lambda bi, i: (bi, 0, 0)),
                  pl.BlockSpec((1, d, 2 * D_FF), lambda bi, i: (layer, 0, 0)),
                  pl.BlockSpec((1, SUBLANES, D_FF), lambda bi, i: (bi, 0, 0)),
                  pl.BlockSpec((FFN_CONV, D_FF), lambda bi, i: (0, 0)),
                  pl.BlockSpec((1, D_FF), lambda bi, i: (0, 0))],
        out_specs=[pl.BlockSpec((1, tm, D_FF), lambda bi, i: (bi, i, 0)),
                   pl.BlockSpec((1, SUBLANES, D_FF), lambda bi, i: (bi, 0, 0))],
        out_shape=[jax.ShapeDtypeStruct((b, t, D_FF), BF16),
                   jax.ShapeDtypeStruct((b, SUBLANES, D_FF), F32)],
        scratch_shapes=[pltpu.VMEM((SUBLANES, D_FF), F32)],
        compiler_params=_params(("arbitrary", "arbitrary")),
        name="ffn_up",
    )(x, sc, sh, w, conv_tail, cw, cbias)


def _pad_tail(buf):
    return jnp.pad(buf, ((0, 0), (SUBLANES - buf.shape[1], 0), (0, 0)))


def _rope_tables(past, t):
    half = A_HEAD_DIM // 2
    inv = ROPE_THETA ** (-jnp.arange(0, A_HEAD_DIM, 2, dtype=F32) / A_HEAD_DIM)
    pos = (past + jnp.arange(t, dtype=jnp.int32)).astype(F32)
    ang = pos[:, None] * inv[None, :]
    cos = jnp.tile(jnp.cos(ang), (1, LANES // half))
    sin = jnp.sin(ang)
    sin_signed = jnp.tile(jnp.concatenate([-sin, sin], axis=1), (1, LANES // A_HEAD_DIM))
    return cos, sin_signed


def _block_diag(w):
    n, i, o = w.shape
    return jnp.einsum("nio,nm->nimo", w, jnp.eye(n, dtype=w.dtype)).reshape(n * i, n * o)


def _prepare(p):
    w = {}
    w_in_ab = p["w_in_ab"][0]
    w["in_ab"] = w_in_ab.astype(BF16)
    w["lam"] = jnp.stack([p["lam_q1"][0], p["lam_k1"][0], p["lam_q2"][0], p["lam_k2"][0]])
    w["gain_row"] = p["attn_gain"][0].reshape(A_HEADS, 1, LANES)
    w["gain_col"] = p["attn_gain"][0].reshape(A_HEADS, LANES, 1)
    w["b_conv_w"] = p["b_conv_w"][0]
    w["b_conv_b"] = p["b_conv_b"][0].reshape(1, B_WIDTH)
    w["rg_a"] = _block_diag(p["w_rg_a"][0]).astype(BF16)
    w["rg_x"] = _block_diag(p["w_rg_x"][0]).astype(BF16)
    w["b_rg_a"] = p["b_rg_a"][0].reshape(1, B_WIDTH)
    w["b_rg_x"] = p["b_rg_x"][0].reshape(1, B_WIDTH)
    w["rg_L"] = p["rg_L"][0].reshape(1, B_WIDTH)
    w_in_c = p["w_in_c"][0]
    w["in_c"] = jnp.pad(w_in_c, ((0, 0), (0, LANES - C_HEADS))).astype(BF16)
    w["in_c_ft"] = w_in_c[:, 3 * C_WIDTH:].T.astype(BF16)
    w["bf_row"] = jnp.pad(p["b_f"][0], (0, LANES - C_HEADS)).reshape(1, LANES)
    w["bf_col"] = p["b_f"][0].reshape(C_HEADS, 1)
    w["place"] = _bias_placement()
    w["seg"] = jnp.asarray(np.repeat(np.eye(LANES, dtype=np.float32)[:C_HEADS], C_HEAD_DIM, axis=0), BF16)
    w["up"] = p["w_up"].astype(BF16)
    w["down"] = p["w_down"].astype(BF16)
    w["out_ab"] = p["w_out_ab"].astype(BF16)
    w["out_c"] = p["w_out_c"].astype(BF16)
    w["ffn_conv_w"] = [p["ffn_conv_w"][i] for i in range(DEPTH)]
    w["ffn_conv_b"] = [p["ffn_conv_b"][i].reshape(1, D_FF) for i in range(DEPTH)]
    w["ln1_g"] = [p["ln1_g"][i].reshape(1, D_MODEL) for i in range(DEPTH)]
    w["ln1_b"] = [p["ln1_b"][i].reshape(1, D_MODEL) for i in range(DEPTH)]
    w["ln2_g"] = [p["ln2_g"][i].reshape(1, D_MODEL) for i in range(DEPTH)]
    w["ln2_b"] = [p["ln2_b"][i].reshape(1, D_MODEL) for i in range(DEPTH)]
    return w


def _softplus_kernel(x_ref, o_ref):
    o_ref[...] = _softplus(-x_ref[...])


def _proj_ln_rows(hs, w, layer, x, gate, ln_g, ln_b):
    b, t, d = x.shape
    flat = lambda a: a.reshape(1, b * t, a.shape[2])
    gate_rows = jnp.broadcast_to(gate, (b, t, d))
    return _proj_ln([flat(h) for h in hs], w, layer, flat(x), flat(gate_rows), ln_g, ln_b).reshape(b, t, d)


def _trunk(x, mods, w, sp, cache_a_k=None, cache_a_v=None, state_b_h=None, state_b_conv=None,
           cache_c_k=None, cache_c_v=None, cache_c_logf=None, state_ffn_conv=None):
    b, t, d = x.shape
    cached = cache_a_k is not None
    proj_ln = _proj_ln_rows if cached else _proj_ln
    past = cache_a_k.shape[2] if cached else 0
    outs = {}
    for i in range(DEPTH):
        sh1, sc1, g1, sh2, sc2, g2 = [m[:, None, :] for m in jnp.split(mods[i], 6, axis=-1)]
        if i % 2 == 0:
            lam_init = 0.8 - 0.6 * math.exp(-0.3 * i)
            cos, sin = _rope_tables(past, t)
            q, kb, vt, k32, v32, xg = _proj_ab(x, sc1, sh1, w["in_ab"], cos, sin)
            if cached:
                o = _cached_attention(q, kb, v32.reshape(b, t, A_WIDTH), cache_a_k, cache_a_v,
                                      "A", lam=w["lam"], gain=w["gain_row"], lam_init=lam_init)
                h0 = state_b_h[0][:, None, :]
                ctail = _pad_tail(state_b_conv[0])
            else:
                first = jnp.zeros((b, A_HEADS, t // (2 * ATTN_BLOCK)), jnp.int32)
                o = _attention(q, kb, vt, first, "A", lam=w["lam"], gain=w["gain_col"], lam_init=lam_init)
                h0 = jnp.zeros((b, 1, B_WIDTH), F32)
                ctail = jnp.zeros((b, SUBLANES, B_WIDTH), F32)
            yb, h_tail, x_tail = _rglru(xg, h0, ctail, w["b_conv_w"], w["b_conv_b"], w["rg_a"], w["b_rg_a"],
                                        w["rg_x"], w["b_rg_x"], sp)
            outs["a_k"] = k32[None]
            outs["a_v"] = v32[None]
            outs["b_h"] = h_tail[:, SUBLANES - 1, :][None]
            outs["b_conv"] = x_tail[:, SUBLANES - (B_CONV - 1):, :][None]
            x = proj_ln([o, yb], w["out_ab"], i // 2, x, g1, w["ln1_g"][i], w["ln1_b"][i])
        else:
            q, kb, bias, vt, k32, v32, lf, lft, bounds = _proj_c(x, sc1, sh1, w["in_c"], w["in_c_ft"],
                                                                 w["bf_row"], w["bf_col"], w["place"], w["seg"])
            if cached:
                o = _cached_attention(q, kb, v32, cache_c_k[0].reshape(b, past, C_WIDTH),
                                      cache_c_v[0].reshape(b, past, C_WIDTH),
                                      "C", cache_lft=jnp.swapaxes(cache_c_logf[0], 1, 2), new_lft=lft)
            else:
                o = _attention(q, kb, vt, _first_visible_block(bounds, 2), "C", bias=bias)
            outs["c_k"] = k32.reshape(1, b, t, C_HEADS, C_HEAD_DIM)
            outs["c_v"] = v32.reshape(1, b, t, C_HEADS, C_HEAD_DIM)
            outs["c_logf"] = lf[None]
            x = proj_ln([o], w["out_c"], i // 2, x, g1, w["ln1_g"][i], w["ln1_b"][i])
        ftail = _pad_tail(state_ffn_conv[i]) if cached else jnp.zeros((b, SUBLANES, D_FF), F32)
        hmid, g_tail = _ffn_up(x, sc2, sh2, w["up"], i, ftail, w["ffn_conv_w"][i], w["ffn_conv_b"][i])
        outs.setdefault("ffn", []).append(g_tail[:, SUBLANES - (FFN_CONV - 1):, :])
        x = proj_ln([hmid], w["down"], i, x, g2, w["ln2_g"][i], w["ln2_b"][i])
    return (x, outs["a_k"], outs["a_v"], outs["b_h"], outs["b_conv"],
            outs["c_k"], outs["c_v"], outs["c_logf"], jnp.stack(outs["ffn"]))


def kernel(x_prompt, x_sample, c_prompt, c_sample, cache_a_k, cache_a_v, state_b_h, state_b_conv, cache_c_k, cache_c_v, cache_c_logf, state_ffn_conv, w_ada, b_ada, ln1_g, ln1_b, ln2_g, ln2_b, w_in_ab, lam_q1, lam_k1, lam_q2, lam_k2, attn_gain, b_conv_w, b_conv_b, w_rg_a, b_rg_a, w_rg_x, b_rg_x, rg_L, w_out_ab, w_in_c, b_f, w_out_c, w_up, ffn_conv_w, ffn_conv_b, w_down):
    p = dict(w_in_ab=w_in_ab, lam_q1=lam_q1, lam_k1=lam_k1, lam_q2=lam_q2, lam_k2=lam_k2, attn_gain=attn_gain,
             b_conv_w=b_conv_w, b_conv_b=b_conv_b, w_rg_a=w_rg_a, b_rg_a=b_rg_a, w_rg_x=w_rg_x, b_rg_x=b_rg_x,
             rg_L=rg_L, w_out_ab=w_out_ab, w_in_c=w_in_c, b_f=b_f, w_out_c=w_out_c, w_up=w_up,
             ffn_conv_w=ffn_conv_w, ffn_conv_b=ffn_conv_b, w_down=w_down,
             ln1_g=ln1_g, ln1_b=ln1_b, ln2_g=ln2_g, ln2_b=ln2_b)
    w = _prepare(p)
    bp = c_prompt.shape[0]
    bs = c_sample.shape[0]
    rows = -(-(bp + bs) // 16) * 16
    c_all = jnp.pad(jnp.concatenate([c_prompt, c_sample], axis=0), ((0, rows - bp - bs), (0, 0)))
    mods = _mods(c_all, w_ada, b_ada)
    sp = pl.pallas_call(_softplus_kernel, out_shape=jax.ShapeDtypeStruct((1, B_WIDTH), F32),
                        name="softplus")(w["rg_L"])
    res_p = _trunk(x_prompt, mods[:, :bp], w, sp)
    res_s = _trunk(x_sample, mods[:, bp:bp + bs], w, sp, cache_a_k, cache_a_v, state_b_h, state_b_conv,
                   cache_c_k, cache_c_v, cache_c_logf, state_ffn_conv)
    return (res_p[0], res_s[0]) + res_p[1:] + res_s[1:]
```

```python
import functools
import math

import numpy as np
import jax
import jax.numpy as jnp
from jax import lax
from jax.experimental import pallas as pl
from jax.experimental.pallas import tpu as pltpu

F32 = jnp.float32
BF16 = jnp.bfloat16

D_MODEL = 1024
DEPTH = 2
CHUNK = 64
CHUNK_SHIFT = 6
A_HEADS = 4
A_HEAD_DIM = 64
A_WIDTH = A_HEADS * 2 * A_HEAD_DIM
B_WIDTH = 512
B_BLOCKS = 8
B_CONV = 4
RG_C = 8.0
C_HEADS = 16
C_HEAD_DIM = 64
C_WIDTH = C_HEADS * C_HEAD_DIM
D_FF = 2816
FFN_CONV = 3
ROPE_THETA = 10000.0
ALPHA = (2 * DEPTH) ** 0.25
LN_EPS = 1e-5
NEG = -1e30
LOG2E = 1.4426950408889634

LANES = 128
SUBLANES = 8
BF16_ROWS = 16
MXU_DIM = 256
VMEM_LIMIT = 56 * 1024 * 1024
BIAS_WIDTH = (C_HEADS // 2) * LANES
FFN_CHUNK = 256
LN_ROWS = 256
SKIP_GAP = 160.0
NORM_SLACK = 1.03
ATTN_BLOCK = 512


def _params(sem, flags=None):
    return pltpu.CompilerParams(dimension_semantics=sem, vmem_limit_bytes=VMEM_LIMIT, flags=flags)


def _row_tile(t, pref):
    if t <= pref:
        return t
    tm = pref
    while t % tm:
        tm //= 2
    return tm


def _modulate(x, sc, sh):
    return (x * (1.0 + sc) + sh).astype(BF16)


def _sigmoid(x):
    return 1.0 / (1.0 + jnp.exp(-x))


def _softplus(x):
    return jnp.maximum(x, 0.0) + jnp.log1p(jnp.exp(-jnp.abs(x)))


def _log_sigmoid(x):
    return jnp.minimum(x, 0.0) - jnp.log1p(jnp.exp(-jnp.abs(x)))


def _split3(x):
    hi = x.astype(BF16)
    r1 = x - hi.astype(F32)
    mid = r1.astype(BF16)
    lo = (r1 - mid.astype(F32)).astype(BF16)
    return hi, mid, lo


def _dot(a, b):
    return jnp.dot(a, b, preferred_element_type=F32)


def _dot_nt(a, b):
    return lax.dot_general(a, b, (((1,), (1,)), ((), ())), preferred_element_type=F32)


def _mods_kernel(c_ref, w_ref, b_ref, o_ref):
    c = c_ref[...]
    s = (c * _sigmoid(c)).astype(BF16)
    o_ref[0] = _dot(s, w_ref[0].astype(BF16)) + b_ref[0]


def _mods(c_all, w_ada, b_ada):
    rows, d = c_all.shape
    n = w_ada.shape[-1]
    tn = 1536
    return pl.pallas_call(
        _mods_kernel,
        grid=(DEPTH, n // tn),
        in_specs=[pl.BlockSpec((rows, d), lambda l, j: (0, 0)),
                  pl.BlockSpec((1, d, tn), lambda l, j: (l, 0, j)),
                  pl.BlockSpec((1, 1, tn), lambda l, j: (l, 0, j))],
        out_specs=pl.BlockSpec((1, rows, tn), lambda l, j: (l, 0, j)),
        out_shape=jax.ShapeDtypeStruct((DEPTH, rows, n), F32),
        compiler_params=_params(("arbitrary", "arbitrary")),
        name="mods",
    )(c_all, w_ada, b_ada.reshape(DEPTH, 1, n))


def _rope_slab(x, cos, sin_signed, first_half):
    fwd = pltpu.roll(x, LANES - A_HEAD_DIM // 2, 1)
    bwd = pltpu.roll(x, A_HEAD_DIM // 2, 1)
    partner = jnp.where(first_half, fwd, bwd)
    return x * cos + partner * sin_signed


def _proj_ab_kernel(x_ref, sc_ref, sh_ref, w_ref, cos_ref, sin_ref,
                    q_ref, kb_ref, vt_ref, k_ref, v_ref, xg_ref, *, q_scale):
    u = _modulate(x_ref[0], sc_ref[0], sh_ref[0])
    pr = _dot(u, w_ref[...])
    cos = cos_ref[...]
    sin = sin_ref[...]
    lane = lax.broadcasted_iota(jnp.int32, (1, LANES), 1)
    first_half = (lane & (A_HEAD_DIM - 1)) < (A_HEAD_DIM // 2)
    for h in range(A_HEADS):
        sl = slice(h * LANES, (h + 1) * LANES)
        q = _rope_slab(pr[:, sl], cos, sin, first_half)
        q_ref[0, :, sl] = (q * q_scale).astype(BF16)
        k = _rope_slab(pr[:, A_WIDTH + h * LANES:A_WIDTH + (h + 1) * LANES], cos, sin, first_half)
        k_ref[0, :, h, :] = k
        kb_ref[0, :, sl] = k.astype(BF16)
    for h in range(A_HEADS):
        v_ref[0, :, h, :] = pr[:, 2 * A_WIDTH + h * LANES:2 * A_WIDTH + (h + 1) * LANES]
    vt_ref[0, 0] = pr[:, 2 * A_WIDTH:3 * A_WIDTH].T.astype(BF16)
    xg_ref[0] = pr[:, 3 * A_WIDTH:]


def _proj_ab(x, sc, sh, w, cos, sin):
    b, t, d = x.shape
    tm = _row_tile(t, ATTN_BLOCK)
    nt = t // tm
    n = w.shape[1]
    q_scale = A_HEAD_DIM ** -0.5 * LOG2E
    row = lambda width: pl.BlockSpec((1, tm, width), lambda bi, i: (bi, i, 0))
    const = lambda shape: pl.BlockSpec(shape, lambda bi, i: tuple(0 for _ in shape))
    heads = pl.BlockSpec((1, tm, A_HEADS, LANES), lambda bi, i: (bi, i, 0, 0))
    return pl.pallas_call(
        functools.partial(_proj_ab_kernel, q_scale=q_scale),
        grid=(b, nt),
        in_specs=[row(d),
                  pl.BlockSpec((1, 1, d), lambda bi, i: (bi, 0, 0)),
                  pl.BlockSpec((1, 1, d), lambda bi, i: (bi, 0, 0)),
                  const((d, n)),
                  pl.BlockSpec((tm, LANES), lambda bi, i: (i, 0)),
                  pl.BlockSpec((tm, LANES), lambda bi, i: (i, 0))],
        out_specs=[row(A_WIDTH), row(A_WIDTH),
                   pl.BlockSpec((1, 1, A_WIDTH, tm), lambda bi, i: (bi, i, 0, 0)),
                   heads, heads, row(2 * B_WIDTH)],
        out_shape=[jax.ShapeDtypeStruct((b, t, A_WIDTH), BF16),
                   jax.ShapeDtypeStruct((b, t, A_WIDTH), BF16),
                   jax.ShapeDtypeStruct((b, nt, A_WIDTH, tm), BF16),
                   jax.ShapeDtypeStruct((b, t, A_HEADS, LANES), F32),
                   jax.ShapeDtypeStruct((b, t, A_HEADS, LANES), F32),
                   jax.ShapeDtypeStruct((b, t, 2 * B_WIDTH), F32)],
        compiler_params=_params(("arbitrary", "arbitrary")),
        name="proj_ab",
    )(x, sc, sh, w, cos, sin)


def _bias_placement():
    e = np.zeros((3 * LANES, BIAS_WIDTH), np.float32)
    for piece in range(3):
        for h in range(C_HEADS):
            e[piece * LANES + h, (h // 2) * LANES + 3 * (h % 2) + piece] = 1.0
    return jnp.asarray(e, BF16)


def _head_norm_max(x, seg):
    n2 = _dot((x * x).astype(BF16), seg)
    return jnp.sqrt(jnp.max(n2, axis=0, keepdims=True))


def _proj_c_kernel(x_ref, sc_ref, sh_ref, w_ref, wft_ref, bfr_ref, bfc_ref, place_ref, seg_ref,
                   q_ref, kb_ref, bias_ref, vt_ref, k_ref, v_ref, lf_ref, lft_ref, bound_ref, run_ref, *, q_scale):
    i = pl.program_id(1)
    tm = x_ref.shape[1]

    @pl.when(i == 0)
    def _():
        run_ref[...] = jnp.zeros_like(run_ref)

    u = _modulate(x_ref[0], sc_ref[0], sh_ref[0])
    pr = _dot(u, w_ref[...])
    qs = pr[:, :C_WIDTH] * q_scale
    q_ref[0] = qs.astype(BF16)
    k = pr[:, C_WIDTH:2 * C_WIDTH]
    k_ref[0] = k
    kb_ref[0] = k.astype(BF16)
    v = pr[:, 2 * C_WIDTH:3 * C_WIDTH]
    v_ref[0] = v
    vt_ref[0, 0] = v.T.astype(BF16)
    lf = _log_sigmoid(pr[:, 3 * C_WIDTH:] + bfr_ref[...])
    lf_ref[0] = lf[:, :C_HEADS]
    lft_ref[0] = _log_sigmoid(_dot_nt(wft_ref[...], u) + bfc_ref[...])
    r = lax.broadcasted_iota(jnp.int32, (tm, tm), 0)
    c = lax.broadcasted_iota(jnp.int32, (tm, tm), 1)
    lower = jnp.where(c <= r, 1.0, 0.0).astype(BF16)
    hi, mid, lo = _split3(lf)
    cum = (_dot(lower, hi) + _dot(lower, mid)) + _dot(lower, lo) + run_ref[...]
    bias = cum * (-LOG2E)
    pieces = jnp.concatenate(_split3(bias), axis=1)
    bias_ref[0] = _dot(pieces, place_ref[...]).astype(BF16)
    seg = seg_ref[...]
    bound_ref[0, 0] = jnp.concatenate(
        [_head_norm_max(qs, seg), _head_norm_max(k, seg),
         jnp.max(bias, axis=0, keepdims=True), jnp.min(bias, axis=0, keepdims=True),
         jnp.zeros((SUBLANES - 4, LANES), F32)], axis=0)
    run_ref[...] = run_ref[...] + jnp.sum(lf, axis=0, keepdims=True)


def _proj_c(x, sc, sh, w, wft, bf_row, bf_col, place, seg):
    b, t, d = x.shape
    tm = _row_tile(t, ATTN_BLOCK)
    nt = t // tm
    n = w.shape[1]
    q_scale = C_HEAD_DIM ** -0.5 * LOG2E
    row = lambda width: pl.BlockSpec((1, tm, width), lambda bi, i: (bi, i, 0))
    const = lambda shape: pl.BlockSpec(shape, lambda bi, i: tuple(0 for _ in shape))
    return pl.pallas_call(
        functools.partial(_proj_c_kernel, q_scale=q_scale),
        grid=(b, nt),
        in_specs=[row(d),
                  pl.BlockSpec((1, 1, d), lambda bi, i: (bi, 0, 0)),
                  pl.BlockSpec((1, 1, d), lambda bi, i: (bi, 0, 0)),
                  const((d, n)), const((C_HEADS, d)),
                  const((1, LANES)), const((C_HEADS, 1)), const((3 * LANES, BIAS_WIDTH)), const((C_WIDTH, LANES))],
        out_specs=[row(C_WIDTH), row(C_WIDTH), row(BIAS_WIDTH),
                   pl.BlockSpec((1, 1, C_WIDTH, tm), lambda bi, i: (bi, i, 0, 0)),
                   row(C_WIDTH), row(C_WIDTH), row(C_HEADS),
                   pl.BlockSpec((1, C_HEADS, tm), lambda bi, i: (bi, 0, i)),
                   pl.BlockSpec((1, 1, SUBLANES, LANES), lambda bi, i: (bi, i, 0, 0))],
        out_shape=[jax.ShapeDtypeStruct((b, t, C_WIDTH), BF16),
                   jax.ShapeDtypeStruct((b, t, C_WIDTH), BF16),
                   jax.ShapeDtypeStruct((b, t, BIAS_WIDTH), BF16),
                   jax.ShapeDtypeStruct((b, nt, C_WIDTH, tm), BF16),
                   jax.ShapeDtypeStruct((b, t, C_WIDTH), F32),
                   jax.ShapeDtypeStruct((b, t, C_WIDTH), F32),
                   jax.ShapeDtypeStruct((b, t, C_HEADS), F32),
                   jax.ShapeDtypeStruct((b, C_HEADS, t), F32),
                   jax.ShapeDtypeStruct((b, nt, SUBLANES, LANES), F32)],
        scratch_shapes=[pltpu.VMEM((1, LANES), F32)],
        compiler_params=_params(("arbitrary", "arbitrary")),
        name="proj_c",
    )(x, sc, sh, w, wft, bf_row, bf_col, place, seg)


def _lambda(lam_ref, lam_init):
    lq1, lk1, lq2, lk2 = (lam_ref[r:r + 1, :] for r in range(4))
    return (jnp.exp(jnp.sum(lq1 * lk1, axis=1, keepdims=True))
            - jnp.exp(jnp.sum(lq2 * lk2, axis=1, keepdims=True)) + lam_init)


def _attn_kernel(*refs, mode, lam_init):
    first_ref, refs = refs[0], refs[1:]
    if mode == "A":
        q_ref, k_ref, vt_ref, lam_ref, gain_ref, o_ref, qc_ref, m_ref, acc_ref, s0_ref, s1_ref, c0_ref, c1_ref = refs
        bias_ref = None
        d_val = 2 * A_HEAD_DIM
    else:
        q_ref, k_ref, bias_ref, vt_ref, o_ref, qc_ref, m_ref, acc_ref, s0_ref, s1_ref, c0_ref, c1_ref = refs
        d_val = C_HEAD_DIM
    tq = q_ref.shape[1]
    tk = k_ref.shape[2]
    qi = pl.program_id(2)
    lane = lax.broadcasted_iota(jnp.int32, (1, LANES), 1)
    low = lane < (LANES // 2)
    q = q_ref[0]
    zero = jnp.zeros_like(q)
    for a in range(2):
        qa = jnp.where(low, q, zero) if a == 0 else jnp.where(low, zero, q)
        if bias_ref is not None:
            pick = jnp.where((lane >= 3 * a) & (lane < 3 * a + 3), 1.0, 0.0).astype(BF16)
            qa = jnp.concatenate([qa, jnp.broadcast_to(pick, (tq, LANES))], axis=1)
        qc_ref[a] = qa
    m_ref[...] = jnp.full(m_ref.shape, NEG, F32)
    acc_ref[...] = jnp.zeros(acc_ref.shape, F32)
    ones = jnp.ones((BF16_ROWS, tk), BF16)

    units = [(a, slice(n * MXU_DIM, (n + 1) * MXU_DIM)) for a in range(2) for n in range(tq // MXU_DIM)]

    def score_chain(j, u):
        a, cs = units[u]
        kc = k_ref[0, j]
        if bias_ref is not None:
            kc = jnp.concatenate([kc, bias_ref[0, j]], axis=1)
        return _dot_nt(kc, qc_ref[a, cs, :])

    def value_chain(j, u, st, cmax, key_off):
        a, cs = units[u]
        masked = key_off is not None
        if masked:
            key = lax.broadcasted_iota(jnp.int32, (tk, MXU_DIM), 0) + key_off
            qry = lax.broadcasted_iota(jnp.int32, (tk, MXU_DIM), 1) + cs.start
            keep = ((key >> CHUNK_SHIFT) <= (qry >> CHUNK_SHIFT)) if mode == "A" else (key <= qry)
            st = jnp.where(keep, st, NEG)
            cmax = jnp.max(st, axis=0, keepdims=True)
        m_prev = m_ref[a, :, cs]
        m_new = jnp.maximum(m_prev, cmax)
        alpha = jnp.exp2(m_prev - m_new)
        pt = jnp.exp2(st - m_new).astype(BF16)
        vt = vt_ref[0, j]
        va = vt if mode == "A" else vt[a * d_val:(a + 1) * d_val]
        va = jnp.concatenate([va, ones], axis=0)
        acc_ref[a, :, cs] = alpha * acc_ref[a, :, cs] + _dot(va, pt)
        m_ref[a, :, cs] = m_new

    def stage(j_scores, dst, j_values, src, key_off=None):
        for t in range(len(units) + 1):
            if j_scores is not None and t < len(units):
                st = score_chain(j_scores, t)
                dst[0][t] = st
                dst[1][t] = jnp.max(st, axis=0, keepdims=True)
            if j_values is not None and t >= 1:
                value_chain(j_values, t - 1, src[0][t - 1], src[1][t - 1], key_off)

    buf0 = (s0_ref, c0_ref)
    buf1 = (s1_ref, c1_ref)
    n_diag = tq // tk
    n_full = qi * n_diag
    j0 = first_ref[pl.program_id(0), pl.program_id(1), qi]
    n_vis = n_full - j0
    stage(j0, buf0, None, None)

    def pair(j):
        stage(j + 1, buf1, j, buf0)
        stage(j + 2, buf0, j + 1, buf1)

    def quad(p, carry):
        pair(j0 + 4 * p)
        pair(j0 + 4 * p + 2)
        return carry

    lax.fori_loop(0, n_vis >> 2, quad, 0)

    @pl.when((n_vis & 2) == 2)
    def _():
        pair(j0 + (n_vis & ~3))

    for d in range(0, n_diag, 2):
        j = n_full + d
        stage(j + 1, buf1, j, buf0, key_off=d * tk)
        stage(j + 2 if d + 2 < n_diag else None, buf0, j + 1, buf1, key_off=(d + 1) * tk)

    outs = []
    for a in range(2):
        acc = acc_ref[a]
        outs.append(acc[:d_val] * (1.0 / acc[d_val:d_val + 1]))
    if mode == "A":
        ot = outs[0] - _lambda(lam_ref, lam_init) * outs[1]
        ot = ot * lax.rsqrt(jnp.mean(ot * ot, axis=0, keepdims=True) + LN_EPS)
        ot = ot * (gain_ref[0] * (1.0 - lam_init))
    else:
        ot = jnp.concatenate(outs, axis=0)
    o_ref[0] = ot.T.astype(o_ref.dtype)


def _attention(q, k, vt, first, mode, lam=None, gain=None, bias=None, lam_init=0.0):
    b, t, width = q.shape
    nk, tk = vt.shape[1], vt.shape[3]
    tq = 2 * tk
    assert t % tq == 0
    groups = width // LANES
    d_aug = (2 * A_HEAD_DIM if mode == "A" else C_HEAD_DIM) + BF16_ROWS
    kspec = pl.BlockSpec((1, nk, tk, LANES), lambda bi, g, i, f: (bi, 0, 0, g))
    in_specs = [pl.BlockSpec((1, tq, LANES), lambda bi, g, i, f: (bi, i, g)), kspec]
    args = [q, k.reshape(b, nk, tk, width)]
    if mode == "C":
        in_specs.append(kspec)
        args.append(bias.reshape(b, nk, tk, groups * LANES))
    in_specs.append(pl.BlockSpec((1, nk, LANES, tk), lambda bi, g, i, f: (bi, 0, g, 0)))
    args.append(vt)
    if mode == "A":
        in_specs += [pl.BlockSpec((4, A_HEAD_DIM), lambda bi, g, i, f: (0, 0)),
                     pl.BlockSpec((1, LANES, 1), lambda bi, g, i, f: (g, 0, 0))]
        args += [lam, gain]
    return pl.pallas_call(
        functools.partial(_attn_kernel, mode=mode, lam_init=lam_init),
        grid_spec=pltpu.PrefetchScalarGridSpec(
            num_scalar_prefetch=1,
            grid=(b, groups, t // tq),
            in_specs=in_specs,
            out_specs=pl.BlockSpec((1, tq, LANES), lambda bi, g, i, f: (bi, i, g)),
            scratch_shapes=[pltpu.VMEM((2, tq, LANES if mode == "A" else 2 * LANES), BF16),
                            pltpu.VMEM((2, 1, tq), F32),
                            pltpu.VMEM((2, d_aug, tq), F32),
                            pltpu.VMEM((2 * tq // MXU_DIM, tk, MXU_DIM), F32),
                            pltpu.VMEM((2 * tq // MXU_DIM, tk, MXU_DIM), F32),
                            pltpu.VMEM((2 * tq // MXU_DIM, 1, MXU_DIM), F32),
                            pltpu.VMEM((2 * tq // MXU_DIM, 1, MXU_DIM), F32)]),
        out_shape=jax.ShapeDtypeStruct((b, t, width), BF16),
        compiler_params=_params(("arbitrary", "arbitrary", "arbitrary")),
        name="attn_" + mode,
    )(first, *args)


def _first_visible_block(bounds, tq_tiles):
    qn, kn, bmax, bmin = (bounds[:, :, r, :C_HEADS] for r in range(4))
    b, tiles, _ = qn.shape
    nq = tiles // tq_tiles
    blk = lambda x, f: f(x.reshape(b, nq, tq_tiles, C_HEADS), axis=2)
    qn_q = blk(qn, jnp.max) * NORM_SLACK
    own = blk(bmin, jnp.min) - qn_q * blk(kn, jnp.max) * NORM_SLACK
    best = qn_q[:, :, None, :] * (kn * NORM_SLACK)[:, None, :, :] + bmax[:, None, :, :]
    dead = best < (own[:, :, None, :] - SKIP_GAP)
    block = jnp.arange(tiles, dtype=jnp.int32)[None, None, :, None]
    lead = jnp.min(jnp.where(dead, tiles, block), axis=2)
    lead = jnp.min(lead.reshape(b, nq, C_HEADS // 2, 2), axis=-1)
    lead = jnp.minimum(lead, (jnp.arange(nq, dtype=jnp.int32) * tq_tiles)[None, :, None])
    return jnp.transpose(lead - lead % 2, (0, 2, 1)).astype(jnp.int32)


def _diff_finish(o0, o1, lam, gain, lam_init):
    o = o0 - lam * o1
    o = o * lax.rsqrt(jnp.mean(o * o, axis=-1, keepdims=True) + LN_EPS)
    return o * gain * (1.0 - lam_init)


def _cached_attn_kernel(*refs, mode, past, lam_init):
    if mode == "A":
        q_ref, kn_ref, vn_ref, ck_ref, cv_ref, lam_ref, gain_ref, o_ref = refs
    else:
        q_ref, kn_ref, vn_ref, ck_ref, cv_ref, clf_ref, nlf_ref, o_ref = refs
    t = q_ref.shape[1]
    width = o_ref.shape[2]
    groups = width // LANES
    lane = lax.broadcasted_iota(jnp.int32, (1, LANES), 1)
    low = lane < (LANES // 2)
    row = lax.broadcasted_iota(jnp.int32, (t, t), 0)
    col = lax.broadcasted_iota(jnp.int32, (t, t), 1)
    if mode == "A":
        keep = ((past + col) >> CHUNK_SHIFT) <= ((past + row) >> CHUNK_SHIFT)
        lam = _lambda(lam_ref, lam_init)
    else:
        keep = col <= row
        r = lax.broadcasted_iota(jnp.int32, (past, past), 0)
        c = lax.broadcasted_iota(jnp.int32, (past, past), 1)
        upper = jnp.where(r <= c, 1.0, 0.0).astype(BF16)
        clf = clf_ref[0]
        hi, mid, lo = _split3(clf)
        cum_c = (_dot(hi, upper) + _dot(mid, upper)) + _dot(lo, upper)
        upper_n = jnp.where(row <= col, 1.0, 0.0).astype(BF16)
        hi, mid, lo = _split3(nlf_ref[0])
        cum_n = ((_dot(hi, upper_n) + _dot(mid, upper_n)) + _dot(lo, upper_n)
                 + jnp.sum(clf, axis=1, keepdims=True))
        bias_c = cum_c * (-LOG2E)
        bias_n = cum_n * (-LOG2E)
    keep2 = jnp.concatenate([keep, keep], axis=0)
    first_rows = lax.broadcasted_iota(jnp.int32, (2 * t, 1), 0) < t
    for g in range(groups):
        sl = slice(g * LANES, (g + 1) * LANES)
        q = q_ref[0, :, sl]
        kn = kn_ref[0, :, sl]
        vn = vn_ref[0, :, sl].astype(BF16)
        if mode == "A":
            kc = ck_ref[0, 0, :, g, :].astype(BF16)
            vc = cv_ref[0, 0, :, g, :].astype(BF16)
        else:
            kc = ck_ref[0, :, sl].astype(BF16)
            vc = cv_ref[0, :, sl].astype(BF16)
        zero = jnp.zeros_like(q)
        q2 = jnp.concatenate([jnp.where(low, q, zero), jnp.where(low, zero, q)], axis=0)
        s_c = _dot_nt(q2, kc)
        s_n = _dot_nt(q2, kn)
        if mode == "C":
            h = 2 * g
            s_c = s_c + jnp.where(first_rows, bias_c[h:h + 1, :], bias_c[h + 1:h + 2, :])
            s_n = s_n + jnp.where(first_rows, bias_n[h:h + 1, :], bias_n[h + 1:h + 2, :])
        s_n = jnp.where(keep2, s_n, NEG)
        m = jnp.maximum(jnp.max(s_c, axis=1, keepdims=True), jnp.max(s_n, axis=1, keepdims=True))
        p_c = jnp.exp2(s_c - m)
        p_n = jnp.exp2(s_n - m)
        l = jnp.sum(p_c, axis=1, keepdims=True) + jnp.sum(p_n, axis=1, keepdims=True)
        acc = (_dot(p_c.astype(BF16), vc) + _dot(p_n.astype(BF16), vn)) * (1.0 / l)
        outs = [acc[:t], acc[t:]]
        if mode == "A":
            o = _diff_finish(outs[0], outs[1], lam, gain_ref[g], lam_init)
        else:
            o = jnp.where(low, outs[0], outs[1])
        o_ref[0, :, sl] = o.astype(o_ref.dtype)


def _cached_attention(q, kn, vn, cache_k, cache_v, mode, lam=None, gain=None, cache_lft=None, new_lft=None,
                      lam_init=0.0):
    b, t, width = q.shape
    new = pl.BlockSpec((1, t, width), lambda bi: (bi, 0, 0))
    if mode == "A":
        _, _, past, heads, dh = cache_k.shape
        old = pl.BlockSpec((1, 1, past, heads, dh), lambda bi: (0, bi, 0, 0, 0))
    else:
        past = cache_k.shape[1]
        old = pl.BlockSpec((1, past, width), lambda bi: (bi, 0, 0))
    in_specs = [new, new, new, old, old]
    args = [q, kn, vn, cache_k, cache_v]
    if mode == "A":
        in_specs += [pl.BlockSpec((4, A_HEAD_DIM), lambda bi: (0, 0)),
                     pl.BlockSpec((A_HEADS, 1, LANES), lambda bi: (0, 0, 0))]
        args += [lam, gain]
    else:
        in_specs += [pl.BlockSpec((1, C_HEADS, past), lambda bi: (bi, 0, 0)),
                     pl.BlockSpec((1, C_HEADS, t), lambda bi: (bi, 0, 0))]
        args += [cache_lft, new_lft]
    return pl.pallas_call(
        functools.partial(_cached_attn_kernel, mode=mode, past=past, lam_init=lam_init),
        grid=(b,),
        in_specs=in_specs,
        out_specs=new,
        out_shape=jax.ShapeDtypeStruct((b, t, width), BF16),
        compiler_params=_params(("arbitrary",)),
        name="cached_attn_" + mode,
    )(*args)


def _gelu_tanh(x):
    return 0.5 * x * (1.0 + jnp.tanh(math.sqrt(2.0 / math.pi) * (x + 0.044715 * (x * x * x))))


def _causal_conv(x, tail, w_ref, b_ref):
    width = w_ref.shape[0]
    tm = x.shape[0]
    cat = jnp.concatenate([tail, x], axis=0)
    y = None
    for j in range(width):
        back = width - 1 - j
        src = cat if back == 0 else pltpu.roll(cat, back, 0)
        term = src[SUBLANES:SUBLANES + tm] * w_ref[j:j + 1, :]
        y = term if y is None else y + term
    return y + b_ref[...]


def _rglru_kernel(xg_ref, h0_ref, cb_ref, cw_ref, cbias_ref, wa_ref, ba_ref, wx_ref, bx_ref, sp_ref,
                  y_ref, hl_ref, ct_ref, hc_ref, tail_ref):
    i = pl.program_id(1)
    tm = xg_ref.shape[1]

    @pl.when(i == 0)
    def _():
        hc_ref[...] = h0_ref[0]
        tail_ref[...] = cb_ref[0]

    x = xg_ref[0, :, :B_WIDTH]
    gate_in = xg_ref[0, :, B_WIDTH:]
    xc = _causal_conv(x, tail_ref[...], cw_ref, cbias_ref)
    xcb = xc.astype(BF16)
    r = _sigmoid(_dot(xcb, wa_ref[...]) + ba_ref[...])
    ig = _sigmoid(_dot(xcb, wx_ref[...]) + bx_ref[...])
    log_a = (-RG_C) * r * sp_ref[...]
    a = jnp.exp(log_a)
    th = jnp.tanh(log_a)
    bx = jnp.sqrt((-2.0 * th) / (1.0 - th)) * (ig * xc)
    pos = lax.broadcasted_iota(jnp.int32, (tm, 1), 0) & (SUBLANES - 1)
    s = 1
    while s < SUBLANES:
        valid = pos >= s
        a_sh = pltpu.roll(a, s, 0)
        b_sh = pltpu.roll(bx, s, 0)
        bx = jnp.where(valid, a * b_sh + bx, bx)
        a = jnp.where(valid, a * a_sh, a)
        s *= 2
    carry = hc_ref[...]
    groups = []
    for g in range(tm // SUBLANES):
        rows = slice(g * SUBLANES, (g + 1) * SUBLANES)
        hg = a[rows] * carry + bx[rows]
        groups.append(hg)
        carry = hg[SUBLANES - 1:, :]
    h = jnp.concatenate(groups, axis=0)
    y_ref[0] = (h * _gelu_tanh(gate_in)).astype(y_ref.dtype)
    h_tail = h[tm - SUBLANES:, :]
    x_tail = x[tm - SUBLANES:, :]
    hc_ref[...] = h_tail[SUBLANES - 1:, :]
    tail_ref[...] = x_tail
    hl_ref[0] = h_tail
    ct_ref[0] = x_tail


def _rglru(xg, h0, conv_tail, cw, cbias, wa, ba, wx, bx, sp, tm_pref=256):
    b, t, _ = xg.shape
    tm = _row_tile(t, tm_pref)
    wspec = lambda shape: pl.BlockSpec(shape, lambda bi, i: tuple(0 for _ in shape))
    return pl.pallas_call(
        _rglru_kernel,
        grid=(b, t // tm),
        in_specs=[pl.BlockSpec((1, tm, 2 * B_WIDTH), lambda bi, i: (bi, i, 0)),
                  pl.BlockSpec((1, 1, B_WIDTH), lambda bi, i: (bi, 0, 0)),
                  pl.BlockSpec((1, SUBLANES, B_WIDTH), lambda bi, i: (bi, 0, 0)),
                  wspec((B_CONV, B_WIDTH)), wspec((1, B_WIDTH)),
                  wspec((B_WIDTH, B_WIDTH)), wspec((1, B_WIDTH)),
                  wspec((B_WIDTH, B_WIDTH)), wspec((1, B_WIDTH)), wspec((1, B_WIDTH))],
        out_specs=[pl.BlockSpec((1, tm, B_WIDTH), lambda bi, i: (bi, i, 0)),
                   pl.BlockSpec((1, SUBLANES, B_WIDTH), lambda bi, i: (bi, 0, 0)),
                   pl.BlockSpec((1, SUBLANES, B_WIDTH), lambda bi, i: (bi, 0, 0))],
        out_shape=[jax.ShapeDtypeStruct((b, t, B_WIDTH), BF16),
                   jax.ShapeDtypeStruct((b, SUBLANES, B_WIDTH), F32),
                   jax.ShapeDtypeStruct((b, SUBLANES, B_WIDTH), F32)],
        scratch_shapes=[pltpu.VMEM((1, B_WIDTH), F32), pltpu.VMEM((SUBLANES, B_WIDTH), F32)],
        compiler_params=_params(("arbitrary", "arbitrary")),
        name="rglru",
    )(xg, h0, conv_tail, cw, cbias, wa, ba, wx, bx, sp)


def _proj_ln_kernel(*refs, n_in):
    h_refs = refs[:n_in]
    w_ref, x_ref, gate_ref, g_ref, b_ref, o_ref = refs[n_in:]
    tm = x_ref.shape[1]
    rows = [slice(r, min(r + LN_ROWS, tm)) for r in range(0, tm, LN_ROWS)]

    def matmul(rs):
        hs = [r[0, rs, :] for r in h_refs]
        return _dot(hs[0] if n_in == 1 else jnp.concatenate(hs, axis=-1), w_ref[0])

    def norm(rs, proj):
        y = ALPHA * x_ref[0, rs, :] + gate_ref[0] * proj
        mu = jnp.mean(y, axis=-1, keepdims=True)
        yc = y - mu
        var = jnp.mean(yc * yc, axis=-1, keepdims=True)
        o_ref[0, rs, :] = yc * lax.rsqrt(var + LN_EPS) * g_ref[...] + b_ref[...]

    pending = matmul(rows[0])
    for r in range(len(rows)):
        nxt = matmul(rows[r + 1]) if r + 1 < len(rows) else None
        norm(rows[r], pending)
        pending = nxt


def _proj_ln(hs, w, layer, x, gate, ln_g, ln_b, tm_pref=1024):
    b, t, d = x.shape
    tm = _row_tile(t, tm_pref)
    k = w.shape[1]
    gate_spec = (pl.BlockSpec((1, 1, d), lambda bi, i: (bi, 0, 0)) if gate.shape[1] == 1
                 else pl.BlockSpec((1, tm, d), lambda bi, i: (bi, i, 0)))
    in_specs = [pl.BlockSpec((1, tm, h.shape[2]), lambda bi, i: (bi, i, 0)) for h in hs]
    in_specs += [pl.BlockSpec((1, k, d), lambda bi, i: (layer, 0, 0)),
                 pl.BlockSpec((1, tm, d), lambda bi, i: (bi, i, 0)),
                 gate_spec,
                 pl.BlockSpec((1, d), lambda bi, i: (0, 0)),
                 pl.BlockSpec((1, d), lambda bi, i: (0, 0))]
    return pl.pallas_call(
        functools.partial(_proj_ln_kernel, n_in=len(hs)),
        grid=(b, t // tm),
        in_specs=in_specs,
        out_specs=pl.BlockSpec((1, tm, d), lambda bi, i: (bi, i, 0)),
        out_shape=jax.ShapeDtypeStruct((b, t, d), F32),
        compiler_params=_params(("arbitrary", "arbitrary")),
        name="proj_ln",
    )(*hs, w, x, gate, ln_g, ln_b)


def _ffn_up_kernel(x_ref, sc_ref, sh_ref, w_ref, cb_ref, cw_ref, cbias_ref, h_ref, ct_ref, tail_ref):
    i = pl.program_id(1)
    seqs, tm = x_ref.shape[0], x_ref.shape[1]

    @pl.when(i == 0)
    def _():
        tail_ref[...] = cb_ref[...]

    us = [_modulate(x_ref[s], sc_ref[s], sh_ref[s]) for s in range(seqs)]
    u = us[0] if seqs == 1 else jnp.concatenate(us, axis=0)
    chunks = [slice(c, min(c + FFN_CHUNK, D_FF)) for c in range(0, D_FF, FFN_CHUNK)]

    def matmuls(cs):
        gs = slice(D_FF + cs.start, D_FF + cs.stop)
        return _dot(u, w_ref[0, :, cs]), _dot(u, w_ref[0, :, gs])

    def gate(cs, a_all, g_all):
        for s in range(seqs):
            rows = slice(s * tm, (s + 1) * tm)
            a, g = a_all[rows], g_all[rows]
            gc = _causal_conv(g, tail_ref[s, :, cs], cw_ref.at[:, cs], cbias_ref.at[:, cs])
            silu = gc * (0.5 * jnp.tanh(0.5 * gc) + 0.5)
            h_ref[s, :, cs] = (a * silu).astype(h_ref.dtype)
            g_tail = g[tm - SUBLANES:, :]
            tail_ref[s, :, cs] = g_tail
            ct_ref[s, :, cs] = g_tail

    pending = matmuls(chunks[0])
    for c in range(len(chunks)):
        nxt = matmuls(chunks[c + 1]) if c + 1 < len(chunks) else None
        gate(chunks[c], *pending)
        pending = nxt


def _ffn_up(x, sc, sh, w, layer, conv_tail, cw, cbias, tm_pref=256):
    b, t, d = x.shape
    tm = _row_tile(t, tm_pref)
    seqs = max(1, min(b, LANES // tm))
    while b % seqs:
        seqs -= 1
    return pl.pallas_call(
        _ffn_up_kernel,
        grid=(b // seqs, t // tm),
        in_specs=[pl.BlockSpec((seqs, tm, d), lambda bi, i: (bi, i, 0)),
                  pl.BlockSpec((seqs, 1, d), lambda bi, i: (bi, 0, 0)),
                  pl.BlockSpec((seqs, 1, d), lambda bi, i: (bi, 0, 0)),
                  pl.BlockSpec((1, d, 2 * D_FF), lambda bi, i: (layer, 0, 0)),
                  pl.BlockSpec((seqs, SUBLANES, D_FF), lambda bi, i: (bi, 0, 0)),
                  pl.BlockSpec((FFN_CONV, D_FF), lambda bi, i: (0, 0)),
                  pl.BlockSpec((1, D_FF), lambda bi, i: (0, 0))],
        out_specs=[pl.BlockSpec((seqs, tm, D_FF), lambda bi, i: (bi, i, 0)),
                   pl.BlockSpec((seqs, SUBLANES, D_FF), lambda bi, i: (bi, 0, 0))],
        out_shape=[jax.ShapeDtypeStruct((b, t, D_FF), BF16),
                   jax.ShapeDtypeStruct((b, SUBLANES, D_FF), F32)],
        scratch_shapes=[pltpu.VMEM((seqs, SUBLANES, D_FF), F32)],
        compiler_params=_params(("arbitrary", "arbitrary")),
        name="ffn_up",
    )(x, sc, sh, w, conv_tail, cw, cbias)


def _pad_tail(buf):
    return jnp.pad(buf, ((0, 0), (SUBLANES - buf.shape[1], 0), (0, 0)))


def _rope_tables(past, t):
    half = A_HEAD_DIM // 2
    inv = ROPE_THETA ** (-jnp.arange(0, A_HEAD_DIM, 2, dtype=F32) / A_HEAD_DIM)
    pos = (past + jnp.arange(t, dtype=jnp.int32)).astype(F32)
    ang = pos[:, None] * inv[None, :]
    cos = jnp.tile(jnp.cos(ang), (1, LANES // half))
    sin = jnp.sin(ang)
    sin_signed = jnp.tile(jnp.concatenate([-sin, sin], axis=1), (1, LANES // A_HEAD_DIM))
    return cos, sin_signed


def _block_diag(w):
    n, i, o = w.shape
    return jnp.einsum("nio,nm->nimo", w, jnp.eye(n, dtype=w.dtype)).reshape(n * i, n * o)


def _prepare(p):
    w = {}
    w_in_ab = p["w_in_ab"][0]
    w["in_ab"] = w_in_ab.astype(BF16)
    w["lam"] = jnp.stack([p["lam_q1"][0], p["lam_k1"][0], p["lam_q2"][0], p["lam_k2"][0]])
    w["gain_row"] = p["attn_gain"][0].reshape(A_HEADS, 1, LANES)
    w["gain_col"] = p["attn_gain"][0].reshape(A_HEADS, LANES, 1)
    w["b_conv_w"] = p["b_conv_w"][0]
    w["b_conv_b"] = p["b_conv_b"][0].reshape(1, B_WIDTH)
    w["rg_a"] = _block_diag(p["w_rg_a"][0]).astype(BF16)
    w["rg_x"] = _block_diag(p["w_rg_x"][0]).astype(BF16)
    w["b_rg_a"] = p["b_rg_a"][0].reshape(1, B_WIDTH)
    w["b_rg_x"] = p["b_rg_x"][0].reshape(1, B_WIDTH)
    w["rg_L"] = p["rg_L"][0].reshape(1, B_WIDTH)
    w_in_c = p["w_in_c"][0]
    w["in_c"] = jnp.pad(w_in_c, ((0, 0), (0, LANES - C_HEADS))).astype(BF16)
    w["in_c_ft"] = w_in_c[:, 3 * C_WIDTH:].T.astype(BF16)
    w["bf_row"] = jnp.pad(p["b_f"][0], (0, LANES - C_HEADS)).reshape(1, LANES)
    w["bf_col"] = p["b_f"][0].reshape(C_HEADS, 1)
    w["place"] = _bias_placement()
    w["seg"] = jnp.asarray(np.repeat(np.eye(LANES, dtype=np.float32)[:C_HEADS], C_HEAD_DIM, axis=0), BF16)
    w["up"] = p["w_up"].astype(BF16)
    w["down"] = p["w_down"].astype(BF16)
    w["out_ab"] = p["w_out_ab"].astype(BF16)
    w["out_c"] = p["w_out_c"].astype(BF16)
    w["ffn_conv_w"] = [p["ffn_conv_w"][i] for i in range(DEPTH)]
    w["ffn_conv_b"] = [p["ffn_conv_b"][i].reshape(1, D_FF) for i in range(DEPTH)]
    w["ln1_g"] = [p["ln1_g"][i].reshape(1, D_MODEL) for i in range(DEPTH)]
    w["ln1_b"] = [p["ln1_b"][i].reshape(1, D_MODEL) for i in range(DEPTH)]
    w["ln2_g"] = [p["ln2_g"][i].reshape(1, D_MODEL) for i in range(DEPTH)]
    w["ln2_b"] = [p["ln2_b"][i].reshape(1, D_MODEL) for i in range(DEPTH)]
    return w


def _softplus_kernel(x_ref, o_ref):
    o_ref[...] = _softplus(-x_ref[...])


def _proj_ln_rows(hs, w, layer, x, gate, ln_g, ln_b):
    b, t, d = x.shape
    flat = lambda a: a.reshape(1, b * t, a.shape[2])
    gate_rows = jnp.broadcast_to(gate, (b, t, d))
    return _proj_ln([flat(h) for h in hs], w, layer, flat(x), flat(gate_rows), ln_g, ln_b).reshape(b, t, d)


def _trunk(x, mods, w, sp, cache_a_k=None, cache_a_v=None, state_b_h=None, state_b_conv=None,
           cache_c_k=None, cache_c_v=None, cache_c_logf=None, state_ffn_conv=None):
    b, t, d = x.shape
    cached = cache_a_k is not None
    proj_ln = _proj_ln_rows if cached else _proj_ln
    past = cache_a_k.shape[2] if cached else 0
    outs = {}
    for i in range(DEPTH):
        sh1, sc1, g1, sh2, sc2, g2 = [m[:, None, :] for m in jnp.split(mods[i], 6, axis=-1)]
        if i % 2 == 0:
            lam_init = 0.8 - 0.6 * math.exp(-0.3 * i)
            cos, sin = _rope_tables(past, t)
            q, kb, vt, k32, v32, xg = _proj_ab(x, sc1, sh1, w["in_ab"], cos, sin)
            if cached:
                o = _cached_attention(q, kb, v32.reshape(b, t, A_WIDTH), cache_a_k, cache_a_v,
                                      "A", lam=w["lam"], gain=w["gain_row"], lam_init=lam_init)
                h0 = state_b_h[0][:, None, :]
                ctail = _pad_tail(state_b_conv[0])
            else:
                first = jnp.zeros((b, A_HEADS, t // (2 * ATTN_BLOCK)), jnp.int32)
                o = _attention(q, kb, vt, first, "A", lam=w["lam"], gain=w["gain_col"], lam_init=lam_init)
                h0 = jnp.zeros((b, 1, B_WIDTH), F32)
                ctail = jnp.zeros((b, SUBLANES, B_WIDTH), F32)
            yb, h_tail, x_tail = _rglru(xg, h0, ctail, w["b_conv_w"], w["b_conv_b"], w["rg_a"], w["b_rg_a"],
                                        w["rg_x"], w["b_rg_x"], sp)
            outs["a_k"] = k32[None]
            outs["a_v"] = v32[None]
            outs["b_h"] = h_tail[:, SUBLANES - 1, :][None]
            outs["b_conv"] = x_tail[:, SUBLANES - (B_CONV - 1):, :][None]
            x = proj_ln([o, yb], w["out_ab"], i // 2, x, g1, w["ln1_g"][i], w["ln1_b"][i])
        else:
            q, kb, bias, vt, k32, v32, lf, lft, bounds = _proj_c(x, sc1, sh1, w["in_c"], w["in_c_ft"],
                                                                 w["bf_row"], w["bf_col"], w["place"], w["seg"])
            if cached:
                o = _cached_attention(q, kb, v32, cache_c_k[0].reshape(b, past, C_WIDTH),
                                      cache_c_v[0].reshape(b, past, C_WIDTH),
                                      "C", cache_lft=jnp.swapaxes(cache_c_logf[0], 1, 2), new_lft=lft)
            else:
                o = _attention(q, kb, vt, _first_visible_block(bounds, 2), "C", bias=bias)
            outs["c_k"] = k32.reshape(1, b, t, C_HEADS, C_HEAD_DIM)
            outs["c_v"] = v32.reshape(1, b, t, C_HEADS, C_HEAD_DIM)
            outs["c_logf"] = lf[None]
            x = proj_ln([o], w["out_c"], i // 2, x, g1, w["ln1_g"][i], w["ln1_b"][i])
        ftail = _pad_tail(state_ffn_conv[i]) if cached else jnp.zeros((b, SUBLANES, D_FF), F32)
        hmid, g_tail = _ffn_up(x, sc2, sh2, w["up"], i, ftail, w["ffn_conv_w"][i], w["ffn_conv_b"][i])
        outs.setdefault("ffn", []).append(g_tail[:, SUBLANES - (FFN_CONV - 1):, :])
        x = proj_ln([hmid], w["down"], i, x, g2, w["ln2_g"][i], w["ln2_b"][i])
    return (x, outs["a_k"], outs["a_v"], outs["b_h"], outs["b_conv"],
            outs["c_k"], outs["c_v"], outs["c_logf"], jnp.stack(outs["ffn"]))


def kernel(x_prompt, x_sample, c_prompt, c_sample, cache_a_k, cache_a_v, state_b_h, state_b_conv, cache_c_k, cache_c_v, cache_c_logf, state_ffn_conv, w_ada, b_ada, ln1_g, ln1_b, ln2_g, ln2_b, w_in_ab, lam_q1, lam_k1, lam_q2, lam_k2, attn_gain, b_conv_w, b_conv_b, w_rg_a, b_rg_a, w_rg_x, b_rg_x, rg_L, w_out_ab, w_in_c, b_f, w_out_c, w_up, ffn_conv_w, ffn_conv_b, w_down):
    p = dict(w_in_ab=w_in_ab, lam_q1=lam_q1, lam_k1=lam_k1, lam_q2=lam_q2, lam_k2=lam_k2, attn_gain=attn_gain,
             b_conv_w=b_conv_w, b_conv_b=b_conv_b, w_rg_a=w_rg_a, b_rg_a=b_rg_a, w_rg_x=w_rg_x, b_rg_x=b_rg_x,
             rg_L=rg_L, w_out_ab=w_out_ab, w_in_c=w_in_c, b_f=b_f, w_out_c=w_out_c, w_up=w_up,
             ffn_conv_w=ffn_conv_w, ffn_conv_b=ffn_conv_b, w_down=w_down,
             ln1_g=ln1_g, ln1_b=ln1_b, ln2_g=ln2_g, ln2_b=ln2_b)
    w = _prepare(p)
    bp = c_prompt.shape[0]
    bs = c_sample.shape[0]
    rows = -(-(bp + bs) // 16) * 16
    c_all = jnp.pad(jnp.concatenate([c_prompt, c_sample], axis=0), ((0, rows - bp - bs), (0, 0)))
    mods = _mods(c_all, w_ada, b_ada)
    sp = pl.pallas_call(_softplus_kernel, out_shape=jax.ShapeDtypeStruct((1, B_WIDTH), F32),
                        name="softplus")(w["rg_L"])
    res_p = _trunk(x_prompt, mods[:, :bp], w, sp)
    res_s = _trunk(x_sample, mods[:, bp:bp + bs], w, sp, cache_a_k, cache_a_v, state_b_h, state_b_conv,
                   cache_c_k, cache_c_v, cache_c_logf, state_ffn_conv)
    return (res_p[0], res_s[0]) + res_p[1:] + res_s[1:]
```

```python
import functools
import math

import numpy as np
import jax
import jax.numpy as jnp
from jax import lax
from jax.experimental import pallas as pl
from jax.experimental.pallas import tpu as pltpu

F32 = jnp.float32
BF16 = jnp.bfloat16

D_MODEL = 1024
DEPTH = 2
CHUNK = 64
CHUNK_SHIFT = 6
A_HEADS = 4
A_HEAD_DIM = 64
A_WIDTH = A_HEADS * 2 * A_HEAD_DIM
B_WIDTH = 512
B_BLOCKS = 8
B_CONV = 4
RG_C = 8.0
C_HEADS = 16
C_HEAD_DIM = 64
C_WIDTH = C_HEADS * C_HEAD_DIM
D_FF = 2816
FFN_CONV = 3
ROPE_THETA = 10000.0
ALPHA = (2 * DEPTH) ** 0.25
LN_EPS = 1e-5
NEG = -1e30
LOG2E = 1.4426950408889634

LANES = 128
SUBLANES = 8
BF16_ROWS = 16
MXU_DIM = 256
VMEM_LIMIT = 56 * 1024 * 1024
BIAS_WIDTH = (C_HEADS // 2) * LANES
FFN_CHUNK = 256
LN_ROWS = 256
SKIP_GAP = 160.0
NORM_SLACK = 1.03
SHORT_WINDOWS = (4, 6)
ATTN_BLOCK = 512


def _params(sem, flags=None):
    return pltpu.CompilerParams(dimension_semantics=sem, vmem_limit_bytes=VMEM_LIMIT, flags=flags)


def _row_tile(t, pref):
    if t <= pref:
        return t
    tm = pref
    while t % tm:
        tm //= 2
    return tm


def _modulate(x, sc, sh):
    return (x * (1.0 + sc) + sh).astype(BF16)


def _sigmoid(x):
    return 1.0 / (1.0 + jnp.exp(-x))


def _softplus(x):
    return jnp.maximum(x, 0.0) + jnp.log1p(jnp.exp(-jnp.abs(x)))


def _log_sigmoid(x):
    return jnp.minimum(x, 0.0) - jnp.log1p(jnp.exp(-jnp.abs(x)))


def _split3(x):
    hi = x.astype(BF16)
    r1 = x - hi.astype(F32)
    mid = r1.astype(BF16)
    lo = (r1 - mid.astype(F32)).astype(BF16)
    return hi, mid, lo


def _dot(a, b):
    return jnp.dot(a, b, preferred_element_type=F32)


def _dot_nt(a, b):
    return lax.dot_general(a, b, (((1,), (1,)), ((), ())), preferred_element_type=F32)


def _mods_kernel(c_ref, w_ref, b_ref, o_ref):
    c = c_ref[...]
    s = (c * _sigmoid(c)).astype(BF16)
    o_ref[0] = _dot(s, w_ref[0].astype(BF16)) + b_ref[0]


def _mods(c_all, w_ada, b_ada):
    rows, d = c_all.shape
    n = w_ada.shape[-1]
    tn = 1536
    return pl.pallas_call(
        _mods_kernel,
        grid=(DEPTH, n // tn),
        in_specs=[pl.BlockSpec((rows, d), lambda l, j: (0, 0)),
                  pl.BlockSpec((1, d, tn), lambda l, j: (l, 0, j)),
                  pl.BlockSpec((1, 1, tn), lambda l, j: (l, 0, j))],
        out_specs=pl.BlockSpec((1, rows, tn), lambda l, j: (l, 0, j)),
        out_shape=jax.ShapeDtypeStruct((DEPTH, rows, n), F32),
        compiler_params=_params(("arbitrary", "arbitrary")),
        name="mods",
    )(c_all, w_ada, b_ada.reshape(DEPTH, 1, n))


def _rope_slab(x, cos, sin_signed, first_half):
    fwd = pltpu.roll(x, LANES - A_HEAD_DIM // 2, 1)
    bwd = pltpu.roll(x, A_HEAD_DIM // 2, 1)
    partner = jnp.where(first_half, fwd, bwd)
    return x * cos + partner * sin_signed


def _proj_ab_kernel(x_ref, sc_ref, sh_ref, w_ref, cos_ref, sin_ref,
                    q_ref, kb_ref, vt_ref, k_ref, v_ref, xg_ref, *, q_scale):
    u = _modulate(x_ref[0], sc_ref[0], sh_ref[0])
    pr = _dot(u, w_ref[...])
    cos = cos_ref[...]
    sin = sin_ref[...]
    lane = lax.broadcasted_iota(jnp.int32, (1, LANES), 1)
    first_half = (lane & (A_HEAD_DIM - 1)) < (A_HEAD_DIM // 2)
    for h in range(A_HEADS):
        sl = slice(h * LANES, (h + 1) * LANES)
        q = _rope_slab(pr[:, sl], cos, sin, first_half)
        q_ref[0, :, sl] = (q * q_scale).astype(BF16)
        k = _rope_slab(pr[:, A_WIDTH + h * LANES:A_WIDTH + (h + 1) * LANES], cos, sin, first_half)
        k_ref[0, :, h, :] = k
        kb_ref[0, :, sl] = k.astype(BF16)
    for h in range(A_HEADS):
        v_ref[0, :, h, :] = pr[:, 2 * A_WIDTH + h * LANES:2 * A_WIDTH + (h + 1) * LANES]
    vt_ref[0, 0] = pr[:, 2 * A_WIDTH:3 * A_WIDTH].T.astype(BF16)
    xg_ref[0] = pr[:, 3 * A_WIDTH:]


def _proj_ab(x, sc, sh, w, cos, sin):
    b, t, d = x.shape
    tm = _row_tile(t, ATTN_BLOCK)
    nt = t // tm
    n = w.shape[1]
    q_scale = A_HEAD_DIM ** -0.5 * LOG2E
    row = lambda width: pl.BlockSpec((1, tm, width), lambda bi, i: (bi, i, 0))
    const = lambda shape: pl.BlockSpec(shape, lambda bi, i: tuple(0 for _ in shape))
    heads = pl.BlockSpec((1, tm, A_HEADS, LANES), lambda bi, i: (bi, i, 0, 0))
    return pl.pallas_call(
        functools.partial(_proj_ab_kernel, q_scale=q_scale),
        grid=(b, nt),
        in_specs=[row(d),
                  pl.BlockSpec((1, 1, d), lambda bi, i: (bi, 0, 0)),
                  pl.BlockSpec((1, 1, d), lambda bi, i: (bi, 0, 0)),
                  const((d, n)),
                  pl.BlockSpec((tm, LANES), lambda bi, i: (i, 0)),
                  pl.BlockSpec((tm, LANES), lambda bi, i: (i, 0))],
        out_specs=[row(A_WIDTH), row(A_WIDTH),
                   pl.BlockSpec((1, 1, A_WIDTH, tm), lambda bi, i: (bi, i, 0, 0)),
                   heads, heads, row(2 * B_WIDTH)],
        out_shape=[jax.ShapeDtypeStruct((b, t, A_WIDTH), BF16),
                   jax.ShapeDtypeStruct((b, t, A_WIDTH), BF16),
                   jax.ShapeDtypeStruct((b, nt, A_WIDTH, tm), BF16),
                   jax.ShapeDtypeStruct((b, t, A_HEADS, LANES), F32),
                   jax.ShapeDtypeStruct((b, t, A_HEADS, LANES), F32),
                   jax.ShapeDtypeStruct((b, t, 2 * B_WIDTH), F32)],
        compiler_params=_params(("arbitrary", "arbitrary")),
        name="proj_ab",
    )(x, sc, sh, w, cos, sin)


def _bias_placement():
    e = np.zeros((3 * LANES, BIAS_WIDTH), np.float32)
    for piece in range(3):
        for h in range(C_HEADS):
            e[piece * LANES + h, (h // 2) * LANES + 3 * (h % 2) + piece] = 1.0
    return jnp.asarray(e, BF16)


def _head_norm_max(x, seg):
    n2 = _dot((x * x).astype(BF16), seg)
    return jnp.sqrt(jnp.max(n2, axis=0, keepdims=True))


def _proj_c_kernel(x_ref, sc_ref, sh_ref, w_ref, wft_ref, bfr_ref, bfc_ref, place_ref, seg_ref,
                   q_ref, kb_ref, bias_ref, vt_ref, k_ref, v_ref, lf_ref, lft_ref, bound_ref, run_ref, *, q_scale):
    i = pl.program_id(1)
    tm = x_ref.shape[1]

    @pl.when(i == 0)
    def _():
        run_ref[...] = jnp.zeros_like(run_ref)

    u = _modulate(x_ref[0], sc_ref[0], sh_ref[0])
    pr = _dot(u, w_ref[...])
    qs = pr[:, :C_WIDTH] * q_scale
    q_ref[0] = qs.astype(BF16)
    k = pr[:, C_WIDTH:2 * C_WIDTH]
    k_ref[0] = k
    kb_ref[0] = k.astype(BF16)
    v = pr[:, 2 * C_WIDTH:3 * C_WIDTH]
    v_ref[0] = v
    vt_ref[0, 0] = v.T.astype(BF16)
    lf = _log_sigmoid(pr[:, 3 * C_WIDTH:] + bfr_ref[...])
    lf_ref[0] = lf[:, :C_HEADS]
    lft_ref[0] = _log_sigmoid(_dot_nt(wft_ref[...], u) + bfc_ref[...])
    r = lax.broadcasted_iota(jnp.int32, (tm, tm), 0)
    c = lax.broadcasted_iota(jnp.int32, (tm, tm), 1)
    lower = jnp.where(c <= r, 1.0, 0.0).astype(BF16)
    hi, mid, lo = _split3(lf)
    cum = (_dot(lower, hi) + _dot(lower, mid)) + _dot(lower, lo) + run_ref[...]
    bias = cum * (-LOG2E)
    pieces = jnp.concatenate(_split3(bias), axis=1)
    bias_ref[0] = _dot(pieces, place_ref[...]).astype(BF16)
    seg = seg_ref[...]
    bound_ref[0, 0] = jnp.concatenate(
        [_head_norm_max(qs, seg), _head_norm_max(k, seg),
         jnp.max(bias, axis=0, keepdims=True), jnp.min(bias, axis=0, keepdims=True),
         jnp.zeros((SUBLANES - 4, LANES), F32)], axis=0)
    run_ref[...] = run_ref[...] + jnp.sum(lf, axis=0, keepdims=True)


def _proj_c(x, sc, sh, w, wft, bf_row, bf_col, place, seg):
    b, t, d = x.shape
    tm = _row_tile(t, ATTN_BLOCK)
    nt = t // tm
    n = w.shape[1]
    q_scale = C_HEAD_DIM ** -0.5 * LOG2E
    row = lambda width: pl.BlockSpec((1, tm, width), lambda bi, i: (bi, i, 0))
    const = lambda shape: pl.BlockSpec(shape, lambda bi, i: tuple(0 for _ in shape))
    return pl.pallas_call(
        functools.partial(_proj_c_kernel, q_scale=q_scale),
        grid=(b, nt),
        in_specs=[row(d),
                  pl.BlockSpec((1, 1, d), lambda bi, i: (bi, 0, 0)),
                  pl.BlockSpec((1, 1, d), lambda bi, i: (bi, 0, 0)),
                  const((d, n)), const((C_HEADS, d)),
                  const((1, LANES)), const((C_HEADS, 1)), const((3 * LANES, BIAS_WIDTH)), const((C_WIDTH, LANES))],
        out_specs=[row(C_WIDTH), row(C_WIDTH), row(BIAS_WIDTH),
                   pl.BlockSpec((1, 1, C_WIDTH, tm), lambda bi, i: (bi, i, 0, 0)),
                   row(C_WIDTH), row(C_WIDTH), row(C_HEADS),
                   pl.BlockSpec((1, C_HEADS, tm), lambda bi, i: (bi, 0, i)),
                   pl.BlockSpec((1, 1, SUBLANES, LANES), lambda bi, i: (bi, i, 0, 0))],
        out_shape=[jax.ShapeDtypeStruct((b, t, C_WIDTH), BF16),
                   jax.ShapeDtypeStruct((b, t, C_WIDTH), BF16),
                   jax.ShapeDtypeStruct((b, t, BIAS_WIDTH), BF16),
                   jax.ShapeDtypeStruct((b, nt, C_WIDTH, tm), BF16),
                   jax.ShapeDtypeStruct((b, t, C_WIDTH), F32),
                   jax.ShapeDtypeStruct((b, t, C_WIDTH), F32),
                   jax.ShapeDtypeStruct((b, t, C_HEADS), F32),
                   jax.ShapeDtypeStruct((b, C_HEADS, t), F32),
                   jax.ShapeDtypeStruct((b, nt, SUBLANES, LANES), F32)],
        scratch_shapes=[pltpu.VMEM((1, LANES), F32)],
        compiler_params=_params(("arbitrary", "arbitrary")),
        name="proj_c",
    )(x, sc, sh, w, wft, bf_row, bf_col, place, seg)


def _lambda(lam_ref, lam_init):
    lq1, lk1, lq2, lk2 = (lam_ref[r:r + 1, :] for r in range(4))
    return (jnp.exp(jnp.sum(lq1 * lk1, axis=1, keepdims=True))
            - jnp.exp(jnp.sum(lq2 * lk2, axis=1, keepdims=True)) + lam_init)


def _attn_kernel(*refs, mode, lam_init):
    first_ref, refs = refs[0], refs[1:]
    if mode == "A":
        q_ref, k_ref, vt_ref, lam_ref, gain_ref, o_ref, qc_ref, m_ref, acc_ref, s0_ref, s1_ref, c0_ref, c1_ref = refs
        bias_ref = None
        d_val = 2 * A_HEAD_DIM
    else:
        q_ref, k_ref, bias_ref, vt_ref, o_ref, qc_ref, m_ref, acc_ref, s0_ref, s1_ref, c0_ref, c1_ref = refs
        d_val = C_HEAD_DIM
    tq = q_ref.shape[1]
    tk = k_ref.shape[2]
    qi = pl.program_id(2)
    lane = lax.broadcasted_iota(jnp.int32, (1, LANES), 1)
    low = lane < (LANES // 2)
    q = q_ref[0]
    zero = jnp.zeros_like(q)
    for a in range(2):
        qa = jnp.where(low, q, zero) if a == 0 else jnp.where(low, zero, q)
        if bias_ref is not None:
            pick = jnp.where((lane >= 3 * a) & (lane < 3 * a + 3), 1.0, 0.0).astype(BF16)
            qa = jnp.concatenate([qa, jnp.broadcast_to(pick, (tq, LANES))], axis=1)
        qc_ref[a] = qa
    m_ref[...] = jnp.full(m_ref.shape, NEG, F32)
    acc_ref[...] = jnp.zeros(acc_ref.shape, F32)
    ones = jnp.ones((BF16_ROWS, tk), BF16)

    units = [(a, slice(n * MXU_DIM, (n + 1) * MXU_DIM)) for a in range(2) for n in range(tq // MXU_DIM)]

    def score_chain(j, u):
        a, cs = units[u]
        kc = k_ref[0, j]
        if bias_ref is not None:
            kc = jnp.concatenate([kc, bias_ref[0, j]], axis=1)
        return _dot_nt(kc, qc_ref[a, cs, :])

    def value_chain(j, u, st, cmax, key_off):
        a, cs = units[u]
        masked = key_off is not None
        if masked:
            key = lax.broadcasted_iota(jnp.int32, (tk, MXU_DIM), 0) + key_off
            qry = lax.broadcasted_iota(jnp.int32, (tk, MXU_DIM), 1) + cs.start
            keep = ((key >> CHUNK_SHIFT) <= (qry >> CHUNK_SHIFT)) if mode == "A" else (key <= qry)
            st = jnp.where(keep, st, NEG)
            cmax = jnp.max(st, axis=0, keepdims=True)
        m_prev = m_ref[a, :, cs]
        m_new = jnp.maximum(m_prev, cmax)
        alpha = jnp.exp2(m_prev - m_new)
        pt = jnp.exp2(st - m_new).astype(BF16)
        vt = vt_ref[0, j]
        va = vt if mode == "A" else vt[a * d_val:(a + 1) * d_val]
        va = jnp.concatenate([va, ones], axis=0)
        acc_ref[a, :, cs] = alpha * acc_ref[a, :, cs] + _dot(va, pt)
        m_ref[a, :, cs] = m_new

    def stage(j_scores, dst, j_values, src, key_off=None):
        for t in range(len(units) + 1):
            if j_scores is not None and t < len(units):
                st = score_chain(j_scores, t)
                dst[0][t] = st
                dst[1][t] = jnp.max(st, axis=0, keepdims=True)
            if j_values is not None and t >= 1:
                value_chain(j_values, t - 1, src[0][t - 1], src[1][t - 1], key_off)

    buf0 = (s0_ref, c0_ref)
    buf1 = (s1_ref, c1_ref)
    n_diag = tq // tk
    n_full = qi * n_diag
    j0 = first_ref[pl.program_id(0), pl.program_id(1), qi]
    n_vis = n_full - j0
    def pair(j):
        stage(j + 1, buf1, j, buf0)
        stage(j + 2, buf0, j + 1, buf1)

    def diagonal():
        for d in range(0, n_diag, 2):
            j = n_full + d
            stage(j + 1, buf1, j, buf0, key_off=d * tk)
            stage(j + 2 if d + 2 < n_diag else None, buf0, j + 1, buf1, key_off=(d + 1) * tk)

    def any_window():
        stage(j0, buf0, None, None)

        def quad(p, carry):
            pair(j0 + 4 * p)
            pair(j0 + 4 * p + 2)
            return carry

        lax.fori_loop(0, n_vis >> 2, quad, 0)

        @pl.when((n_vis & 2) == 2)
        def _():
            pair(j0 + (n_vis & ~3))

        diagonal()

    if mode == "C":
        def straight(window):
            stage(j0, buf0, None, None)
            for d in range(0, window, 2):
                pair(j0 + d)
            diagonal()

        for window in SHORT_WINDOWS:
            pl.when(n_vis == window)(functools.partial(straight, window))

        @pl.when(functools.reduce(jnp.logical_and, [n_vis != window for window in SHORT_WINDOWS]))
        def _():
            any_window()
    else:
        any_window()

    outs = []
    for a in range(2):
        acc = acc_ref[a]
        outs.append(acc[:d_val] * (1.0 / acc[d_val:d_val + 1]))
    if mode == "A":
        ot = outs[0] - _lambda(lam_ref, lam_init) * outs[1]
        ot = ot * lax.rsqrt(jnp.mean(ot * ot, axis=0, keepdims=True) + LN_EPS)
        ot = ot * (gain_ref[0] * (1.0 - lam_init))
    else:
        ot = jnp.concatenate(outs, axis=0)
    o_ref[0] = ot.T.astype(o_ref.dtype)


def _attention(q, k, vt, first, mode, lam=None, gain=None, bias=None, lam_init=0.0):
    b, t, width = q.shape
    nk, tk = vt.shape[1], vt.shape[3]
    tq = 2 * tk
    assert t % tq == 0
    groups = width // LANES
    d_aug = (2 * A_HEAD_DIM if mode == "A" else C_HEAD_DIM) + BF16_ROWS
    kspec = pl.BlockSpec((1, nk, tk, LANES), lambda bi, g, i, f: (bi, 0, 0, g))
    in_specs = [pl.BlockSpec((1, tq, LANES), lambda bi, g, i, f: (bi, i, g)), kspec]
    args = [q, k.reshape(b, nk, tk, width)]
    if mode == "C":
        in_specs.append(kspec)
        args.append(bias.reshape(b, nk, tk, groups * LANES))
    in_specs.append(pl.BlockSpec((1, nk, LANES, tk), lambda bi, g, i, f: (bi, 0, g, 0)))
    args.append(vt)
    if mode == "A":
        in_specs += [pl.BlockSpec((4, A_HEAD_DIM), lambda bi, g, i, f: (0, 0)),
                     pl.BlockSpec((1, LANES, 1), lambda bi, g, i, f: (g, 0, 0))]
        args += [lam, gain]
    return pl.pallas_call(
        functools.partial(_attn_kernel, mode=mode, lam_init=lam_init),
        grid_spec=pltpu.PrefetchScalarGridSpec(
            num_scalar_prefetch=1,
            grid=(b, groups, t // tq),
            in_specs=in_specs,
            out_specs=pl.BlockSpec((1, tq, LANES), lambda bi, g, i, f: (bi, i, g)),
            scratch_shapes=[pltpu.VMEM((2, tq, LANES if mode == "A" else 2 * LANES), BF16),
                            pltpu.VMEM((2, 1, tq), F32),
                            pltpu.VMEM((2, d_aug, tq), F32),
                            pltpu.VMEM((2 * tq // MXU_DIM, tk, MXU_DIM), F32),
                            pltpu.VMEM((2 * tq // MXU_DIM, tk, MXU_DIM), F32),
                            pltpu.VMEM((2 * tq // MXU_DIM, 1, MXU_DIM), F32),
                            pltpu.VMEM((2 * tq // MXU_DIM, 1, MXU_DIM), F32)]),
        out_shape=jax.ShapeDtypeStruct((b, t, width), BF16),
        compiler_params=_params(("arbitrary", "arbitrary", "arbitrary")),
        name="attn_" + mode,
    )(first, *args)


def _first_visible_block(bounds, tq_tiles):
    qn, kn, bmax, bmin = (bounds[:, :, r, :C_HEADS] for r in range(4))
    b, tiles, _ = qn.shape
    nq = tiles // tq_tiles
    blk = lambda x, f: f(x.reshape(b, nq, tq_tiles, C_HEADS), axis=2)
    qn_q = blk(qn, jnp.max) * NORM_SLACK
    own = blk(bmin, jnp.min) - qn_q * blk(kn, jnp.max) * NORM_SLACK
    best = qn_q[:, :, None, :] * (kn * NORM_SLACK)[:, None, :, :] + bmax[:, None, :, :]
    dead = best < (own[:, :, None, :] - SKIP_GAP)
    block = jnp.arange(tiles, dtype=jnp.int32)[None, None, :, None]
    lead = jnp.min(jnp.where(dead, tiles, block), axis=2)
    lead = jnp.min(lead.reshape(b, nq, C_HEADS // 2, 2), axis=-1)
    lead = jnp.minimum(lead, (jnp.arange(nq, dtype=jnp.int32) * tq_tiles)[None, :, None])
    return jnp.transpose(lead - lead % 2, (0, 2, 1)).astype(jnp.int32)


def _diff_finish(o0, o1, lam, gain, lam_init):
    o = o0 - lam * o1
    o = o * lax.rsqrt(jnp.mean(o * o, axis=-1, keepdims=True) + LN_EPS)
    return o * gain * (1.0 - lam_init)


def _cached_attn_kernel(*refs, mode, past, lam_init):
    if mode == "A":
        q_ref, kn_ref, vn_ref, ck_ref, cv_ref, lam_ref, gain_ref, o_ref = refs
    else:
        q_ref, kn_ref, vn_ref, ck_ref, cv_ref, clf_ref, nlf_ref, o_ref = refs
    t = q_ref.shape[1]
    width = o_ref.shape[2]
    groups = width // LANES
    lane = lax.broadcasted_iota(jnp.int32, (1, LANES), 1)
    low = lane < (LANES // 2)
    row = lax.broadcasted_iota(jnp.int32, (t, t), 0)
    col = lax.broadcasted_iota(jnp.int32, (t, t), 1)
    if mode == "A":
        keep = ((past + col) >> CHUNK_SHIFT) <= ((past + row) >> CHUNK_SHIFT)
        lam = _lambda(lam_ref, lam_init)
    else:
        keep = col <= row
        r = lax.broadcasted_iota(jnp.int32, (past, past), 0)
        c = lax.broadcasted_iota(jnp.int32, (past, past), 1)
        upper = jnp.where(r <= c, 1.0, 0.0).astype(BF16)
        clf = clf_ref[0]
        hi, mid, lo = _split3(clf)
        cum_c = (_dot(hi, upper) + _dot(mid, upper)) + _dot(lo, upper)
        upper_n = jnp.where(row <= col, 1.0, 0.0).astype(BF16)
        hi, mid, lo = _split3(nlf_ref[0])
        cum_n = ((_dot(hi, upper_n) + _dot(mid, upper_n)) + _dot(lo, upper_n)
                 + jnp.sum(clf, axis=1, keepdims=True))
        bias_c = cum_c * (-LOG2E)
        bias_n = cum_n * (-LOG2E)
    keep2 = jnp.concatenate([keep, keep], axis=0)
    first_rows = lax.broadcasted_iota(jnp.int32, (2 * t, 1), 0) < t
    for g in range(groups):
        sl = slice(g * LANES, (g + 1) * LANES)
        q = q_ref[0, :, sl]
        kn = kn_ref[0, :, sl]
        vn = vn_ref[0, :, sl].astype(BF16)
        if mode == "A":
            kc = ck_ref[0, 0, :, g, :].astype(BF16)
            vc = cv_ref[0, 0, :, g, :].astype(BF16)
        else:
            kc = ck_ref[0, :, sl].astype(BF16)
            vc = cv_ref[0, :, sl].astype(BF16)
        zero = jnp.zeros_like(q)
        q2 = jnp.concatenate([jnp.where(low, q, zero), jnp.where(low, zero, q)], axis=0)
        s_c = _dot_nt(q2, kc)
        s_n = _dot_nt(q2, kn)
        if mode == "C":
            h = 2 * g
            s_c = s_c + jnp.where(first_rows, bias_c[h:h + 1, :], bias_c[h + 1:h + 2, :])
            s_n = s_n + jnp.where(first_rows, bias_n[h:h + 1, :], bias_n[h + 1:h + 2, :])
        s_n = jnp.where(keep2, s_n, NEG)
        m = jnp.maximum(jnp.max(s_c, axis=1, keepdims=True), jnp.max(s_n, axis=1, keepdims=True))
        p_c = jnp.exp2(s_c - m)
        p_n = jnp.exp2(s_n - m)
        l = jnp.sum(p_c, axis=1, keepdims=True) + jnp.sum(p_n, axis=1, keepdims=True)
        acc = (_dot(p_c.astype(BF16), vc) + _dot(p_n.astype(BF16), vn)) * (1.0 / l)
        outs = [acc[:t], acc[t:]]
        if mode == "A":
            o = _diff_finish(outs[0], outs[1], lam, gain_ref[g], lam_init)
        else:
            o = jnp.where(low, outs[0], outs[1])
        o_ref[0, :, sl] = o.astype(o_ref.dtype)


def _cached_attention(q, kn, vn, cache_k, cache_v, mode, lam=None, gain=None, cache_lft=None, new_lft=None,
                      lam_init=0.0):
    b, t, width = q.shape
    new = pl.BlockSpec((1, t, width), lambda bi: (bi, 0, 0))
    if mode == "A":
        _, _, past, heads, dh = cache_k.shape
        old = pl.BlockSpec((1, 1, past, heads, dh), lambda bi: (0, bi, 0, 0, 0))
    else:
        past = cache_k.shape[1]
        old = pl.BlockSpec((1, past, width), lambda bi: (bi, 0, 0))
    in_specs = [new, new, new, old, old]
    args = [q, kn, vn, cache_k, cache_v]
    if mode == "A":
        in_specs += [pl.BlockSpec((4, A_HEAD_DIM), lambda bi: (0, 0)),
                     pl.BlockSpec((A_HEADS, 1, LANES), lambda bi: (0, 0, 0))]
        args += [lam, gain]
    else:
        in_specs += [pl.BlockSpec((1, C_HEADS, past), lambda bi: (bi, 0, 0)),
                     pl.BlockSpec((1, C_HEADS, t), lambda bi: (bi, 0, 0))]
        args += [cache_lft, new_lft]
    return pl.pallas_call(
        functools.partial(_cached_attn_kernel, mode=mode, past=past, lam_init=lam_init),
        grid=(b,),
        in_specs=in_specs,
        out_specs=new,
        out_shape=jax.ShapeDtypeStruct((b, t, width), BF16),
        compiler_params=_params(("arbitrary",)),
        name="cached_attn_" + mode,
    )(*args)


def _gelu_tanh(x):
    return 0.5 * x * (1.0 + jnp.tanh(math.sqrt(2.0 / math.pi) * (x + 0.044715 * (x * x * x))))


def _causal_conv(x, tail, w_ref, b_ref):
    width = w_ref.shape[0]
    tm = x.shape[0]
    cat = jnp.concatenate([tail, x], axis=0)
    y = None
    for j in range(width):
        back = width - 1 - j
        src = cat if back == 0 else pltpu.roll(cat, back, 0)
        term = src[SUBLANES:SUBLANES + tm] * w_ref[j:j + 1, :]
        y = term if y is None else y + term
    return y + b_ref[...]


def _rglru_kernel(xg_ref, h0_ref, cb_ref, cw_ref, cbias_ref, wa_ref, ba_ref, wx_ref, bx_ref, sp_ref,
                  y_ref, hl_ref, ct_ref, hc_ref, tail_ref):
    i = pl.program_id(1)
    tm = xg_ref.shape[1]

    @pl.when(i == 0)
    def _():
        hc_ref[...] = h0_ref[0]
        tail_ref[...] = cb_ref[0]

    x = xg_ref[0, :, :B_WIDTH]
    gate_in = xg_ref[0, :, B_WIDTH:]
    xc = _causal_conv(x, tail_ref[...], cw_ref, cbias_ref)
    xcb = xc.astype(BF16)
    r = _sigmoid(_dot(xcb, wa_ref[...]) + ba_ref[...])
    ig = _sigmoid(_dot(xcb, wx_ref[...]) + bx_ref[...])
    log_a = (-RG_C) * r * sp_ref[...]
    a = jnp.exp(log_a)
    th = jnp.tanh(log_a)
    bx = jnp.sqrt((-2.0 * th) / (1.0 - th)) * (ig * xc)
    pos = lax.broadcasted_iota(jnp.int32, (tm, 1), 0) & (SUBLANES - 1)
    s = 1
    while s < SUBLANES:
        valid = pos >= s
        a_sh = pltpu.roll(a, s, 0)
        b_sh = pltpu.roll(bx, s, 0)
        bx = jnp.where(valid, a * b_sh + bx, bx)
        a = jnp.where(valid, a * a_sh, a)
        s *= 2
    carry = hc_ref[...]
    groups = []
    for g in range(tm // SUBLANES):
        rows = slice(g * SUBLANES, (g + 1) * SUBLANES)
        hg = a[rows] * carry + bx[rows]
        groups.append(hg)
        carry = hg[SUBLANES - 1:, :]
    h = jnp.concatenate(groups, axis=0)
    y_ref[0] = (h * _gelu_tanh(gate_in)).astype(y_ref.dtype)
    h_tail = h[tm - SUBLANES:, :]
    x_tail = x[tm - SUBLANES:, :]
    hc_ref[...] = h_tail[SUBLANES - 1:, :]
    tail_ref[...] = x_tail
    hl_ref[0] = h_tail
    ct_ref[0] = x_tail


def _rglru(xg, h0, conv_tail, cw, cbias, wa, ba, wx, bx, sp, tm_pref=256):
    b, t, _ = xg.shape
    tm = _row_tile(t, tm_pref)
    wspec = lambda shape: pl.BlockSpec(shape, lambda bi, i: tuple(0 for _ in shape))
    return pl.pallas_call(
        _rglru_kernel,
        grid=(b, t // tm),
        in_specs=[pl.BlockSpec((1, tm, 2 * B_WIDTH), lambda bi, i: (bi, i, 0)),
                  pl.BlockSpec((1, 1, B_WIDTH), lambda bi, i: (bi, 0, 0)),
                  pl.BlockSpec((1, SUBLANES, B_WIDTH), lambda bi, i: (bi, 0, 0)),
                  wspec((B_CONV, B_WIDTH)), wspec((1, B_WIDTH)),
                  wspec((B_WIDTH, B_WIDTH)), wspec((1, B_WIDTH)),
                  wspec((B_WIDTH, B_WIDTH)), wspec((1, B_WIDTH)), wspec((1, B_WIDTH))],
        out_specs=[pl.BlockSpec((1, tm, B_WIDTH), lambda bi, i: (bi, i, 0)),
                   pl.BlockSpec((1, SUBLANES, B_WIDTH), lambda bi, i: (bi, 0, 0)),
                   pl.BlockSpec((1, SUBLANES, B_WIDTH), lambda bi, i: (bi, 0, 0))],
        out_shape=[jax.ShapeDtypeStruct((b, t, B_WIDTH), BF16),
                   jax.ShapeDtypeStruct((b, SUBLANES, B_WIDTH), F32),
                   jax.ShapeDtypeStruct((b, SUBLANES, B_WIDTH), F32)],
        scratch_shapes=[pltpu.VMEM((1, B_WIDTH), F32), pltpu.VMEM((SUBLANES, B_WIDTH), F32)],
        compiler_params=_params(("arbitrary", "arbitrary")),
        name="rglru",
    )(xg, h0, conv_tail, cw, cbias, wa, ba, wx, bx, sp)


def _proj_ln_kernel(*refs, n_in):
    h_refs = refs[:n_in]
    w_ref, x_ref, gate_ref, g_ref, b_ref, o_ref = refs[n_in:]
    tm = x_ref.shape[1]
    rows = [slice(r, min(r + LN_ROWS, tm)) for r in range(0, tm, LN_ROWS)]

    def matmul(rs):
        hs = [r[0, rs, :] for r in h_refs]
        return _dot(hs[0] if n_in == 1 else jnp.concatenate(hs, axis=-1), w_ref[0])

    def norm(rs, proj):
        y = ALPHA * x_ref[0, rs, :] + gate_ref[0] * proj
        mu = jnp.mean(y, axis=-1, keepdims=True)
        yc = y - mu
        var = jnp.mean(yc * yc, axis=-1, keepdims=True)
        o_ref[0, rs, :] = yc * lax.rsqrt(var + LN_EPS) * g_ref[...] + b_ref[...]

    pending = matmul(rows[0])
    for r in range(len(rows)):
        nxt = matmul(rows[r + 1]) if r + 1 < len(rows) else None
        norm(rows[r], pending)
        pending = nxt


def _proj_ln(hs, w, layer, x, gate, ln_g, ln_b, tm_pref=1024):
    b, t, d = x.shape
    tm = _row_tile(t, tm_pref)
    k = w.shape[1]
    gate_spec = (pl.BlockSpec((1, 1, d), lambda bi, i: (bi, 0, 0)) if gate.shape[1] == 1
                 else pl.BlockSpec((1, tm, d), lambda bi, i: (bi, i, 0)))
    in_specs = [pl.BlockSpec((1, tm, h.shape[2]), lambda bi, i: (bi, i, 0)) for h in hs]
    in_specs += [pl.BlockSpec((1, k, d), lambda bi, i: (layer, 0, 0)),
                 pl.BlockSpec((1, tm, d), lambda bi, i: (bi, i, 0)),
                 gate_spec,
                 pl.BlockSpec((1, d), lambda bi, i: (0, 0)),
                 pl.BlockSpec((1, d), lambda bi, i: (0, 0))]
    return pl.pallas_call(
        functools.partial(_proj_ln_kernel, n_in=len(hs)),
        grid=(b, t // tm),
        in_specs=in_specs,
        out_specs=pl.BlockSpec((1, tm, d), lambda bi, i: (bi, i, 0)),
        out_shape=jax.ShapeDtypeStruct((b, t, d), F32),
        compiler_params=_params(("arbitrary", "arbitrary")),
        name="proj_ln",
    )(*hs, w, x, gate, ln_g, ln_b)


def _ffn_up_kernel(x_ref, sc_ref, sh_ref, w_ref, cb_ref, cw_ref, cbias_ref, h_ref, ct_ref, tail_ref):
    i = pl.program_id(1)
    seqs, tm = x_ref.shape[0], x_ref.shape[1]

    @pl.when(i == 0)
    def _():
        tail_ref[...] = cb_ref[...]

    us = [_modulate(x_ref[s], sc_ref[s], sh_ref[s]) for s in range(seqs)]
    u = us[0] if seqs == 1 else jnp.concatenate(us, axis=0)
    chunks = [slice(c, min(c + FFN_CHUNK, D_FF)) for c in range(0, D_FF, FFN_CHUNK)]

    def matmuls(cs):
        gs = slice(D_FF + cs.start, D_FF + cs.stop)
        return _dot(u, w_ref[0, :, cs]), _dot(u, w_ref[0, :, gs])

    def gate(cs, a_all, g_all):
        for s in range(seqs):
            rows = slice(s * tm, (s + 1) * tm)
            a, g = a_all[rows], g_all[rows]
            gc = _causal_conv(g, tail_ref[s, :, cs], cw_ref.at[:, cs], cbias_ref.at[:, cs])
            silu = gc * (0.5 * jnp.tanh(0.5 * gc) + 0.5)
            h_ref[s, :, cs] = (a * silu).astype(h_ref.dtype)
            g_tail = g[tm - SUBLANES:, :]
            tail_ref[s, :, cs] = g_tail
            ct_ref[s, :, cs] = g_tail

    pending = matmuls(chunks[0])
    for c in range(len(chunks)):
        nxt = matmuls(chunks[c + 1]) if c + 1 < len(chunks) else None
        gate(chunks[c], *pending)
        pending = nxt


def _ffn_up(x, sc, sh, w, layer, conv_tail, cw, cbias, tm_pref=256):
    b, t, d = x.shape
    tm = _row_tile(t, tm_pref)
    seqs = max(1, min(b, LANES // tm))
    while b % seqs:
        seqs -= 1
    return pl.pallas_call(
        _ffn_up_kernel,
        grid=(b // seqs, t // tm),
        in_specs=[pl.BlockSpec((seqs, tm, d), lambda bi, i: (bi, i, 0)),
                  pl.BlockSpec((seqs, 1, d), lambda bi, i: (bi, 0, 0)),
                  pl.BlockSpec((seqs, 1, d), lambda bi, i: (bi, 0, 0)),
                  pl.BlockSpec((1, d, 2 * D_FF), lambda bi, i: (layer, 0, 0)),
                  pl.BlockSpec((seqs, SUBLANES, D_FF), lambda bi, i: (bi, 0, 0)),
                  pl.BlockSpec((FFN_CONV, D_FF), lambda bi, i: (0, 0)),
                  pl.BlockSpec((1, D_FF), lambda bi, i: (0, 0))],
        out_specs=[pl.BlockSpec((seqs, tm, D_FF), lambda bi, i: (bi, i, 0)),
                   pl.BlockSpec((seqs, SUBLANES, D_FF), lambda bi, i: (bi, 0, 0))],
        out_shape=[jax.ShapeDtypeStruct((b, t, D_FF), BF16),
                   jax.ShapeDtypeStruct((b, SUBLANES, D_FF), F32)],
        scratch_shapes=[pltpu.VMEM((seqs, SUBLANES, D_FF), F32)],
        compiler_params=_params(("arbitrary", "arbitrary")),
        name="ffn_up",
    )(x, sc, sh, w, conv_tail, cw, cbias)


def _pad_tail(buf):
    return jnp.pad(buf, ((0, 0), (SUBLANES - buf.shape[1], 0), (0, 0)))


def _rope_tables(past, t):
    half = A_HEAD_DIM // 2
    inv = ROPE_THETA ** (-jnp.arange(0, A_HEAD_DIM, 2, dtype=F32) / A_HEAD_DIM)
    pos = (past + jnp.arange(t, dtype=jnp.int32)).astype(F32)
    ang = pos[:, None] * inv[None, :]
    cos = jnp.tile(jnp.cos(ang), (1, LANES // half))
    sin = jnp.sin(ang)
    sin_signed = jnp.tile(jnp.concatenate([-sin, sin], axis=1), (1, LANES // A_HEAD_DIM))
    return cos, sin_signed


def _block_diag(w):
    n, i, o = w.shape
    return jnp.einsum("nio,nm->nimo", w, jnp.eye(n, dtype=w.dtype)).reshape(n * i, n * o)


def _prepare(p):
    w = {}
    w_in_ab = p["w_in_ab"][0]
    w["in_ab"] = w_in_ab.astype(BF16)
    w["lam"] = jnp.stack([p["lam_q1"][0], p["lam_k1"][0], p["lam_q2"][0], p["lam_k2"][0]])
    w["gain_row"] = p["attn_gain"][0].reshape(A_HEADS, 1, LANES)
    w["gain_col"] = p["attn_gain"][0].reshape(A_HEADS, LANES, 1)
    w["b_conv_w"] = p["b_conv_w"][0]
    w["b_conv_b"] = p["b_conv_b"][0].reshape(1, B_WIDTH)
    w["rg_a"] = _block_diag(p["w_rg_a"][0]).astype(BF16)
    w["rg_x"] = _block_diag(p["w_rg_x"][0]).astype(BF16)
    w["b_rg_a"] = p["b_rg_a"][0].reshape(1, B_WIDTH)
    w["b_rg_x"] = p["b_rg_x"][0].reshape(1, B_WIDTH)
    w["rg_L"] = p["rg_L"][0].reshape(1, B_WIDTH)
    w_in_c = p["w_in_c"][0]
    w["in_c"] = jnp.pad(w_in_c, ((0, 0), (0, LANES - C_HEADS))).astype(BF16)
    w["in_c_ft"] = w_in_c[:, 3 * C_WIDTH:].T.astype(BF16)
    w["bf_row"] = jnp.pad(p["b_f"][0], (0, LANES - C_HEADS)).reshape(1, LANES)
    w["bf_col"] = p["b_f"][0].reshape(C_HEADS, 1)
    w["place"] = _bias_placement()
    w["seg"] = jnp.asarray(np.repeat(np.eye(LANES, dtype=np.float32)[:C_HEADS], C_HEAD_DIM, axis=0), BF16)
    w["up"] = p["w_up"].astype(BF16)
    w["down"] = p["w_down"].astype(BF16)
    w["out_ab"] = p["w_out_ab"].astype(BF16)
    w["out_c"] = p["w_out_c"].astype(BF16)
    w["ffn_conv_w"] = [p["ffn_conv_w"][i] for i in range(DEPTH)]
    w["ffn_conv_b"] = [p["ffn_conv_b"][i].reshape(1, D_FF) for i in range(DEPTH)]
    w["ln1_g"] = [p["ln1_g"][i].reshape(1, D_MODEL) for i in range(DEPTH)]
    w["ln1_b"] = [p["ln1_b"][i].reshape(1, D_MODEL) for i in range(DEPTH)]
    w["ln2_g"] = [p["ln2_g"][i].reshape(1, D_MODEL) for i in range(DEPTH)]
    w["ln2_b"] = [p["ln2_b"][i].reshape(1, D_MODEL) for i in range(DEPTH)]
    return w


def _softplus_kernel(x_ref, o_ref):
    o_ref[...] = _softplus(-x_ref[...])


def _proj_ln_rows(hs, w, layer, x, gate, ln_g, ln_b):
    b, t, d = x.shape
    flat = lambda a: a.reshape(1, b * t, a.shape[2])
    gate_rows = jnp.broadcast_to(gate, (b, t, d))
    return _proj_ln([flat(h) for h in hs], w, layer, flat(x), flat(gate_rows), ln_g, ln_b).reshape(b, t, d)


def _trunk(x, mods, w, sp, cache_a_k=None, cache_a_v=None, state_b_h=None, state_b_conv=None,
           cache_c_k=None, cache_c_v=None, cache_c_logf=None, state_ffn_conv=None):
    b, t, d = x.shape
    cached = cache_a_k is not None
    proj_ln = _proj_ln_rows if cached else _proj_ln
    past = cache_a_k.shape[2] if cached else 0
    outs = {}
    for i in range(DEPTH):
        sh1, sc1, g1, sh2, sc2, g2 = [m[:, None, :] for m in jnp.split(mods[i], 6, axis=-1)]
        if i % 2 == 0:
            lam_init = 0.8 - 0.6 * math.exp(-0.3 * i)
            cos, sin = _rope_tables(past, t)
            q, kb, vt, k32, v32, xg = _proj_ab(x, sc1, sh1, w["in_ab"], cos, sin)
            if cached:
                o = _cached_attention(q, kb, v32.reshape(b, t, A_WIDTH), cache_a_k, cache_a_v,
                                      "A", lam=w["lam"], gain=w["gain_row"], lam_init=lam_init)
                h0 = state_b_h[0][:, None, :]
                ctail = _pad_tail(state_b_conv[0])
            else:
                first = jnp.zeros((b, A_HEADS, t // (2 * ATTN_BLOCK)), jnp.int32)
                o = _attention(q, kb, vt, first, "A", lam=w["lam"], gain=w["gain_col"], lam_init=lam_init)
                h0 = jnp.zeros((b, 1, B_WIDTH), F32)
                ctail = jnp.zeros((b, SUBLANES, B_WIDTH), F32)
            yb, h_tail, x_tail = _rglru(xg, h0, ctail, w["b_conv_w"], w["b_conv_b"], w["rg_a"], w["b_rg_a"],
                                        w["rg_x"], w["b_rg_x"], sp)
            outs["a_k"] = k32[None]
            outs["a_v"] = v32[None]
            outs["b_h"] = h_tail[:, SUBLANES - 1, :][None]
            outs["b_conv"] = x_tail[:, SUBLANES - (B_CONV - 1):, :][None]
            x = proj_ln([o, yb], w["out_ab"], i // 2, x, g1, w["ln1_g"][i], w["ln1_b"][i])
        else:
            q, kb, bias, vt, k32, v32, lf, lft, bounds = _proj_c(x, sc1, sh1, w["in_c"], w["in_c_ft"],
                                                                 w["bf_row"], w["bf_col"], w["place"], w["seg"])
            if cached:
                o = _cached_attention(q, kb, v32, cache_c_k[0].reshape(b, past, C_WIDTH),
                                      cache_c_v[0].reshape(b, past, C_WIDTH),
                                      "C", cache_lft=jnp.swapaxes(cache_c_logf[0], 1, 2), new_lft=lft)
            else:
                o = _attention(q, kb, vt, _first_visible_block(bounds, 2), "C", bias=bias)
            outs["c_k"] = k32.reshape(1, b, t, C_HEADS, C_HEAD_DIM)
            outs["c_v"] = v32.reshape(1, b, t, C_HEADS, C_HEAD_DIM)
            outs["c_logf"] = lf[None]
            x = proj_ln([o], w["out_c"], i // 2, x, g1, w["ln1_g"][i], w["ln1_b"][i])
        ftail = _pad_tail(state_ffn_conv[i]) if cached else jnp.zeros((b, SUBLANES, D_FF), F32)
        hmid, g_tail = _ffn_up(x, sc2, sh2, w["up"], i, ftail, w["ffn_conv_w"][i], w["ffn_conv_b"][i])
        outs.setdefault("ffn", []).append(g_tail[:, SUBLANES - (FFN_CONV - 1):, :])
        x = proj_ln([hmid], w["down"], i, x, g2, w["ln2_g"][i], w["ln2_b"][i])
    return (x, outs["a_k"], outs["a_v"], outs["b_h"], outs["b_conv"],
            outs["c_k"], outs["c_v"], outs["c_logf"], jnp.stack(outs["ffn"]))


def kernel(x_prompt, x_sample, c_prompt, c_sample, cache_a_k, cache_a_v, state_b_h, state_b_conv, cache_c_k, cache_c_v, cache_c_logf, state_ffn_conv, w_ada, b_ada, ln1_g, ln1_b, ln2_g, ln2_b, w_in_ab, lam_q1, lam_k1, lam_q2, lam_k2, attn_gain, b_conv_w, b_conv_b, w_rg_a, b_rg_a, w_rg_x, b_rg_x, rg_L, w_out_ab, w_in_c, b_f, w_out_c, w_up, ffn_conv_w, ffn_conv_b, w_down):
    p = dict(w_in_ab=w_in_ab, lam_q1=lam_q1, lam_k1=lam_k1, lam_q2=lam_q2, lam_k2=lam_k2, attn_gain=attn_gain,
             b_conv_w=b_conv_w, b_conv_b=b_conv_b, w_rg_a=w_rg_a, b_rg_a=b_rg_a, w_rg_x=w_rg_x, b_rg_x=b_rg_x,
             rg_L=rg_L, w_out_ab=w_out_ab, w_in_c=w_in_c, b_f=b_f, w_out_c=w_out_c, w_up=w_up,
             ffn_conv_w=ffn_conv_w, ffn_conv_b=ffn_conv_b, w_down=w_down,
             ln1_g=ln1_g, ln1_b=ln1_b, ln2_g=ln2_g, ln2_b=ln2_b)
    w = _prepare(p)
    bp = c_prompt.shape[0]
    bs = c_sample.shape[0]
    rows = -(-(bp + bs) // 16) * 16
    c_all = jnp.pad(jnp.concatenate([c_prompt, c_sample], axis=0), ((0, rows - bp - bs), (0, 0)))
    mods = _mods(c_all, w_ada, b_ada)
    sp = pl.pallas_call(_softplus_kernel, out_shape=jax.ShapeDtypeStruct((1, B_WIDTH), F32),
                        name="softplus")(w["rg_L"])
    res_p = _trunk(x_prompt, mods[:, :bp], w, sp)
    res_s = _trunk(x_sample, mods[:, bp:bp + bs], w, sp, cache_a_k, cache_a_v, state_b_h, state_b_conv,
                   cache_c_k, cache_c_v, cache_c_logf, state_ffn_conv)
    return (res_p[0], res_s[0]) + res_p[1:] + res_s[1:]
```

```python
import functools
import math

import numpy as np
import jax
import jax.numpy as jnp
from jax import lax
from jax.experimental import pallas as pl
from jax.experimental.pallas import tpu as pltpu

F32 = jnp.float32
BF16 = jnp.bfloat16

D_MODEL = 1024
DEPTH = 2
CHUNK = 64
CHUNK_SHIFT = 6
A_HEADS = 4
A_HEAD_DIM = 64
A_WIDTH = A_HEADS * 2 * A_HEAD_DIM
B_WIDTH = 512
B_BLOCKS = 8
B_CONV = 4
RG_C = 8.0
C_HEADS = 16
C_HEAD_DIM = 64
C_WIDTH = C_HEADS * C_HEAD_DIM
D_FF = 2816
FFN_CONV = 3
ROPE_THETA = 10000.0
ALPHA = (2 * DEPTH) ** 0.25
LN_EPS = 1e-5
NEG = -1e30
LOG2E = 1.4426950408889634

LANES = 128
SUBLANES = 8
BF16_ROWS = 16
MXU_DIM = 256
VMEM_LIMIT = 56 * 1024 * 1024
BIAS_WIDTH = (C_HEADS // 2) * LANES
FFN_CHUNK = 256
LN_ROWS = 256
SKIP_GAP = 160.0
NORM_SLACK = 1.03
SHORT_WINDOWS = (4, 6)
ATTN_BLOCK = 512


def _params(sem, flags=None):
    return pltpu.CompilerParams(dimension_semantics=sem, vmem_limit_bytes=VMEM_LIMIT, flags=flags)


def _row_tile(t, pref):
    if t <= pref:
        return t
    tm = pref
    while t % tm:
        tm //= 2
    return tm


def _modulate(x, sc, sh):
    return (x * (1.0 + sc) + sh).astype(BF16)


def _sigmoid(x):
    return 1.0 / (1.0 + jnp.exp(-x))


def _softplus(x):
    return jnp.maximum(x, 0.0) + jnp.log1p(jnp.exp(-jnp.abs(x)))


def _log_sigmoid(x):
    return jnp.minimum(x, 0.0) - jnp.log1p(jnp.exp(-jnp.abs(x)))


def _split3(x):
    hi = x.astype(BF16)
    r1 = x - hi.astype(F32)
    mid = r1.astype(BF16)
    lo = (r1 - mid.astype(F32)).astype(BF16)
    return hi, mid, lo


def _dot(a, b):
    return jnp.dot(a, b, preferred_element_type=F32)


def _dot_nt(a, b):
    return lax.dot_general(a, b, (((1,), (1,)), ((), ())), preferred_element_type=F32)


def _mods_kernel(c_ref, w_ref, b_ref, o_ref):
    c = c_ref[...]
    s = (c * _sigmoid(c)).astype(BF16)
    o_ref[0] = _dot(s, w_ref[0].astype(BF16)) + b_ref[0]


def _mods(c_all, w_ada, b_ada):
    rows, d = c_all.shape
    n = w_ada.shape[-1]
    tn = 1536
    return pl.pallas_call(
        _mods_kernel,
        grid=(DEPTH, n // tn),
        in_specs=[pl.BlockSpec((rows, d), lambda l, j: (0, 0)),
                  pl.BlockSpec((1, d, tn), lambda l, j: (l, 0, j)),
                  pl.BlockSpec((1, 1, tn), lambda l, j: (l, 0, j))],
        out_specs=pl.BlockSpec((1, rows, tn), lambda l, j: (l, 0, j)),
        out_shape=jax.ShapeDtypeStruct((DEPTH, rows, n), F32),
        compiler_params=_params(("arbitrary", "arbitrary")),
        name="mods",
    )(c_all, w_ada, b_ada.reshape(DEPTH, 1, n))


def _rope_slab(x, cos, sin_signed, first_half):
    fwd = pltpu.roll(x, LANES - A_HEAD_DIM // 2, 1)
    bwd = pltpu.roll(x, A_HEAD_DIM // 2, 1)
    partner = jnp.where(first_half, fwd, bwd)
    return x * cos + partner * sin_signed


def _proj_ab_kernel(x_ref, sc_ref, sh_ref, w_ref, cos_ref, sin_ref,
                    q_ref, kb_ref, vt_ref, k_ref, v_ref, xg_ref, *, q_scale):
    seqs, tm = x_ref.shape[0], x_ref.shape[1]
    us = [_modulate(x_ref[s], sc_ref[s], sh_ref[s]) for s in range(seqs)]
    pr = _dot(us[0] if seqs == 1 else jnp.concatenate(us, axis=0), w_ref[...])
    cos = cos_ref[...] if seqs == 1 else jnp.concatenate([cos_ref[...]] * seqs, axis=0)
    sin = sin_ref[...] if seqs == 1 else jnp.concatenate([sin_ref[...]] * seqs, axis=0)
    lane = lax.broadcasted_iota(jnp.int32, (1, LANES), 1)
    first_half = (lane & (A_HEAD_DIM - 1)) < (A_HEAD_DIM // 2)
    seq_rows = [slice(s * tm, (s + 1) * tm) for s in range(seqs)]
    for h in range(A_HEADS):
        sl = slice(h * LANES, (h + 1) * LANES)
        q = (_rope_slab(pr[:, sl], cos, sin, first_half) * q_scale).astype(BF16)
        k = _rope_slab(pr[:, A_WIDTH + h * LANES:A_WIDTH + (h + 1) * LANES], cos, sin, first_half)
        v = pr[:, 2 * A_WIDTH + h * LANES:2 * A_WIDTH + (h + 1) * LANES]
        for s, rows in enumerate(seq_rows):
            q_ref[s, :, sl] = q[rows]
            k_ref[s, :, h, :] = k[rows]
            kb_ref[s, :, sl] = k[rows].astype(BF16)
            v_ref[s, :, h, :] = v[rows]
    for s, rows in enumerate(seq_rows):
        vt_ref[s, 0] = pr[rows, 2 * A_WIDTH:3 * A_WIDTH].T.astype(BF16)
        xg_ref[s] = pr[rows, 3 * A_WIDTH:]


def _proj_ab(x, sc, sh, w, cos, sin):
    b, t, d = x.shape
    tm = _row_tile(t, ATTN_BLOCK)
    nt = t // tm
    n = w.shape[1]
    q_scale = A_HEAD_DIM ** -0.5 * LOG2E
    seqs = max(1, min(b, LANES // tm))
    while b % seqs:
        seqs -= 1
    row = lambda width: pl.BlockSpec((seqs, tm, width), lambda bi, i: (bi, i, 0))
    const = lambda shape: pl.BlockSpec(shape, lambda bi, i: tuple(0 for _ in shape))
    heads = pl.BlockSpec((seqs, tm, A_HEADS, LANES), lambda bi, i: (bi, i, 0, 0))
    return pl.pallas_call(
        functools.partial(_proj_ab_kernel, q_scale=q_scale),
        grid=(b // seqs, nt),
        in_specs=[row(d),
                  pl.BlockSpec((seqs, 1, d), lambda bi, i: (bi, 0, 0)),
                  pl.BlockSpec((seqs, 1, d), lambda bi, i: (bi, 0, 0)),
                  const((d, n)),
                  pl.BlockSpec((tm, LANES), lambda bi, i: (i, 0)),
                  pl.BlockSpec((tm, LANES), lambda bi, i: (i, 0))],
        out_specs=[row(A_WIDTH), row(A_WIDTH),
                   pl.BlockSpec((seqs, 1, A_WIDTH, tm), lambda bi, i: (bi, i, 0, 0)),
                   heads, heads, row(2 * B_WIDTH)],
        out_shape=[jax.ShapeDtypeStruct((b, t, A_WIDTH), BF16),
                   jax.ShapeDtypeStruct((b, t, A_WIDTH), BF16),
                   jax.ShapeDtypeStruct((b, nt, A_WIDTH, tm), BF16),
                   jax.ShapeDtypeStruct((b, t, A_HEADS, LANES), F32),
                   jax.ShapeDtypeStruct((b, t, A_HEADS, LANES), F32),
                   jax.ShapeDtypeStruct((b, t, 2 * B_WIDTH), F32)],
        compiler_params=_params(("arbitrary", "arbitrary")),
        name="proj_ab",
    )(x, sc, sh, w, cos, sin)


def _bias_placement():
    e = np.zeros((3 * LANES, BIAS_WIDTH), np.float32)
    for piece in range(3):
        for h in range(C_HEADS):
            e[piece * LANES + h, (h // 2) * LANES + 3 * (h % 2) + piece] = 1.0
    return jnp.asarray(e, BF16)


def _head_norm_max(x, seg):
    n2 = _dot((x * x).astype(BF16), seg)
    return jnp.sqrt(jnp.max(n2, axis=0, keepdims=True))


def _proj_c_kernel(x_ref, sc_ref, sh_ref, w_ref, wft_ref, bfr_ref, bfc_ref, place_ref, seg_ref,
                   q_ref, kb_ref, bias_ref, vt_ref, k_ref, v_ref, lf_ref, lft_ref, bound_ref, run_ref, *, q_scale):
    i = pl.program_id(1)
    tm = x_ref.shape[1]

    @pl.when(i == 0)
    def _():
        run_ref[...] = jnp.zeros_like(run_ref)

    u = _modulate(x_ref[0], sc_ref[0], sh_ref[0])
    pr = _dot(u, w_ref[...])
    qs = pr[:, :C_WIDTH] * q_scale
    q_ref[0] = qs.astype(BF16)
    k = pr[:, C_WIDTH:2 * C_WIDTH]
    k_ref[0] = k
    kb_ref[0] = k.astype(BF16)
    v = pr[:, 2 * C_WIDTH:3 * C_WIDTH]
    v_ref[0] = v
    vt_ref[0, 0] = v.T.astype(BF16)
    lf = _log_sigmoid(pr[:, 3 * C_WIDTH:] + bfr_ref[...])
    lf_ref[0] = lf[:, :C_HEADS]
    lft_ref[0] = _log_sigmoid(_dot_nt(wft_ref[...], u) + bfc_ref[...])
    r = lax.broadcasted_iota(jnp.int32, (tm, tm), 0)
    c = lax.broadcasted_iota(jnp.int32, (tm, tm), 1)
    lower = jnp.where(c <= r, 1.0, 0.0).astype(BF16)
    hi, mid, lo = _split3(lf)
    cum = (_dot(lower, hi) + _dot(lower, mid)) + _dot(lower, lo) + run_ref[...]
    bias = cum * (-LOG2E)
    pieces = jnp.concatenate(_split3(bias), axis=1)
    bias_ref[0] = _dot(pieces, place_ref[...]).astype(BF16)
    seg = seg_ref[...]
    bound_ref[0, 0] = jnp.concatenate(
        [_head_norm_max(qs, seg), _head_norm_max(k, seg),
         jnp.max(bias, axis=0, keepdims=True), jnp.min(bias, axis=0, keepdims=True),
         jnp.zeros((SUBLANES - 4, LANES), F32)], axis=0)
    run_ref[...] = run_ref[...] + jnp.sum(lf, axis=0, keepdims=True)


def _proj_c(x, sc, sh, w, wft, bf_row, bf_col, place, seg):
    b, t, d = x.shape
    tm = _row_tile(t, ATTN_BLOCK)
    nt = t // tm
    n = w.shape[1]
    q_scale = C_HEAD_DIM ** -0.5 * LOG2E
    row = lambda width: pl.BlockSpec((1, tm, width), lambda bi, i: (bi, i, 0))
    const = lambda shape: pl.BlockSpec(shape, lambda bi, i: tuple(0 for _ in shape))
    return pl.pallas_call(
        functools.partial(_proj_c_kernel, q_scale=q_scale),
        grid=(b, nt),
        in_specs=[row(d),
                  pl.BlockSpec((1, 1, d), lambda bi, i: (bi, 0, 0)),
                  pl.BlockSpec((1, 1, d), lambda bi, i: (bi, 0, 0)),
                  const((d, n)), const((C_HEADS, d)),
                  const((1, LANES)), const((C_HEADS, 1)), const((3 * LANES, BIAS_WIDTH)), const((C_WIDTH, LANES))],
        out_specs=[row(C_WIDTH), row(C_WIDTH), row(BIAS_WIDTH),
                   pl.BlockSpec((1, 1, C_WIDTH, tm), lambda bi, i: (bi, i, 0, 0)),
                   row(C_WIDTH), row(C_WIDTH), row(C_HEADS),
                   pl.BlockSpec((1, C_HEADS, tm), lambda bi, i: (bi, 0, i)),
                   pl.BlockSpec((1, 1, SUBLANES, LANES), lambda bi, i: (bi, i, 0, 0))],
        out_shape=[jax.ShapeDtypeStruct((b, t, C_WIDTH), BF16),
                   jax.ShapeDtypeStruct((b, t, C_WIDTH), BF16),
                   jax.ShapeDtypeStruct((b, t, BIAS_WIDTH), BF16),
                   jax.ShapeDtypeStruct((b, nt, C_WIDTH, tm), BF16),
                   jax.ShapeDtypeStruct((b, t, C_WIDTH), F32),
                   jax.ShapeDtypeStruct((b, t, C_WIDTH), F32),
                   jax.ShapeDtypeStruct((b, t, C_HEADS), F32),
                   jax.ShapeDtypeStruct((b, C_HEADS, t), F32),
                   jax.ShapeDtypeStruct((b, nt, SUBLANES, LANES), F32)],
        scratch_shapes=[pltpu.VMEM((1, LANES), F32)],
        compiler_params=_params(("arbitrary", "arbitrary")),
        name="proj_c",
    )(x, sc, sh, w, wft, bf_row, bf_col, place, seg)


def _lambda(lam_ref, lam_init):
    lq1, lk1, lq2, lk2 = (lam_ref[r:r + 1, :] for r in range(4))
    return (jnp.exp(jnp.sum(lq1 * lk1, axis=1, keepdims=True))
            - jnp.exp(jnp.sum(lq2 * lk2, axis=1, keepdims=True)) + lam_init)


def _attn_kernel(*refs, mode, lam_init):
    first_ref, refs = refs[0], refs[1:]
    if mode == "A":
        q_ref, k_ref, vt_ref, lam_ref, gain_ref, o_ref, qc_ref, m_ref, acc_ref, s0_ref, s1_ref, c0_ref, c1_ref = refs
        bias_ref = None
        d_val = 2 * A_HEAD_DIM
    else:
        q_ref, k_ref, bias_ref, vt_ref, o_ref, qc_ref, m_ref, acc_ref, s0_ref, s1_ref, c0_ref, c1_ref = refs
        d_val = C_HEAD_DIM
    tq = q_ref.shape[1]
    tk = k_ref.shape[2]
    qi = pl.program_id(2)
    lane = lax.broadcasted_iota(jnp.int32, (1, LANES), 1)
    low = lane < (LANES // 2)
    q = q_ref[0]
    zero = jnp.zeros_like(q)
    for a in range(2):
        qa = jnp.where(low, q, zero) if a == 0 else jnp.where(low, zero, q)
        if bias_ref is not None:
            pick = jnp.where((lane >= 3 * a) & (lane < 3 * a + 3), 1.0, 0.0).astype(BF16)
            qa = jnp.concatenate([qa, jnp.broadcast_to(pick, (tq, LANES))], axis=1)
        qc_ref[a] = qa
    m_ref[...] = jnp.full(m_ref.shape, NEG, F32)
    acc_ref[...] = jnp.zeros(acc_ref.shape, F32)
    ones = jnp.ones((BF16_ROWS, tk), BF16)

    units = [(a, slice(n * MXU_DIM, (n + 1) * MXU_DIM)) for a in range(2) for n in range(tq // MXU_DIM)]

    def score_chain(j, u):
        a, cs = units[u]
        kc = k_ref[0, j]
        if bias_ref is not None:
            kc = jnp.concatenate([kc, bias_ref[0, j]], axis=1)
        return _dot_nt(kc, qc_ref[a, cs, :])

    def value_chain(j, u, st, cmax, key_off):
        a, cs = units[u]
        masked = key_off is not None
        if masked:
            key = lax.broadcasted_iota(jnp.int32, (tk, MXU_DIM), 0) + key_off
            qry = lax.broadcasted_iota(jnp.int32, (tk, MXU_DIM), 1) + cs.start
            keep = ((key >> CHUNK_SHIFT) <= (qry >> CHUNK_SHIFT)) if mode == "A" else (key <= qry)
            st = jnp.where(keep, st, NEG)
            cmax = jnp.max(st, axis=0, keepdims=True)
        m_prev = m_ref[a, :, cs]
        m_new = jnp.maximum(m_prev, cmax)
        alpha = jnp.exp2(m_prev - m_new)
        pt = jnp.exp2(st - m_new).astype(BF16)
        vt = vt_ref[0, j]
        va = vt if mode == "A" else vt[a * d_val:(a + 1) * d_val]
        va = jnp.concatenate([va, ones], axis=0)
        acc_ref[a, :, cs] = alpha * acc_ref[a, :, cs] + _dot(va, pt)
        m_ref[a, :, cs] = m_new

    def stage(j_scores, dst, j_values, src, key_off=None):
        for t in range(len(units) + 1):
            if j_scores is not None and t < len(units):
                st = score_chain(j_scores, t)
                dst[0][t] = st
                dst[1][t] = jnp.max(st, axis=0, keepdims=True)
            if j_values is not None and t >= 1:
                value_chain(j_values, t - 1, src[0][t - 1], src[1][t - 1], key_off)

    buf0 = (s0_ref, c0_ref)
    buf1 = (s1_ref, c1_ref)
    n_diag = tq // tk
    n_full = qi * n_diag
    j0 = first_ref[pl.program_id(0), pl.program_id(1), qi]
    n_vis = n_full - j0
    def pair(j):
        stage(j + 1, buf1, j, buf0)
        stage(j + 2, buf0, j + 1, buf1)

    def diagonal():
        for d in range(0, n_diag, 2):
            j = n_full + d
            stage(j + 1, buf1, j, buf0, key_off=d * tk)
            stage(j + 2 if d + 2 < n_diag else None, buf0, j + 1, buf1, key_off=(d + 1) * tk)

    def any_window():
        stage(j0, buf0, None, None)

        def quad(p, carry):
            pair(j0 + 4 * p)
            pair(j0 + 4 * p + 2)
            return carry

        lax.fori_loop(0, n_vis >> 2, quad, 0)

        @pl.when((n_vis & 2) == 2)
        def _():
            pair(j0 + (n_vis & ~3))

        diagonal()

    if mode == "C":
        def straight(window):
            stage(j0, buf0, None, None)
            for d in range(0, window, 2):
                pair(j0 + d)
            diagonal()

        for window in SHORT_WINDOWS:
            pl.when(n_vis == window)(functools.partial(straight, window))

        @pl.when(functools.reduce(jnp.logical_and, [n_vis != window for window in SHORT_WINDOWS]))
        def _():
            any_window()
    else:
        any_window()

    outs = []
    for a in range(2):
        acc = acc_ref[a]
        outs.append(acc[:d_val] * (1.0 / acc[d_val:d_val + 1]))
    if mode == "A":
        ot = outs[0] - _lambda(lam_ref, lam_init) * outs[1]
        ot = ot * lax.rsqrt(jnp.mean(ot * ot, axis=0, keepdims=True) + LN_EPS)
        ot = ot * (gain_ref[0] * (1.0 - lam_init))
    else:
        ot = jnp.concatenate(outs, axis=0)
    o_ref[0] = ot.T.astype(o_ref.dtype)


def _attention(q, k, vt, first, mode, lam=None, gain=None, bias=None, lam_init=0.0):
    b, t, width = q.shape
    nk, tk = vt.shape[1], vt.shape[3]
    tq = 2 * tk
    assert t % tq == 0
    groups = width // LANES
    d_aug = (2 * A_HEAD_DIM if mode == "A" else C_HEAD_DIM) + BF16_ROWS
    kspec = pl.BlockSpec((1, nk, tk, LANES), lambda bi, g, i, f: (bi, 0, 0, g))
    in_specs = [pl.BlockSpec((1, tq, LANES), lambda bi, g, i, f: (bi, i, g)), kspec]
    args = [q, k.reshape(b, nk, tk, width)]
    if mode == "C":
        in_specs.append(kspec)
        args.append(bias.reshape(b, nk, tk, groups * LANES))
    in_specs.append(pl.BlockSpec((1, nk, LANES, tk), lambda bi, g, i, f: (bi, 0, g, 0)))
    args.append(vt)
    if mode == "A":
        in_specs += [pl.BlockSpec((4, A_HEAD_DIM), lambda bi, g, i, f: (0, 0)),
                     pl.BlockSpec((1, LANES, 1), lambda bi, g, i, f: (g, 0, 0))]
        args += [lam, gain]
    return pl.pallas_call(
        functools.partial(_attn_kernel, mode=mode, lam_init=lam_init),
        grid_spec=pltpu.PrefetchScalarGridSpec(
            num_scalar_prefetch=1,
            grid=(b, groups, t // tq),
            in_specs=in_specs,
            out_specs=pl.BlockSpec((1, tq, LANES), lambda bi, g, i, f: (bi, i, g)),
            scratch_shapes=[pltpu.VMEM((2, tq, LANES if mode == "A" else 2 * LANES), BF16),
                            pltpu.VMEM((2, 1, tq), F32),
                            pltpu.VMEM((2, d_aug, tq), F32),
                            pltpu.VMEM((2 * tq // MXU_DIM, tk, MXU_DIM), F32),
                            pltpu.VMEM((2 * tq // MXU_DIM, tk, MXU_DIM), F32),
                            pltpu.VMEM((2 * tq // MXU_DIM, 1, MXU_DIM), F32),
                            pltpu.VMEM((2 * tq // MXU_DIM, 1, MXU_DIM), F32)]),
        out_shape=jax.ShapeDtypeStruct((b, t, width), BF16),
        compiler_params=_params(("arbitrary", "arbitrary", "arbitrary")),
        name="attn_" + mode,
    )(first, *args)


def _first_visible_block(bounds, tq_tiles):
    qn, kn, bmax, bmin = (bounds[:, :, r, :C_HEADS] for r in range(4))
    b, tiles, _ = qn.shape
    nq = tiles // tq_tiles
    blk = lambda x, f: f(x.reshape(b, nq, tq_tiles, C_HEADS), axis=2)
    qn_q = blk(qn, jnp.max) * NORM_SLACK
    own = blk(bmin, jnp.min) - qn_q * blk(kn, jnp.max) * NORM_SLACK
    best = qn_q[:, :, None, :] * (kn * NORM_SLACK)[:, None, :, :] + bmax[:, None, :, :]
    dead = best < (own[:, :, None, :] - SKIP_GAP)
    block = jnp.arange(tiles, dtype=jnp.int32)[None, None, :, None]
    lead = jnp.min(jnp.where(dead, tiles, block), axis=2)
    lead = jnp.min(lead.reshape(b, nq, C_HEADS // 2, 2), axis=-1)
    lead = jnp.minimum(lead, (jnp.arange(nq, dtype=jnp.int32) * tq_tiles)[None, :, None])
    return jnp.transpose(lead - lead % 2, (0, 2, 1)).astype(jnp.int32)


def _diff_finish(o0, o1, lam, gain, lam_init):
    o = o0 - lam * o1
    o = o * lax.rsqrt(jnp.mean(o * o, axis=-1, keepdims=True) + LN_EPS)
    return o * gain * (1.0 - lam_init)


def _cached_attn_kernel(*refs, mode, past, lam_init):
    if mode == "A":
        q_ref, kn_ref, vn_ref, ck_ref, cv_ref, lam_ref, gain_ref, o_ref = refs
    else:
        q_ref, kn_ref, vn_ref, ck_ref, cv_ref, clf_ref, nlf_ref, o_ref = refs
    t = q_ref.shape[1]
    width = o_ref.shape[2]
    groups = width // LANES
    lane = lax.broadcasted_iota(jnp.int32, (1, LANES), 1)
    low = lane < (LANES // 2)
    row = lax.broadcasted_iota(jnp.int32, (t, t), 0)
    col = lax.broadcasted_iota(jnp.int32, (t, t), 1)
    if mode == "A":
        keep = ((past + col) >> CHUNK_SHIFT) <= ((past + row) >> CHUNK_SHIFT)
        lam = _lambda(lam_ref, lam_init)
    else:
        keep = col <= row
        r = lax.broadcasted_iota(jnp.int32, (past, past), 0)
        c = lax.broadcasted_iota(jnp.int32, (past, past), 1)
        upper = jnp.where(r <= c, 1.0, 0.0).astype(BF16)
        clf = clf_ref[0]
        hi, mid, lo = _split3(clf)
        cum_c = (_dot(hi, upper) + _dot(mid, upper)) + _dot(lo, upper)
        upper_n = jnp.where(row <= col, 1.0, 0.0).astype(BF16)
        hi, mid, lo = _split3(nlf_ref[0])
        cum_n = ((_dot(hi, upper_n) + _dot(mid, upper_n)) + _dot(lo, upper_n)
                 + jnp.sum(clf, axis=1, keepdims=True))
        bias_c = cum_c * (-LOG2E)
        bias_n = cum_n * (-LOG2E)
    keep2 = jnp.concatenate([keep, keep], axis=0)
    first_rows = lax.broadcasted_iota(jnp.int32, (2 * t, 1), 0) < t
    for g in range(groups):
        sl = slice(g * LANES, (g + 1) * LANES)
        q = q_ref[0, :, sl]
        kn = kn_ref[0, :, sl]
        vn = vn_ref[0, :, sl].astype(BF16)
        if mode == "A":
            kc = ck_ref[0, 0, :, g, :].astype(BF16)
            vc = cv_ref[0, 0, :, g, :].astype(BF16)
        else:
            kc = ck_ref[0, :, sl].astype(BF16)
            vc = cv_ref[0, :, sl].astype(BF16)
        zero = jnp.zeros_like(q)
        q2 = jnp.concatenate([jnp.where(low, q, zero), jnp.where(low, zero, q)], axis=0)
        s_c = _dot_nt(q2, kc)
        s_n = _dot_nt(q2, kn)
        if mode == "C":
            h = 2 * g
            s_c = s_c + jnp.where(first_rows, bias_c[h:h + 1, :], bias_c[h + 1:h + 2, :])
            s_n = s_n + jnp.where(first_rows, bias_n[h:h + 1, :], bias_n[h + 1:h + 2, :])
        s_n = jnp.where(keep2, s_n, NEG)
        m = jnp.maximum(jnp.max(s_c, axis=1, keepdims=True), jnp.max(s_n, axis=1, keepdims=True))
        p_c = jnp.exp2(s_c - m)
        p_n = jnp.exp2(s_n - m)
        l = jnp.sum(p_c, axis=1, keepdims=True) + jnp.sum(p_n, axis=1, keepdims=True)
        acc = (_dot(p_c.astype(BF16), vc) + _dot(p_n.astype(BF16), vn)) * (1.0 / l)
        outs = [acc[:t], acc[t:]]
        if mode == "A":
            o = _diff_finish(outs[0], outs[1], lam, gain_ref[g], lam_init)
        else:
            o = jnp.where(low, outs[0], outs[1])
        o_ref[0, :, sl] = o.astype(o_ref.dtype)


def _cached_attention(q, kn, vn, cache_k, cache_v, mode, lam=None, gain=None, cache_lft=None, new_lft=None,
                      lam_init=0.0):
    b, t, width = q.shape
    new = pl.BlockSpec((1, t, width), lambda bi: (bi, 0, 0))
    if mode == "A":
        _, _, past, heads, dh = cache_k.shape
        old = pl.BlockSpec((1, 1, past, heads, dh), lambda bi: (0, bi, 0, 0, 0))
    else:
        past = cache_k.shape[1]
        old = pl.BlockSpec((1, past, width), lambda bi: (bi, 0, 0))
    in_specs = [new, new, new, old, old]
    args = [q, kn, vn, cache_k, cache_v]
    if mode == "A":
        in_specs += [pl.BlockSpec((4, A_HEAD_DIM), lambda bi: (0, 0)),
                     pl.BlockSpec((A_HEADS, 1, LANES), lambda bi: (0, 0, 0))]
        args += [lam, gain]
    else:
        in_specs += [pl.BlockSpec((1, C_HEADS, past), lambda bi: (bi, 0, 0)),
                     pl.BlockSpec((1, C_HEADS, t), lambda bi: (bi, 0, 0))]
        args += [cache_lft, new_lft]
    return pl.pallas_call(
        functools.partial(_cached_attn_kernel, mode=mode, past=past, lam_init=lam_init),
        grid=(b,),
        in_specs=in_specs,
        out_specs=new,
        out_shape=jax.ShapeDtypeStruct((b, t, width), BF16),
        compiler_params=_params(("arbitrary",)),
        name="cached_attn_" + mode,
    )(*args)


def _gelu_tanh(x):
    return 0.5 * x * (1.0 + jnp.tanh(math.sqrt(2.0 / math.pi) * (x + 0.044715 * (x * x * x))))


def _causal_conv(x, tail, w_ref, b_ref):
    width = w_ref.shape[0]
    tm = x.shape[0]
    cat = jnp.concatenate([tail, x], axis=0)
    y = None
    for j in range(width):
        back = width - 1 - j
        src = cat if back == 0 else pltpu.roll(cat, back, 0)
        term = src[SUBLANES:SUBLANES + tm] * w_ref[j:j + 1, :]
        y = term if y is None else y + term
    return y + b_ref[...]


def _rglru_kernel(xg_ref, h0_ref, cb_ref, cw_ref, cbias_ref, wa_ref, ba_ref, wx_ref, bx_ref, sp_ref,
                  y_ref, hl_ref, ct_ref, hc_ref, tail_ref):
    i = pl.program_id(1)
    tm = xg_ref.shape[1]

    @pl.when(i == 0)
    def _():
        hc_ref[...] = h0_ref[0]
        tail_ref[...] = cb_ref[0]

    x = xg_ref[0, :, :B_WIDTH]
    gate_in = xg_ref[0, :, B_WIDTH:]
    xc = _causal_conv(x, tail_ref[...], cw_ref, cbias_ref)
    xcb = xc.astype(BF16)
    r = _sigmoid(_dot(xcb, wa_ref[...]) + ba_ref[...])
    ig = _sigmoid(_dot(xcb, wx_ref[...]) + bx_ref[...])
    log_a = (-RG_C) * r * sp_ref[...]
    a = jnp.exp(log_a)
    th = jnp.tanh(log_a)
    bx = jnp.sqrt((-2.0 * th) / (1.0 - th)) * (ig * xc)
    pos = lax.broadcasted_iota(jnp.int32, (tm, 1), 0) & (SUBLANES - 1)
    s = 1
    while s < SUBLANES:
        valid = pos >= s
        a_sh = pltpu.roll(a, s, 0)
        b_sh = pltpu.roll(bx, s, 0)
        bx = jnp.where(valid, a * b_sh + bx, bx)
        a = jnp.where(valid, a * a_sh, a)
        s *= 2
    carry = hc_ref[...]
    groups = []
    for g in range(tm // SUBLANES):
        rows = slice(g * SUBLANES, (g + 1) * SUBLANES)
        hg = a[rows] * carry + bx[rows]
        groups.append(hg)
        carry = hg[SUBLANES - 1:, :]
    h = jnp.concatenate(groups, axis=0)
    y_ref[0] = (h * _gelu_tanh(gate_in)).astype(y_ref.dtype)
    h_tail = h[tm - SUBLANES:, :]
    x_tail = x[tm - SUBLANES:, :]
    hc_ref[...] = h_tail[SUBLANES - 1:, :]
    tail_ref[...] = x_tail
    hl_ref[0] = h_tail
    ct_ref[0] = x_tail


def _rglru(xg, h0, conv_tail, cw, cbias, wa, ba, wx, bx, sp, tm_pref=256):
    b, t, _ = xg.shape
    tm = _row_tile(t, tm_pref)
    wspec = lambda shape: pl.BlockSpec(shape, lambda bi, i: tuple(0 for _ in shape))
    return pl.pallas_call(
        _rglru_kernel,
        grid=(b, t // tm),
        in_specs=[pl.BlockSpec((1, tm, 2 * B_WIDTH), lambda bi, i: (bi, i, 0)),
                  pl.BlockSpec((1, 1, B_WIDTH), lambda bi, i: (bi, 0, 0)),
                  pl.BlockSpec((1, SUBLANES, B_WIDTH), lambda bi, i: (bi, 0, 0)),
                  wspec((B_CONV, B_WIDTH)), wspec((1, B_WIDTH)),
                  wspec((B_WIDTH, B_WIDTH)), wspec((1, B_WIDTH)),
                  wspec((B_WIDTH, B_WIDTH)), wspec((1, B_WIDTH)), wspec((1, B_WIDTH))],
        out_specs=[pl.BlockSpec((1, tm, B_WIDTH), lambda bi, i: (bi, i, 0)),
                   pl.BlockSpec((1, SUBLANES, B_WIDTH), lambda bi, i: (bi, 0, 0)),
                   pl.BlockSpec((1, SUBLANES, B_WIDTH), lambda bi, i: (bi, 0, 0))],
        out_shape=[jax.ShapeDtypeStruct((b, t, B_WIDTH), BF16),
                   jax.ShapeDtypeStruct((b, SUBLANES, B_WIDTH), F32),
                   jax.ShapeDtypeStruct((b, SUBLANES, B_WIDTH), F32)],
        scratch_shapes=[pltpu.VMEM((1, B_WIDTH), F32), pltpu.VMEM((SUBLANES, B_WIDTH), F32)],
        compiler_params=_params(("arbitrary", "arbitrary")),
        name="rglru",
    )(xg, h0, conv_tail, cw, cbias, wa, ba, wx, bx, sp)


def _proj_ln_kernel(*refs, n_in):
    h_refs = refs[:n_in]
    w_ref, x_ref, gate_ref, g_ref, b_ref, o_ref = refs[n_in:]
    tm = x_ref.shape[1]
    rows = [slice(r, min(r + LN_ROWS, tm)) for r in range(0, tm, LN_ROWS)]

    def matmul(rs):
        hs = [r[0, rs, :] for r in h_refs]
        return _dot(hs[0] if n_in == 1 else jnp.concatenate(hs, axis=-1), w_ref[0])

    def norm(rs, proj):
        y = ALPHA * x_ref[0, rs, :] + gate_ref[0] * proj
        mu = jnp.mean(y, axis=-1, keepdims=True)
        yc = y - mu
        var = jnp.mean(yc * yc, axis=-1, keepdims=True)
        o_ref[0, rs, :] = yc * lax.rsqrt(var + LN_EPS) * g_ref[...] + b_ref[...]

    pending = matmul(rows[0])
    for r in range(len(rows)):
        nxt = matmul(rows[r + 1]) if r + 1 < len(rows) else None
        norm(rows[r], pending)
        pending = nxt


def _proj_ln(hs, w, layer, x, gate, ln_g, ln_b, tm_pref=1024):
    b, t, d = x.shape
    tm = _row_tile(t, tm_pref)
    k = w.shape[1]
    gate_spec = (pl.BlockSpec((1, 1, d), lambda bi, i: (bi, 0, 0)) if gate.shape[1] == 1
                 else pl.BlockSpec((1, tm, d), lambda bi, i: (bi, i, 0)))
    in_specs = [pl.BlockSpec((1, tm, h.shape[2]), lambda bi, i: (bi, i, 0)) for h in hs]
    in_specs += [pl.BlockSpec((1, k, d), lambda bi, i: (layer, 0, 0)),
                 pl.BlockSpec((1, tm, d), lambda bi, i: (bi, i, 0)),
                 gate_spec,
                 pl.BlockSpec((1, d), lambda bi, i: (0, 0)),
                 pl.BlockSpec((1, d), lambda bi, i: (0, 0))]
    return pl.pallas_call(
        functools.partial(_proj_ln_kernel, n_in=len(hs)),
        grid=(b, t // tm),
        in_specs=in_specs,
        out_specs=pl.BlockSpec((1, tm, d), lambda bi, i: (bi, i, 0)),
        out_shape=jax.ShapeDtypeStruct((b, t, d), F32),
        compiler_params=_params(("arbitrary", "arbitrary")),
        name="proj_ln",
    )(*hs, w, x, gate, ln_g, ln_b)


def _ffn_up_kernel(x_ref, sc_ref, sh_ref, w_ref, cb_ref, cw_ref, cbias_ref, h_ref, ct_ref, tail_ref):
    i = pl.program_id(1)
    seqs, tm = x_ref.shape[0], x_ref.shape[1]

    @pl.when(i == 0)
    def _():
        tail_ref[...] = cb_ref[...]

    us = [_modulate(x_ref[s], sc_ref[s], sh_ref[s]) for s in range(seqs)]
    u = us[0] if seqs == 1 else jnp.concatenate(us, axis=0)
    chunks = [slice(c, min(c + FFN_CHUNK, D_FF)) for c in range(0, D_FF, FFN_CHUNK)]

    def matmuls(cs):
        gs = slice(D_FF + cs.start, D_FF + cs.stop)
        return _dot(u, w_ref[0, :, cs]), _dot(u, w_ref[0, :, gs])

    def gate(cs, a_all, g_all):
        for s in range(seqs):
            rows = slice(s * tm, (s + 1) * tm)
            a, g = a_all[rows], g_all[rows]
            gc = _causal_conv(g, tail_ref[s, :, cs], cw_ref.at[:, cs], cbias_ref.at[:, cs])
            silu = gc * (0.5 * jnp.tanh(0.5 * gc) + 0.5)
            h_ref[s, :, cs] = (a * silu).astype(h_ref.dtype)
            g_tail = g[tm - SUBLANES:, :]
            tail_ref[s, :, cs] = g_tail
            ct_ref[s, :, cs] = g_tail

    pending = matmuls(chunks[0])
    for c in range(len(chunks)):
        nxt = matmuls(chunks[c + 1]) if c + 1 < len(chunks) else None
        gate(chunks[c], *pending)
        pending = nxt


def _ffn_up(x, sc, sh, w, layer, conv_tail, cw, cbias, tm_pref=256):
    b, t, d = x.shape
    tm = _row_tile(t, tm_pref)
    seqs = max(1, min(b, LANES // tm))
    while b % seqs:
        seqs -= 1
    return pl.pallas_call(
        _ffn_up_kernel,
        grid=(b // seqs, t // tm),
        in_specs=[pl.BlockSpec((seqs, tm, d), lambda bi, i: (bi, i, 0)),
                  pl.BlockSpec((seqs, 1, d), lambda bi, i: (bi, 0, 0)),
                  pl.BlockSpec((seqs, 1, d), lambda bi, i: (bi, 0, 0)),
                  pl.BlockSpec((1, d, 2 * D_FF), lambda bi, i: (layer, 0, 0)),
                  pl.BlockSpec((seqs, SUBLANES, D_FF), lambda bi, i: (bi, 0, 0)),
                  pl.BlockSpec((FFN_CONV, D_FF), lambda bi, i: (0, 0)),
                  pl.BlockSpec((1, D_FF), lambda bi, i: (0, 0))],
        out_specs=[pl.BlockSpec((seqs, tm, D_FF), lambda bi, i: (bi, i, 0)),
                   pl.BlockSpec((seqs, SUBLANES, D_FF), lambda bi, i: (bi, 0, 0))],
        out_shape=[jax.ShapeDtypeStruct((b, t, D_FF), BF16),
                   jax.ShapeDtypeStruct((b, SUBLANES, D_FF), F32)],
        scratch_shapes=[pltpu.VMEM((seqs, SUBLANES, D_FF), F32)],
        compiler_params=_params(("arbitrary", "arbitrary")),
        name="ffn_up",
    )(x, sc, sh, w, conv_tail, cw, cbias)


def _pad_tail(buf):
    return jnp.pad(buf, ((0, 0), (SUBLANES - buf.shape[1], 0), (0, 0)))


def _rope_tables(past, t):
    half = A_HEAD_DIM // 2
    inv = ROPE_THETA ** (-jnp.arange(0, A_HEAD_DIM, 2, dtype=F32) / A_HEAD_DIM)
    pos = (past + jnp.arange(t, dtype=jnp.int32)).astype(F32)
    ang = pos[:, None] * inv[None, :]
    cos = jnp.tile(jnp.cos(ang), (1, LANES // half))
    sin = jnp.sin(ang)
    sin_signed = jnp.tile(jnp.concatenate([-sin, sin], axis=1), (1, LANES // A_HEAD_DIM))
    return cos, sin_signed


def _block_diag(w):
    n, i, o = w.shape
    return jnp.einsum("nio,nm->nimo", w, jnp.eye(n, dtype=w.dtype)).reshape(n * i, n * o)


def _prepare(p):
    w = {}
    w_in_ab = p["w_in_ab"][0]
    w["in_ab"] = w_in_ab.astype(BF16)
    w["lam"] = jnp.stack([p["lam_q1"][0], p["lam_k1"][0], p["lam_q2"][0], p["lam_k2"][0]])
    w["gain_row"] = p["attn_gain"][0].reshape(A_HEADS, 1, LANES)
    w["gain_col"] = p["attn_gain"][0].reshape(A_HEADS, LANES, 1)
    w["b_conv_w"] = p["b_conv_w"][0]
    w["b_conv_b"] = p["b_conv_b"][0].reshape(1, B_WIDTH)
    w["rg_a"] = _block_diag(p["w_rg_a"][0]).astype(BF16)
    w["rg_x"] = _block_diag(p["w_rg_x"][0]).astype(BF16)
    w["b_rg_a"] = p["b_rg_a"][0].reshape(1, B_WIDTH)
    w["b_rg_x"] = p["b_rg_x"][0].reshape(1, B_WIDTH)
    w["rg_L"] = p["rg_L"][0].reshape(1, B_WIDTH)
    w_in_c = p["w_in_c"][0]
    w["in_c"] = jnp.pad(w_in_c, ((0, 0), (0, LANES - C_HEADS))).astype(BF16)
    w["in_c_ft"] = w_in_c[:, 3 * C_WIDTH:].T.astype(BF16)
    w["bf_row"] = jnp.pad(p["b_f"][0], (0, LANES - C_HEADS)).reshape(1, LANES)
    w["bf_col"] = p["b_f"][0].reshape(C_HEADS, 1)
    w["place"] = _bias_placement()
    w["seg"] = jnp.asarray(np.repeat(np.eye(LANES, dtype=np.float32)[:C_HEADS], C_HEAD_DIM, axis=0), BF16)
    w["up"] = p["w_up"].astype(BF16)
    w["down"] = p["w_down"].astype(BF16)
    w["out_ab"] = p["w_out_ab"].astype(BF16)
    w["out_c"] = p["w_out_c"].astype(BF16)
    w["ffn_conv_w"] = [p["ffn_conv_w"][i] for i in range(DEPTH)]
    w["ffn_conv_b"] = [p["ffn_conv_b"][i].reshape(1, D_FF) for i in range(DEPTH)]
    w["ln1_g"] = [p["ln1_g"][i].reshape(1, D_MODEL) for i in range(DEPTH)]
    w["ln1_b"] = [p["ln1_b"][i].reshape(1, D_MODEL) for i in range(DEPTH)]
    w["ln2_g"] = [p["ln2_g"][i].reshape(1, D_MODEL) for i in range(DEPTH)]
    w["ln2_b"] = [p["ln2_b"][i].reshape(1, D_MODEL) for i in range(DEPTH)]
    return w


def _softplus_kernel(x_ref, o_ref):
    o_ref[...] = _softplus(-x_ref[...])


def _proj_ln_rows(hs, w, layer, x, gate, ln_g, ln_b):
    b, t, d = x.shape
    flat = lambda a: a.reshape(1, b * t, a.shape[2])
    gate_rows = jnp.broadcast_to(gate, (b, t, d))
    return _proj_ln([flat(h) for h in hs], w, layer, flat(x), flat(gate_rows), ln_g, ln_b).reshape(b, t, d)


def _trunk(x, mods, w, sp, cache_a_k=None, cache_a_v=None, state_b_h=None, state_b_conv=None,
           cache_c_k=None, cache_c_v=None, cache_c_logf=None, state_ffn_conv=None):
    b, t, d = x.shape
    cached = cache_a_k is not None
    proj_ln = _proj_ln_rows if cached else _proj_ln
    past = cache_a_k.shape[2] if cached else 0
    outs = {}
    for i in range(DEPTH):
        sh1, sc1, g1, sh2, sc2, g2 = [m[:, None, :] for m in jnp.split(mods[i], 6, axis=-1)]
        if i % 2 == 0:
            lam_init = 0.8 - 0.6 * math.exp(-0.3 * i)
            cos, sin = _rope_tables(past, t)
            q, kb, vt, k32, v32, xg = _proj_ab(x, sc1, sh1, w["in_ab"], cos, sin)
            if cached:
                o = _cached_attention(q, kb, v32.reshape(b, t, A_WIDTH), cache_a_k, cache_a_v,
                                      "A", lam=w["lam"], gain=w["gain_row"], lam_init=lam_init)
                h0 = state_b_h[0][:, None, :]
                ctail = _pad_tail(state_b_conv[0])
            else:
                first = jnp.zeros((b, A_HEADS, t // (2 * ATTN_BLOCK)), jnp.int32)
                o = _attention(q, kb, vt, first, "A", lam=w["lam"], gain=w["gain_col"], lam_init=lam_init)
                h0 = jnp.zeros((b, 1, B_WIDTH), F32)
                ctail = jnp.zeros((b, SUBLANES, B_WIDTH), F32)
            yb, h_tail, x_tail = _rglru(xg, h0, ctail, w["b_conv_w"], w["b_conv_b"], w["rg_a"], w["b_rg_a"],
                                        w["rg_x"], w["b_rg_x"], sp)
            outs["a_k"] = k32[None]
            outs["a_v"] = v32[None]
            outs["b_h"] = h_tail[:, SUBLANES - 1, :][None]
            outs["b_conv"] = x_tail[:, SUBLANES - (B_CONV - 1):, :][None]
            x = proj_ln([o, yb], w["out_ab"], i // 2, x, g1, w["ln1_g"][i], w["ln1_b"][i])
        else:
            q, kb, bias, vt, k32, v32, lf, lft, bounds = _proj_c(x, sc1, sh1, w["in_c"], w["in_c_ft"],
                                                                 w["bf_row"], w["bf_col"], w["place"], w["seg"])
            if cached:
                o = _cached_attention(q, kb, v32, cache_c_k[0].reshape(b, past, C_WIDTH),
                                      cache_c_v[0].reshape(b, past, C_WIDTH),
                                      "C", cache_lft=jnp.swapaxes(cache_c_logf[0], 1, 2), new_lft=lft)
            else:
                o = _attention(q, kb, vt, _first_visible_block(bounds, 2), "C", bias=bias)
            outs["c_k"] = k32.reshape(1, b, t, C_HEADS, C_HEAD_DIM)
            outs["c_v"] = v32.reshape(1, b, t, C_HEADS, C_HEAD_DIM)
            outs["c_logf"] = lf[None]
            x = proj_ln([o], w["out_c"], i // 2, x, g1, w["ln1_g"][i], w["ln1_b"][i])
        ftail = _pad_tail(state_ffn_conv[i]) if cached else jnp.zeros((b, SUBLANES, D_FF), F32)
        hmid, g_tail = _ffn_up(x, sc2, sh2, w["up"], i, ftail, w["ffn_conv_w"][i], w["ffn_conv_b"][i])
        outs.setdefault("ffn", []).append(g_tail[:, SUBLANES - (FFN_CONV - 1):, :])
        x = proj_ln([hmid], w["down"], i, x, g2, w["ln2_g"][i], w["ln2_b"][i])
    return (x, outs["a_k"], outs["a_v"], outs["b_h"], outs["b_conv"],
            outs["c_k"], outs["c_v"], outs["c_logf"], jnp.stack(outs["ffn"]))


def kernel(x_prompt, x_sample, c_prompt, c_sample, cache_a_k, cache_a_v, state_b_h, state_b_conv, cache_c_k, cache_c_v, cache_c_logf, state_ffn_conv, w_ada, b_ada, ln1_g, ln1_b, ln2_g, ln2_b, w_in_ab, lam_q1, lam_k1, lam_q2, lam_k2, attn_gain, b_conv_w, b_conv_b, w_rg_a, b_rg_a, w_rg_x, b_rg_x, rg_L, w_out_ab, w_in_c, b_f, w_out_c, w_up, ffn_conv_w, ffn_conv_b, w_down):
    p = dict(w_in_ab=w_in_ab, lam_q1=lam_q1, lam_k1=lam_k1, lam_q2=lam_q2, lam_k2=lam_k2, attn_gain=attn_gain,
             b_conv_w=b_conv_w, b_conv_b=b_conv_b, w_rg_a=w_rg_a, b_rg_a=b_rg_a, w_rg_x=w_rg_x, b_rg_x=b_rg_x,
             rg_L=rg_L, w_out_ab=w_out_ab, w_in_c=w_in_c, b_f=b_f, w_out_c=w_out_c, w_up=w_up,
             ffn_conv_w=ffn_conv_w, ffn_conv_b=ffn_conv_b, w_down=w_down,
             ln1_g=ln1_g, ln1_b=ln1_b, ln2_g=ln2_g, ln2_b=ln2_b)
    w = _prepare(p)
    bp = c_prompt.shape[0]
    bs = c_sample.shape[0]
    rows = -(-(bp + bs) // 16) * 16
    c_all = jnp.pad(jnp.concatenate([c_prompt, c_sample], axis=0), ((0, rows - bp - bs), (0, 0)))
    mods = _mods(c_all, w_ada, b_ada)
    sp = pl.pallas_call(_softplus_kernel, out_shape=jax.ShapeDtypeStruct((1, B_WIDTH), F32),
                        name="softplus")(w["rg_L"])
    res_p = _trunk(x_prompt, mods[:, :bp], w, sp)
    res_s = _trunk(x_sample, mods[:, bp:bp + bs], w, sp, cache_a_k, cache_a_v, state_b_h, state_b_conv,
                   cache_c_k, cache_c_v, cache_c_logf, state_ffn_conv)
    return (res_p[0], res_s[0]) + res_p[1:] + res_s[1:]
```

```python
import functools
import math

import numpy as np
import jax
import jax.numpy as jnp
from jax import lax
from jax.experimental import pallas as pl
from jax.experimental.pallas import tpu as pltpu

F32 = jnp.float32
BF16 = jnp.bfloat16

D_MODEL = 1024
DEPTH = 2
CHUNK = 64
CHUNK_SHIFT = 6
A_HEADS = 4
A_HEAD_DIM = 64
A_WIDTH = A_HEADS * 2 * A_HEAD_DIM
B_WIDTH = 512
B_BLOCKS = 8
B_CONV = 4
RG_C = 8.0
C_HEADS = 16
C_HEAD_DIM = 64
C_WIDTH = C_HEADS * C_HEAD_DIM
D_FF = 2816
FFN_CONV = 3
ROPE_THETA = 10000.0
ALPHA = (2 * DEPTH) ** 0.25
LN_EPS = 1e-5
NEG = -1e30
LOG2E = 1.4426950408889634

LANES = 128
SUBLANES = 8
BF16_ROWS = 16
MXU_DIM = 256
VMEM_LIMIT = 56 * 1024 * 1024
BIAS_WIDTH = (C_HEADS // 2) * LANES
FFN_CHUNK = 256
LN_ROWS = 256
SKIP_GAP = 160.0
NORM_SLACK = 1.03
SHORT_WINDOWS = (4, 6)
ATTN_BLOCK = 512


def _params(sem, flags=None):
    return pltpu.CompilerParams(dimension_semantics=sem, vmem_limit_bytes=VMEM_LIMIT, flags=flags)


def _row_tile(t, pref):
    if t <= pref:
        return t
    tm = pref
    while t % tm:
        tm //= 2
    return tm


def _modulate(x, sc, sh):
    return (x * (1.0 + sc) + sh).astype(BF16)


def _sigmoid(x):
    return 1.0 / (1.0 + jnp.exp(-x))


def _softplus(x):
    return jnp.maximum(x, 0.0) + jnp.log1p(jnp.exp(-jnp.abs(x)))


def _log_sigmoid(x):
    return jnp.minimum(x, 0.0) - jnp.log1p(jnp.exp(-jnp.abs(x)))


def _split3(x):
    hi = x.astype(BF16)
    r1 = x - hi.astype(F32)
    mid = r1.astype(BF16)
    lo = (r1 - mid.astype(F32)).astype(BF16)
    return hi, mid, lo


def _dot(a, b):
    return jnp.dot(a, b, preferred_element_type=F32)


def _dot_nt(a, b):
    return lax.dot_general(a, b, (((1,), (1,)), ((), ())), preferred_element_type=F32)


def _mods_kernel(c_ref, w_ref, b_ref, o_ref):
    c = c_ref[...]
    s = (c * _sigmoid(c)).astype(BF16)
    o_ref[0] = _dot(s, w_ref[0].astype(BF16)) + b_ref[0]


def _mods(c_all, w_ada, b_ada):
    rows, d = c_all.shape
    n = w_ada.shape[-1]
    tn = 1536
    return pl.pallas_call(
        _mods_kernel,
        grid=(DEPTH, n // tn),
        in_specs=[pl.BlockSpec((rows, d), lambda l, j: (0, 0)),
                  pl.BlockSpec((1, d, tn), lambda l, j: (l, 0, j)),
                  pl.BlockSpec((1, 1, tn), lambda l, j: (l, 0, j))],
        out_specs=pl.BlockSpec((1, rows, tn), lambda l, j: (l, 0, j)),
        out_shape=jax.ShapeDtypeStruct((DEPTH, rows, n), F32),
        compiler_params=_params(("arbitrary", "arbitrary")),
        name="mods",
    )(c_all, w_ada, b_ada.reshape(DEPTH, 1, n))


def _rope_slab(x, cos, sin_signed, first_half):
    fwd = pltpu.roll(x, LANES - A_HEAD_DIM // 2, 1)
    bwd = pltpu.roll(x, A_HEAD_DIM // 2, 1)
    partner = jnp.where(first_half, fwd, bwd)
    return x * cos + partner * sin_signed


def _proj_ab_kernel(x_ref, sc_ref, sh_ref, w_ref, cos_ref, sin_ref,
                    q_ref, kb_ref, vt_ref, k_ref, v_ref, xg_ref, *, q_scale):
    u = _modulate(x_ref[0], sc_ref[0], sh_ref[0])
    pr = _dot(u, w_ref[...])
    cos = cos_ref[...]
    sin = sin_ref[...]
    lane = lax.broadcasted_iota(jnp.int32, (1, LANES), 1)
    first_half = (lane & (A_HEAD_DIM - 1)) < (A_HEAD_DIM // 2)
    for h in range(A_HEADS):
        sl = slice(h * LANES, (h + 1) * LANES)
        q = _rope_slab(pr[:, sl], cos, sin, first_half)
        q_ref[0, :, sl] = (q * q_scale).astype(BF16)
        k = _rope_slab(pr[:, A_WIDTH + h * LANES:A_WIDTH + (h + 1) * LANES], cos, sin, first_half)
        k_ref[0, :, h, :] = k
        kb_ref[0, :, sl] = k.astype(BF16)
    for h in range(A_HEADS):
        v_ref[0, :, h, :] = pr[:, 2 * A_WIDTH + h * LANES:2 * A_WIDTH + (h + 1) * LANES]
    vt_ref[0, 0] = pr[:, 2 * A_WIDTH:3 * A_WIDTH].T.astype(BF16)
    xg_ref[0] = pr[:, 3 * A_WIDTH:]


def _proj_ab(x, sc, sh, w, cos, sin):
    b, t, d = x.shape
    tm = _row_tile(t, ATTN_BLOCK)
    nt = t // tm
    n = w.shape[1]
    q_scale = A_HEAD_DIM ** -0.5 * LOG2E
    row = lambda width: pl.BlockSpec((1, tm, width), lambda bi, i: (bi, i, 0))
    const = lambda shape: pl.BlockSpec(shape, lambda bi, i: tuple(0 for _ in shape))
    heads = pl.BlockSpec((1, tm, A_HEADS, LANES), lambda bi, i: (bi, i, 0, 0))
    return pl.pallas_call(
        functools.partial(_proj_ab_kernel, q_scale=q_scale),
        grid=(b, nt),
        in_specs=[row(d),
                  pl.BlockSpec((1, 1, d), lambda bi, i: (bi, 0, 0)),
                  pl.BlockSpec((1, 1, d), lambda bi, i: (bi, 0, 0)),
                  const((d, n)),
                  pl.BlockSpec((tm, LANES), lambda bi, i: (i, 0)),
                  pl.BlockSpec((tm, LANES), lambda bi, i: (i, 0))],
        out_specs=[row(A_WIDTH), row(A_WIDTH),
                   pl.BlockSpec((1, 1, A_WIDTH, tm), lambda bi, i: (bi, i, 0, 0)),
                   heads, heads, row(2 * B_WIDTH)],
        out_shape=[jax.ShapeDtypeStruct((b, t, A_WIDTH), BF16),
                   jax.ShapeDtypeStruct((b, t, A_WIDTH), BF16),
                   jax.ShapeDtypeStruct((b, nt, A_WIDTH, tm), BF16),
                   jax.ShapeDtypeStruct((b, t, A_HEADS, LANES), F32),
                   jax.ShapeDtypeStruct((b, t, A_HEADS, LANES), F32),
                   jax.ShapeDtypeStruct((b, t, 2 * B_WIDTH), F32)],
        compiler_params=_params(("arbitrary", "arbitrary")),
        name="proj_ab",
    )(x, sc, sh, w, cos, sin)


def _bias_placement():
    e = np.zeros((3 * LANES, BIAS_WIDTH), np.float32)
    for piece in range(3):
        for h in range(C_HEADS):
            e[piece * LANES + h, (h // 2) * LANES + 3 * (h % 2) + piece] = 1.0
    return jnp.asarray(e, BF16)


def _head_norm_max(x, seg):
    n2 = _dot((x * x).astype(BF16), seg)
    return jnp.sqrt(jnp.max(n2, axis=0, keepdims=True))


def _proj_c_kernel(x_ref, sc_ref, sh_ref, w_ref, wft_ref, bfr_ref, bfc_ref, place_ref, seg_ref,
                   q_ref, kb_ref, bias_ref, vt_ref, k_ref, v_ref, lf_ref, lft_ref, bound_ref, run_ref, *, q_scale):
    i = pl.program_id(1)
    tm = x_ref.shape[1]

    @pl.when(i == 0)
    def _():
        run_ref[...] = jnp.zeros_like(run_ref)

    u = _modulate(x_ref[0], sc_ref[0], sh_ref[0])
    pr = _dot(u, w_ref[...])
    qs = pr[:, :C_WIDTH] * q_scale
    q_ref[0] = qs.astype(BF16)
    k = pr[:, C_WIDTH:2 * C_WIDTH]
    k_ref[0] = k
    kb_ref[0] = k.astype(BF16)
    v = pr[:, 2 * C_WIDTH:3 * C_WIDTH]
    v_ref[0] = v
    vt_ref[0, 0] = v.T.astype(BF16)
    lf = _log_sigmoid(pr[:, 3 * C_WIDTH:] + bfr_ref[...])
    lf_ref[0] = lf[:, :C_HEADS]
    lft_ref[0] = _log_sigmoid(_dot_nt(wft_ref[...], u) + bfc_ref[...])
    r = lax.broadcasted_iota(jnp.int32, (tm, tm), 0)
    c = lax.broadcasted_iota(jnp.int32, (tm, tm), 1)
    lower = jnp.where(c <= r, 1.0, 0.0).astype(BF16)
    hi, mid, lo = _split3(lf)
    cum = (_dot(lower, hi) + _dot(lower, mid)) + _dot(lower, lo) + run_ref[...]
    bias = cum * (-LOG2E)
    pieces = jnp.concatenate(_split3(bias), axis=1)
    bias_ref[0] = _dot(pieces, place_ref[...]).astype(BF16)
    seg = seg_ref[...]
    bound_ref[0, 0] = jnp.concatenate(
        [_head_norm_max(qs, seg), _head_norm_max(k, seg),
         jnp.max(bias, axis=0, keepdims=True), jnp.min(bias, axis=0, keepdims=True),
         jnp.zeros((SUBLANES - 4, LANES), F32)], axis=0)
    run_ref[...] = run_ref[...] + jnp.sum(lf, axis=0, keepdims=True)


def _proj_c(x, sc, sh, w, wft, bf_row, bf_col, place, seg):
    b, t, d = x.shape
    tm = _row_tile(t, ATTN_BLOCK)
    nt = t // tm
    n = w.shape[1]
    q_scale = C_HEAD_DIM ** -0.5 * LOG2E
    row = lambda width: pl.BlockSpec((1, tm, width), lambda bi, i: (bi, i, 0))
    const = lambda shape: pl.BlockSpec(shape, lambda bi, i: tuple(0 for _ in shape))
    return pl.pallas_call(
        functools.partial(_proj_c_kernel, q_scale=q_scale),
        grid=(b, nt),
        in_specs=[row(d),
                  pl.BlockSpec((1, 1, d), lambda bi, i: (bi, 0, 0)),
                  pl.BlockSpec((1, 1, d), lambda bi, i: (bi, 0, 0)),
                  const((d, n)), const((C_HEADS, d)),
                  const((1, LANES)), const((C_HEADS, 1)), const((3 * LANES, BIAS_WIDTH)), const((C_WIDTH, LANES))],
        out_specs=[row(C_WIDTH), row(C_WIDTH), row(BIAS_WIDTH),
                   pl.BlockSpec((1, 1, C_WIDTH, tm), lambda bi, i: (bi, i, 0, 0)),
                   row(C_WIDTH), row(C_WIDTH), row(C_HEADS),
                   pl.BlockSpec((1, C_HEADS, tm), lambda bi, i: (bi, 0, i)),
                   pl.BlockSpec((1, 1, SUBLANES, LANES), lambda bi, i: (bi, i, 0, 0))],
        out_shape=[jax.ShapeDtypeStruct((b, t, C_WIDTH), BF16),
                   jax.ShapeDtypeStruct((b, t, C_WIDTH), BF16),
                   jax.ShapeDtypeStruct((b, t, BIAS_WIDTH), BF16),
                   jax.ShapeDtypeStruct((b, nt, C_WIDTH, tm), BF16),
                   jax.ShapeDtypeStruct((b, t, C_WIDTH), F32),
                   jax.ShapeDtypeStruct((b, t, C_WIDTH), F32),
                   jax.ShapeDtypeStruct((b, t, C_HEADS), F32),
                   jax.ShapeDtypeStruct((b, C_HEADS, t), F32),
                   jax.ShapeDtypeStruct((b, nt, SUBLANES, LANES), F32)],
        scratch_shapes=[pltpu.VMEM((1, LANES), F32)],
        compiler_params=_params(("arbitrary", "arbitrary")),
        name="proj_c",
    )(x, sc, sh, w, wft, bf_row, bf_col, place, seg)


def _lambda(lam_ref, lam_init):
    lq1, lk1, lq2, lk2 = (lam_ref[r:r + 1, :] for r in range(4))
    return (jnp.exp(jnp.sum(lq1 * lk1, axis=1, keepdims=True))
            - jnp.exp(jnp.sum(lq2 * lk2, axis=1, keepdims=True)) + lam_init)


def _attn_kernel(*refs, mode, lam_init):
    first_ref, refs = refs[0], refs[1:]
    if mode == "A":
        q_ref, k_ref, vt_ref, lam_ref, gain_ref, o_ref, qc_ref, m_ref, acc_ref, s0_ref, s1_ref, c0_ref, c1_ref = refs
        bias_ref = None
        d_val = 2 * A_HEAD_DIM
    else:
        q_ref, k_ref, bias_ref, vt_ref, o_ref, qc_ref, m_ref, acc_ref, s0_ref, s1_ref, c0_ref, c1_ref = refs
        d_val = C_HEAD_DIM
    tq = q_ref.shape[1]
    tk = k_ref.shape[2]
    qi = pl.program_id(2)
    lane = lax.broadcasted_iota(jnp.int32, (1, LANES), 1)
    low = lane < (LANES // 2)
    q = q_ref[0]
    zero = jnp.zeros_like(q)
    for a in range(2):
        qa = jnp.where(low, q, zero) if a == 0 else jnp.where(low, zero, q)
        if bias_ref is not None:
            pick = jnp.where((lane >= 3 * a) & (lane < 3 * a + 3), 1.0, 0.0).astype(BF16)
            qa = jnp.concatenate([qa, jnp.broadcast_to(pick, (tq, LANES))], axis=1)
        qc_ref[a] = qa
    m_ref[...] = jnp.full(m_ref.shape, NEG, F32)
    acc_ref[...] = jnp.zeros(acc_ref.shape, F32)
    ones = jnp.ones((BF16_ROWS, tk), BF16)

    units = [(a, slice(n * MXU_DIM, (n + 1) * MXU_DIM)) for a in range(2) for n in range(tq // MXU_DIM)]

    def score_chain(j, u):
        a, cs = units[u]
        kc = k_ref[0, j]
        if bias_ref is not None:
            kc = jnp.concatenate([kc, bias_ref[0, j]], axis=1)
        return _dot_nt(kc, qc_ref[a, cs, :])

    def value_chain(j, u, st, cmax, key_off):
        a, cs = units[u]
        masked = key_off is not None
        if masked:
            key = lax.broadcasted_iota(jnp.int32, (tk, MXU_DIM), 0) + key_off
            qry = lax.broadcasted_iota(jnp.int32, (tk, MXU_DIM), 1) + cs.start
            keep = ((key >> CHUNK_SHIFT) <= (qry >> CHUNK_SHIFT)) if mode == "A" else (key <= qry)
            st = jnp.where(keep, st, NEG)
            cmax = jnp.max(st, axis=0, keepdims=True)
        m_prev = m_ref[a, :, cs]
        m_new = jnp.maximum(m_prev, cmax)
        alpha = jnp.exp2(m_prev - m_new)
        pt = jnp.exp2(st - m_new).astype(BF16)
        vt = vt_ref[0, j]
        va = vt if mode == "A" else vt[a * d_val:(a + 1) * d_val]
        va = jnp.concatenate([va, ones], axis=0)
        acc_ref[a, :, cs] = alpha * acc_ref[a, :, cs] + _dot(va, pt)
        m_ref[a, :, cs] = m_new

    def stage(j_scores, dst, j_values, src, key_off=None):
        for t in range(len(units) + 1):
            if j_scores is not None and t < len(units):
                st = score_chain(j_scores, t)
                dst[0][t] = st
                dst[1][t] = jnp.max(st, axis=0, keepdims=True)
            if j_values is not None and t >= 1:
                value_chain(j_values, t - 1, src[0][t - 1], src[1][t - 1], key_off)

    buf0 = (s0_ref, c0_ref)
    buf1 = (s1_ref, c1_ref)
    n_diag = tq // tk
    n_full = qi * n_diag
    j0 = first_ref[pl.program_id(0), pl.program_id(1), qi]
    n_vis = n_full - j0
    def pair(j):
        stage(j + 1, buf1, j, buf0)
        stage(j + 2, buf0, j + 1, buf1)

    def diagonal():
        for d in range(0, n_diag, 2):
            j = n_full + d
            stage(j + 1, buf1, j, buf0, key_off=d * tk)
            stage(j + 2 if d + 2 < n_diag else None, buf0, j + 1, buf1, key_off=(d + 1) * tk)

    def any_window():
        stage(j0, buf0, None, None)

        def quad(p, carry):
            pair(j0 + 4 * p)
            pair(j0 + 4 * p + 2)
            return carry

        lax.fori_loop(0, n_vis >> 2, quad, 0)

        @pl.when((n_vis & 2) == 2)
        def _():
            pair(j0 + (n_vis & ~3))

        diagonal()

    if mode == "C":
        def straight(window):
            stage(j0, buf0, None, None)
            for d in range(0, window, 2):
                pair(j0 + d)
            diagonal()

        for window in SHORT_WINDOWS:
            pl.when(n_vis == window)(functools.partial(straight, window))

        @pl.when(functools.reduce(jnp.logical_and, [n_vis != window for window in SHORT_WINDOWS]))
        def _():
            any_window()
    else:
        any_window()

    outs = []
    for a in range(2):
        acc = acc_ref[a]
        outs.append(acc[:d_val] * (1.0 / acc[d_val:d_val + 1]))
    if mode == "A":
        ot = outs[0] - _lambda(lam_ref, lam_init) * outs[1]
        ot = ot * lax.rsqrt(jnp.mean(ot * ot, axis=0, keepdims=True) + LN_EPS)
        ot = ot * (gain_ref[0] * (1.0 - lam_init))
    else:
        ot = jnp.concatenate(outs, axis=0)
    o_ref[0] = ot.T.astype(o_ref.dtype)


def _attention(q, k, vt, first, mode, lam=None, gain=None, bias=None, lam_init=0.0):
    b, t, width = q.shape
    nk, tk = vt.shape[1], vt.shape[3]
    tq = 2 * tk
    assert t % tq == 0
    groups = width // LANES
    d_aug = (2 * A_HEAD_DIM if mode == "A" else C_HEAD_DIM) + BF16_ROWS
    kspec = pl.BlockSpec((1, nk, tk, LANES), lambda bi, g, i, f: (bi, 0, 0, g))
    in_specs = [pl.BlockSpec((1, tq, LANES), lambda bi, g, i, f: (bi, i, g)), kspec]
    args = [q, k.reshape(b, nk, tk, width)]
    if mode == "C":
        in_specs.append(kspec)
        args.append(bias.reshape(b, nk, tk, groups * LANES))
    in_specs.append(pl.BlockSpec((1, nk, LANES, tk), lambda bi, g, i, f: (bi, 0, g, 0)))
    args.append(vt)
    if mode == "A":
        in_specs += [pl.BlockSpec((4, A_HEAD_DIM), lambda bi, g, i, f: (0, 0)),
                     pl.BlockSpec((1, LANES, 1), lambda bi, g, i, f: (g, 0, 0))]
        args += [lam, gain]
    return pl.pallas_call(
        functools.partial(_attn_kernel, mode=mode, lam_init=lam_init),
        grid_spec=pltpu.PrefetchScalarGridSpec(
            num_scalar_prefetch=1,
            grid=(b, groups, t // tq),
            in_specs=in_specs,
            out_specs=pl.BlockSpec((1, tq, LANES), lambda bi, g, i, f: (bi, i, g)),
            scratch_shapes=[pltpu.VMEM((2, tq, LANES if mode == "A" else 2 * LANES), BF16),
                            pltpu.VMEM((2, 1, tq), F32),
                            pltpu.VMEM((2, d_aug, tq), F32),
                            pltpu.VMEM((2 * tq // MXU_DIM, tk, MXU_DIM), F32),
                            pltpu.VMEM((2 * tq // MXU_DIM, tk, MXU_DIM), F32),
                            pltpu.VMEM((2 * tq // MXU_DIM, 1, MXU_DIM), F32),
                            pltpu.VMEM((2 * tq // MXU_DIM, 1, MXU_DIM), F32)]),
        out_shape=jax.ShapeDtypeStruct((b, t, width), BF16),
        compiler_params=_params(("arbitrary", "arbitrary", "arbitrary")),
        name="attn_" + mode,
    )(first, *args)


def _first_visible_block(bounds, tq_tiles):
    qn, kn, bmax, bmin = (bounds[:, :, r, :C_HEADS] for r in range(4))
    b, tiles, _ = qn.shape
    nq = tiles // tq_tiles
    blk = lambda x, f: f(x.reshape(b, nq, tq_tiles, C_HEADS), axis=2)
    qn_q = blk(qn, jnp.max) * NORM_SLACK
    own = blk(bmin, jnp.min) - qn_q * blk(kn, jnp.max) * NORM_SLACK
    best = qn_q[:, :, None, :] * (kn * NORM_SLACK)[:, None, :, :] + bmax[:, None, :, :]
    dead = best < (own[:, :, None, :] - SKIP_GAP)
    block = jnp.arange(tiles, dtype=jnp.int32)[None, None, :, None]
    lead = jnp.min(jnp.where(dead, tiles, block), axis=2)
    lead = jnp.min(lead.reshape(b, nq, C_HEADS // 2, 2), axis=-1)
    lead = jnp.minimum(lead, (jnp.arange(nq, dtype=jnp.int32) * tq_tiles)[None, :, None])
    return jnp.transpose(lead - lead % 2, (0, 2, 1)).astype(jnp.int32)


def _diff_finish(o0, o1, lam, gain, lam_init):
    o = o0 - lam * o1
    o = o * lax.rsqrt(jnp.mean(o * o, axis=-1, keepdims=True) + LN_EPS)
    return o * gain * (1.0 - lam_init)


def _cached_attn_kernel(*refs, mode, past, lam_init):
    if mode == "A":
        q_ref, kn_ref, vn_ref, ck_ref, cv_ref, lam_ref, gain_ref, o_ref = refs
    else:
        q_ref, kn_ref, vn_ref, ck_ref, cv_ref, clf_ref, nlf_ref, o_ref = refs
    t = q_ref.shape[1]
    width = o_ref.shape[2]
    groups = width // LANES
    lane = lax.broadcasted_iota(jnp.int32, (1, LANES), 1)
    low = lane < (LANES // 2)
    row = lax.broadcasted_iota(jnp.int32, (t, t), 0)
    col = lax.broadcasted_iota(jnp.int32, (t, t), 1)
    if mode == "A":
        keep = ((past + col) >> CHUNK_SHIFT) <= ((past + row) >> CHUNK_SHIFT)
        lam = _lambda(lam_ref, lam_init)
    else:
        keep = col <= row
        r = lax.broadcasted_iota(jnp.int32, (past, past), 0)
        c = lax.broadcasted_iota(jnp.int32, (past, past), 1)
        upper = jnp.where(r <= c, 1.0, 0.0).astype(BF16)
        clf = clf_ref[0]
        hi, mid, lo = _split3(clf)
        cum_c = (_dot(hi, upper) + _dot(mid, upper)) + _dot(lo, upper)
        upper_n = jnp.where(row <= col, 1.0, 0.0).astype(BF16)
        hi, mid, lo = _split3(nlf_ref[0])
        cum_n = ((_dot(hi, upper_n) + _dot(mid, upper_n)) + _dot(lo, upper_n)
                 + jnp.sum(clf, axis=1, keepdims=True))
        bias_c = cum_c * (-LOG2E)
        bias_n = cum_n * (-LOG2E)
    keep2 = jnp.concatenate([keep, keep], axis=0)
    first_rows = lax.broadcasted_iota(jnp.int32, (2 * t, 1), 0) < t
    for g in range(groups):
        sl = slice(g * LANES, (g + 1) * LANES)
        q = q_ref[0, :, sl]
        kn = kn_ref[0, :, sl]
        vn = vn_ref[0, :, sl].astype(BF16)
        if mode == "A":
            kc = ck_ref[0, 0, :, g, :].astype(BF16)
            vc = cv_ref[0, 0, :, g, :].astype(BF16)
        else:
            kc = ck_ref[0, :, sl].astype(BF16)
            vc = cv_ref[0, :, sl].astype(BF16)
        zero = jnp.zeros_like(q)
        q2 = jnp.concatenate([jnp.where(low, q, zero), jnp.where(low, zero, q)], axis=0)
        s_c = _dot_nt(q2, kc)
        s_n = _dot_nt(q2, kn)
        if mode == "C":
            h = 2 * g
            s_c = s_c + jnp.where(first_rows, bias_c[h:h + 1, :], bias_c[h + 1:h + 2, :])
            s_n = s_n + jnp.where(first_rows, bias_n[h:h + 1, :], bias_n[h + 1:h + 2, :])
        s_n = jnp.where(keep2, s_n, NEG)
        m = jnp.maximum(jnp.max(s_c, axis=1, keepdims=True), jnp.max(s_n, axis=1, keepdims=True))
        p_c = jnp.exp2(s_c - m)
        p_n = jnp.exp2(s_n - m)
        l = jnp.sum(p_c, axis=1, keepdims=True) + jnp.sum(p_n, axis=1, keepdims=True)
        acc = (_dot(p_c.astype(BF16), vc) + _dot(p_n.astype(BF16), vn)) * (1.0 / l)
        outs = [acc[:t], acc[t:]]
        if mode == "A":
            o = _diff_finish(outs[0], outs[1], lam, gain_ref[g], lam_init)
        else:
            o = jnp.where(low, outs[0], outs[1])
        o_ref[0, :, sl] = o.astype(o_ref.dtype)


def _cached_attention(q, kn, vn, cache_k, cache_v, mode, lam=None, gain=None, cache_lft=None, new_lft=None,
                      lam_init=0.0):
    b, t, width = q.shape
    new = pl.BlockSpec((1, t, width), lambda bi: (bi, 0, 0))
    if mode == "A":
        _, _, past, heads, dh = cache_k.shape
        old = pl.BlockSpec((1, 1, past, heads, dh), lambda bi: (0, bi, 0, 0, 0))
    else:
        past = cache_k.shape[1]
        old = pl.BlockSpec((1, past, width), lambda bi: (bi, 0, 0))
    in_specs = [new, new, new, old, old]
    args = [q, kn, vn, cache_k, cache_v]
    if mode == "A":
        in_specs += [pl.BlockSpec((4, A_HEAD_DIM), lambda bi: (0, 0)),
                     pl.BlockSpec((A_HEADS, 1, LANES), lambda bi: (0, 0, 0))]
        args += [lam, gain]
    else:
        in_specs += [pl.BlockSpec((1, C_HEADS, past), lambda bi: (bi, 0, 0)),
                     pl.BlockSpec((1, C_HEADS, t), lambda bi: (bi, 0, 0))]
        args += [cache_lft, new_lft]
    return pl.pallas_call(
        functools.partial(_cached_attn_kernel, mode=mode, past=past, lam_init=lam_init),
        grid=(b,),
        in_specs=in_specs,
        out_specs=new,
        out_shape=jax.ShapeDtypeStruct((b, t, width), BF16),
        compiler_params=_params(("arbitrary",)),
        name="cached_attn_" + mode,
    )(*args)


def _gelu_tanh(x):
    return 0.5 * x * (1.0 + jnp.tanh(math.sqrt(2.0 / math.pi) * (x + 0.044715 * (x * x * x))))


def _causal_conv(x, tail, w_ref, b_ref):
    width = w_ref.shape[0]
    tm = x.shape[0]
    cat = jnp.concatenate([tail, x], axis=0)
    y = None
    for j in range(width):
        back = width - 1 - j
        src = cat if back == 0 else pltpu.roll(cat, back, 0)
        term = src[SUBLANES:SUBLANES + tm] * w_ref[j:j + 1, :]
        y = term if y is None else y + term
    return y + b_ref[...]


def _rglru_kernel(xg_ref, h0_ref, cb_ref, cw_ref, cbias_ref, wa_ref, ba_ref, wx_ref, bx_ref, sp_ref,
                  y_ref, hl_ref, ct_ref, hc_ref, tail_ref):
    i = pl.program_id(1)
    tm = xg_ref.shape[1]

    @pl.when(i == 0)
    def _():
        hc_ref[...] = h0_ref[0]
        tail_ref[...] = cb_ref[0]

    pos = lax.broadcasted_iota(jnp.int32, (tm, 1), 0) & (SUBLANES - 1)
    for c in range(B_WIDTH // MXU_DIM):
        cs = slice(c * MXU_DIM, (c + 1) * MXU_DIM)
        x = xg_ref[0, :, cs]
        gate_in = xg_ref[0, :, B_WIDTH + c * MXU_DIM:B_WIDTH + (c + 1) * MXU_DIM]
        xc = _causal_conv(x, tail_ref[:, cs], cw_ref.at[:, cs], cbias_ref.at[:, cs])
        xcb = xc.astype(BF16)
        r = _sigmoid(_dot(xcb, wa_ref[cs, cs]) + ba_ref[:, cs])
        ig = _sigmoid(_dot(xcb, wx_ref[cs, cs]) + bx_ref[:, cs])
        log_a = (-RG_C) * r * sp_ref[:, cs]
        a = jnp.exp(log_a)
        th = jnp.tanh(log_a)
        bx = jnp.sqrt((-2.0 * th) / (1.0 - th)) * (ig * xc)
        s = 1
        while s < SUBLANES:
            valid = pos >= s
            a_sh = pltpu.roll(a, s, 0)
            b_sh = pltpu.roll(bx, s, 0)
            bx = jnp.where(valid, a * b_sh + bx, bx)
            a = jnp.where(valid, a * a_sh, a)
            s *= 2
        carry = hc_ref[:, cs]
        groups = []
        for g in range(tm // SUBLANES):
            rows = slice(g * SUBLANES, (g + 1) * SUBLANES)
            hg = a[rows] * carry + bx[rows]
            groups.append(hg)
            carry = hg[SUBLANES - 1:, :]
        h = jnp.concatenate(groups, axis=0)
        y_ref[0, :, cs] = (h * _gelu_tanh(gate_in)).astype(y_ref.dtype)
        h_tail = h[tm - SUBLANES:, :]
        x_tail = x[tm - SUBLANES:, :]
        hc_ref[:, cs] = h_tail[SUBLANES - 1:, :]
        tail_ref[:, cs] = x_tail
        hl_ref[0, :, cs] = h_tail
        ct_ref[0, :, cs] = x_tail


def _rglru(xg, h0, conv_tail, cw, cbias, wa, ba, wx, bx, sp, tm_pref=256):
    b, t, _ = xg.shape
    tm = _row_tile(t, tm_pref)
    wspec = lambda shape: pl.BlockSpec(shape, lambda bi, i: tuple(0 for _ in shape))
    return pl.pallas_call(
        _rglru_kernel,
        grid=(b, t // tm),
        in_specs=[pl.BlockSpec((1, tm, 2 * B_WIDTH), lambda bi, i: (bi, i, 0)),
                  pl.BlockSpec((1, 1, B_WIDTH), lambda bi, i: (bi, 0, 0)),
                  pl.BlockSpec((1, SUBLANES, B_WIDTH), lambda bi, i: (bi, 0, 0)),
                  wspec((B_CONV, B_WIDTH)), wspec((1, B_WIDTH)),
                  wspec((B_WIDTH, B_WIDTH)), wspec((1, B_WIDTH)),
                  wspec((B_WIDTH, B_WIDTH)), wspec((1, B_WIDTH)), wspec((1, B_WIDTH))],
        out_specs=[pl.BlockSpec((1, tm, B_WIDTH), lambda bi, i: (bi, i, 0)),
                   pl.BlockSpec((1, SUBLANES, B_WIDTH), lambda bi, i: (bi, 0, 0)),
                   pl.BlockSpec((1, SUBLANES, B_WIDTH), lambda bi, i: (bi, 0, 0))],
        out_shape=[jax.ShapeDtypeStruct((b, t, B_WIDTH), BF16),
                   jax.ShapeDtypeStruct((b, SUBLANES, B_WIDTH), F32),
                   jax.ShapeDtypeStruct((b, SUBLANES, B_WIDTH), F32)],
        scratch_shapes=[pltpu.VMEM((1, B_WIDTH), F32), pltpu.VMEM((SUBLANES, B_WIDTH), F32)],
        compiler_params=_params(("arbitrary", "arbitrary")),
        name="rglru",
    )(xg, h0, conv_tail, cw, cbias, wa, ba, wx, bx, sp)


def _proj_ln_kernel(*refs, n_in):
    h_refs = refs[:n_in]
    w_ref, x_ref, gate_ref, g_ref, b_ref, o_ref = refs[n_in:]
    tm = x_ref.shape[1]
    rows = [slice(r, min(r + LN_ROWS, tm)) for r in range(0, tm, LN_ROWS)]

    def matmul(rs):
        hs = [r[0, rs, :] for r in h_refs]
        return _dot(hs[0] if n_in == 1 else jnp.concatenate(hs, axis=-1), w_ref[0])

    def norm(rs, proj):
        y = ALPHA * x_ref[0, rs, :] + gate_ref[0] * proj
        mu = jnp.mean(y, axis=-1, keepdims=True)
        yc = y - mu
        var = jnp.mean(yc * yc, axis=-1, keepdims=True)
        o_ref[0, rs, :] = yc * lax.rsqrt(var + LN_EPS) * g_ref[...] + b_ref[...]

    pending = matmul(rows[0])
    for r in range(len(rows)):
        nxt = matmul(rows[r + 1]) if r + 1 < len(rows) else None
        norm(rows[r], pending)
        pending = nxt


def _proj_ln(hs, w, layer, x, gate, ln_g, ln_b, tm_pref=1024):
    b, t, d = x.shape
    tm = _row_tile(t, tm_pref)
    k = w.shape[1]
    gate_spec = (pl.BlockSpec((1, 1, d), lambda bi, i: (bi, 0, 0)) if gate.shape[1] == 1
                 else pl.BlockSpec((1, tm, d), lambda bi, i: (bi, i, 0)))
    in_specs = [pl.BlockSpec((1, tm, h.shape[2]), lambda bi, i: (bi, i, 0)) for h in hs]
    in_specs += [pl.BlockSpec((1, k, d), lambda bi, i: (layer, 0, 0)),
                 pl.BlockSpec((1, tm, d), lambda bi, i: (bi, i, 0)),
                 gate_spec,
                 pl.BlockSpec((1, d), lambda bi, i: (0, 0)),
                 pl.BlockSpec((1, d), lambda bi, i: (0, 0))]
    return pl.pallas_call(
        functools.partial(_proj_ln_kernel, n_in=len(hs)),
        grid=(b, t // tm),
        in_specs=in_specs,
        out_specs=pl.BlockSpec((1, tm, d), lambda bi, i: (bi, i, 0)),
        out_shape=jax.ShapeDtypeStruct((b, t, d), F32),
        compiler_params=_params(("arbitrary", "arbitrary")),
        name="proj_ln",
    )(*hs, w, x, gate, ln_g, ln_b)


def _ffn_up_kernel(x_ref, sc_ref, sh_ref, w_ref, cb_ref, cw_ref, cbias_ref, h_ref, ct_ref, tail_ref):
    i = pl.program_id(1)
    seqs, tm = x_ref.shape[0], x_ref.shape[1]

    @pl.when(i == 0)
    def _():
        tail_ref[...] = cb_ref[...]

    us = [_modulate(x_ref[s], sc_ref[s], sh_ref[s]) for s in range(seqs)]
    u = us[0] if seqs == 1 else jnp.concatenate(us, axis=0)
    chunks = [slice(c, min(c + FFN_CHUNK, D_FF)) for c in range(0, D_FF, FFN_CHUNK)]

    def matmuls(cs):
        gs = slice(D_FF + cs.start, D_FF + cs.stop)
        return _dot(u, w_ref[0, :, cs]), _dot(u, w_ref[0, :, gs])

    def gate(cs, a_all, g_all):
        for s in range(seqs):
            rows = slice(s * tm, (s + 1) * tm)
            a, g = a_all[rows], g_all[rows]
            gc = _causal_conv(g, tail_ref[s, :, cs], cw_ref.at[:, cs], cbias_ref.at[:, cs])
            silu = gc * (0.5 * jnp.tanh(0.5 * gc) + 0.5)
            h_ref[s, :, cs] = (a * silu).astype(h_ref.dtype)
            g_tail = g[tm - SUBLANES:, :]
            tail_ref[s, :, cs] = g_tail
            ct_ref[s, :, cs] = g_tail

    pending = matmuls(chunks[0])
    for c in range(len(chunks)):
        nxt = matmuls(chunks[c + 1]) if c + 1 < len(chunks) else None
        gate(chunks[c], *pending)
        pending = nxt


def _ffn_up(x, sc, sh, w, layer, conv_tail, cw, cbias, tm_pref=256):
    b, t, d = x.shape
    tm = _row_tile(t, tm_pref)
    seqs = max(1, min(b, LANES // tm))
    while b % seqs:
        seqs -= 1
    return pl.pallas_call(
        _ffn_up_kernel,
        grid=(b // seqs, t // tm),
        in_specs=[pl.BlockSpec((seqs, tm, d), lambda bi, i: (bi, i, 0)),
                  pl.BlockSpec((seqs, 1, d), lambda bi, i: (bi, 0, 0)),
                  pl.BlockSpec((seqs, 1, d), lambda bi, i: (bi, 0, 0)),
                  pl.BlockSpec((1, d, 2 * D_FF), lambda bi, i: (layer, 0, 0)),
                  pl.BlockSpec((seqs, SUBLANES, D_FF), lambda bi, i: (bi, 0, 0)),
                  pl.BlockSpec((FFN_CONV, D_FF), lambda bi, i: (0, 0)),
                  pl.BlockSpec((1, D_FF), lambda bi, i: (0, 0))],
        out_specs=[pl.BlockSpec((seqs, tm, D_FF), lambda bi, i: (bi, i, 0)),
                   pl.BlockSpec((seqs, SUBLANES, D_FF), lambda bi, i: (bi, 0, 0))],
        out_shape=[jax.ShapeDtypeStruct((b, t, D_FF), BF16),
                   jax.ShapeDtypeStruct((b, SUBLANES, D_FF), F32)],
        scratch_shapes=[pltpu.VMEM((seqs, SUBLANES, D_FF), F32)],
        compiler_params=_params(("arbitrary", "arbitrary")),
        name="ffn_up",
    )(x, sc, sh, w, conv_tail, cw, cbias)


def _pad_tail(buf):
    return jnp.pad(buf, ((0, 0), (SUBLANES - buf.shape[1], 0), (0, 0)))


def _rope_tables(past, t):
    half = A_HEAD_DIM // 2
    inv = ROPE_THETA ** (-jnp.arange(0, A_HEAD_DIM, 2, dtype=F32) / A_HEAD_DIM)
    pos = (past + jnp.arange(t, dtype=jnp.int32)).astype(F32)
    ang = pos[:, None] * inv[None, :]
    cos = jnp.tile(jnp.cos(ang), (1, LANES // half))
    sin = jnp.sin(ang)
    sin_signed = jnp.tile(jnp.concatenate([-sin, sin], axis=1), (1, LANES // A_HEAD_DIM))
    return cos, sin_signed


def _block_diag(w):
    n, i, o = w.shape
    return jnp.einsum("nio,nm->nimo", w, jnp.eye(n, dtype=w.dtype)).reshape(n * i, n * o)


def _prepare(p):
    w = {}
    w_in_ab = p["w_in_ab"][0]
    w["in_ab"] = w_in_ab.astype(BF16)
    w["lam"] = jnp.stack([p["lam_q1"][0], p["lam_k1"][0], p["lam_q2"][0], p["lam_k2"][0]])
    w["gain_row"] = p["attn_gain"][0].reshape(A_HEADS, 1, LANES)
    w["gain_col"] = p["attn_gain"][0].reshape(A_HEADS, LANES, 1)
    w["b_conv_w"] = p["b_conv_w"][0]
    w["b_conv_b"] = p["b_conv_b"][0].reshape(1, B_WIDTH)
    w["rg_a"] = _block_diag(p["w_rg_a"][0]).astype(BF16)
    w["rg_x"] = _block_diag(p["w_rg_x"][0]).astype(BF16)
    w["b_rg_a"] = p["b_rg_a"][0].reshape(1, B_WIDTH)
    w["b_rg_x"] = p["b_rg_x"][0].reshape(1, B_WIDTH)
    w["rg_L"] = p["rg_L"][0].reshape(1, B_WIDTH)
    w_in_c = p["w_in_c"][0]
    w["in_c"] = jnp.pad(w_in_c, ((0, 0), (0, LANES - C_HEADS))).astype(BF16)
    w["in_c_ft"] = w_in_c[:, 3 * C_WIDTH:].T.astype(BF16)
    w["bf_row"] = jnp.pad(p["b_f"][0], (0, LANES - C_HEADS)).reshape(1, LANES)
    w["bf_col"] = p["b_f"][0].reshape(C_HEADS, 1)
    w["place"] = _bias_placement()
    w["seg"] = jnp.asarray(np.repeat(np.eye(LANES, dtype=np.float32)[:C_HEADS], C_HEAD_DIM, axis=0), BF16)
    w["up"] = p["w_up"].astype(BF16)
    w["down"] = p["w_down"].astype(BF16)
    w["out_ab"] = p["w_out_ab"].astype(BF16)
    w["out_c"] = p["w_out_c"].astype(BF16)
    w["ffn_conv_w"] = [p["ffn_conv_w"][i] for i in range(DEPTH)]
    w["ffn_conv_b"] = [p["ffn_conv_b"][i].reshape(1, D_FF) for i in range(DEPTH)]
    w["ln1_g"] = [p["ln1_g"][i].reshape(1, D_MODEL) for i in range(DEPTH)]
    w["ln1_b"] = [p["ln1_b"][i].reshape(1, D_MODEL) for i in range(DEPTH)]
    w["ln2_g"] = [p["ln2_g"][i].reshape(1, D_MODEL) for i in range(DEPTH)]
    w["ln2_b"] = [p["ln2_b"][i].reshape(1, D_MODEL) for i in range(DEPTH)]
    return w


def _softplus_kernel(x_ref, o_ref):
    o_ref[...] = _softplus(-x_ref[...])


def _proj_ln_rows(hs, w, layer, x, gate, ln_g, ln_b):
    b, t, d = x.shape
    flat = lambda a: a.reshape(1, b * t, a.shape[2])
    gate_rows = jnp.broadcast_to(gate, (b, t, d))
    return _proj_ln([flat(h) for h in hs], w, layer, flat(x), flat(gate_rows), ln_g, ln_b).reshape(b, t, d)


def _trunk(x, mods, w, sp, cache_a_k=None, cache_a_v=None, state_b_h=None, state_b_conv=None,
           cache_c_k=None, cache_c_v=None, cache_c_logf=None, state_ffn_conv=None):
    b, t, d = x.shape
    cached = cache_a_k is not None
    proj_ln = _proj_ln_rows if cached else _proj_ln
    past = cache_a_k.shape[2] if cached else 0
    outs = {}
    for i in range(DEPTH):
        sh1, sc1, g1, sh2, sc2, g2 = [m[:, None, :] for m in jnp.split(mods[i], 6, axis=-1)]
        if i % 2 == 0:
            lam_init = 0.8 - 0.6 * math.exp(-0.3 * i)
            cos, sin = _rope_tables(past, t)
            q, kb, vt, k32, v32, xg = _proj_ab(x, sc1, sh1, w["in_ab"], cos, sin)
            if cached:
                o = _cached_attention(q, kb, v32.reshape(b, t, A_WIDTH), cache_a_k, cache_a_v,
                                      "A", lam=w["lam"], gain=w["gain_row"], lam_init=lam_init)
                h0 = state_b_h[0][:, None, :]
                ctail = _pad_tail(state_b_conv[0])
            else:
                first = jnp.zeros((b, A_HEADS, t // (2 * ATTN_BLOCK)), jnp.int32)
                o = _attention(q, kb, vt, first, "A", lam=w["lam"], gain=w["gain_col"], lam_init=lam_init)
                h0 = jnp.zeros((b, 1, B_WIDTH), F32)
                ctail = jnp.zeros((b, SUBLANES, B_WIDTH), F32)
            yb, h_tail, x_tail = _rglru(xg, h0, ctail, w["b_conv_w"], w["b_conv_b"], w["rg_a"], w["b_rg_a"],
                                        w["rg_x"], w["b_rg_x"], sp)
            outs["a_k"] = k32[None]
            outs["a_v"] = v32[None]
            outs["b_h"] = h_tail[:, SUBLANES - 1, :][None]
            outs["b_conv"] = x_tail[:, SUBLANES - (B_CONV - 1):, :][None]
            x = proj_ln([o, yb], w["out_ab"], i // 2, x, g1, w["ln1_g"][i], w["ln1_b"][i])
        else:
            q, kb, bias, vt, k32, v32, lf, lft, bounds = _proj_c(x, sc1, sh1, w["in_c"], w["in_c_ft"],
                                                                 w["bf_row"], w["bf_col"], w["place"], w["seg"])
            if cached:
                o = _cached_attention(q, kb, v32, cache_c_k[0].reshape(b, past, C_WIDTH),
                                      cache_c_v[0].reshape(b, past, C_WIDTH),
                                      "C", cache_lft=jnp.swapaxes(cache_c_logf[0], 1, 2), new_lft=lft)
            else:
                o = _attention(q, kb, vt, _first_visible_block(bounds, 2), "C", bias=bias)
            outs["c_k"] = k32.reshape(1, b, t, C_HEADS, C_HEAD_DIM)
            outs["c_v"] = v32.reshape(1, b, t, C_HEADS, C_HEAD_DIM)
            outs["c_logf"] = lf[None]
            x = proj_ln([o], w["out_c"], i // 2, x, g1, w["ln1_g"][i], w["ln1_b"][i])
        ftail = _pad_tail(state_ffn_conv[i]) if cached else jnp.zeros((b, SUBLANES, D_FF), F32)
        hmid, g_tail = _ffn_up(x, sc2, sh2, w["up"], i, ftail, w["ffn_conv_w"][i], w["ffn_conv_b"][i])
        outs.setdefault("ffn", []).append(g_tail[:, SUBLANES - (FFN_CONV - 1):, :])
        x = proj_ln([hmid], w["down"], i, x, g2, w["ln2_g"][i], w["ln2_b"][i])
    return (x, outs["a_k"], outs["a_v"], outs["b_h"], outs["b_conv"],
            outs["c_k"], outs["c_v"], outs["c_logf"], jnp.stack(outs["ffn"]))


def kernel(x_prompt, x_sample, c_prompt, c_sample, cache_a_k, cache_a_v, state_b_h, state_b_conv, cache_c_k, cache_c_v, cache_c_logf, state_ffn_conv, w_ada, b_ada, ln1_g, ln1_b, ln2_g, ln2_b, w_in_ab, lam_q1, lam_k1, lam_q2, lam_k2, attn_gain, b_conv_w, b_conv_b, w_rg_a, b_rg_a, w_rg_x, b_rg_x, rg_L, w_out_ab, w_in_c, b_f, w_out_c, w_up, ffn_conv_w, ffn_conv_b, w_down):
    p = dict(w_in_ab=w_in_ab, lam_q1=lam_q1, lam_k1=lam_k1, lam_q2=lam_q2, lam_k2=lam_k2, attn_gain=attn_gain,
             b_conv_w=b_conv_w, b_conv_b=b_conv_b, w_rg_a=w_rg_a, b_rg_a=b_rg_a, w_rg_x=w_rg_x, b_rg_x=b_rg_x,
             rg_L=rg_L, w_out_ab=w_out_ab, w_in_c=w_in_c, b_f=b_f, w_out_c=w_out_c, w_up=w_up,
             ffn_conv_w=ffn_conv_w, ffn_conv_b=ffn_conv_b, w_down=w_down,
             ln1_g=ln1_g, ln1_b=ln1_b, ln2_g=ln2_g, ln2_b=ln2_b)
    w = _prepare(p)
    bp = c_prompt.shape[0]
    bs = c_sample.shape[0]
    rows = -(-(bp + bs) // 16) * 16
    c_all = jnp.pad(jnp.concatenate([c_prompt, c_sample], axis=0), ((0, rows - bp - bs), (0, 0)))
    mods = _mods(c_all, w_ada, b_ada)
    sp = pl.pallas_call(_softplus_kernel, out_shape=jax.ShapeDtypeStruct((1, B_WIDTH), F32),
                        name="softplus")(w["rg_L"])
    res_p = _trunk(x_prompt, mods[:, :bp], w, sp)
    res_s = _trunk(x_sample, mods[:, bp:bp + bs], w, sp, cache_a_k, cache_a_v, state_b_h, state_b_conv,
                   cache_c_k, cache_c_v, cache_c_logf, state_ffn_conv)
    return (res_p[0], res_s[0]) + res_p[1:] + res_s[1:]
```
